```python
import math
import jax, jax.numpy as jnp
from jax import lax
import numpy as np

D_MODEL = 1024
BATCH = 8
SEQ = 4096
DEPTH = 4

CHUNK = 64
QBLOCK = 128
A_HEADS = 8
A_HEAD_DIM = 64
A_WIDTH = A_HEADS * A_HEAD_DIM
KV_RANK = 128
IDX_HEADS = 8
IDX_DIM = 64
IDX_TOPK_MAX = 256
IDX_W_SCALE = (IDX_HEADS ** -0.5) * (IDX_DIM ** -0.5)
ATTN_SCALE = A_HEAD_DIM ** -0.5
NUM_BUCKETS = 32
MAX_DISTANCE = 128
B_HEADS = 4
B_KEY_DIM = 128
B_VAL_DIM = 128
B_WIDTH = B_HEADS * B_VAL_DIM
B_FDIM = B_HEADS * B_KEY_DIM
COL_AQ = A_WIDTH
COL_CKV = KV_RANK
COL_IQ = IDX_HEADS * IDX_DIM
COL_IK = IDX_DIM
COL_IW = IDX_HEADS
COL_HQ = B_FDIM
COL_HF = B_FDIM
COL_HI = B_WIDTH
COL_HG = B_WIDTH
IN_WIDTH = COL_AQ + COL_CKV + COL_IQ + COL_IK + COL_IW + COL_HQ + COL_HF + COL_HI + COL_HG
SPLIT_POINTS = (
    COL_AQ,
    COL_AQ + COL_CKV,
    COL_AQ + COL_CKV + COL_IQ,
    COL_AQ + COL_CKV + COL_IQ + COL_IK,
    COL_AQ + COL_CKV + COL_IQ + COL_IK + COL_IW,
    COL_AQ + COL_CKV + COL_IQ + COL_IK + COL_IW + COL_HQ,
    COL_AQ + COL_CKV + COL_IQ + COL_IK + COL_IW + COL_HQ + COL_HF,
    COL_AQ + COL_CKV + COL_IQ + COL_IK + COL_IW + COL_HQ + COL_HF + COL_HI,
)
N_EXPERTS = 64
TOP_K = 8
N_GROUPS = 8
TOPK_GROUPS = 4
EXPERT_DIM = 256
SHARED_DIM = 256
ROUTED_SCALE = 2.5
MOE_BLOCK = 512
DEEPNORM_ALPHA = (2 * DEPTH) ** 0.25
DEEPNORM_BETA = (8 * DEPTH) ** -0.25

kernel_name = "hybrid_dsa_hgrn2_moe_deepnorm_adaln"


def _layernorm(x, g, b, eps=1e-5):
    xf = x.astype(jnp.float32)
    mu = jnp.mean(xf, -1, keepdims=True)
    var = jnp.mean(jnp.square(xf - mu), -1, keepdims=True)
    return ((xf - mu) * lax.rsqrt(var + eps) * g + b).astype(x.dtype)


def _rmsnorm(x, g, eps=1e-6):
    xf = x.astype(jnp.float32)
    return (xf * lax.rsqrt(jnp.mean(xf * xf, -1, keepdims=True) + eps) * g).astype(x.dtype)


def _t5_bucket(rel):
    nb = NUM_BUCKETS // 2
    max_exact = nb // 2
    ret = jnp.where(rel > 0, nb, 0)
    n = jnp.abs(rel)
    nf = jnp.maximum(n, 1).astype(jnp.float32)
    large = max_exact + (jnp.log(nf / max_exact) / math.log(MAX_DISTANCE / max_exact)
                         * (nb - max_exact)).astype(jnp.int32)
    large = jnp.minimum(large, nb - 1)
    return ret + jnp.where(n < max_exact, n, large)


def _dsa_attention(q, c_kv, iq, ik, iw, w_uk, w_uv, rel_bias):
    B, S = q.shape[0], q.shape[1]
    n_blk = S // QBLOCK
    k_sel = min(IDX_TOPK_MAX, S // 4)
    q_lat = jnp.einsum('bshd,hdr->bshr', q, w_uk)
    key_pos = jnp.arange(S, dtype=jnp.int32)

    def blocks(a):
        return a.reshape((B, n_blk, QBLOCK) + a.shape[2:]).swapaxes(0, 1)

    def one_block(args):
        j, qb, iqb, iwb = args
        q_pos = j * QBLOCK + jnp.arange(QBLOCK, dtype=jnp.int32)
        limit = (q_pos // CHUNK + 1) * CHUNK
        rel = jax.nn.relu(jnp.einsum('bthd,bsd->bths', iqb, ik))
        idx_score = jnp.einsum('bth,bths->bts', iwb, rel).astype(jnp.float32)
        admissible = key_pos[None, :] < limit[:, None]
        idx_score = jnp.where(admissible[None], idx_score, -jnp.inf)
        _, sel = lax.top_k(idx_score, k_sel)
        valid = sel < limit[None, :, None]
        c_sel = jax.vmap(lambda cc, ii: cc[ii])(c_kv, sel)
        bias = rel_bias[_t5_bucket(sel - q_pos[None, :, None])]
        logits = (jnp.einsum('bthr,btkr->bthk', qb, c_sel).astype(jnp.float32) * ATTN_SCALE
                  + jnp.transpose(bias, (0, 1, 3, 2)).astype(jnp.float32))
        logits = jnp.where(valid[:, :, None, :], logits, -jnp.inf)
        p = jax.nn.softmax(logits, axis=-1).astype(c_sel.dtype)
        o_lat = jnp.einsum('bthk,btkr->bthr', p, c_sel)
        return jnp.einsum('bthr,hre->bthe', o_lat, w_uv).reshape(B, QBLOCK, A_WIDTH)

    out = lax.map(one_block, (jnp.arange(n_blk, dtype=jnp.int32), blocks(q_lat), blocks(iq), blocks(iw)))
    return out.swapaxes(0, 1).reshape(B, S, A_WIDTH)


def _chunked_gla(q, k, v, logf):
    B, S, H, DK = q.shape
    DV = v.shape[-1]
    n = S // CHUNK

    def to_chunks(a):
        return a.reshape(B, n, CHUNK, H, a.shape[-1]).transpose(1, 0, 3, 2, 4)

    causal = jnp.tril(jnp.ones((CHUNK, CHUNK), dtype=bool))

    def step(state, inp):
        qc, kc, vc, gc = inp
        b = jnp.cumsum(gc, axis=2)
        diff = b[:, :, :, None, :] - b[:, :, None, :, :]
        decay = jnp.exp(jnp.where(causal[:, :, None], diff, -jnp.inf))
        att = jnp.einsum('bhtd,bhsd,bhtsd->bhts', qc, kc, decay)
        o = jnp.einsum('bhts,bhsv->bhtv', att, vc) + jnp.einsum('bhtd,bhdv->bhtv', qc * jnp.exp(b), state)
        b_last = b[:, :, -1:, :]
        state = (jnp.exp(b_last[:, :, 0, :])[..., None] * state
                 + jnp.einsum('bhsd,bhsv->bhdv', kc * jnp.exp(b_last - b), vc))
        return state, o

    state0 = jnp.zeros((B, H, DK, DV), jnp.float32)
    _, o = lax.scan(step, state0, (to_chunks(q), to_chunks(k), to_chunks(v), to_chunks(logf)))
    return o.transpose(1, 0, 3, 2, 4).reshape(B, S, H, DV)


def _hgrn2(hq, hf, hi, hg, lb, gnorm_g):
    B, S = hq.shape[0], hq.shape[1]
    q = jax.nn.silu(hq.astype(jnp.float32)).reshape(B, S, B_HEADS, B_KEY_DIM)
    zf = hf.astype(jnp.float32)
    logf = jnp.logaddexp(jnp.log(lb), jnp.log1p(-lb) + jax.nn.log_sigmoid(zf))
    k = (1.0 - jnp.exp(logf)).reshape(B, S, B_HEADS, B_KEY_DIM)
    logf = logf.reshape(B, S, B_HEADS, B_KEY_DIM)
    v = hi.astype(jnp.float32).reshape(B, S, B_HEADS, B_VAL_DIM)
    o = _chunked_gla(q, k, v, logf)
    o = _rmsnorm(o, gnorm_g) * jax.nn.silu(hg.astype(jnp.float32)).reshape(B, S, B_HEADS, B_VAL_DIM)
    return o.reshape(B, S, B_WIDTH).astype(hq.dtype)


def _moe(u, w_router, router_bias, w_gate, w_up, w_down, ws_gate, ws_up, ws_down):
    B, S, D = u.shape
    T = B * S
    t = u.reshape(T, D)
    scores = jax.nn.sigmoid((t @ w_router).astype(jnp.float32))
    sel_scores = scores + router_bias.astype(jnp.float32)
    grp = sel_scores.reshape(T, N_GROUPS, N_EXPERTS // N_GROUPS)
    grp_score = jnp.sum(lax.top_k(grp, 2)[0], axis=-1)
    _, top_g = lax.top_k(grp_score, TOPK_GROUPS)
    g_mask = jnp.any(top_g[..., None] == jnp.arange(N_GROUPS)[None, None, :], axis=1)
    e_mask = jnp.repeat(g_mask, N_EXPERTS // N_GROUPS, axis=1)
    _, top_e = lax.top_k(jnp.where(e_mask, sel_scores, -jnp.inf), TOP_K)
    w = jnp.take_along_axis(scores, top_e, axis=1)
    w = w / jnp.sum(w, -1, keepdims=True) * ROUTED_SCALE
    gates = jnp.zeros((T, N_EXPERTS), jnp.float32).at[jnp.arange(T)[:, None], top_e].set(w)

    n_blk = -(-T // MOE_BLOCK)
    pad = n_blk * MOE_BLOCK - T
    tp = jnp.pad(t, ((0, pad), (0, 0))).reshape(n_blk, MOE_BLOCK, D)
    gp = jnp.pad(gates, ((0, pad), (0, 0))).reshape(n_blk, MOE_BLOCK, N_EXPERTS).astype(t.dtype)

    def expert_block(args):
        xb, gb = args
        h = jax.nn.silu(jnp.einsum('td,edf->tef', xb, w_gate)) * jnp.einsum('td,edf->tef', xb, w_up)
        return jnp.einsum('tef,efd->td', h * gb[:, :, None], w_down)

    routed = lax.map(expert_block, (tp, gp)).reshape(n_blk * MOE_BLOCK, D)[:T]
    shared = (jax.nn.silu(t @ ws_gate) * (t @ ws_up)) @ ws_down
    return (routed + shared).reshape(B, S, D)


def setup_inputs(seed: int = 0) -> dict:
    key = jax.random.key(seed)
    ks = jax.random.split(key, 26)
    L, D = DEPTH, D_MODEL
    nrm = jax.random.normal
    f32 = jnp.float32
    return {
        "x": nrm(ks[0], (BATCH, SEQ, D), f32),
        "c": nrm(ks[1], (BATCH, D), f32),
        "w_ada": nrm(ks[2], (L, D, 6 * D), f32) * (0.1 * D ** -0.5),
        "b_ada": nrm(ks[3], (L, 6 * D), f32) * 0.02,
        "w_in": nrm(ks[4], (L, D, IN_WIDTH), f32) * D ** -0.5,
        "kv_norm_g": 1.0 + 0.02 * nrm(ks[5], (L, KV_RANK), f32),
        "w_uk": nrm(ks[6], (L, A_HEADS, A_HEAD_DIM, KV_RANK), f32) * KV_RANK ** -0.5,
        "w_uv": nrm(ks[7], (L, A_HEADS, KV_RANK, A_HEAD_DIM), f32) * (DEEPNORM_BETA * KV_RANK ** -0.5),
        "rel_bias": nrm(ks[8], (NUM_BUCKETS, A_HEADS), f32) * 0.5,
        "hgrn_lb": nrm(ks[9], (L, B_FDIM), f32),
        "gnorm_g": 1.0 + 0.02 * nrm(ks[10], (L, B_VAL_DIM), f32),
        "w_out": nrm(ks[11], (L, D, D), f32) * (DEEPNORM_BETA * D ** -0.5),
        "ln1_g": 1.0 + 0.02 * nrm(ks[12], (L, D), f32),
        "ln1_b": 0.02 * nrm(ks[13], (L, D), f32),
        "w_router": nrm(ks[14], (L, D, N_EXPERTS), f32) * D ** -0.5,
        "router_bias": 0.01 * nrm(ks[15], (L, N_EXPERTS), f32),
        "w_gate": nrm(ks[16], (L, N_EXPERTS, D, EXPERT_DIM), f32) * D ** -0.5,
        "w_up": nrm(ks[17], (L, N_EXPERTS, D, EXPERT_DIM), f32) * D ** -0.5,
        "w_down": nrm(ks[18], (L, N_EXPERTS, EXPERT_DIM, D), f32) * (DEEPNORM_BETA * EXPERT_DIM ** -0.5),
        "ws_gate": nrm(ks[19], (L, D, SHARED_DIM), f32) * D ** -0.5,
        "ws_up": nrm(ks[20], (L, D, SHARED_DIM), f32) * D ** -0.5,
        "ws_down": nrm(ks[21], (L, SHARED_DIM, D), f32) * (DEEPNORM_BETA * SHARED_DIM ** -0.5),
        "ln2_g": 1.0 + 0.02 * nrm(ks[22], (L, D), f32),
        "ln2_b": 0.02 * nrm(ks[23], (L, D), f32),
    }


def reference(x, c, w_ada, b_ada, w_in, kv_norm_g, w_uk, w_uv, rel_bias, hgrn_lb, gnorm_g, w_out,
              ln1_g, ln1_b, w_router, router_bias, w_gate, w_up, w_down, ws_gate, ws_up, ws_down,
              ln2_g, ln2_b):
    B, S, D = x.shape
    lbs = jnp.cumsum(jax.nn.softmax(hgrn_lb.astype(jnp.float32), axis=0), axis=0)
    lbs = jnp.clip(lbs - lbs[0:1], 0.0, 1.0 - 1e-6)
    cond = jax.nn.silu(c)
    for l in range(DEPTH):
        mod = cond @ w_ada[l] + b_ada[l]
        sh1, sc1, g1, sh2, sc2, g2 = jnp.split(mod, 6, axis=-1)

        u = x * (1.0 + sc1[:, None]) + sh1[:, None]
        z = u @ w_in[l]
        aq, ckv, iq, ik, iw, hq, hf, hi, hg = jnp.split(z, SPLIT_POINTS, axis=-1)
        ya = _dsa_attention(
            aq.reshape(B, S, A_HEADS, A_HEAD_DIM),
            _rmsnorm(ckv, kv_norm_g[l]),
            iq.reshape(B, S, IDX_HEADS, IDX_DIM),
            ik,
            iw * IDX_W_SCALE,
            w_uk[l], w_uv[l], rel_bias)
        yb = _hgrn2(hq, hf, hi, hg, lbs[l], gnorm_g[l])
        y = jnp.concatenate([ya, yb], axis=-1) @ w_out[l]
        x = _layernorm(DEEPNORM_ALPHA * x + (1.0 + g1[:, None]) * y, ln1_g[l], ln1_b[l])

        u = x * (1.0 + sc2[:, None]) + sh2[:, None]
        y = _moe(u, w_router[l], router_bias[l], w_gate[l], w_up[l], w_down[l],
                 ws_gate[l], ws_up[l], ws_down[l])
        x = _layernorm(DEEPNORM_ALPHA * x + (1.0 + g2[:, None]) * y, ln2_g[l], ln2_b[l])
    return x
```

```python
import functools
import math

import numpy as np
import jax
import jax.numpy as jnp
from jax import lax
from jax.experimental import pallas as pl
from jax.experimental.pallas import tpu as pltpu

F32 = jnp.float32
BF16 = jnp.bfloat16
I32 = jnp.int32

D_MODEL = 1024
CHUNK = 64
A_HEADS = 8
A_HEAD_DIM = 64
A_WIDTH = A_HEADS * A_HEAD_DIM
KV_RANK = 128
IDX_HEADS = 8
IDX_DIM = 64
IDX_TOPK_MAX = 256
IDX_W_SCALE = (IDX_HEADS ** -0.5) * (IDX_DIM ** -0.5)
ATTN_SCALE = A_HEAD_DIM ** -0.5
NUM_BUCKETS = 32
MAX_DISTANCE = 128
B_HEADS = 4
B_KEY_DIM = 128
B_VAL_DIM = 128
B_WIDTH = B_HEADS * B_VAL_DIM
B_FDIM = B_HEADS * B_KEY_DIM
N_EXPERTS = 64
TOP_K = 8
N_GROUPS = 8
TOPK_GROUPS = 4
EXPERT_DIM = 256
SHARED_DIM = 256
ROUTED_SCALE = 2.5
LN_EPS = 1e-5
RMS_EPS = 1e-6

LANES = 128
SUBLANES = 8
VMEM_LIMIT_BYTES = 56 * 1024 * 1024

INT_MIN = -(2 ** 31)
NEG_INF = float("-inf")

TM_PROJ = 512
TQ = 128
KT = 128
COUNT_ROWS = 512
HG = 2
TM_MOE = 1024

_C_AQ, _C_CKV, _C_IQ, _C_IKA, _C_IKB, _C_HQ, _C_HF, _C_HG, _C_HI, _C_END = (
    0, 512, 640, 1152, 1280, 1408, 1920, 2432, 2944, 3456)


def _silu(v):
    return v * (1.0 / (1.0 + jnp.exp(-v)))


def _nt_dot(a, b):
    return lax.dot_general(a, b, (((1,), (1,)), ((), ())), preferred_element_type=F32)


def _cparams(sem):
    return pltpu.CompilerParams(dimension_semantics=sem, vmem_limit_bytes=VMEM_LIMIT_BYTES)


def _adaln_kernel(c_ref, w_ref, b_ref, o_ref):
    cond = _silu(c_ref[...])
    o_ref[0] = jnp.dot(cond.astype(BF16), w_ref[0].astype(BF16), preferred_element_type=F32) + b_ref[0]


def _adaln(c, w_ada, b_ada):
    L, D, D6 = w_ada.shape
    B = c.shape[0]
    nb = D6 // D
    return pl.pallas_call(
        _adaln_kernel,
        grid=(L, nb),
        in_specs=[
            pl.BlockSpec((B, D), lambda l, j: (0, 0)),
            pl.BlockSpec((1, D, D), lambda l, j: (l, 0, j)),
            pl.BlockSpec((1, 1, D), lambda l, j: (l, 0, j)),
        ],
        out_specs=pl.BlockSpec((1, B, D), lambda l, j: (l, 0, j)),
        out_shape=jax.ShapeDtypeStruct((L, B, D6), F32),
        compiler_params=_cparams(("arbitrary", "arbitrary")),
        name="adaln_mod",
    )(c, w_ada, b_ada.reshape(L, 1, D6))


_T5_NB = NUM_BUCKETS // 2
_T5_EXACT = _T5_NB // 2
_T5_THRESHOLDS = tuple(
    int(math.ceil(_T5_EXACT * (MAX_DISTANCE / _T5_EXACT) ** (j / (_T5_NB - _T5_EXACT)) - 1e-9))
    for j in range(1, _T5_NB - _T5_EXACT))
FAR_BUCKET = _T5_NB - 1
assert _T5_THRESHOLDS[-1] <= TQ, "keys further than one query block behind must share the far bucket"


def _bias_kernel(rb_ref, o_ref):
    kr = lax.broadcasted_iota(I32, (2 * TQ, TQ), 0)
    ql = lax.broadcasted_iota(I32, (2 * TQ, TQ), 1)
    rel = kr - TQ - ql
    n = jnp.abs(rel)
    large = jnp.full(rel.shape, _T5_EXACT, I32)
    for t in _T5_THRESHOLDS:
        large = large + (n >= t).astype(I32)
    bucket = jnp.where(rel > 0, _T5_NB, 0) + jnp.where(n < _T5_EXACT, n, large)
    for h in range(A_HEADS):
        acc = jnp.zeros(rel.shape, F32)
        for bk in range(NUM_BUCKETS):
            acc = jnp.where(bucket == bk, rb_ref[bk, h], acc)
        o_ref[h] = acc - rb_ref[FAR_BUCKET, h]


def _bias_tile(rel_bias):
    return pl.pallas_call(
        _bias_kernel,
        in_specs=[pl.BlockSpec(memory_space=pltpu.SMEM)],
        out_specs=pl.BlockSpec(memory_space=pltpu.VMEM),
        out_shape=jax.ShapeDtypeStruct((A_HEADS, 2 * TQ, TQ), F32),
        name="rel_bias_tile",
    )(rel_bias)


def _inproj_kernel(x_ref, mod_ref, wp_ref, wblk_ref, wckvT_ref, wiwT_ref, gkv_ref, gkvT_ref, llb_ref, l1m_ref,
                   qlat_ref, ckv_ref, ckvT_ref, iq_ref, ikA_ref, ikB_ref, iwT_ref,
                   hq_ref, hk_ref, hlf_ref, hv_ref, hgate_ref):
    x = x_ref[0]
    sh1 = mod_ref[0, 0:1, :]
    sc1 = mod_ref[0, 1:2, :]
    u = (x * (1.0 + sc1) + sh1).astype(BF16)
    z = jnp.dot(u, wp_ref[0], preferred_element_type=F32)

    ql = jnp.dot(z[:, _C_AQ:_C_CKV].astype(BF16), wblk_ref[0], preferred_element_type=F32)
    for h in range(A_HEADS):
        qlat_ref[0, h] = ql[:, h * KV_RANK:(h + 1) * KV_RANK].astype(BF16)

    zc = z[:, _C_CKV:_C_IQ]
    inv = lax.rsqrt(jnp.mean(zc * zc, axis=-1, keepdims=True) + RMS_EPS)
    ckv_ref[0] = (zc * inv * gkv_ref[0]).astype(BF16)
    zt = _nt_dot(wckvT_ref[0], u)
    inv_t = lax.rsqrt(jnp.mean(zt * zt, axis=0, keepdims=True) + RMS_EPS)
    ckvT_ref[0] = (zt * inv_t * gkvT_ref[0]).astype(BF16)

    for p in range(IDX_HEADS // 2):
        iq_ref[0, p] = z[:, _C_IQ + p * LANES:_C_IQ + (p + 1) * LANES].astype(BF16)
    ikA_ref[0] = z[:, _C_IKA:_C_IKB].astype(BF16)
    ikB_ref[0] = z[:, _C_IKB:_C_HQ].astype(BF16)
    iwT_ref[0] = _nt_dot(wiwT_ref[0], u) * IDX_W_SCALE

    hq_ref[0] = _silu(z[:, _C_HQ:_C_HF])
    zf = z[:, _C_HF:_C_HG]
    log_sig = jnp.minimum(zf, 0.0) - jnp.log1p(jnp.exp(-jnp.abs(zf)))
    a = llb_ref[0]
    c = l1m_ref[0] + log_sig
    logf = jnp.maximum(a, c) + jnp.log1p(jnp.exp(-jnp.abs(a - c)))
    hlf_ref[0] = logf
    hk_ref[0] = 1.0 - jnp.exp(logf)
    hgate_ref[0] = _silu(z[:, _C_HG:_C_HI])
    hv_ref[0] = z[:, _C_HI:_C_END].astype(BF16)


def _inproj(l, x, mod, wp, wblk, wckvT, wiwT, gkv, gkvT, llb, l1m):
    B, S, D = x.shape
    tm = TM_PROJ
    grid = (B, S // tm)
    lw3 = lambda b, i: (l, 0, 0)
    tok = lambda b, i: (b, i, 0)
    tokT = lambda b, i: (b, 0, i)
    hd4 = lambda b, i: (b, 0, i, 0)
    outs = [
        (jax.ShapeDtypeStruct((B, A_HEADS, S, KV_RANK), BF16), pl.BlockSpec((1, A_HEADS, tm, KV_RANK), hd4)),
        (jax.ShapeDtypeStruct((B, S, KV_RANK), BF16), pl.BlockSpec((1, tm, KV_RANK), tok)),
        (jax.ShapeDtypeStruct((B, KV_RANK, S), BF16), pl.BlockSpec((1, KV_RANK, tm), tokT)),
        (jax.ShapeDtypeStruct((B, IDX_HEADS // 2, S, LANES), BF16), pl.BlockSpec((1, IDX_HEADS // 2, tm, LANES), hd4)),
        (jax.ShapeDtypeStruct((B, S, LANES), BF16), pl.BlockSpec((1, tm, LANES), tok)),
        (jax.ShapeDtypeStruct((B, S, LANES), BF16), pl.BlockSpec((1, tm, LANES), tok)),
        (jax.ShapeDtypeStruct((B, IDX_HEADS, S), F32), pl.BlockSpec((1, IDX_HEADS, tm), tokT)),
        (jax.ShapeDtypeStruct((B, S, B_FDIM), F32), pl.BlockSpec((1, tm, B_FDIM), tok)),
        (jax.ShapeDtypeStruct((B, S, B_FDIM), F32), pl.BlockSpec((1, tm, B_FDIM), tok)),
        (jax.ShapeDtypeStruct((B, S, B_FDIM), F32), pl.BlockSpec((1, tm, B_FDIM), tok)),
        (jax.ShapeDtypeStruct((B, S, B_WIDTH), BF16), pl.BlockSpec((1, tm, B_WIDTH), tok)),
        (jax.ShapeDtypeStruct((B, S, B_WIDTH), F32), pl.BlockSpec((1, tm, B_WIDTH), tok)),
    ]
    return pl.pallas_call(
        _inproj_kernel,
        grid=grid,
        in_specs=[
            pl.BlockSpec((1, tm, D), tok),
            pl.BlockSpec((1, 6, D), lambda b, i: (b, 0, 0)),
            pl.BlockSpec((1, D, _C_END), lw3),
            pl.BlockSpec((1, A_WIDTH, A_HEADS * KV_RANK), lw3),
            pl.BlockSpec((1, KV_RANK, D), lw3),
            pl.BlockSpec((1, IDX_HEADS, D), lw3),
            pl.BlockSpec((1, 1, KV_RANK), lw3),
            pl.BlockSpec((1, KV_RANK, tm), lw3),
            pl.BlockSpec((1, 1, B_FDIM), lw3),
            pl.BlockSpec((1, 1, B_FDIM), lw3),
        ],
        out_specs=[o[1] for o in outs],
        out_shape=[o[0] for o in outs],
        compiler_params=_cparams(("arbitrary", "arbitrary")),
        name="inproj",
    )(x, mod, wp, wblk, wckvT, wiwT, gkv, gkvT, llb, l1m)


def _dsa_kernel(iq_ref, iwT_ref, qlat_ref, ikA_ref, ikB_ref, ckv_ref, ckvT_ref, bn_ref, wuvT_ref, out_ref,
                sc_ref, madd_ref, l_ref, m_ref, yaT_ref, *, k_sel, n_idx_bits):
    j = pl.program_id(1)
    q0 = j * TQ
    nkt = j + 1
    lane = lax.broadcasted_iota(I32, (1, TQ), 1)
    limit = (((q0 + lane) >> 6) + 1) << 6
    row_iota = lax.broadcasted_iota(I32, (KT, TQ), 0)

    iqs = iq_ref[0].reshape(IDX_HEADS // 2 * TQ, LANES)
    iw = iwT_ref[0]

    def score_tile(kt, masked):
        r0 = pl.multiple_of(kt * KT, KT)
        xe = _nt_dot(ikA_ref[0, pl.ds(r0, KT), :], iqs)
        xo = _nt_dot(ikB_ref[0, pl.ds(r0, KT), :], iqs)
        acc = jnp.zeros((KT, TQ), F32)
        for p in range(IDX_HEADS // 2):
            acc = acc + iw[2 * p:2 * p + 1, :] * jnp.maximum(xe[:, p * TQ:(p + 1) * TQ], 0.0)
            acc = acc + iw[2 * p + 1:2 * p + 2, :] * jnp.maximum(xo[:, p * TQ:(p + 1) * TQ], 0.0)
        bits = lax.bitcast_convert_type(acc, I32)
        key = bits ^ ((bits >> 31) & 0x7FFFFFFF)
        if masked:
            key = jnp.where(row_iota + r0 < limit, key, INT_MIN)
        sc_ref[pl.ds(r0, KT), :] = key

    def score_body(kt, carry):
        score_tile(kt, False)
        return carry

    lax.fori_loop(0, nkt - 1, score_body, 0)
    score_tile(nkt - 1, True)

    tiles_per_count = COUNT_ROWS // KT
    ncount = (nkt + tiles_per_count - 1) // tiles_per_count

    def fill_body(kt, carry):
        sc_ref[pl.ds(pl.multiple_of(kt * KT, KT), KT), :] = jnp.full((KT, TQ), INT_MIN, I32)
        return carry

    lax.fori_loop(nkt, ncount * tiles_per_count, fill_body, 0)

    def count_where(pred):
        def body(u, acc):
            r0 = pl.multiple_of(u * COUNT_ROWS, COUNT_ROWS)
            hit = pred(sc_ref[pl.ds(r0, COUNT_ROWS), :], r0).astype(I32)
            return acc + jnp.sum(hit.reshape(COUNT_ROWS // SUBLANES, SUBLANES, TQ), axis=0)
        acc = lax.fori_loop(0, ncount, body, jnp.zeros((SUBLANES, TQ), I32))
        return jnp.sum(acc, axis=0, keepdims=True)

    def search_body(i, carry):
        u_thr, c_ge = carry
        cand_u = u_thr | (jnp.int32(1) << (31 - i))
        cand = cand_u ^ INT_MIN
        cnt = count_where(lambda blk, r0: blk >= cand)
        ok = cnt >= k_sel
        return jnp.where(ok, cand_u, u_thr), jnp.where(ok, cnt, c_ge)

    u_thr, c_ge = lax.fori_loop(0, 32, search_body, (jnp.zeros((1, TQ), I32), jnp.zeros((1, TQ), I32)))
    thr = jnp.maximum(u_thr ^ INT_MIN, INT_MIN + 1)
    straddle = (c_ge > k_sel).astype(I32)

    def tie_bound():
        c_gt = count_where(lambda blk, r0: blk > thr)
        need = k_sel - c_gt
        idx_iota = lax.broadcasted_iota(I32, (COUNT_ROWS, TQ), 0)

        def tie_body(i, j0):
            cand = j0 | (jnp.int32(1) << (n_idx_bits - 1 - i))
            cnt = count_where(lambda blk, r0: jnp.where(blk == thr, idx_iota + r0, cand) < cand)
            return jnp.where(cnt < need, cand, j0)

        j0 = lax.fori_loop(0, n_idx_bits, tie_body, jnp.zeros((1, TQ), I32))
        return jnp.where(straddle > 0, j0 + 1, jnp.int32(2 ** n_idx_bits))

    jstar = lax.cond(jnp.max(straddle) > 0, tie_bound, lambda: jnp.full((1, TQ), 2 ** n_idx_bits, I32))

    def madd_body(kt, carry):
        r0 = pl.multiple_of(kt * KT, KT)
        key = sc_ref[pl.ds(r0, KT), :]
        tie_keep = jnp.where(row_iota + r0 < jstar, 0.0, NEG_INF)
        madd_ref[pl.ds(r0, KT), :] = jnp.where(key > thr, 0.0, jnp.where(key == thr, tie_keep, NEG_INF))
        return carry

    lax.fori_loop(0, nkt, madd_body, 0)

    for g in range(A_HEADS // HG):
        qg = qlat_ref[0, g * HG:(g + 1) * HG].reshape(HG * TQ, KV_RANK)
        m_ref[...] = jnp.full((SUBLANES, HG * TQ), NEG_INF, F32)

        def logit_tile(kt, bias_row):
            r0 = pl.multiple_of(kt * KT, KT)
            xl = _nt_dot(ckv_ref[0, pl.ds(r0, KT), :], qg)
            md = madd_ref[pl.ds(r0, KT), :]
            xl = xl + jnp.concatenate([md] * HG, axis=1)
            if bias_row is not None:
                xl = xl + jnp.concatenate(
                    [bn_ref[g * HG + hh, bias_row:bias_row + KT, :] for hh in range(HG)], axis=1)
            l_ref[pl.ds(r0, KT), :] = xl
            m_ref[...] = jnp.maximum(m_ref[...], jnp.max(xl.reshape(KT // SUBLANES, SUBLANES, HG * TQ), axis=0))

        def far_body(kt, carry):
            logit_tile(kt, None)
            return carry

        lax.fori_loop(0, nkt - 2, far_body, 0)

        @pl.when(nkt >= 2)
        def _():
            logit_tile(nkt - 2, 0)

        logit_tile(nkt - 1, KT)
        mx = jnp.max(m_ref[...], axis=0, keepdims=True)

        def pv_body(kt, carry):
            ssum, o_t = carry
            r0 = pl.multiple_of(kt * KT, KT)
            p = jnp.exp(l_ref[pl.ds(r0, KT), :] - mx)
            ssum = ssum + jnp.sum(p.reshape(KT // SUBLANES, SUBLANES, HG * TQ), axis=0)
            o_t = o_t + jnp.dot(ckvT_ref[0, :, pl.ds(r0, KT)], p.astype(BF16), preferred_element_type=F32)
            return ssum, o_t

        ssum, o_t = lax.fori_loop(
            0, nkt, pv_body, (jnp.zeros((SUBLANES, HG * TQ), F32), jnp.zeros((KV_RANK, HG * TQ), F32)))
        o_t = (o_t * (1.0 / jnp.sum(ssum, axis=0, keepdims=True))).astype(BF16)
        for hh in range(HG):
            h = g * HG + hh
            yaT_ref[h * A_HEAD_DIM:(h + 1) * A_HEAD_DIM, :] = jnp.dot(
                wuvT_ref[0, h], o_t[:, hh * TQ:(hh + 1) * TQ], preferred_element_type=F32)

    out_ref[0] = yaT_ref[...].T.astype(BF16)


def _dsa(l, iq, iwT, qlat, ikA, ikB, ckv, ckvT, bn, wuvT):
    B, S = ckv.shape[0], ckv.shape[1]
    assert S % COUNT_ROWS == 0 and S % TQ == 0 and TQ % CHUNK == 0 and KT == TQ and CHUNK == 64
    k_sel = min(IDX_TOPK_MAX, S // 4)
    n_idx_bits = int(math.log2(S))
    assert 2 ** n_idx_bits == S
    grid = (B, S // TQ)
    blk = lambda b, i: (b, 0, i, 0)
    full = lambda b, i: (b, 0, 0)
    kern = functools.partial(_dsa_kernel, k_sel=k_sel, n_idx_bits=n_idx_bits)
    return pl.pallas_call(
        kern,
        grid=grid,
        in_specs=[
            pl.BlockSpec((1, IDX_HEADS // 2, TQ, LANES), blk),
            pl.BlockSpec((1, IDX_HEADS, TQ), lambda b, i: (b, 0, i)),
            pl.BlockSpec((1, A_HEADS, TQ, KV_RANK), blk),
            pl.BlockSpec((1, S, LANES), full),
            pl.BlockSpec((1, S, LANES), full),
            pl.BlockSpec((1, S, KV_RANK), full),
            pl.BlockSpec((1, KV_RANK, S), full),
            pl.BlockSpec((A_HEADS, 2 * TQ, TQ), lambda b, i: (0, 0, 0)),
            pl.BlockSpec((1, A_HEADS, A_HEAD_DIM, KV_RANK), lambda b, i: (l, 0, 0, 0)),
        ],
        out_specs=pl.BlockSpec((1, TQ, A_WIDTH), lambda b, i: (b, i, 0)),
        out_shape=jax.ShapeDtypeStruct((B, S, A_WIDTH), BF16),
        scratch_shapes=[
            pltpu.VMEM((S, TQ), I32),
            pltpu.VMEM((S, TQ), F32),
            pltpu.VMEM((S, HG * TQ), F32),
            pltpu.VMEM((SUBLANES, HG * TQ), F32),
            pltpu.VMEM((A_WIDTH, TQ), F32),
        ],
        compiler_params=_cparams(("arbitrary", "arbitrary")),
        name="dsa_attention",
    )(iq, iwT, qlat, ikA, ikB, ckv, ckvT, bn, wuvT)


def _hgrn_constants():
    c = CHUNK
    r = np.arange(c)[:, None]
    jj = np.arange(c)[None, :]
    mats = [(jj <= r), (jj > r)]
    masks = [np.eye(c, dtype=bool)]
    m = c // 2
    while m >= 1:
        start = (r // (2 * m)) * (2 * m)
        bd = start + m - 1
        upper = r > bd
        mats.append(np.where(upper, (jj > bd) & (jj <= r), (jj > r) & (jj <= bd)))
        same_parent = (r // (2 * m)) == (jj // (2 * m))
        masks.append(same_parent & upper & (jj <= (jj // (2 * m)) * (2 * m) + m - 1))
        m //= 2
    m_all = np.concatenate(mats, axis=0).astype(np.float32)
    total = np.zeros((c, c), np.int32)
    for mk in masks:
        total += mk
    assert (total == np.tril(np.ones((c, c), np.int32))).all()
    return np.concatenate([m_all] * 3, axis=1), np.stack(masks).astype(np.float32)


_HGRN_M3, _HGRN_MASKS = _hgrn_constants()
_HGRN_LEVELS = _HGRN_MASKS.shape[0] - 1
HGRN_STEP_CHUNKS = 2


def _hgrn_kernel(q_ref, k_ref, lf_ref, v_ref, gate_ref, m3_ref, mask_ref, gn_ref, out_ref, st_ref):
    @pl.when(pl.program_id(1) == 0)
    def _():
        st_ref[...] = jnp.zeros(st_ref.shape, F32)

    c = CHUNK
    for ci in range(HGRN_STEP_CHUNKS):
        rows = slice(ci * c, (ci + 1) * c)
        g = lf_ref[0, rows, :]
        g_hi = g.astype(BF16)
        r1 = g - g_hi.astype(F32)
        g_mid = r1.astype(BF16)
        g_lo = (r1 - g_mid.astype(F32)).astype(BF16)
        sums = jnp.dot(m3_ref[...], jnp.concatenate([g_hi, g_mid, g_lo], axis=0), preferred_element_type=F32)
        e_all = jnp.exp(sums)
        for h in range(B_HEADS):
            cols = slice(h * B_KEY_DIM, (h + 1) * B_KEY_DIM)
            qh = q_ref[0, rows, cols]
            kh = k_ref[0, rows, cols]
            vh = v_ref[0, rows, cols]
            att = mask_ref[0] * _nt_dot(qh.astype(BF16), kh.astype(BF16))
            for lv in range(_HGRN_LEVELS):
                e_l = e_all[(2 + lv) * c:(3 + lv) * c, cols]
                att = att + mask_ref[lv + 1] * _nt_dot((qh * e_l).astype(BF16), (kh * e_l).astype(BF16))
            e_b = e_all[0:c, cols]
            e_rem = e_all[c:2 * c, cols]
            st = st_ref[h]
            o = jnp.dot(att.astype(BF16), vh, preferred_element_type=F32)
            o = o + _nt_dot((qh * e_b).astype(BF16), st.astype(BF16))
            upd = lax.dot_general(vh, (kh * e_rem).astype(BF16), (((0,), (0,)), ((), ())),
                                  preferred_element_type=F32)
            st_ref[h] = st * e_b[c - 1:c, :] + upd
            o = o * lax.rsqrt(jnp.mean(o * o, axis=-1, keepdims=True) + RMS_EPS) * gn_ref[0]
            out_ref[0, rows, cols] = (o * gate_ref[0, rows, cols]).astype(BF16)


def _hgrn(l, hq, hk, hlf, hv, hgate, gnorm):
    B, S, W = hq.shape
    ts = CHUNK * HGRN_STEP_CHUNKS
    tok = lambda b, i: (b, i, 0)
    return pl.pallas_call(
        _hgrn_kernel,
        grid=(B, S // ts),
        in_specs=[
            pl.BlockSpec((1, ts, W), tok),
            pl.BlockSpec((1, ts, W), tok),
            pl.BlockSpec((1, ts, W), tok),
            pl.BlockSpec((1, ts, W), tok),
            pl.BlockSpec((1, ts, W), tok),
            pl.BlockSpec(_HGRN_M3.shape, lambda b, i: (0, 0)),
            pl.BlockSpec(_HGRN_MASKS.shape, lambda b, i: (0, 0, 0)),
            pl.BlockSpec((1, 1, B_VAL_DIM), lambda b, i: (l, 0, 0)),
        ],
        out_specs=pl.BlockSpec((1, ts, W), tok),
        out_shape=jax.ShapeDtypeStruct((B, S, W), BF16),
        scratch_shapes=[pltpu.VMEM((B_HEADS, B_VAL_DIM, B_KEY_DIM), F32)],
        compiler_params=_cparams(("arbitrary", "arbitrary")),
        name="hgrn2",
    )(hq, hk, hlf, hv, hgate, jnp.asarray(_HGRN_M3, BF16), jnp.asarray(_HGRN_MASKS), gnorm)


def _layernorm(v, g, b):
    mu = jnp.mean(v, axis=-1, keepdims=True)
    d = v - mu
    var = jnp.mean(d * d, axis=-1, keepdims=True)
    return d * lax.rsqrt(var + LN_EPS) * g + b


def _first_argmax(v, idx, axes, big):
    mx = v
    for ax in axes:
        mx = jnp.max(mx, axis=ax, keepdims=True)
    pos = jnp.where(v == mx, idx, big)
    for ax in axes:
        pos = jnp.min(pos, axis=ax, keepdims=True)
    return mx, pos


def _outproj_kernel(ya_ref, yb_ref, x_ref, mod_ref, wo_ref, lng_ref, lnb_ref, wrT_ref, rbias_ref,
                    x1_ref, u2_ref, gates_ref, *, alpha):
    y = jnp.dot(ya_ref[0], wo_ref[0, 0:A_WIDTH, :], preferred_element_type=F32)
    y = y + jnp.dot(yb_ref[0], wo_ref[0, A_WIDTH:, :], preferred_element_type=F32)
    g1 = mod_ref[0, 2:3, :]
    x1 = _layernorm(alpha * x_ref[0] + (1.0 + g1) * y, lng_ref[0], lnb_ref[0])
    x1_ref[0] = x1
    u2 = (x1 * (1.0 + mod_ref[0, 4:5, :]) + mod_ref[0, 3:4, :]).astype(BF16)
    u2_ref[0] = u2

    tm = u2.shape[0]
    gsz = N_EXPERTS // N_GROUPS
    scores = 1.0 / (1.0 + jnp.exp(-_nt_dot(wrT_ref[0], u2)))
    sel = (scores + rbias_ref[0]).reshape(N_GROUPS, gsz, tm)
    scores = scores.reshape(N_GROUPS, gsz, tm)
    i_m = lax.broadcasted_iota(I32, (N_GROUPS, gsz, tm), 1)
    i_g = lax.broadcasted_iota(I32, (N_GROUPS, 1, tm), 0)
    i_e = lax.broadcasted_iota(I32, (N_GROUPS, gsz, tm), 0) * gsz + i_m
    m1, p1 = _first_argmax(sel, i_m, (1,), gsz)
    m2 = jnp.max(jnp.where(i_m == p1, NEG_INF, sel), axis=1, keepdims=True)
    gs = m1 + m2
    gmask = jnp.zeros(gs.shape, F32)
    for _ in range(TOPK_GROUPS):
        _, pg = _first_argmax(gs, i_g, (0,), N_GROUPS)
        hit = i_g == pg
        gmask = jnp.where(hit, 1.0, gmask)
        gs = jnp.where(hit, NEG_INF, gs)
    cand = jnp.where(jnp.broadcast_to(gmask, sel.shape) > 0.0, sel, NEG_INF)
    w = jnp.zeros(sel.shape, F32)
    for _ in range(TOP_K):
        _, pe = _first_argmax(cand, i_e, (1, 0), N_EXPERTS)
        hit = i_e == pe
        w = jnp.where(hit, scores, w)
        cand = jnp.where(hit, NEG_INF, cand)
    wsum = jnp.sum(jnp.sum(w, axis=1, keepdims=True), axis=0, keepdims=True)
    gates = (w / wsum * ROUTED_SCALE).reshape(N_EXPERTS, tm)
    g_hi = gates.astype(BF16).astype(F32)
    g_lo = (gates - g_hi).astype(BF16).astype(F32)
    gates_ref[0] = jnp.concatenate([g_hi, g_lo], axis=0).T.astype(BF16)


def _outproj(l, ya, yb, x, mod, wo, ln_g, ln_b, wrT, rbias, alpha):
    B, S, D = x.shape
    tm = TM_PROJ
    tok = lambda b, i: (b, i, 0)
    lw3 = lambda b, i: (l, 0, 0)
    return pl.pallas_call(
        functools.partial(_outproj_kernel, alpha=alpha),
        grid=(B, S // tm),
        in_specs=[
            pl.BlockSpec((1, tm, A_WIDTH), tok),
            pl.BlockSpec((1, tm, B_WIDTH), tok),
            pl.BlockSpec((1, tm, D), tok),
            pl.BlockSpec((1, 6, D), lambda b, i: (b, 0, 0)),
            pl.BlockSpec((1, D, D), lw3),
            pl.BlockSpec((1, 1, D), lw3),
            pl.BlockSpec((1, 1, D), lw3),
            pl.BlockSpec((1, N_EXPERTS, D), lw3),
            pl.BlockSpec((1, N_EXPERTS, tm), lw3),
        ],
        out_specs=[pl.BlockSpec((1, tm, D), tok), pl.BlockSpec((1, tm, D), tok),
                   pl.BlockSpec((1, tm, 2 * N_EXPERTS), tok)],
        out_shape=[jax.ShapeDtypeStruct((B, S, D), F32), jax.ShapeDtypeStruct((B, S, D), BF16),
                   jax.ShapeDtypeStruct((B, S, 2 * N_EXPERTS), BF16)],
        compiler_params=_cparams(("arbitrary", "arbitrary")),
        name="outproj_router",
    )(ya, yb, x, mod, wo, ln_g, ln_b, wrT, rbias)


def _moe_kernel(u_ref, gates_ref, x1_ref, mod_ref, wgu_ref, wd_ref, sgu_ref, sd_ref, lng_ref, lnb_ref,
                out_ref, acc_ref, *, alpha):
    e = pl.program_id(1)
    u = u_ref[...]

    def ffn(wgu, wd, gate):
        hgu = jnp.dot(u, wgu, preferred_element_type=F32)
        h = _silu(hgu[:, :EXPERT_DIM]) * hgu[:, EXPERT_DIM:]
        if gate is not None:
            h = h * gate
        return jnp.dot(h.astype(BF16), wd, preferred_element_type=F32)

    @pl.when(e == 0)
    def _():
        acc_ref[...] = ffn(sgu_ref[0], sd_ref[0], None)

    rows = lax.broadcasted_iota(I32, (2 * N_EXPERTS, EXPERT_DIM), 0)
    onehot = jnp.where((rows & (N_EXPERTS - 1)) == e, 1.0, 0.0).astype(BF16)
    gate = jnp.dot(gates_ref[...], onehot, preferred_element_type=F32)
    acc_ref[...] += ffn(wgu_ref[0, 0], wd_ref[0, 0], gate)

    @pl.when(e == pl.num_programs(1) - 1)
    def _():
        g2 = mod_ref[0, 5:6, :]
        out_ref[...] = _layernorm(alpha * x1_ref[...] + (1.0 + g2) * acc_ref[...], lng_ref[0], lnb_ref[0])


def _moe(l, u2, gates, x1, mod, wgu, wd, sgu, sd, ln_g, ln_b, alpha, seq):
    T, D = u2.shape
    tm = TM_MOE
    assert seq % tm == 0
    tok = lambda i, e: (i, 0)
    lw3 = lambda i, e: (l, 0, 0)
    return pl.pallas_call(
        functools.partial(_moe_kernel, alpha=alpha),
        grid=(T // tm, N_EXPERTS),
        in_specs=[
            pl.BlockSpec((tm, D), tok),
            pl.BlockSpec((tm, 2 * N_EXPERTS), tok),
            pl.BlockSpec((tm, D), tok),
            pl.BlockSpec((1, 6, D), lambda i, e: ((i * tm) // seq, 0, 0)),
            pl.BlockSpec((1, 1, D, 2 * EXPERT_DIM), lambda i, e: (l, e, 0, 0)),
            pl.BlockSpec((1, 1, EXPERT_DIM, D), lambda i, e: (l, e, 0, 0)),
            pl.BlockSpec((1, D, 2 * SHARED_DIM), lw3),
            pl.BlockSpec((1, SHARED_DIM, D), lw3),
            pl.BlockSpec((1, 1, D), lw3),
            pl.BlockSpec((1, 1, D), lw3),
        ],
        out_specs=pl.BlockSpec((tm, D), tok),
        out_shape=jax.ShapeDtypeStruct((T, D), F32),
        scratch_shapes=[pltpu.VMEM((tm, D), F32)],
        compiler_params=_cparams(("arbitrary", "arbitrary")),
        name="moe_dense",
    )(u2, gates, x1, mod, wgu, wd, sgu, sd, ln_g, ln_b)


def _prepare_params(w_in, kv_norm_g, w_uk, w_uv, hgrn_lb, w_out, w_router, router_bias,
                    w_gate, w_up, w_down, ws_gate, ws_up, ws_down):
    L = w_in.shape[0]
    sizes = (A_WIDTH, KV_RANK, IDX_HEADS * IDX_DIM, IDX_DIM, IDX_HEADS, B_FDIM, B_FDIM, B_WIDTH, B_WIDTH)
    offs = np.concatenate([[0], np.cumsum(sizes)])
    seg = lambda i: w_in[:, :, offs[i]:offs[i + 1]]
    w_aq, w_ckv, w_iq, w_ik, w_iw, w_hq, w_hf, w_hi, w_hg = (seg(i) for i in range(9))
    zik = jnp.zeros_like(w_ik)
    wp = jnp.concatenate([w_aq, w_ckv, w_iq, w_ik, zik, zik, w_ik, w_hq, w_hf, w_hg, w_hi], axis=-1).astype(BF16)
    assert wp.shape[-1] == _C_END
    eye = jnp.eye(A_HEADS, dtype=F32)
    wblk = (jnp.einsum('lhdr,hg->lhdgr', w_uk * ATTN_SCALE, eye)
            .reshape(L, A_WIDTH, A_HEADS * KV_RANK).astype(BF16))
    p = dict(
        wp=wp, wblk=wblk,
        wckvT=jnp.swapaxes(w_ckv, 1, 2).astype(BF16),
        wiwT=jnp.swapaxes(w_iw, 1, 2).astype(BF16),
        gkv=kv_norm_g.reshape(L, 1, KV_RANK),
        gkvT=jnp.broadcast_to(kv_norm_g[:, :, None], (L, KV_RANK, TM_PROJ)),
        wuvT=jnp.swapaxes(w_uv, 2, 3).astype(BF16),
        wo=w_out.astype(BF16),
        wrT=jnp.swapaxes(w_router, 1, 2).astype(BF16),
        rbias=jnp.broadcast_to(router_bias[:, :, None], (L, N_EXPERTS, TM_PROJ)),
        wgu=jnp.concatenate([w_gate, w_up], axis=-1).astype(BF16),
        wd=w_down.astype(BF16),
        sgu=jnp.concatenate([ws_gate, ws_up], axis=-1).astype(BF16),
        sd=ws_down.astype(BF16),
    )
    lbs = jnp.cumsum(jax.nn.softmax(hgrn_lb.astype(F32), axis=0), axis=0)
    lbs = jnp.clip(lbs - lbs[0:1], 0.0, 1.0 - 1e-6)
    p["llb"] = jnp.log(lbs).reshape(L, 1, B_FDIM)
    p["l1m"] = jnp.log1p(-lbs).reshape(L, 1, B_FDIM)
    return p


def kernel(x, c, w_ada, b_ada, w_in, kv_norm_g, w_uk, w_uv, rel_bias, hgrn_lb, gnorm_g, w_out, ln1_g, ln1_b,
           w_router, router_bias, w_gate, w_up, w_down, ws_gate, ws_up, ws_down, ln2_g, ln2_b):
    B, S, D = x.shape
    L = w_in.shape[0]
    alpha = (2 * L) ** 0.25
    p = _prepare_params(w_in, kv_norm_g, w_uk, w_uv, hgrn_lb, w_out, w_router, router_bias,
                        w_gate, w_up, w_down, ws_gate, ws_up, ws_down)
    mods = _adaln(c, w_ada, b_ada).reshape(L, B, 6, D)
    bn = _bias_tile(rel_bias)
    gn = gnorm_g.reshape(L, 1, B_VAL_DIM)
    ln1g, ln1b = ln1_g.reshape(L, 1, D), ln1_b.reshape(L, 1, D)
    ln2g, ln2b = ln2_g.reshape(L, 1, D), ln2_b.reshape(L, 1, D)
    for l in range(L):
        mod = mods[l]
        (qlat, ckv, ckvT, iq, ikA, ikB, iwT, hq, hk, hlf, hv, hgate) = _inproj(
            l, x, mod, p["wp"], p["wblk"], p["wckvT"], p["wiwT"], p["gkv"], p["gkvT"], p["llb"], p["l1m"])
        ya = _dsa(l, iq, iwT, qlat, ikA, ikB, ckv, ckvT, bn, p["wuvT"])
        yb = _hgrn(l, hq, hk, hlf, hv, hgate, gn)
        x1, u2, gates = _outproj(l, ya, yb, x, mod, p["wo"], ln1g, ln1b, p["wrT"], p["rbias"], alpha)
        x = _moe(l, u2.reshape(B * S, D), gates.reshape(B * S, 2 * N_EXPERTS), x1.reshape(B * S, D), mod,
                 p["wgu"], p["wd"], p["sgu"], p["sd"], ln2g, ln2b, alpha, S).reshape(B, S, D)
    return x
```

```python
import functools
import math

import numpy as np
import jax
import jax.numpy as jnp
from jax import lax
from jax.experimental import pallas as pl
from jax.experimental.pallas import tpu as pltpu

F32 = jnp.float32
BF16 = jnp.bfloat16
I32 = jnp.int32

D_MODEL = 1024
CHUNK = 64
A_HEADS = 8
A_HEAD_DIM = 64
A_WIDTH = A_HEADS * A_HEAD_DIM
KV_RANK = 128
IDX_HEADS = 8
IDX_DIM = 64
IDX_TOPK_MAX = 256
IDX_W_SCALE = (IDX_HEADS ** -0.5) * (IDX_DIM ** -0.5)
ATTN_SCALE = A_HEAD_DIM ** -0.5
NUM_BUCKETS = 32
MAX_DISTANCE = 128
B_HEADS = 4
B_KEY_DIM = 128
B_VAL_DIM = 128
B_WIDTH = B_HEADS * B_VAL_DIM
B_FDIM = B_HEADS * B_KEY_DIM
N_EXPERTS = 64
TOP_K = 8
N_GROUPS = 8
TOPK_GROUPS = 4
EXPERT_DIM = 256
SHARED_DIM = 256
ROUTED_SCALE = 2.5
LN_EPS = 1e-5
RMS_EPS = 1e-6

LANES = 128
SUBLANES = 8
VMEM_LIMIT_BYTES = 56 * 1024 * 1024

INT_MIN = -(2 ** 31)
NEG_INF = float("-inf")

TM_PROJ = 512
TQ = 128
UNIT = 512
NEAR = 2 * TQ
HG = 2
TM_MOE = 1024

_C_AQ, _C_CKV, _C_IQ, _C_IKA, _C_IKB, _C_HQ, _C_HF, _C_HG, _C_HI, _C_END = (
    0, 512, 640, 1152, 1280, 1408, 1920, 2432, 2944, 3456)


def _silu(v):
    return v * (1.0 / (1.0 + jnp.exp(-v)))


def _nt_dot(a, b):
    return lax.dot_general(a, b, (((1,), (1,)), ((), ())), preferred_element_type=F32)


def _cparams(sem):
    return pltpu.CompilerParams(dimension_semantics=sem, vmem_limit_bytes=VMEM_LIMIT_BYTES)


def _adaln_kernel(c_ref, w_ref, b_ref, o_ref):
    cond = _silu(c_ref[...])
    o_ref[0] = jnp.dot(cond.astype(BF16), w_ref[0].astype(BF16), preferred_element_type=F32) + b_ref[0]


def _adaln(c, w_ada, b_ada):
    L, D, D6 = w_ada.shape
    B = c.shape[0]
    nb = D6 // D
    return pl.pallas_call(
        _adaln_kernel,
        grid=(L, nb),
        in_specs=[
            pl.BlockSpec((B, D), lambda l, j: (0, 0)),
            pl.BlockSpec((1, D, D), lambda l, j: (l, 0, j)),
            pl.BlockSpec((1, 1, D), lambda l, j: (l, 0, j)),
        ],
        out_specs=pl.BlockSpec((1, B, D), lambda l, j: (l, 0, j)),
        out_shape=jax.ShapeDtypeStruct((L, B, D6), F32),
        compiler_params=_cparams(("arbitrary", "arbitrary")),
        name="adaln_mod",
    )(c, w_ada, b_ada.reshape(L, 1, D6))


_T5_NB = NUM_BUCKETS // 2
_T5_EXACT = _T5_NB // 2
_T5_THRESHOLDS = tuple(
    int(math.ceil(_T5_EXACT * (MAX_DISTANCE / _T5_EXACT) ** (j / (_T5_NB - _T5_EXACT)) - 1e-9))
    for j in range(1, _T5_NB - _T5_EXACT))
FAR_BUCKET = _T5_NB - 1
assert _T5_THRESHOLDS[-1] <= TQ, "keys further than one query block behind must share the far bucket"


def _bias_kernel(rb_ref, o_ref):
    kr = lax.broadcasted_iota(I32, (NEAR + TQ, TQ), 0)
    ql = lax.broadcasted_iota(I32, (NEAR + TQ, TQ), 1)
    rel = kr - TQ - ql
    n = jnp.abs(rel)
    large = jnp.full(rel.shape, _T5_EXACT, I32)
    for t in _T5_THRESHOLDS:
        large = large + (n >= t).astype(I32)
    bucket = jnp.where(rel > 0, _T5_NB, 0) + jnp.where(n < _T5_EXACT, n, large)
    for h in range(A_HEADS):
        acc = jnp.zeros(rel.shape, F32)
        for bk in range(NUM_BUCKETS):
            acc = jnp.where(bucket == bk, rb_ref[bk, h], acc)
        o_ref[h] = acc - rb_ref[FAR_BUCKET, h]


def _bias_tile(rel_bias):
    return pl.pallas_call(
        _bias_kernel,
        in_specs=[pl.BlockSpec(memory_space=pltpu.SMEM)],
        out_specs=pl.BlockSpec(memory_space=pltpu.VMEM),
        out_shape=jax.ShapeDtypeStruct((A_HEADS, NEAR + TQ, TQ), F32),
        name="rel_bias_tile",
    )(rel_bias)


def _inproj_kernel(x_ref, mod_ref, wp_ref, wblk_ref, wckvT_ref, wiwT_ref, gkv_ref, gkvT_ref, llb_ref, l1m_ref,
                   qlat_ref, ckv_ref, ckvT_ref, iq_ref, ikA_ref, ikB_ref, iwT_ref,
                   hq_ref, hk_ref, hlf_ref, hv_ref, hgate_ref):
    x = x_ref[0]
    sh1 = mod_ref[0, 0:1, :]
    sc1 = mod_ref[0, 1:2, :]
    u = (x * (1.0 + sc1) + sh1).astype(BF16)
    z = jnp.dot(u, wp_ref[0], preferred_element_type=F32)

    ql = jnp.dot(z[:, _C_AQ:_C_CKV].astype(BF16), wblk_ref[0], preferred_element_type=F32)
    for h in range(A_HEADS):
        qlat_ref[0, h] = ql[:, h * KV_RANK:(h + 1) * KV_RANK].astype(BF16)

    zc = z[:, _C_CKV:_C_IQ]
    inv = lax.rsqrt(jnp.mean(zc * zc, axis=-1, keepdims=True) + RMS_EPS)
    ckv_ref[0] = (zc * inv * gkv_ref[0]).astype(BF16)
    zt = _nt_dot(wckvT_ref[0], u)
    inv_t = lax.rsqrt(jnp.mean(zt * zt, axis=0, keepdims=True) + RMS_EPS)
    ckvT_ref[0] = (zt * inv_t * gkvT_ref[0]).astype(BF16)

    for p in range(IDX_HEADS // 2):
        iq_ref[0, p] = z[:, _C_IQ + p * LANES:_C_IQ + (p + 1) * LANES].astype(BF16)
    ikA_ref[0] = z[:, _C_IKA:_C_IKB].astype(BF16)
    ikB_ref[0] = z[:, _C_IKB:_C_HQ].astype(BF16)
    iwT_ref[0] = _nt_dot(wiwT_ref[0], u) * IDX_W_SCALE

    hq_ref[0] = _silu(z[:, _C_HQ:_C_HF])
    zf = z[:, _C_HF:_C_HG]
    log_sig = jnp.minimum(zf, 0.0) - jnp.log1p(jnp.exp(-jnp.abs(zf)))
    a = llb_ref[0]
    c = l1m_ref[0] + log_sig
    logf = jnp.maximum(a, c) + jnp.log1p(jnp.exp(-jnp.abs(a - c)))
    hlf_ref[0] = logf
    hk_ref[0] = 1.0 - jnp.exp(logf)
    hgate_ref[0] = _silu(z[:, _C_HG:_C_HI])
    hv_ref[0] = z[:, _C_HI:_C_END].astype(BF16)


def _inproj(l, x, mod, wp, wblk, wckvT, wiwT, gkv, gkvT, llb, l1m):
    B, S, D = x.shape
    tm = TM_PROJ
    grid = (B, S // tm)
    lw3 = lambda b, i: (l, 0, 0)
    tok = lambda b, i: (b, i, 0)
    tokT = lambda b, i: (b, 0, i)
    hd4 = lambda b, i: (b, 0, i, 0)
    outs = [
        (jax.ShapeDtypeStruct((B, A_HEADS, S, KV_RANK), BF16), pl.BlockSpec((1, A_HEADS, tm, KV_RANK), hd4)),
        (jax.ShapeDtypeStruct((B, S, KV_RANK), BF16), pl.BlockSpec((1, tm, KV_RANK), tok)),
        (jax.ShapeDtypeStruct((B, KV_RANK, S), BF16), pl.BlockSpec((1, KV_RANK, tm), tokT)),
        (jax.ShapeDtypeStruct((B, IDX_HEADS // 2, S, LANES), BF16), pl.BlockSpec((1, IDX_HEADS // 2, tm, LANES), hd4)),
        (jax.ShapeDtypeStruct((B, S, LANES), BF16), pl.BlockSpec((1, tm, LANES), tok)),
        (jax.ShapeDtypeStruct((B, S, LANES), BF16), pl.BlockSpec((1, tm, LANES), tok)),
        (jax.ShapeDtypeStruct((B, IDX_HEADS, S), F32), pl.BlockSpec((1, IDX_HEADS, tm), tokT)),
        (jax.ShapeDtypeStruct((B, S, B_FDIM), F32), pl.BlockSpec((1, tm, B_FDIM), tok)),
        (jax.ShapeDtypeStruct((B, S, B_FDIM), F32), pl.BlockSpec((1, tm, B_FDIM), tok)),
        (jax.ShapeDtypeStruct((B, S, B_FDIM), F32), pl.BlockSpec((1, tm, B_FDIM), tok)),
        (jax.ShapeDtypeStruct((B, S, B_WIDTH), BF16), pl.BlockSpec((1, tm, B_WIDTH), tok)),
        (jax.ShapeDtypeStruct((B, S, B_WIDTH), F32), pl.BlockSpec((1, tm, B_WIDTH), tok)),
    ]
    return pl.pallas_call(
        _inproj_kernel,
        grid=grid,
        in_specs=[
            pl.BlockSpec((1, tm, D), tok),
            pl.BlockSpec((1, 6, D), lambda b, i: (b, 0, 0)),
            pl.BlockSpec((1, D, _C_END), lw3),
            pl.BlockSpec((1, A_WIDTH, A_HEADS * KV_RANK), lw3),
            pl.BlockSpec((1, KV_RANK, D), lw3),
            pl.BlockSpec((1, IDX_HEADS, D), lw3),
            pl.BlockSpec((1, 1, KV_RANK), lw3),
            pl.BlockSpec((1, KV_RANK, tm), lw3),
            pl.BlockSpec((1, 1, B_FDIM), lw3),
            pl.BlockSpec((1, 1, B_FDIM), lw3),
        ],
        out_specs=[o[1] for o in outs],
        out_shape=[o[0] for o in outs],
        compiler_params=_cparams(("arbitrary", "arbitrary")),
        name="inproj",
    )(x, mod, wp, wblk, wckvT, wiwT, gkv, gkvT, llb, l1m)


def _dsa_kernel(iq_ref, iwT_ref, qlat_ref, ikA_ref, ikB_ref, ckv_ref, ckvT_ref, bn_ref, wuvT_ref, out_ref,
                sc_ref, madd_ref, maddn_ref, l_ref, yaT_ref, *, k_sel, n_idx_bits):
    j = pl.program_id(1)
    q0 = j * TQ
    nk = q0 + TQ
    nunit = (nk + UNIT - 1) // UNIT
    near0 = pl.multiple_of(jnp.maximum(nk - NEAR, 0), TQ)
    bn_row0 = pl.multiple_of(jnp.where(j == 0, TQ, 0), TQ)
    lane = lax.broadcasted_iota(I32, (1, TQ), 1)
    limit = (((q0 + lane) >> 6) + 1) << 6
    row_iota = lax.broadcasted_iota(I32, (UNIT, TQ), 0)

    def unit_rows(u):
        return pl.ds(pl.multiple_of(u * UNIT, UNIT), UNIT)

    iqs = iq_ref[0].reshape(IDX_HEADS // 2 * TQ, LANES)
    iw = iwT_ref[0]

    def score_unit(u, carry):
        rows = unit_rows(u)
        xe = _nt_dot(ikA_ref[0, rows, :], iqs)
        xo = _nt_dot(ikB_ref[0, rows, :], iqs)
        acc = jnp.zeros((UNIT, TQ), F32)
        for p in range(IDX_HEADS // 2):
            acc = acc + iw[2 * p:2 * p + 1, :] * jnp.maximum(xe[:, p * TQ:(p + 1) * TQ], 0.0)
            acc = acc + iw[2 * p + 1:2 * p + 2, :] * jnp.maximum(xo[:, p * TQ:(p + 1) * TQ], 0.0)
        bits = lax.bitcast_convert_type(acc, I32)
        key = bits ^ ((bits >> 31) & 0x7FFFFFFF)
        sc_ref[rows, :] = jnp.where(row_iota + u * UNIT < limit, key, INT_MIN)
        return carry

    lax.fori_loop(0, nunit, score_unit, 0)

    def count_where(pred):
        def body(u, acc):
            hit = pred(sc_ref[unit_rows(u), :], u * UNIT).astype(I32)
            return acc + jnp.sum(hit.reshape(UNIT // SUBLANES, SUBLANES, TQ), axis=0)
        acc = lax.fori_loop(0, nunit, body, jnp.zeros((SUBLANES, TQ), I32))
        return jnp.sum(acc, axis=0, keepdims=True)

    def search_body(i, carry):
        u_thr, c_ge = carry
        cand_u = u_thr | (jnp.int32(1) << (31 - i))
        cand = cand_u ^ INT_MIN
        cnt = count_where(lambda blk, r0: blk >= cand)
        ok = cnt >= k_sel
        return jnp.where(ok, cand_u, u_thr), jnp.where(ok, cnt, c_ge)

    u_thr, c_ge = lax.fori_loop(0, 32, search_body, (jnp.zeros((1, TQ), I32), jnp.zeros((1, TQ), I32)))
    thr = jnp.maximum(u_thr ^ INT_MIN, INT_MIN + 1)
    straddle = (c_ge > k_sel).astype(I32)

    def tie_bound():
        c_gt = count_where(lambda blk, r0: blk > thr)
        need = k_sel - c_gt

        def tie_body(i, j0):
            cand = j0 | (jnp.int32(1) << (n_idx_bits - 1 - i))
            cnt = count_where(lambda blk, r0: jnp.where(blk == thr, row_iota + r0, cand) < cand)
            return jnp.where(cnt < need, cand, j0)

        j0 = lax.fori_loop(0, n_idx_bits, tie_body, jnp.zeros((1, TQ), I32))
        return jnp.where(straddle > 0, j0 + 1, jnp.int32(2 ** n_idx_bits))

    jstar = lax.cond(jnp.max(straddle) > 0, tie_bound, lambda: jnp.full((1, TQ), 2 ** n_idx_bits, I32))

    def madd_unit(u, carry):
        rows = unit_rows(u)
        key = sc_ref[rows, :]
        tie_keep = jnp.where(row_iota + u * UNIT < jstar, 0.0, NEG_INF)
        madd_ref[rows, :] = jnp.where(key > thr, 0.0, jnp.where(key == thr, tie_keep, NEG_INF))
        return carry

    lax.fori_loop(0, nunit, madd_unit, 0)
    maddn_ref[...] = madd_ref[pl.ds(near0, NEAR), :]
    madd_ref[pl.ds(near0, NEAR), :] = jnp.full((NEAR, TQ), NEG_INF, F32)

    def col_max(v):
        return jnp.max(v.reshape(v.shape[0] // SUBLANES, SUBLANES, HG * TQ), axis=0)

    for g in range(A_HEADS // HG):
        qg = qlat_ref[0, g * HG:(g + 1) * HG].reshape(HG * TQ, KV_RANK)

        def logit_unit(u, m):
            rows = unit_rows(u)
            xl = _nt_dot(ckv_ref[0, rows, :], qg) + jnp.concatenate([madd_ref[rows, :]] * HG, axis=1)
            l_ref[rows, :] = xl
            return jnp.maximum(m, col_max(xl))

        m = lax.fori_loop(0, nunit, logit_unit, jnp.full((SUBLANES, HG * TQ), NEG_INF, F32))
        xn = _nt_dot(ckv_ref[0, pl.ds(near0, NEAR), :], qg) + jnp.concatenate([maddn_ref[...]] * HG, axis=1)
        xn = xn + jnp.concatenate([bn_ref[g * HG + hh, pl.ds(bn_row0, NEAR), :] for hh in range(HG)], axis=1)
        l_ref[pl.ds(near0, NEAR), :] = xn
        mx = jnp.max(jnp.maximum(m, col_max(xn)), axis=0, keepdims=True)

        def pv_unit(u, carry):
            ssum, o_t = carry
            rows = unit_rows(u)
            p = jnp.exp(l_ref[rows, :] - mx)
            ssum = ssum + jnp.sum(p.reshape(UNIT // SUBLANES, SUBLANES, HG * TQ), axis=0)
            o_t = o_t + jnp.dot(ckvT_ref[0, :, rows], p.astype(BF16), preferred_element_type=F32)
            return ssum, o_t

        ssum, o_t = lax.fori_loop(
            0, nunit, pv_unit, (jnp.zeros((SUBLANES, HG * TQ), F32), jnp.zeros((KV_RANK, HG * TQ), F32)))
        o_t = (o_t * (1.0 / jnp.sum(ssum, axis=0, keepdims=True))).astype(BF16)
        for hh in range(HG):
            h = g * HG + hh
            yaT_ref[h * A_HEAD_DIM:(h + 1) * A_HEAD_DIM, :] = jnp.dot(
                wuvT_ref[0, h], o_t[:, hh * TQ:(hh + 1) * TQ], preferred_element_type=F32)

    out_ref[0] = yaT_ref[...].T.astype(BF16)


def _dsa(l, iq, iwT, qlat, ikA, ikB, ckv, ckvT, bn, wuvT):
    B, S = ckv.shape[0], ckv.shape[1]
    assert S % UNIT == 0 and UNIT % TQ == 0 and TQ % CHUNK == 0 and CHUNK == 64 and NEAR <= UNIT
    k_sel = min(IDX_TOPK_MAX, S // 4)
    n_idx_bits = int(math.log2(S))
    assert 2 ** n_idx_bits == S
    grid = (B, S // TQ)
    blk = lambda b, i: (b, 0, i, 0)
    full = lambda b, i: (b, 0, 0)
    kern = functools.partial(_dsa_kernel, k_sel=k_sel, n_idx_bits=n_idx_bits)
    return pl.pallas_call(
        kern,
        grid=grid,
        in_specs=[
            pl.BlockSpec((1, IDX_HEADS // 2, TQ, LANES), blk),
            pl.BlockSpec((1, IDX_HEADS, TQ), lambda b, i: (b, 0, i)),
            pl.BlockSpec((1, A_HEADS, TQ, KV_RANK), blk),
            pl.BlockSpec((1, S, LANES), full),
            pl.BlockSpec((1, S, LANES), full),
            pl.BlockSpec((1, S, KV_RANK), full),
            pl.BlockSpec((1, KV_RANK, S), full),
            pl.BlockSpec((A_HEADS, NEAR + TQ, TQ), lambda b, i: (0, 0, 0)),
            pl.BlockSpec((1, A_HEADS, A_HEAD_DIM, KV_RANK), lambda b, i: (l, 0, 0, 0)),
        ],
        out_specs=pl.BlockSpec((1, TQ, A_WIDTH), lambda b, i: (b, i, 0)),
        out_shape=jax.ShapeDtypeStruct((B, S, A_WIDTH), BF16),
        scratch_shapes=[
            pltpu.VMEM((S, TQ), I32),
            pltpu.VMEM((S, TQ), F32),
            pltpu.VMEM((NEAR, TQ), F32),
            pltpu.VMEM((S, HG * TQ), F32),
            pltpu.VMEM((A_WIDTH, TQ), F32),
        ],
        compiler_params=_cparams(("arbitrary", "arbitrary")),
        name="dsa_attention",
    )(iq, iwT, qlat, ikA, ikB, ckv, ckvT, bn, wuvT)


def _hgrn_constants():
    c = CHUNK
    r = np.arange(c)[:, None]
    jj = np.arange(c)[None, :]
    mats = [(jj <= r), (jj > r)]
    masks = [np.eye(c, dtype=bool)]
    m = c // 2
    while m >= 1:
        start = (r // (2 * m)) * (2 * m)
        bd = start + m - 1
        upper = r > bd
        mats.append(np.where(upper, (jj > bd) & (jj <= r), (jj > r) & (jj <= bd)))
        same_parent = (r // (2 * m)) == (jj // (2 * m))
        masks.append(same_parent & upper & (jj <= (jj // (2 * m)) * (2 * m) + m - 1))
        m //= 2
    m_all = np.concatenate(mats, axis=0).astype(np.float32)
    total = np.zeros((c, c), np.int32)
    for mk in masks:
        total += mk
    assert (total == np.tril(np.ones((c, c), np.int32))).all()
    return np.concatenate([m_all] * 3, axis=1), np.stack(masks).astype(np.float32)


_HGRN_M3, _HGRN_MASKS = _hgrn_constants()
_HGRN_LEVELS = _HGRN_MASKS.shape[0] - 1
HGRN_STEP_CHUNKS = 2


def _hgrn_kernel(q_ref, k_ref, lf_ref, v_ref, gate_ref, m3_ref, mask_ref, gn_ref, out_ref, st_ref):
    @pl.when(pl.program_id(1) == 0)
    def _():
        st_ref[...] = jnp.zeros(st_ref.shape, F32)

    c = CHUNK
    for ci in range(HGRN_STEP_CHUNKS):
        rows = slice(ci * c, (ci + 1) * c)
        g = lf_ref[0, rows, :]
        g_hi = g.astype(BF16)
        r1 = g - g_hi.astype(F32)
        g_mid = r1.astype(BF16)
        g_lo = (r1 - g_mid.astype(F32)).astype(BF16)
        sums = jnp.dot(m3_ref[...], jnp.concatenate([g_hi, g_mid, g_lo], axis=0), preferred_element_type=F32)
        e_all = jnp.exp(sums)
        for h in range(B_HEADS):
            cols = slice(h * B_KEY_DIM, (h + 1) * B_KEY_DIM)
            qh = q_ref[0, rows, cols]
            kh = k_ref[0, rows, cols]
            vh = v_ref[0, rows, cols]
            att = mask_ref[0] * _nt_dot(qh.astype(BF16), kh.astype(BF16))
            for lv in range(_HGRN_LEVELS):
                e_l = e_all[(2 + lv) * c:(3 + lv) * c, cols]
                att = att + mask_ref[lv + 1] * _nt_dot((qh * e_l).astype(BF16), (kh * e_l).astype(BF16))
            e_b = e_all[0:c, cols]
            e_rem = e_all[c:2 * c, cols]
            st = st_ref[h]
            o = jnp.dot(att.astype(BF16), vh, preferred_element_type=F32)
            o = o + _nt_dot((qh * e_b).astype(BF16), st.astype(BF16))
            upd = lax.dot_general(vh, (kh * e_rem).astype(BF16), (((0,), (0,)), ((), ())),
                                  preferred_element_type=F32)
            st_ref[h] = st * e_b[c - 1:c, :] + upd
            o = o * lax.rsqrt(jnp.mean(o * o, axis=-1, keepdims=True) + RMS_EPS) * gn_ref[0]
            out_ref[0, rows, cols] = (o * gate_ref[0, rows, cols]).astype(BF16)


def _hgrn(l, hq, hk, hlf, hv, hgate, gnorm):
    B, S, W = hq.shape
    ts = CHUNK * HGRN_STEP_CHUNKS
    tok = lambda b, i: (b, i, 0)
    return pl.pallas_call(
        _hgrn_kernel,
        grid=(B, S // ts),
        in_specs=[
            pl.BlockSpec((1, ts, W), tok),
            pl.BlockSpec((1, ts, W), tok),
            pl.BlockSpec((1, ts, W), tok),
            pl.BlockSpec((1, ts, W), tok),
            pl.BlockSpec((1, ts, W), tok),
            pl.BlockSpec(_HGRN_M3.shape, lambda b, i: (0, 0)),
            pl.BlockSpec(_HGRN_MASKS.shape, lambda b, i: (0, 0, 0)),
            pl.BlockSpec((1, 1, B_VAL_DIM), lambda b, i: (l, 0, 0)),
        ],
        out_specs=pl.BlockSpec((1, ts, W), tok),
        out_shape=jax.ShapeDtypeStruct((B, S, W), BF16),
        scratch_shapes=[pltpu.VMEM((B_HEADS, B_VAL_DIM, B_KEY_DIM), F32)],
        compiler_params=_cparams(("arbitrary", "arbitrary")),
        name="hgrn2",
    )(hq, hk, hlf, hv, hgate, jnp.asarray(_HGRN_M3, BF16), jnp.asarray(_HGRN_MASKS), gnorm)


def _layernorm(v, g, b):
    mu = jnp.mean(v, axis=-1, keepdims=True)
    d = v - mu
    var = jnp.mean(d * d, axis=-1, keepdims=True)
    return d * lax.rsqrt(var + LN_EPS) * g + b


def _first_argmax(v, idx, axes, big):
    mx = v
    for ax in axes:
        mx = jnp.max(mx, axis=ax, keepdims=True)
    pos = jnp.where(v == mx, idx, big)
    for ax in axes:
        pos = jnp.min(pos, axis=ax, keepdims=True)
    return mx, pos


def _outproj_kernel(ya_ref, yb_ref, x_ref, mod_ref, wo_ref, lng_ref, lnb_ref, wrT_ref, rbias_ref,
                    x1_ref, u2_ref, gates_ref, *, alpha):
    y = jnp.dot(ya_ref[0], wo_ref[0, 0:A_WIDTH, :], preferred_element_type=F32)
    y = y + jnp.dot(yb_ref[0], wo_ref[0, A_WIDTH:, :], preferred_element_type=F32)
    g1 = mod_ref[0, 2:3, :]
    x1 = _layernorm(alpha * x_ref[0] + (1.0 + g1) * y, lng_ref[0], lnb_ref[0])
    x1_ref[0] = x1
    u2 = (x1 * (1.0 + mod_ref[0, 4:5, :]) + mod_ref[0, 3:4, :]).astype(BF16)
    u2_ref[0] = u2

    tm = u2.shape[0]
    gsz = N_EXPERTS // N_GROUPS
    scores = 1.0 / (1.0 + jnp.exp(-_nt_dot(wrT_ref[0], u2)))
    sel = (scores + rbias_ref[0]).reshape(N_GROUPS, gsz, tm)
    scores = scores.reshape(N_GROUPS, gsz, tm)
    i_m = lax.broadcasted_iota(I32, (N_GROUPS, gsz, tm), 1)
    i_g = lax.broadcasted_iota(I32, (N_GROUPS, 1, tm), 0)
    i_e = lax.broadcasted_iota(I32, (N_GROUPS, gsz, tm), 0) * gsz + i_m
    m1, p1 = _first_argmax(sel, i_m, (1,), gsz)
    m2 = jnp.max(jnp.where(i_m == p1, NEG_INF, sel), axis=1, keepdims=True)
    gs = m1 + m2
    gmask = jnp.zeros(gs.shape, F32)
    for _ in range(TOPK_GROUPS):
        _, pg = _first_argmax(gs, i_g, (0,), N_GROUPS)
        hit = i_g == pg
        gmask = jnp.where(hit, 1.0, gmask)
        gs = jnp.where(hit, NEG_INF, gs)
    cand = jnp.where(jnp.broadcast_to(gmask, sel.shape) > 0.0, sel, NEG_INF)
    w = jnp.zeros(sel.shape, F32)
    for _ in range(TOP_K):
        _, pe = _first_argmax(cand, i_e, (1, 0), N_EXPERTS)
        hit = i_e == pe
        w = jnp.where(hit, scores, w)
        cand = jnp.where(hit, NEG_INF, cand)
    wsum = jnp.sum(jnp.sum(w, axis=1, keepdims=True), axis=0, keepdims=True)
    gates = (w / wsum * ROUTED_SCALE).reshape(N_EXPERTS, tm)
    g_hi = gates.astype(BF16).astype(F32)
    g_lo = (gates - g_hi).astype(BF16).astype(F32)
    gates_ref[0] = jnp.concatenate([g_hi, g_lo], axis=0).T.astype(BF16)


def _outproj(l, ya, yb, x, mod, wo, ln_g, ln_b, wrT, rbias, alpha):
    B, S, D = x.shape
    tm = TM_PROJ
    tok = lambda b, i: (b, i, 0)
    lw3 = lambda b, i: (l, 0, 0)
    return pl.pallas_call(
        functools.partial(_outproj_kernel, alpha=alpha),
        grid=(B, S // tm),
        in_specs=[
            pl.BlockSpec((1, tm, A_WIDTH), tok),
            pl.BlockSpec((1, tm, B_WIDTH), tok),
            pl.BlockSpec((1, tm, D), tok),
            pl.BlockSpec((1, 6, D), lambda b, i: (b, 0, 0)),
            pl.BlockSpec((1, D, D), lw3),
            pl.BlockSpec((1, 1, D), lw3),
            pl.BlockSpec((1, 1, D), lw3),
            pl.BlockSpec((1, N_EXPERTS, D), lw3),
            pl.BlockSpec((1, N_EXPERTS, tm), lw3),
        ],
        out_specs=[pl.BlockSpec((1, tm, D), tok), pl.BlockSpec((1, tm, D), tok),
                   pl.BlockSpec((1, tm, 2 * N_EXPERTS), tok)],
        out_shape=[jax.ShapeDtypeStruct((B, S, D), F32), jax.ShapeDtypeStruct((B, S, D), BF16),
                   jax.ShapeDtypeStruct((B, S, 2 * N_EXPERTS), BF16)],
        compiler_params=_cparams(("arbitrary", "arbitrary")),
        name="outproj_router",
    )(ya, yb, x, mod, wo, ln_g, ln_b, wrT, rbias)


def _moe_kernel(u_ref, gates_ref, x1_ref, mod_ref, wgu_ref, wd_ref, sgu_ref, sd_ref, lng_ref, lnb_ref,
                out_ref, acc_ref, *, alpha):
    e = pl.program_id(1)
    u = u_ref[...]

    def ffn(wgu, wd, gate):
        hgu = jnp.dot(u, wgu, preferred_element_type=F32)
        h = _silu(hgu[:, :EXPERT_DIM]) * hgu[:, EXPERT_DIM:]
        if gate is not None:
            h = h * gate
        return jnp.dot(h.astype(BF16), wd, preferred_element_type=F32)

    @pl.when(e == 0)
    def _():
        acc_ref[...] = ffn(sgu_ref[0], sd_ref[0], None)

    rows = lax.broadcasted_iota(I32, (2 * N_EXPERTS, EXPERT_DIM), 0)
    onehot = jnp.where((rows & (N_EXPERTS - 1)) == e, 1.0, 0.0).astype(BF16)
    gate = jnp.dot(gates_ref[...], onehot, preferred_element_type=F32)
    acc_ref[...] += ffn(wgu_ref[0, 0], wd_ref[0, 0], gate)

    @pl.when(e == pl.num_programs(1) - 1)
    def _():
        g2 = mod_ref[0, 5:6, :]
        out_ref[...] = _layernorm(alpha * x1_ref[...] + (1.0 + g2) * acc_ref[...], lng_ref[0], lnb_ref[0])


def _moe(l, u2, gates, x1, mod, wgu, wd, sgu, sd, ln_g, ln_b, alpha, seq):
    T, D = u2.shape
    tm = TM_MOE
    assert seq % tm == 0
    tok = lambda i, e: (i, 0)
    lw3 = lambda i, e: (l, 0, 0)
    return pl.pallas_call(
        functools.partial(_moe_kernel, alpha=alpha),
        grid=(T // tm, N_EXPERTS),
        in_specs=[
            pl.BlockSpec((tm, D), tok),
            pl.BlockSpec((tm, 2 * N_EXPERTS), tok),
            pl.BlockSpec((tm, D), tok),
            pl.BlockSpec((1, 6, D), lambda i, e: ((i * tm) // seq, 0, 0)),
            pl.BlockSpec((1, 1, D, 2 * EXPERT_DIM), lambda i, e: (l, e, 0, 0)),
            pl.BlockSpec((1, 1, EXPERT_DIM, D), lambda i, e: (l, e, 0, 0)),
            pl.BlockSpec((1, D, 2 * SHARED_DIM), lw3),
            pl.BlockSpec((1, SHARED_DIM, D), lw3),
            pl.BlockSpec((1, 1, D), lw3),
            pl.BlockSpec((1, 1, D), lw3),
        ],
        out_specs=pl.BlockSpec((tm, D), tok),
        out_shape=jax.ShapeDtypeStruct((T, D), F32),
        scratch_shapes=[pltpu.VMEM((tm, D), F32)],
        compiler_params=_cparams(("arbitrary", "arbitrary")),
        name="moe_dense",
    )(u2, gates, x1, mod, wgu, wd, sgu, sd, ln_g, ln_b)


def _prepare_params(w_in, kv_norm_g, w_uk, w_uv, hgrn_lb, w_out, w_router, router_bias,
                    w_gate, w_up, w_down, ws_gate, ws_up, ws_down):
    L = w_in.shape[0]
    sizes = (A_WIDTH, KV_RANK, IDX_HEADS * IDX_DIM, IDX_DIM, IDX_HEADS, B_FDIM, B_FDIM, B_WIDTH, B_WIDTH)
    offs = np.concatenate([[0], np.cumsum(sizes)])
    seg = lambda i: w_in[:, :, offs[i]:offs[i + 1]]
    w_aq, w_ckv, w_iq, w_ik, w_iw, w_hq, w_hf, w_hi, w_hg = (seg(i) for i in range(9))
    zik = jnp.zeros_like(w_ik)
    wp = jnp.concatenate([w_aq, w_ckv, w_iq, w_ik, zik, zik, w_ik, w_hq, w_hf, w_hg, w_hi], axis=-1).astype(BF16)
    assert wp.shape[-1] == _C_END
    eye = jnp.eye(A_HEADS, dtype=F32)
    wblk = (jnp.einsum('lhdr,hg->lhdgr', w_uk * ATTN_SCALE, eye)
            .reshape(L, A_WIDTH, A_HEADS * KV_RANK).astype(BF16))
    p = dict(
        wp=wp, wblk=wblk,
        wckvT=jnp.swapaxes(w_ckv, 1, 2).astype(BF16),
        wiwT=jnp.swapaxes(w_iw, 1, 2).astype(BF16),
        gkv=kv_norm_g.reshape(L, 1, KV_RANK),
        gkvT=jnp.broadcast_to(kv_norm_g[:, :, None], (L, KV_RANK, TM_PROJ)),
        wuvT=jnp.swapaxes(w_uv, 2, 3).astype(BF16),
        wo=w_out.astype(BF16),
        wrT=jnp.swapaxes(w_router, 1, 2).astype(BF16),
        rbias=jnp.broadcast_to(router_bias[:, :, None], (L, N_EXPERTS, TM_PROJ)),
        wgu=jnp.concatenate([w_gate, w_up], axis=-1).astype(BF16),
        wd=w_down.astype(BF16),
        sgu=jnp.concatenate([ws_gate, ws_up], axis=-1).astype(BF16),
        sd=ws_down.astype(BF16),
    )
    lbs = jnp.cumsum(jax.nn.softmax(hgrn_lb.astype(F32), axis=0), axis=0)
    lbs = jnp.clip(lbs - lbs[0:1], 0.0, 1.0 - 1e-6)
    p["llb"] = jnp.log(lbs).reshape(L, 1, B_FDIM)
    p["l1m"] = jnp.log1p(-lbs).reshape(L, 1, B_FDIM)
    return p


def kernel(x, c, w_ada, b_ada, w_in, kv_norm_g, w_uk, w_uv, rel_bias, hgrn_lb, gnorm_g, w_out, ln1_g, ln1_b,
           w_router, router_bias, w_gate, w_up, w_down, ws_gate, ws_up, ws_down, ln2_g, ln2_b):
    B, S, D = x.shape
    L = w_in.shape[0]
    alpha = (2 * L) ** 0.25
    p = _prepare_params(w_in, kv_norm_g, w_uk, w_uv, hgrn_lb, w_out, w_router, router_bias,
                        w_gate, w_up, w_down, ws_gate, ws_up, ws_down)
    mods = _adaln(c, w_ada, b_ada).reshape(L, B, 6, D)
    bn = _bias_tile(rel_bias)
    gn = gnorm_g.reshape(L, 1, B_VAL_DIM)
    ln1g, ln1b = ln1_g.reshape(L, 1, D), ln1_b.reshape(L, 1, D)
    ln2g, ln2b = ln2_g.reshape(L, 1, D), ln2_b.reshape(L, 1, D)
    for l in range(L):
        mod = mods[l]
        (qlat, ckv, ckvT, iq, ikA, ikB, iwT, hq, hk, hlf, hv, hgate) = _inproj(
            l, x, mod, p["wp"], p["wblk"], p["wckvT"], p["wiwT"], p["gkv"], p["gkvT"], p["llb"], p["l1m"])
        ya = _dsa(l, iq, iwT, qlat, ikA, ikB, ckv, ckvT, bn, p["wuvT"])
        yb = _hgrn(l, hq, hk, hlf, hv, hgate, gn)
        x1, u2, gates = _outproj(l, ya, yb, x, mod, p["wo"], ln1g, ln1b, p["wrT"], p["rbias"], alpha)
        x = _moe(l, u2.reshape(B * S, D), gates.reshape(B * S, 2 * N_EXPERTS), x1.reshape(B * S, D), mod,
                 p["wgu"], p["wd"], p["sgu"], p["sd"], ln2g, ln2b, alpha, S).reshape(B, S, D)
    return x
```

```python
import functools
import math

import numpy as np
import jax
import jax.numpy as jnp
from jax import lax
from jax.experimental import pallas as pl
from jax.experimental.pallas import tpu as pltpu

F32 = jnp.float32
BF16 = jnp.bfloat16
I32 = jnp.int32

D_MODEL = 1024
CHUNK = 64
A_HEADS = 8
A_HEAD_DIM = 64
A_WIDTH = A_HEADS * A_HEAD_DIM
KV_RANK = 128
IDX_HEADS = 8
IDX_DIM = 64
IDX_TOPK_MAX = 256
IDX_W_SCALE = (IDX_HEADS ** -0.5) * (IDX_DIM ** -0.5)
ATTN_SCALE = A_HEAD_DIM ** -0.5
NUM_BUCKETS = 32
MAX_DISTANCE = 128
B_HEADS = 4
B_KEY_DIM = 128
B_VAL_DIM = 128
B_WIDTH = B_HEADS * B_VAL_DIM
B_FDIM = B_HEADS * B_KEY_DIM
N_EXPERTS = 64
TOP_K = 8
N_GROUPS = 8
TOPK_GROUPS = 4
EXPERT_DIM = 256
SHARED_DIM = 256
ROUTED_SCALE = 2.5
LN_EPS = 1e-5
RMS_EPS = 1e-6

LANES = 128
SUBLANES = 8
VMEM_LIMIT_BYTES = 56 * 1024 * 1024

INT_MIN = -(2 ** 31)
NEG_INF = float("-inf")

TM_PROJ = 512
TQ = 128
UNIT = 512
NEAR = 2 * TQ
HG = 8
COUNT_ACCS = 8
TM_MOE = 1024

_C_AQ, _C_CKV, _C_IQ, _C_IKA, _C_IKB, _C_HQ, _C_HF, _C_HG, _C_HI, _C_END = (
    0, 512, 640, 1152, 1280, 1408, 1920, 2432, 2944, 3456)


def _silu(v):
    return v * (1.0 / (1.0 + jnp.exp(-v)))


def _nt_dot(a, b):
    return lax.dot_general(a, b, (((1,), (1,)), ((), ())), preferred_element_type=F32)


def _cparams(sem):
    return pltpu.CompilerParams(dimension_semantics=sem, vmem_limit_bytes=VMEM_LIMIT_BYTES)


def _adaln_kernel(c_ref, w_ref, b_ref, o_ref):
    cond = _silu(c_ref[...])
    o_ref[0] = jnp.dot(cond.astype(BF16), w_ref[0].astype(BF16), preferred_element_type=F32) + b_ref[0]


def _adaln(c, w_ada, b_ada):
    L, D, D6 = w_ada.shape
    B = c.shape[0]
    nb = D6 // D
    return pl.pallas_call(
        _adaln_kernel,
        grid=(L, nb),
        in_specs=[
            pl.BlockSpec((B, D), lambda l, j: (0, 0)),
            pl.BlockSpec((1, D, D), lambda l, j: (l, 0, j)),
            pl.BlockSpec((1, 1, D), lambda l, j: (l, 0, j)),
        ],
        out_specs=pl.BlockSpec((1, B, D), lambda l, j: (l, 0, j)),
        out_shape=jax.ShapeDtypeStruct((L, B, D6), F32),
        compiler_params=_cparams(("arbitrary", "arbitrary")),
        name="adaln_mod",
    )(c, w_ada, b_ada.reshape(L, 1, D6))


_T5_NB = NUM_BUCKETS // 2
_T5_EXACT = _T5_NB // 2
_T5_THRESHOLDS = tuple(
    int(math.ceil(_T5_EXACT * (MAX_DISTANCE / _T5_EXACT) ** (j / (_T5_NB - _T5_EXACT)) - 1e-9))
    for j in range(1, _T5_NB - _T5_EXACT))
FAR_BUCKET = _T5_NB - 1
assert _T5_THRESHOLDS[-1] <= TQ, "keys further than one query block behind must share the far bucket"


def _bias_kernel(rb_ref, o_ref):
    kr = lax.broadcasted_iota(I32, (NEAR + TQ, TQ), 0)
    ql = lax.broadcasted_iota(I32, (NEAR + TQ, TQ), 1)
    rel = kr - TQ - ql
    n = jnp.abs(rel)
    large = jnp.full(rel.shape, _T5_EXACT, I32)
    for t in _T5_THRESHOLDS:
        large = large + (n >= t).astype(I32)
    bucket = jnp.where(rel > 0, _T5_NB, 0) + jnp.where(n < _T5_EXACT, n, large)
    for h in range(A_HEADS):
        acc = jnp.zeros(rel.shape, F32)
        for bk in range(NUM_BUCKETS):
            acc = jnp.where(bucket == bk, rb_ref[bk, h], acc)
        o_ref[h] = acc - rb_ref[FAR_BUCKET, h]


def _bias_tile(rel_bias):
    return pl.pallas_call(
        _bias_kernel,
        in_specs=[pl.BlockSpec(memory_space=pltpu.SMEM)],
        out_specs=pl.BlockSpec(memory_space=pltpu.VMEM),
        out_shape=jax.ShapeDtypeStruct((A_HEADS, NEAR + TQ, TQ), F32),
        name="rel_bias_tile",
    )(rel_bias)


def _inproj_kernel(x_ref, mod_ref, wp_ref, wblk_ref, wckvT_ref, wiwT_ref, gkv_ref, gkvT_ref, llb_ref, l1m_ref,
                   qlat_ref, ckv_ref, ckvT_ref, iq_ref, ikA_ref, ikB_ref, iwT_ref,
                   hq_ref, hk_ref, hlf_ref, hv_ref, hgate_ref):
    x = x_ref[0]
    sh1 = mod_ref[0, 0:1, :]
    sc1 = mod_ref[0, 1:2, :]
    u = (x * (1.0 + sc1) + sh1).astype(BF16)
    z = jnp.dot(u, wp_ref[0], preferred_element_type=F32)

    ql = jnp.dot(z[:, _C_AQ:_C_CKV].astype(BF16), wblk_ref[0], preferred_element_type=F32)
    for h in range(A_HEADS):
        qlat_ref[0, h] = ql[:, h * KV_RANK:(h + 1) * KV_RANK].astype(BF16)

    zc = z[:, _C_CKV:_C_IQ]
    inv = lax.rsqrt(jnp.mean(zc * zc, axis=-1, keepdims=True) + RMS_EPS)
    ckv_ref[0] = (zc * inv * gkv_ref[0]).astype(BF16)
    zt = _nt_dot(wckvT_ref[0], u)
    inv_t = lax.rsqrt(jnp.mean(zt * zt, axis=0, keepdims=True) + RMS_EPS)
    ckvT_ref[0] = (zt * inv_t * gkvT_ref[0]).astype(BF16)

    for p in range(IDX_HEADS // 2):
        iq_ref[0, p] = z[:, _C_IQ + p * LANES:_C_IQ + (p + 1) * LANES].astype(BF16)
    ikA_ref[0] = z[:, _C_IKA:_C_IKB].astype(BF16)
    ikB_ref[0] = z[:, _C_IKB:_C_HQ].astype(BF16)
    iwT_ref[0] = _nt_dot(wiwT_ref[0], u) * IDX_W_SCALE

    hq_ref[0] = _silu(z[:, _C_HQ:_C_HF])
    zf = z[:, _C_HF:_C_HG]
    log_sig = jnp.minimum(zf, 0.0) - jnp.log1p(jnp.exp(-jnp.abs(zf)))
    a = llb_ref[0]
    c = l1m_ref[0] + log_sig
    logf = jnp.maximum(a, c) + jnp.log1p(jnp.exp(-jnp.abs(a - c)))
    hlf_ref[0] = logf
    hk_ref[0] = 1.0 - jnp.exp(logf)
    hgate_ref[0] = _silu(z[:, _C_HG:_C_HI])
    hv_ref[0] = z[:, _C_HI:_C_END].astype(BF16)


def _inproj(l, x, mod, wp, wblk, wckvT, wiwT, gkv, gkvT, llb, l1m):
    B, S, D = x.shape
    tm = TM_PROJ
    grid = (B, S // tm)
    lw3 = lambda b, i: (l, 0, 0)
    tok = lambda b, i: (b, i, 0)
    tokT = lambda b, i: (b, 0, i)
    hd4 = lambda b, i: (b, 0, i, 0)
    outs = [
        (jax.ShapeDtypeStruct((B, A_HEADS, S, KV_RANK), BF16), pl.BlockSpec((1, A_HEADS, tm, KV_RANK), hd4)),
        (jax.ShapeDtypeStruct((B, S, KV_RANK), BF16), pl.BlockSpec((1, tm, KV_RANK), tok)),
        (jax.ShapeDtypeStruct((B, KV_RANK, S), BF16), pl.BlockSpec((1, KV_RANK, tm), tokT)),
        (jax.ShapeDtypeStruct((B, IDX_HEADS // 2, S, LANES), BF16), pl.BlockSpec((1, IDX_HEADS // 2, tm, LANES), hd4)),
        (jax.ShapeDtypeStruct((B, S, LANES), BF16), pl.BlockSpec((1, tm, LANES), tok)),
        (jax.ShapeDtypeStruct((B, S, LANES), BF16), pl.BlockSpec((1, tm, LANES), tok)),
        (jax.ShapeDtypeStruct((B, IDX_HEADS, S), F32), pl.BlockSpec((1, IDX_HEADS, tm), tokT)),
        (jax.ShapeDtypeStruct((B, S, B_FDIM), F32), pl.BlockSpec((1, tm, B_FDIM), tok)),
        (jax.ShapeDtypeStruct((B, S, B_FDIM), F32), pl.BlockSpec((1, tm, B_FDIM), tok)),
        (jax.ShapeDtypeStruct((B, S, B_FDIM), F32), pl.BlockSpec((1, tm, B_FDIM), tok)),
        (jax.ShapeDtypeStruct((B, S, B_WIDTH), BF16), pl.BlockSpec((1, tm, B_WIDTH), tok)),
        (jax.ShapeDtypeStruct((B, S, B_WIDTH), F32), pl.BlockSpec((1, tm, B_WIDTH), tok)),
    ]
    return pl.pallas_call(
        _inproj_kernel,
        grid=grid,
        in_specs=[
            pl.BlockSpec((1, tm, D), tok),
            pl.BlockSpec((1, 6, D), lambda b, i: (b, 0, 0)),
            pl.BlockSpec((1, D, _C_END), lw3),
            pl.BlockSpec((1, A_WIDTH, A_HEADS * KV_RANK), lw3),
            pl.BlockSpec((1, KV_RANK, D), lw3),
            pl.BlockSpec((1, IDX_HEADS, D), lw3),
            pl.BlockSpec((1, 1, KV_RANK), lw3),
            pl.BlockSpec((1, KV_RANK, tm), lw3),
            pl.BlockSpec((1, 1, B_FDIM), lw3),
            pl.BlockSpec((1, 1, B_FDIM), lw3),
        ],
        out_specs=[o[1] for o in outs],
        out_shape=[o[0] for o in outs],
        compiler_params=_cparams(("arbitrary", "arbitrary")),
        name="inproj",
    )(x, mod, wp, wblk, wckvT, wiwT, gkv, gkvT, llb, l1m)


def _dsa_kernel(iq_ref, iwT_ref, qlat_ref, ikA_ref, ikB_ref, ckv_ref, ckvT_ref, bn_ref, wuvT_ref, out_ref,
                sc_ref, madd_ref, maddn_ref, l_ref, ot_ref, yaT_ref, *, k_sel, n_idx_bits):
    j = pl.program_id(1)
    q0 = j * TQ
    nk = q0 + TQ
    nunit = (nk + UNIT - 1) // UNIT
    near0 = pl.multiple_of(jnp.maximum(nk - NEAR, 0), TQ)
    bn_row0 = pl.multiple_of(jnp.where(j == 0, TQ, 0), TQ)
    lane = lax.broadcasted_iota(I32, (1, TQ), 1)
    limit = (((q0 + lane) >> 6) + 1) << 6
    row_iota = lax.broadcasted_iota(I32, (UNIT, TQ), 0)

    def unit_rows(u):
        return pl.ds(pl.multiple_of(u * UNIT, UNIT), UNIT)

    iqs = iq_ref[0].reshape(IDX_HEADS // 2 * TQ, LANES)
    iw = iwT_ref[0]

    def score_unit(u, carry):
        rows = unit_rows(u)
        xe = _nt_dot(ikA_ref[0, rows, :], iqs)
        xo = _nt_dot(ikB_ref[0, rows, :], iqs)
        acc = jnp.zeros((UNIT, TQ), F32)
        for p in range(IDX_HEADS // 2):
            acc = acc + iw[2 * p:2 * p + 1, :] * jnp.maximum(xe[:, p * TQ:(p + 1) * TQ], 0.0)
            acc = acc + iw[2 * p + 1:2 * p + 2, :] * jnp.maximum(xo[:, p * TQ:(p + 1) * TQ], 0.0)
        bits = lax.bitcast_convert_type(acc, I32)
        key = bits ^ ((bits >> 31) & 0x7FFFFFFF)
        sc_ref[rows, :] = jnp.where(row_iota + u * UNIT < limit, key, INT_MIN)
        return carry

    lax.fori_loop(0, nunit, score_unit, 0)

    def count_where(pred):
        def body(u, acc):
            hit = pred(sc_ref[unit_rows(u), :], u * UNIT).reshape(-1, COUNT_ACCS * SUBLANES, TQ)
            for s in range(hit.shape[0]):
                acc = jnp.where(hit[s], acc + 1, acc)
            return acc
        acc = lax.fori_loop(0, nunit, body, jnp.zeros((COUNT_ACCS * SUBLANES, TQ), I32))
        return jnp.sum(acc, axis=0, keepdims=True)

    def search_body(i, carry):
        u_thr, c_ge = carry
        cand_u = u_thr | (jnp.int32(1) << (31 - i))
        cand = cand_u ^ INT_MIN
        cnt = count_where(lambda blk, r0: blk >= cand)
        ok = cnt >= k_sel
        return jnp.where(ok, cand_u, u_thr), jnp.where(ok, cnt, c_ge)

    u_thr, c_ge = lax.fori_loop(0, 32, search_body, (jnp.zeros((1, TQ), I32), jnp.zeros((1, TQ), I32)))
    thr = jnp.maximum(u_thr ^ INT_MIN, INT_MIN + 1)
    straddle = (c_ge > k_sel).astype(I32)

    def tie_bound():
        c_gt = count_where(lambda blk, r0: blk > thr)
        need = k_sel - c_gt

        def tie_body(i, j0):
            cand = j0 | (jnp.int32(1) << (n_idx_bits - 1 - i))
            cnt = count_where(lambda blk, r0: jnp.where(blk == thr, row_iota + r0, cand) < cand)
            return jnp.where(cnt < need, cand, j0)

        j0 = lax.fori_loop(0, n_idx_bits, tie_body, jnp.zeros((1, TQ), I32))
        return jnp.where(straddle > 0, j0 + 1, jnp.int32(2 ** n_idx_bits))

    jstar = lax.cond(jnp.max(straddle) > 0, tie_bound, lambda: jnp.full((1, TQ), 2 ** n_idx_bits, I32))

    def madd_unit(u, carry):
        rows = unit_rows(u)
        key = sc_ref[rows, :]
        tie_keep = jnp.where(row_iota + u * UNIT < jstar, 0.0, NEG_INF)
        madd_ref[rows, :] = jnp.where(key > thr, 0.0, jnp.where(key == thr, tie_keep, NEG_INF))
        return carry

    lax.fori_loop(0, nunit, madd_unit, 0)
    maddn_ref[...] = madd_ref[pl.ds(near0, NEAR), :]
    madd_ref[pl.ds(near0, NEAR), :] = jnp.full((NEAR, TQ), NEG_INF, F32)

    def col_max(v):
        return jnp.max(v.reshape(v.shape[0] // SUBLANES, SUBLANES, HG * TQ), axis=0)

    for g in range(A_HEADS // HG):
        qg = qlat_ref[0, g * HG:(g + 1) * HG].reshape(HG * TQ, KV_RANK)

        def logit_unit(u, m):
            rows = unit_rows(u)
            xl = _nt_dot(ckv_ref[0, rows, :], qg) + jnp.concatenate([madd_ref[rows, :]] * HG, axis=1)
            l_ref[rows, :] = xl
            return jnp.maximum(m, col_max(xl))

        m = lax.fori_loop(0, nunit, logit_unit, jnp.full((SUBLANES, HG * TQ), NEG_INF, F32))
        xn = _nt_dot(ckv_ref[0, pl.ds(near0, NEAR), :], qg) + jnp.concatenate([maddn_ref[...]] * HG, axis=1)
        xn = xn + jnp.concatenate([bn_ref[g * HG + hh, pl.ds(bn_row0, NEAR), :] for hh in range(HG)], axis=1)
        l_ref[pl.ds(near0, NEAR), :] = xn
        mx = jnp.max(jnp.maximum(m, col_max(xn)), axis=0, keepdims=True)

        ot_ref[...] = jnp.zeros(ot_ref.shape, F32)

        def pv_unit(u, ssum):
            rows = unit_rows(u)
            p = jnp.exp(l_ref[rows, :] - mx)
            ot_ref[...] += jnp.dot(ckvT_ref[0, :, rows], p.astype(BF16), preferred_element_type=F32)
            return ssum + jnp.sum(p.reshape(UNIT // SUBLANES, SUBLANES, HG * TQ), axis=0)

        ssum = lax.fori_loop(0, nunit, pv_unit, jnp.zeros((SUBLANES, HG * TQ), F32))
        o_t = (ot_ref[...] * (1.0 / jnp.sum(ssum, axis=0, keepdims=True))).astype(BF16)
        for hh in range(HG):
            h = g * HG + hh
            yaT_ref[h * A_HEAD_DIM:(h + 1) * A_HEAD_DIM, :] = jnp.dot(
                wuvT_ref[0, h], o_t[:, hh * TQ:(hh + 1) * TQ], preferred_element_type=F32)

    out_ref[0] = yaT_ref[...].T.astype(BF16)


def _dsa(l, iq, iwT, qlat, ikA, ikB, ckv, ckvT, bn, wuvT):
    B, S = ckv.shape[0], ckv.shape[1]
    assert S % UNIT == 0 and UNIT % TQ == 0 and TQ % CHUNK == 0 and CHUNK == 64 and NEAR <= UNIT
    k_sel = min(IDX_TOPK_MAX, S // 4)
    n_idx_bits = int(math.log2(S))
    assert 2 ** n_idx_bits == S
    grid = (B, S // TQ)
    blk = lambda b, i: (b, 0, i, 0)
    full = lambda b, i: (b, 0, 0)
    kern = functools.partial(_dsa_kernel, k_sel=k_sel, n_idx_bits=n_idx_bits)
    return pl.pallas_call(
        kern,
        grid=grid,
        in_specs=[
            pl.BlockSpec((1, IDX_HEADS // 2, TQ, LANES), blk),
            pl.BlockSpec((1, IDX_HEADS, TQ), lambda b, i: (b, 0, i)),
            pl.BlockSpec((1, A_HEADS, TQ, KV_RANK), blk),
            pl.BlockSpec((1, S, LANES), full),
            pl.BlockSpec((1, S, LANES), full),
            pl.BlockSpec((1, S, KV_RANK), full),
            pl.BlockSpec((1, KV_RANK, S), full),
            pl.BlockSpec((A_HEADS, NEAR + TQ, TQ), lambda b, i: (0, 0, 0)),
            pl.BlockSpec((1, A_HEADS, A_HEAD_DIM, KV_RANK), lambda b, i: (l, 0, 0, 0)),
        ],
        out_specs=pl.BlockSpec((1, TQ, A_WIDTH), lambda b, i: (b, i, 0)),
        out_shape=jax.ShapeDtypeStruct((B, S, A_WIDTH), BF16),
        scratch_shapes=[
            pltpu.VMEM((S, TQ), I32),
            pltpu.VMEM((S, TQ), F32),
            pltpu.VMEM((NEAR, TQ), F32),
            pltpu.VMEM((S, HG * TQ), F32),
            pltpu.VMEM((KV_RANK, HG * TQ), F32),
            pltpu.VMEM((A_WIDTH, TQ), F32),
        ],
        compiler_params=_cparams(("arbitrary", "arbitrary")),
        name="dsa_attention",
    )(iq, iwT, qlat, ikA, ikB, ckv, ckvT, bn, wuvT)


def _hgrn_constants():
    c = CHUNK
    r = np.arange(c)[:, None]
    jj = np.arange(c)[None, :]
    mats = [(jj <= r), (jj > r)]
    masks = [np.eye(c, dtype=bool)]
    m = c // 2
    while m >= 1:
        start = (r // (2 * m)) * (2 * m)
        bd = start + m - 1
        upper = r > bd
        mats.append(np.where(upper, (jj > bd) & (jj <= r), (jj > r) & (jj <= bd)))
        same_parent = (r // (2 * m)) == (jj // (2 * m))
        masks.append(same_parent & upper & (jj <= (jj // (2 * m)) * (2 * m) + m - 1))
        m //= 2
    m_all = np.concatenate(mats, axis=0).astype(np.float32)
    total = np.zeros((c, c), np.int32)
    for mk in masks:
        total += mk
    assert (total == np.tril(np.ones((c, c), np.int32))).all()
    return np.concatenate([m_all] * 3, axis=1), np.stack(masks).astype(np.float32)


_HGRN_M3, _HGRN_MASKS = _hgrn_constants()
_HGRN_LEVELS = _HGRN_MASKS.shape[0] - 1
HGRN_STEP_CHUNKS = 2


def _hgrn_kernel(q_ref, k_ref, lf_ref, v_ref, gate_ref, m3_ref, mask_ref, gn_ref, out_ref, st_ref):
    @pl.when(pl.program_id(1) == 0)
    def _():
        st_ref[...] = jnp.zeros(st_ref.shape, F32)

    c = CHUNK
    for ci in range(HGRN_STEP_CHUNKS):
        rows = slice(ci * c, (ci + 1) * c)
        g = lf_ref[0, rows, :]
        g_hi = g.astype(BF16)
        r1 = g - g_hi.astype(F32)
        g_mid = r1.astype(BF16)
        g_lo = (r1 - g_mid.astype(F32)).astype(BF16)
        sums = jnp.dot(m3_ref[...], jnp.concatenate([g_hi, g_mid, g_lo], axis=0), preferred_element_type=F32)
        e_all = jnp.exp(sums)
        for h in range(B_HEADS):
            cols = slice(h * B_KEY_DIM, (h + 1) * B_KEY_DIM)
            qh = q_ref[0, rows, cols]
            kh = k_ref[0, rows, cols]
            vh = v_ref[0, rows, cols]
            att = mask_ref[0] * _nt_dot(qh.astype(BF16), kh.astype(BF16))
            for lv in range(_HGRN_LEVELS):
                e_l = e_all[(2 + lv) * c:(3 + lv) * c, cols]
                att = att + mask_ref[lv + 1] * _nt_dot((qh * e_l).astype(BF16), (kh * e_l).astype(BF16))
            e_b = e_all[0:c, cols]
            e_rem = e_all[c:2 * c, cols]
            st = st_ref[h]
            o = jnp.dot(att.astype(BF16), vh, preferred_element_type=F32)
            o = o + _nt_dot((qh * e_b).astype(BF16), st.astype(BF16))
            upd = lax.dot_general(vh, (kh * e_rem).astype(BF16), (((0,), (0,)), ((), ())),
                                  preferred_element_type=F32)
            st_ref[h] = st * e_b[c - 1:c, :] + upd
            o = o * lax.rsqrt(jnp.mean(o * o, axis=-1, keepdims=True) + RMS_EPS) * gn_ref[0]
            out_ref[0, rows, cols] = (o * gate_ref[0, rows, cols]).astype(BF16)


def _hgrn(l, hq, hk, hlf, hv, hgate, gnorm):
    B, S, W = hq.shape
    ts = CHUNK * HGRN_STEP_CHUNKS
    tok = lambda b, i: (b, i, 0)
    return pl.pallas_call(
        _hgrn_kernel,
        grid=(B, S // ts),
        in_specs=[
            pl.BlockSpec((1, ts, W), tok),
            pl.BlockSpec((1, ts, W), tok),
            pl.BlockSpec((1, ts, W), tok),
            pl.BlockSpec((1, ts, W), tok),
            pl.BlockSpec((1, ts, W), tok),
            pl.BlockSpec(_HGRN_M3.shape, lambda b, i: (0, 0)),
            pl.BlockSpec(_HGRN_MASKS.shape, lambda b, i: (0, 0, 0)),
            pl.BlockSpec((1, 1, B_VAL_DIM), lambda b, i: (l, 0, 0)),
        ],
        out_specs=pl.BlockSpec((1, ts, W), tok),
        out_shape=jax.ShapeDtypeStruct((B, S, W), BF16),
        scratch_shapes=[pltpu.VMEM((B_HEADS, B_VAL_DIM, B_KEY_DIM), F32)],
        compiler_params=_cparams(("arbitrary", "arbitrary")),
        name="hgrn2",
    )(hq, hk, hlf, hv, hgate, jnp.asarray(_HGRN_M3, BF16), jnp.asarray(_HGRN_MASKS), gnorm)


def _layernorm(v, g, b):
    mu = jnp.mean(v, axis=-1, keepdims=True)
    d = v - mu
    var = jnp.mean(d * d, axis=-1, keepdims=True)
    return d * lax.rsqrt(var + LN_EPS) * g + b


def _first_argmax(v, idx, axes, big):
    mx = v
    for ax in axes:
        mx = jnp.max(mx, axis=ax, keepdims=True)
    pos = jnp.where(v == mx, idx, big)
    for ax in axes:
        pos = jnp.min(pos, axis=ax, keepdims=True)
    return mx, pos


def _outproj_kernel(ya_ref, yb_ref, x_ref, mod_ref, wo_ref, lng_ref, lnb_ref, wrT_ref, rbias_ref,
                    x1_ref, u2_ref, gates_ref, *, alpha):
    y = jnp.dot(ya_ref[0], wo_ref[0, 0:A_WIDTH, :], preferred_element_type=F32)
    y = y + jnp.dot(yb_ref[0], wo_ref[0, A_WIDTH:, :], preferred_element_type=F32)
    g1 = mod_ref[0, 2:3, :]
    x1 = _layernorm(alpha * x_ref[0] + (1.0 + g1) * y, lng_ref[0], lnb_ref[0])
    x1_ref[0] = x1
    u2 = (x1 * (1.0 + mod_ref[0, 4:5, :]) + mod_ref[0, 3:4, :]).astype(BF16)
    u2_ref[0] = u2

    tm = u2.shape[0]
    gsz = N_EXPERTS // N_GROUPS
    scores = 1.0 / (1.0 + jnp.exp(-_nt_dot(wrT_ref[0], u2)))
    sel = (scores + rbias_ref[0]).reshape(N_GROUPS, gsz, tm)
    scores = scores.reshape(N_GROUPS, gsz, tm)
    i_m = lax.broadcasted_iota(I32, (N_GROUPS, gsz, tm), 1)
    i_g = lax.broadcasted_iota(I32, (N_GROUPS, 1, tm), 0)
    i_e = lax.broadcasted_iota(I32, (N_GROUPS, gsz, tm), 0) * gsz + i_m
    m1, p1 = _first_argmax(sel, i_m, (1,), gsz)
    m2 = jnp.max(jnp.where(i_m == p1, NEG_INF, sel), axis=1, keepdims=True)
    gs = m1 + m2
    gmask = jnp.zeros(gs.shape, F32)
    for _ in range(TOPK_GROUPS):
        _, pg = _first_argmax(gs, i_g, (0,), N_GROUPS)
        hit = i_g == pg
        gmask = jnp.where(hit, 1.0, gmask)
        gs = jnp.where(hit, NEG_INF, gs)
    cand = jnp.where(jnp.broadcast_to(gmask, sel.shape) > 0.0, sel, NEG_INF)
    w = jnp.zeros(sel.shape, F32)
    for _ in range(TOP_K):
        _, pe = _first_argmax(cand, i_e, (1, 0), N_EXPERTS)
        hit = i_e == pe
        w = jnp.where(hit, scores, w)
        cand = jnp.where(hit, NEG_INF, cand)
    wsum = jnp.sum(jnp.sum(w, axis=1, keepdims=True), axis=0, keepdims=True)
    gates = (w / wsum * ROUTED_SCALE).reshape(N_EXPERTS, tm)
    g_hi = gates.astype(BF16).astype(F32)
    g_lo = (gates - g_hi).astype(BF16).astype(F32)
    gates_ref[0] = jnp.concatenate([g_hi, g_lo], axis=0).T.astype(BF16)


def _outproj(l, ya, yb, x, mod, wo, ln_g, ln_b, wrT, rbias, alpha):
    B, S, D = x.shape
    tm = TM_PROJ
    tok = lambda b, i: (b, i, 0)
    lw3 = lambda b, i: (l, 0, 0)
    return pl.pallas_call(
        functools.partial(_outproj_kernel, alpha=alpha),
        grid=(B, S // tm),
        in_specs=[
            pl.BlockSpec((1, tm, A_WIDTH), tok),
            pl.BlockSpec((1, tm, B_WIDTH), tok),
            pl.BlockSpec((1, tm, D), tok),
            pl.BlockSpec((1, 6, D), lambda b, i: (b, 0, 0)),
            pl.BlockSpec((1, D, D), lw3),
            pl.BlockSpec((1, 1, D), lw3),
            pl.BlockSpec((1, 1, D), lw3),
            pl.BlockSpec((1, N_EXPERTS, D), lw3),
            pl.BlockSpec((1, N_EXPERTS, tm), lw3),
        ],
        out_specs=[pl.BlockSpec((1, tm, D), tok), pl.BlockSpec((1, tm, D), tok),
                   pl.BlockSpec((1, tm, 2 * N_EXPERTS), tok)],
        out_shape=[jax.ShapeDtypeStruct((B, S, D), F32), jax.ShapeDtypeStruct((B, S, D), BF16),
                   jax.ShapeDtypeStruct((B, S, 2 * N_EXPERTS), BF16)],
        compiler_params=_cparams(("arbitrary", "arbitrary")),
        name="outproj_router",
    )(ya, yb, x, mod, wo, ln_g, ln_b, wrT, rbias)


def _moe_kernel(u_ref, gates_ref, x1_ref, mod_ref, wgu_ref, wd_ref, sgu_ref, sd_ref, lng_ref, lnb_ref,
                out_ref, acc_ref, *, alpha):
    e = pl.program_id(1)
    u = u_ref[...]

    def ffn(wgu, wd, gate):
        hgu = jnp.dot(u, wgu, preferred_element_type=F32)
        h = _silu(hgu[:, :EXPERT_DIM]) * hgu[:, EXPERT_DIM:]
        if gate is not None:
            h = h * gate
        return jnp.dot(h.astype(BF16), wd, preferred_element_type=F32)

    @pl.when(e == 0)
    def _():
        acc_ref[...] = ffn(sgu_ref[0], sd_ref[0], None)

    rows = lax.broadcasted_iota(I32, (2 * N_EXPERTS, EXPERT_DIM), 0)
    onehot = jnp.where((rows & (N_EXPERTS - 1)) == e, 1.0, 0.0).astype(BF16)
    gate = jnp.dot(gates_ref[...], onehot, preferred_element_type=F32)
    acc_ref[...] += ffn(wgu_ref[0, 0], wd_ref[0, 0], gate)

    @pl.when(e == pl.num_programs(1) - 1)
    def _():
        g2 = mod_ref[0, 5:6, :]
        out_ref[...] = _layernorm(alpha * x1_ref[...] + (1.0 + g2) * acc_ref[...], lng_ref[0], lnb_ref[0])


def _moe(l, u2, gates, x1, mod, wgu, wd, sgu, sd, ln_g, ln_b, alpha, seq):
    T, D = u2.shape
    tm = TM_MOE
    assert seq % tm == 0
    tok = lambda i, e: (i, 0)
    lw3 = lambda i, e: (l, 0, 0)
    return pl.pallas_call(
        functools.partial(_moe_kernel, alpha=alpha),
        grid=(T // tm, N_EXPERTS),
        in_specs=[
            pl.BlockSpec((tm, D), tok),
            pl.BlockSpec((tm, 2 * N_EXPERTS), tok),
            pl.BlockSpec((tm, D), tok),
            pl.BlockSpec((1, 6, D), lambda i, e: ((i * tm) // seq, 0, 0)),
            pl.BlockSpec((1, 1, D, 2 * EXPERT_DIM), lambda i, e: (l, e, 0, 0)),
            pl.BlockSpec((1, 1, EXPERT_DIM, D), lambda i, e: (l, e, 0, 0)),
            pl.BlockSpec((1, D, 2 * SHARED_DIM), lw3),
            pl.BlockSpec((1, SHARED_DIM, D), lw3),
            pl.BlockSpec((1, 1, D), lw3),
            pl.BlockSpec((1, 1, D), lw3),
        ],
        out_specs=pl.BlockSpec((tm, D), tok),
        out_shape=jax.ShapeDtypeStruct((T, D), F32),
        scratch_shapes=[pltpu.VMEM((tm, D), F32)],
        compiler_params=_cparams(("arbitrary", "arbitrary")),
        name="moe_dense",
    )(u2, gates, x1, mod, wgu, wd, sgu, sd, ln_g, ln_b)


def _prepare_params(w_in, kv_norm_g, w_uk, w_uv, hgrn_lb, w_out, w_router, router_bias,
                    w_gate, w_up, w_down, ws_gate, ws_up, ws_down):
    L = w_in.shape[0]
    sizes = (A_WIDTH, KV_RANK, IDX_HEADS * IDX_DIM, IDX_DIM, IDX_HEADS, B_FDIM, B_FDIM, B_WIDTH, B_WIDTH)
    offs = np.concatenate([[0], np.cumsum(sizes)])
    seg = lambda i: w_in[:, :, offs[i]:offs[i + 1]]
    w_aq, w_ckv, w_iq, w_ik, w_iw, w_hq, w_hf, w_hi, w_hg = (seg(i) for i in range(9))
    zik = jnp.zeros_like(w_ik)
    wp = jnp.concatenate([w_aq, w_ckv, w_iq, w_ik, zik, zik, w_ik, w_hq, w_hf, w_hg, w_hi], axis=-1).astype(BF16)
    assert wp.shape[-1] == _C_END
    eye = jnp.eye(A_HEADS, dtype=F32)
    wblk = (jnp.einsum('lhdr,hg->lhdgr', w_uk * ATTN_SCALE, eye)
            .reshape(L, A_WIDTH, A_HEADS * KV_RANK).astype(BF16))
    p = dict(
        wp=wp, wblk=wblk,
        wckvT=jnp.swapaxes(w_ckv, 1, 2).astype(BF16),
        wiwT=jnp.swapaxes(w_iw, 1, 2).astype(BF16),
        gkv=kv_norm_g.reshape(L, 1, KV_RANK),
        gkvT=jnp.broadcast_to(kv_norm_g[:, :, None], (L, KV_RANK, TM_PROJ)),
        wuvT=jnp.swapaxes(w_uv, 2, 3).astype(BF16),
        wo=w_out.astype(BF16),
        wrT=jnp.swapaxes(w_router, 1, 2).astype(BF16),
        rbias=jnp.broadcast_to(router_bias[:, :, None], (L, N_EXPERTS, TM_PROJ)),
        wgu=jnp.concatenate([w_gate, w_up], axis=-1).astype(BF16),
        wd=w_down.astype(BF16),
        sgu=jnp.concatenate([ws_gate, ws_up], axis=-1).astype(BF16),
        sd=ws_down.astype(BF16),
    )
    lbs = jnp.cumsum(jax.nn.softmax(hgrn_lb.astype(F32), axis=0), axis=0)
    lbs = jnp.clip(lbs - lbs[0:1], 0.0, 1.0 - 1e-6)
    p["llb"] = jnp.log(lbs).reshape(L, 1, B_FDIM)
    p["l1m"] = jnp.log1p(-lbs).reshape(L, 1, B_FDIM)
    return p


def kernel(x, c, w_ada, b_ada, w_in, kv_norm_g, w_uk, w_uv, rel_bias, hgrn_lb, gnorm_g, w_out, ln1_g, ln1_b,
           w_router, router_bias, w_gate, w_up, w_down, ws_gate, ws_up, ws_down, ln2_g, ln2_b):
    B, S, D = x.shape
    L = w_in.shape[0]
    alpha = (2 * L) ** 0.25
    p = _prepare_params(w_in, kv_norm_g, w_uk, w_uv, hgrn_lb, w_out, w_router, router_bias,
                        w_gate, w_up, w_down, ws_gate, ws_up, ws_down)
    mods = _adaln(c, w_ada, b_ada).reshape(L, B, 6, D)
    bn = _bias_tile(rel_bias)
    gn = gnorm_g.reshape(L, 1, B_VAL_DIM)
    ln1g, ln1b = ln1_g.reshape(L, 1, D), ln1_b.reshape(L, 1, D)
    ln2g, ln2b = ln2_g.reshape(L, 1, D), ln2_b.reshape(L, 1, D)
    for l in range(L):
        mod = mods[l]
        (qlat, ckv, ckvT, iq, ikA, ikB, iwT, hq, hk, hlf, hv, hgate) = _inproj(
            l, x, mod, p["wp"], p["wblk"], p["wckvT"], p["wiwT"], p["gkv"], p["gkvT"], p["llb"], p["l1m"])
        ya = _dsa(l, iq, iwT, qlat, ikA, ikB, ckv, ckvT, bn, p["wuvT"])
        yb = _hgrn(l, hq, hk, hlf, hv, hgate, gn)
        x1, u2, gates = _outproj(l, ya, yb, x, mod, p["wo"], ln1g, ln1b, p["wrT"], p["rbias"], alpha)
        x = _moe(l, u2.reshape(B * S, D), gates.reshape(B * S, 2 * N_EXPERTS), x1.reshape(B * S, D), mod,
                 p["wgu"], p["wd"], p["sgu"], p["sd"], ln2g, ln2b, alpha, S).reshape(B, S, D)
    return x
```

```python
import functools
import math

import numpy as np
import jax
import jax.numpy as jnp
from jax import lax
from jax.experimental import pallas as pl
from jax.experimental.pallas import tpu as pltpu

F32 = jnp.float32
BF16 = jnp.bfloat16
I32 = jnp.int32

D_MODEL = 1024
CHUNK = 64
A_HEADS = 8
A_HEAD_DIM = 64
A_WIDTH = A_HEADS * A_HEAD_DIM
KV_RANK = 128
IDX_HEADS = 8
IDX_DIM = 64
IDX_TOPK_MAX = 256
IDX_W_SCALE = (IDX_HEADS ** -0.5) * (IDX_DIM ** -0.5)
ATTN_SCALE = A_HEAD_DIM ** -0.5
NUM_BUCKETS = 32
MAX_DISTANCE = 128
B_HEADS = 4
B_KEY_DIM = 128
B_VAL_DIM = 128
B_WIDTH = B_HEADS * B_VAL_DIM
B_FDIM = B_HEADS * B_KEY_DIM
N_EXPERTS = 64
TOP_K = 8
N_GROUPS = 8
TOPK_GROUPS = 4
EXPERT_DIM = 256
SHARED_DIM = 256
ROUTED_SCALE = 2.5
LN_EPS = 1e-5
RMS_EPS = 1e-6

LANES = 128
SUBLANES = 8
VMEM_LIMIT_BYTES = 56 * 1024 * 1024

INT_MIN = -(2 ** 31)
NEG_INF = float("-inf")

TM_PROJ = 512
TQ = 128
UNIT = 512
NEAR = 2 * TQ
COUNT_ACCS = 8
TM_MOE = 1024
MOE_EXPERTS_PER_STEP = 4

_C_AQ, _C_CKV, _C_IQ, _C_IKA, _C_IKB, _C_HQ, _C_HF, _C_HG, _C_HI, _C_END = (
    0, 512, 640, 1152, 1280, 1408, 1920, 2432, 2944, 3456)


def _silu(v):
    return v * (1.0 / (1.0 + jnp.exp(-v)))


def _nt_dot(a, b):
    return lax.dot_general(a, b, (((1,), (1,)), ((), ())), preferred_element_type=F32)


def _cparams(sem):
    return pltpu.CompilerParams(dimension_semantics=sem, vmem_limit_bytes=VMEM_LIMIT_BYTES)


def _adaln_kernel(c_ref, w_ref, b_ref, o_ref):
    cond = _silu(c_ref[...])
    o_ref[0] = jnp.dot(cond.astype(BF16), w_ref[0].astype(BF16), preferred_element_type=F32) + b_ref[0]


def _adaln(c, w_ada, b_ada):
    L, D, D6 = w_ada.shape
    B = c.shape[0]
    nb = D6 // D
    return pl.pallas_call(
        _adaln_kernel,
        grid=(L, nb),
        in_specs=[
            pl.BlockSpec((B, D), lambda l, j: (0, 0)),
            pl.BlockSpec((1, D, D), lambda l, j: (l, 0, j)),
            pl.BlockSpec((1, 1, D), lambda l, j: (l, 0, j)),
        ],
        out_specs=pl.BlockSpec((1, B, D), lambda l, j: (l, 0, j)),
        out_shape=jax.ShapeDtypeStruct((L, B, D6), F32),
        compiler_params=_cparams(("arbitrary", "arbitrary")),
        name="adaln_mod",
    )(c, w_ada, b_ada.reshape(L, 1, D6))


_T5_NB = NUM_BUCKETS // 2
_T5_EXACT = _T5_NB // 2
_T5_THRESHOLDS = tuple(
    int(math.ceil(_T5_EXACT * (MAX_DISTANCE / _T5_EXACT) ** (j / (_T5_NB - _T5_EXACT)) - 1e-9))
    for j in range(1, _T5_NB - _T5_EXACT))
FAR_BUCKET = _T5_NB - 1
assert _T5_THRESHOLDS[-1] <= TQ, "keys further than one query block behind must share the far bucket"


def _bias_kernel(rb_ref, o_ref):
    kr = lax.broadcasted_iota(I32, (NEAR + TQ, TQ), 0)
    ql = lax.broadcasted_iota(I32, (NEAR + TQ, TQ), 1)
    rel = kr - TQ - ql
    n = jnp.abs(rel)
    large = jnp.full(rel.shape, _T5_EXACT, I32)
    for t in _T5_THRESHOLDS:
        large = large + (n >= t).astype(I32)
    bucket = jnp.where(rel > 0, _T5_NB, 0) + jnp.where(n < _T5_EXACT, n, large)
    for h in range(A_HEADS):
        acc = jnp.zeros(rel.shape, F32)
        for bk in range(NUM_BUCKETS):
            acc = jnp.where(bucket == bk, rb_ref[bk, h], acc)
        o_ref[h] = acc - rb_ref[FAR_BUCKET, h]


def _bias_tile(rel_bias):
    return pl.pallas_call(
        _bias_kernel,
        in_specs=[pl.BlockSpec(memory_space=pltpu.SMEM)],
        out_specs=pl.BlockSpec(memory_space=pltpu.VMEM),
        out_shape=jax.ShapeDtypeStruct((A_HEADS, NEAR + TQ, TQ), F32),
        name="rel_bias_tile",
    )(rel_bias)


def _inproj_kernel(x_ref, mod_ref, wp_ref, wblk_ref, wckvT_ref, wiwT_ref, gkv_ref, gkvT_ref, llb_ref, l1m_ref,
                   qlat_ref, ckv_ref, ckvT_ref, iq_ref, ikA_ref, ikB_ref, iwT_ref,
                   hq_ref, hk_ref, hlf_ref, hv_ref, hgate_ref):
    x = x_ref[0]
    sh1 = mod_ref[0, 0:1, :]
    sc1 = mod_ref[0, 1:2, :]
    u = (x * (1.0 + sc1) + sh1).astype(BF16)
    z = jnp.dot(u, wp_ref[0], preferred_element_type=F32)

    ql = jnp.dot(z[:, _C_AQ:_C_CKV].astype(BF16), wblk_ref[0], preferred_element_type=F32)
    for h in range(A_HEADS):
        qlat_ref[0, h] = ql[:, h * KV_RANK:(h + 1) * KV_RANK].astype(BF16)

    zc = z[:, _C_CKV:_C_IQ]
    inv = lax.rsqrt(jnp.mean(zc * zc, axis=-1, keepdims=True) + RMS_EPS)
    ckv_ref[0] = (zc * inv * gkv_ref[0]).astype(BF16)
    zt = _nt_dot(wckvT_ref[0], u)
    inv_t = lax.rsqrt(jnp.mean(zt * zt, axis=0, keepdims=True) + RMS_EPS)
    ckvT_ref[0] = (zt * inv_t * gkvT_ref[0]).astype(BF16)

    for p in range(IDX_HEADS // 2):
        iq_ref[0, p] = z[:, _C_IQ + p * LANES:_C_IQ + (p + 1) * LANES].astype(BF16)
    ikA_ref[0] = z[:, _C_IKA:_C_IKB].astype(BF16)
    ikB_ref[0] = z[:, _C_IKB:_C_HQ].astype(BF16)
    iwT_ref[0] = _nt_dot(wiwT_ref[0], u) * IDX_W_SCALE

    hq_ref[0] = _silu(z[:, _C_HQ:_C_HF])
    zf = z[:, _C_HF:_C_HG]
    log_sig = jnp.minimum(zf, 0.0) - jnp.log1p(jnp.exp(-jnp.abs(zf)))
    a = llb_ref[0]
    c = l1m_ref[0] + log_sig
    logf = jnp.maximum(a, c) + jnp.log1p(jnp.exp(-jnp.abs(a - c)))
    hlf_ref[0] = logf
    hk_ref[0] = 1.0 - jnp.exp(logf)
    hgate_ref[0] = _silu(z[:, _C_HG:_C_HI])
    hv_ref[0] = z[:, _C_HI:_C_END].astype(BF16)


def _inproj(l, x, mod, wp, wblk, wckvT, wiwT, gkv, gkvT, llb, l1m):
    B, S, D = x.shape
    tm = TM_PROJ
    grid = (B, S // tm)
    lw3 = lambda b, i: (l, 0, 0)
    tok = lambda b, i: (b, i, 0)
    tokT = lambda b, i: (b, 0, i)
    hd4 = lambda b, i: (b, 0, i, 0)
    outs = [
        (jax.ShapeDtypeStruct((B, A_HEADS, S, KV_RANK), BF16), pl.BlockSpec((1, A_HEADS, tm, KV_RANK), hd4)),
        (jax.ShapeDtypeStruct((B, S, KV_RANK), BF16), pl.BlockSpec((1, tm, KV_RANK), tok)),
        (jax.ShapeDtypeStruct((B, KV_RANK, S), BF16), pl.BlockSpec((1, KV_RANK, tm), tokT)),
        (jax.ShapeDtypeStruct((B, IDX_HEADS // 2, S, LANES), BF16), pl.BlockSpec((1, IDX_HEADS // 2, tm, LANES), hd4)),
        (jax.ShapeDtypeStruct((B, S, LANES), BF16), pl.BlockSpec((1, tm, LANES), tok)),
        (jax.ShapeDtypeStruct((B, S, LANES), BF16), pl.BlockSpec((1, tm, LANES), tok)),
        (jax.ShapeDtypeStruct((B, IDX_HEADS, S), F32), pl.BlockSpec((1, IDX_HEADS, tm), tokT)),
        (jax.ShapeDtypeStruct((B, S, B_FDIM), F32), pl.BlockSpec((1, tm, B_FDIM), tok)),
        (jax.ShapeDtypeStruct((B, S, B_FDIM), F32), pl.BlockSpec((1, tm, B_FDIM), tok)),
        (jax.ShapeDtypeStruct((B, S, B_FDIM), F32), pl.BlockSpec((1, tm, B_FDIM), tok)),
        (jax.ShapeDtypeStruct((B, S, B_WIDTH), BF16), pl.BlockSpec((1, tm, B_WIDTH), tok)),
        (jax.ShapeDtypeStruct((B, S, B_WIDTH), F32), pl.BlockSpec((1, tm, B_WIDTH), tok)),
    ]
    return pl.pallas_call(
        _inproj_kernel,
        grid=grid,
        in_specs=[
            pl.BlockSpec((1, tm, D), tok),
            pl.BlockSpec((1, 6, D), lambda b, i: (b, 0, 0)),
            pl.BlockSpec((1, D, _C_END), lw3),
            pl.BlockSpec((1, A_WIDTH, A_HEADS * KV_RANK), lw3),
            pl.BlockSpec((1, KV_RANK, D), lw3),
            pl.BlockSpec((1, IDX_HEADS, D), lw3),
            pl.BlockSpec((1, 1, KV_RANK), lw3),
            pl.BlockSpec((1, KV_RANK, tm), lw3),
            pl.BlockSpec((1, 1, B_FDIM), lw3),
            pl.BlockSpec((1, 1, B_FDIM), lw3),
        ],
        out_specs=[o[1] for o in outs],
        out_shape=[o[0] for o in outs],
        compiler_params=_cparams(("arbitrary", "arbitrary")),
        name="inproj",
    )(x, mod, wp, wblk, wckvT, wiwT, gkv, gkvT, llb, l1m)


def _dsa_kernel(iq_ref, iwT_ref, qlat_ref, ikA_ref, ikB_ref, ckv_ref, ckvT_ref, bn_ref, wuvT_ref, out_ref,
                sc_ref, madd_ref, maddn_ref, la_ref, lb_ref, ot_ref, yaT_ref, *, k_sel, n_idx_bits):
    j = pl.program_id(1)
    q0 = j * TQ
    nk = q0 + TQ
    nunit = (nk + UNIT - 1) // UNIT
    near0 = pl.multiple_of(jnp.maximum(nk - NEAR, 0), TQ)
    bn_row0 = pl.multiple_of(jnp.where(j == 0, TQ, 0), TQ)
    lane = lax.broadcasted_iota(I32, (1, TQ), 1)
    limit = (((q0 + lane) >> 6) + 1) << 6
    row_iota = lax.broadcasted_iota(I32, (UNIT, TQ), 0)

    def unit_rows(u):
        return pl.ds(pl.multiple_of(u * UNIT, UNIT), UNIT)

    iqs = iq_ref[0].reshape(IDX_HEADS // 2 * TQ, LANES)
    iw = iwT_ref[0]

    def score_unit(u, carry):
        rows = unit_rows(u)
        xe = _nt_dot(ikA_ref[0, rows, :], iqs)
        xo = _nt_dot(ikB_ref[0, rows, :], iqs)
        acc = jnp.zeros((UNIT, TQ), F32)
        for p in range(IDX_HEADS // 2):
            acc = acc + iw[2 * p:2 * p + 1, :] * jnp.maximum(xe[:, p * TQ:(p + 1) * TQ], 0.0)
            acc = acc + iw[2 * p + 1:2 * p + 2, :] * jnp.maximum(xo[:, p * TQ:(p + 1) * TQ], 0.0)
        bits = lax.bitcast_convert_type(acc, I32)
        key = bits ^ ((bits >> 31) & 0x7FFFFFFF)
        sc_ref[rows, :] = jnp.where(row_iota + u * UNIT < limit, key, INT_MIN)
        return carry

    lax.fori_loop(0, nunit, score_unit, 0)

    def count_where(pred):
        def body(u, acc):
            hit = pred(sc_ref[unit_rows(u), :], u * UNIT).reshape(-1, COUNT_ACCS * SUBLANES, TQ)
            for s in range(hit.shape[0]):
                acc = jnp.where(hit[s], acc + 1, acc)
            return acc
        acc = lax.fori_loop(0, nunit, body, jnp.zeros((COUNT_ACCS * SUBLANES, TQ), I32))
        return jnp.sum(acc, axis=0, keepdims=True)

    def search_body(i, carry):
        u_thr, c_ge = carry
        cand_u = u_thr | (jnp.int32(1) << (31 - i))
        cand = cand_u ^ INT_MIN
        cnt = count_where(lambda blk, r0: blk >= cand)
        ok = cnt >= k_sel
        return jnp.where(ok, cand_u, u_thr), jnp.where(ok, cnt, c_ge)

    u_thr, c_ge = lax.fori_loop(0, 32, search_body, (jnp.zeros((1, TQ), I32), jnp.zeros((1, TQ), I32)))
    thr = jnp.maximum(u_thr ^ INT_MIN, INT_MIN + 1)
    straddle = (c_ge > k_sel).astype(I32)

    def tie_bound():
        c_gt = count_where(lambda blk, r0: blk > thr)
        need = k_sel - c_gt

        def tie_body(i, j0):
            cand = j0 | (jnp.int32(1) << (n_idx_bits - 1 - i))
            cnt = count_where(lambda blk, r0: jnp.where(blk == thr, row_iota + r0, cand) < cand)
            return jnp.where(cnt < need, cand, j0)

        j0 = lax.fori_loop(0, n_idx_bits, tie_body, jnp.zeros((1, TQ), I32))
        return jnp.where(straddle > 0, j0 + 1, jnp.int32(2 ** n_idx_bits))

    jstar = lax.cond(jnp.max(straddle) > 0, tie_bound, lambda: jnp.full((1, TQ), 2 ** n_idx_bits, I32))

    def madd_unit(u, carry):
        rows = unit_rows(u)
        key = sc_ref[rows, :]
        tie_keep = jnp.where(row_iota + u * UNIT < jstar, 0.0, NEG_INF)
        madd_ref[rows, :] = jnp.where(key > thr, 0.0, jnp.where(key == thr, tie_keep, NEG_INF))
        return carry

    lax.fori_loop(0, nunit, madd_unit, 0)
    maddn_ref[...] = madd_ref[pl.ds(near0, NEAR), :]
    madd_ref[pl.ds(near0, NEAR), :] = jnp.full((NEAR, TQ), NEG_INF, F32)

    qall = qlat_ref[0].reshape(A_HEADS * TQ, KV_RANK)

    def softmax_step(xl, ckv_t, carry):
        m_old, ssum = carry
        part = jnp.max(xl.reshape(xl.shape[0] // SUBLANES, SUBLANES, A_HEADS * TQ), axis=0)
        m_new = jnp.maximum(m_old, jnp.max(part, axis=0, keepdims=True))
        m_use = jnp.where(m_new == NEG_INF, 0.0, m_new)
        scale = jnp.exp(m_old - m_use)
        p = jnp.exp(xl - m_use)
        ot_ref[...] = ot_ref[...] * scale + jnp.dot(ckv_t, p.astype(BF16), preferred_element_type=F32)
        ssum = ssum * scale + jnp.sum(p.reshape(p.shape[0] // SUBLANES, SUBLANES, A_HEADS * TQ), axis=0)
        return m_new, ssum

    ot_ref[...] = jnp.zeros(ot_ref.shape, F32)
    near_rows = pl.ds(near0, NEAR)
    xn = _nt_dot(ckv_ref[0, near_rows, :], qall) + jnp.concatenate([maddn_ref[...]] * A_HEADS, axis=1)
    xn = xn + jnp.concatenate([bn_ref[h, pl.ds(bn_row0, NEAR), :] for h in range(A_HEADS)], axis=1)
    carry = softmax_step(xn, ckvT_ref[0, :, near_rows],
                         (jnp.full((1, A_HEADS * TQ), NEG_INF, F32), jnp.zeros((SUBLANES, A_HEADS * TQ), F32)))

    last_unit = sc_ref.shape[0] // UNIT - 1

    def issue_logits(u, buf_ref):
        buf_ref[...] = _nt_dot(ckv_ref[0, unit_rows(jnp.minimum(u, last_unit)), :], qall)

    def consume_logits(u, buf_ref, carry):
        rows = unit_rows(u)
        xl = buf_ref[...] + jnp.concatenate([madd_ref[rows, :]] * A_HEADS, axis=1)
        return softmax_step(xl, ckvT_ref[0, :, rows], carry)

    @pl.when(nunit % 2 == 1)
    def _():
        madd_ref[unit_rows(nunit), :] = jnp.full((UNIT, TQ), NEG_INF, F32)

    issue_logits(0, la_ref)

    def pair_step(i, carry):
        issue_logits(2 * i + 1, lb_ref)
        carry = consume_logits(2 * i, la_ref, carry)
        issue_logits(2 * i + 2, la_ref)
        return consume_logits(2 * i + 1, lb_ref, carry)

    _, ssum = lax.fori_loop(0, (nunit + 1) // 2, pair_step, carry)
    o_t = (ot_ref[...] * (1.0 / jnp.sum(ssum, axis=0, keepdims=True))).astype(BF16)
    for h in range(A_HEADS):
        yaT_ref[h * A_HEAD_DIM:(h + 1) * A_HEAD_DIM, :] = jnp.dot(
            wuvT_ref[0, h], o_t[:, h * TQ:(h + 1) * TQ], preferred_element_type=F32)

    out_ref[0] = yaT_ref[...].T.astype(BF16)


def _dsa(l, iq, iwT, qlat, ikA, ikB, ckv, ckvT, bn, wuvT):
    B, S = ckv.shape[0], ckv.shape[1]
    assert S % (2 * UNIT) == 0 and UNIT % TQ == 0 and TQ % CHUNK == 0 and CHUNK == 64 and NEAR <= UNIT
    k_sel = min(IDX_TOPK_MAX, S // 4)
    n_idx_bits = int(math.log2(S))
    assert 2 ** n_idx_bits == S
    grid = (B, S // TQ)
    blk = lambda b, i: (b, 0, i, 0)
    full = lambda b, i: (b, 0, 0)
    kern = functools.partial(_dsa_kernel, k_sel=k_sel, n_idx_bits=n_idx_bits)
    return pl.pallas_call(
        kern,
        grid=grid,
        in_specs=[
            pl.BlockSpec((1, IDX_HEADS // 2, TQ, LANES), blk),
            pl.BlockSpec((1, IDX_HEADS, TQ), lambda b, i: (b, 0, i)),
            pl.BlockSpec((1, A_HEADS, TQ, KV_RANK), blk),
            pl.BlockSpec((1, S, LANES), full),
            pl.BlockSpec((1, S, LANES), full),
            pl.BlockSpec((1, S, KV_RANK), full),
            pl.BlockSpec((1, KV_RANK, S), full),
            pl.BlockSpec((A_HEADS, NEAR + TQ, TQ), lambda b, i: (0, 0, 0)),
            pl.BlockSpec((1, A_HEADS, A_HEAD_DIM, KV_RANK), lambda b, i: (l, 0, 0, 0)),
        ],
        out_specs=pl.BlockSpec((1, TQ, A_WIDTH), lambda b, i: (b, i, 0)),
        out_shape=jax.ShapeDtypeStruct((B, S, A_WIDTH), BF16),
        scratch_shapes=[
            pltpu.VMEM((S, TQ), I32),
            pltpu.VMEM((S, TQ), F32),
            pltpu.VMEM((NEAR, TQ), F32),
            pltpu.VMEM((UNIT, A_HEADS * TQ), F32),
            pltpu.VMEM((UNIT, A_HEADS * TQ), F32),
            pltpu.VMEM((KV_RANK, A_HEADS * TQ), F32),
            pltpu.VMEM((A_WIDTH, TQ), F32),
        ],
        compiler_params=_cparams(("arbitrary", "arbitrary")),
        name="dsa_attention",
    )(iq, iwT, qlat, ikA, ikB, ckv, ckvT, bn, wuvT)


def _hgrn_constants():
    c = CHUNK
    r = np.arange(c)[:, None]
    jj = np.arange(c)[None, :]
    mats = [(jj <= r), (jj > r)]
    masks = [np.eye(c, dtype=bool)]
    m = c // 2
    while m >= 1:
        start = (r // (2 * m)) * (2 * m)
        bd = start + m - 1
        upper = r > bd
        mats.append(np.where(upper, (jj > bd) & (jj <= r), (jj > r) & (jj <= bd)))
        same_parent = (r // (2 * m)) == (jj // (2 * m))
        masks.append(same_parent & upper & (jj <= (jj // (2 * m)) * (2 * m) + m - 1))
        m //= 2
    m_all = np.concatenate(mats, axis=0).astype(np.float32)
    total = np.zeros((c, c), np.int32)
    for mk in masks:
        total += mk
    assert (total == np.tril(np.ones((c, c), np.int32))).all()
    return np.concatenate([m_all] * 3, axis=1), np.stack(masks).astype(np.float32)


_HGRN_M3, _HGRN_MASKS = _hgrn_constants()
_HGRN_LEVELS = _HGRN_MASKS.shape[0] - 1
HGRN_STEP_CHUNKS = 2


def _hgrn_kernel(q_ref, k_ref, lf_ref, v_ref, gate_ref, m3_ref, mask_ref, gn_ref, out_ref, st_ref):
    @pl.when(pl.program_id(1) == 0)
    def _():
        st_ref[...] = jnp.zeros(st_ref.shape, F32)

    c = CHUNK
    for ci in range(HGRN_STEP_CHUNKS):
        rows = slice(ci * c, (ci + 1) * c)
        g = lf_ref[0, rows, :]
        g_hi = g.astype(BF16)
        r1 = g - g_hi.astype(F32)
        g_mid = r1.astype(BF16)
        g_lo = (r1 - g_mid.astype(F32)).astype(BF16)
        sums = jnp.dot(m3_ref[...], jnp.concatenate([g_hi, g_mid, g_lo], axis=0), preferred_element_type=F32)
        e_all = jnp.exp(sums)
        for h in range(B_HEADS):
            cols = slice(h * B_KEY_DIM, (h + 1) * B_KEY_DIM)
            qh = q_ref[0, rows, cols]
            kh = k_ref[0, rows, cols]
            vh = v_ref[0, rows, cols]
            att = mask_ref[0] * _nt_dot(qh.astype(BF16), kh.astype(BF16))
            for lv in range(_HGRN_LEVELS):
                e_l = e_all[(2 + lv) * c:(3 + lv) * c, cols]
                att = att + mask_ref[lv + 1] * _nt_dot((qh * e_l).astype(BF16), (kh * e_l).astype(BF16))
            e_b = e_all[0:c, cols]
            e_rem = e_all[c:2 * c, cols]
            st = st_ref[h]
            o = jnp.dot(att.astype(BF16), vh, preferred_element_type=F32)
            o = o + _nt_dot((qh * e_b).astype(BF16), st.astype(BF16))
            upd = lax.dot_general(vh, (kh * e_rem).astype(BF16), (((0,), (0,)), ((), ())),
                                  preferred_element_type=F32)
            st_ref[h] = st * e_b[c - 1:c, :] + upd
            o = o * lax.rsqrt(jnp.mean(o * o, axis=-1, keepdims=True) + RMS_EPS) * gn_ref[0]
            out_ref[0, rows, cols] = (o * gate_ref[0, rows, cols]).astype(BF16)


def _hgrn(l, hq, hk, hlf, hv, hgate, gnorm):
    B, S, W = hq.shape
    ts = CHUNK * HGRN_STEP_CHUNKS
    tok = lambda b, i: (b, i, 0)
    return pl.pallas_call(
        _hgrn_kernel,
        grid=(B, S // ts),
        in_specs=[
            pl.BlockSpec((1, ts, W), tok),
            pl.BlockSpec((1, ts, W), tok),
            pl.BlockSpec((1, ts, W), tok),
            pl.BlockSpec((1, ts, W), tok),
            pl.BlockSpec((1, ts, W), tok),
            pl.BlockSpec(_HGRN_M3.shape, lambda b, i: (0, 0)),
            pl.BlockSpec(_HGRN_MASKS.shape, lambda b, i: (0, 0, 0)),
            pl.BlockSpec((1, 1, B_VAL_DIM), lambda b, i: (l, 0, 0)),
        ],
        out_specs=pl.BlockSpec((1, ts, W), tok),
        out_shape=jax.ShapeDtypeStruct((B, S, W), BF16),
        scratch_shapes=[pltpu.VMEM((B_HEADS, B_VAL_DIM, B_KEY_DIM), F32)],
        compiler_params=_cparams(("arbitrary", "arbitrary")),
        name="hgrn2",
    )(hq, hk, hlf, hv, hgate, jnp.asarray(_HGRN_M3, BF16), jnp.asarray(_HGRN_MASKS), gnorm)


def _layernorm(v, g, b):
    mu = jnp.mean(v, axis=-1, keepdims=True)
    d = v - mu
    var = jnp.mean(d * d, axis=-1, keepdims=True)
    return d * lax.rsqrt(var + LN_EPS) * g + b


def _first_argmax(v, idx, axes, big):
    mx = v
    for ax in axes:
        mx = jnp.max(mx, axis=ax, keepdims=True)
    pos = jnp.where(v == mx, idx, big)
    for ax in axes:
        pos = jnp.min(pos, axis=ax, keepdims=True)
    return mx, pos


def _outproj_kernel(ya_ref, yb_ref, x_ref, mod_ref, wo_ref, lng_ref, lnb_ref, wrT_ref, rbias_ref,
                    x1_ref, u2_ref, gates_ref, *, alpha):
    y = jnp.dot(ya_ref[0], wo_ref[0, 0:A_WIDTH, :], preferred_element_type=F32)
    y = y + jnp.dot(yb_ref[0], wo_ref[0, A_WIDTH:, :], preferred_element_type=F32)
    g1 = mod_ref[0, 2:3, :]
    x1 = _layernorm(alpha * x_ref[0] + (1.0 + g1) * y, lng_ref[0], lnb_ref[0])
    x1_ref[0] = x1
    u2 = (x1 * (1.0 + mod_ref[0, 4:5, :]) + mod_ref[0, 3:4, :]).astype(BF16)
    u2_ref[0] = u2

    tm = u2.shape[0]
    gsz = N_EXPERTS // N_GROUPS
    scores = 1.0 / (1.0 + jnp.exp(-_nt_dot(wrT_ref[0], u2)))
    sel = (scores + rbias_ref[0]).reshape(N_GROUPS, gsz, tm)
    scores = scores.reshape(N_GROUPS, gsz, tm)
    i_m = lax.broadcasted_iota(I32, (N_GROUPS, gsz, tm), 1)
    i_g = lax.broadcasted_iota(I32, (N_GROUPS, 1, tm), 0)
    i_e = lax.broadcasted_iota(I32, (N_GROUPS, gsz, tm), 0) * gsz + i_m
    m1, p1 = _first_argmax(sel, i_m, (1,), gsz)
    m2 = jnp.max(jnp.where(i_m == p1, NEG_INF, sel), axis=1, keepdims=True)
    gs = m1 + m2
    gmask = jnp.zeros(gs.shape, F32)
    for _ in range(TOPK_GROUPS):
        _, pg = _first_argmax(gs, i_g, (0,), N_GROUPS)
        hit = i_g == pg
        gmask = jnp.where(hit, 1.0, gmask)
        gs = jnp.where(hit, NEG_INF, gs)
    cand = jnp.where(jnp.broadcast_to(gmask, sel.shape) > 0.0, sel, NEG_INF)
    w = jnp.zeros(sel.shape, F32)
    for _ in range(TOP_K):
        _, pe = _first_argmax(cand, i_e, (1, 0), N_EXPERTS)
        hit = i_e == pe
        w = jnp.where(hit, scores, w)
        cand = jnp.where(hit, NEG_INF, cand)
    wsum = jnp.sum(jnp.sum(w, axis=1, keepdims=True), axis=0, keepdims=True)
    gates = (w / wsum * ROUTED_SCALE).reshape(N_EXPERTS, tm)
    g_hi = gates.astype(BF16).astype(F32)
    g_lo = (gates - g_hi).astype(BF16).astype(F32)
    gates_ref[0] = jnp.concatenate([g_hi, g_lo], axis=0).T.astype(BF16)


def _outproj(l, ya, yb, x, mod, wo, ln_g, ln_b, wrT, rbias, alpha):
    B, S, D = x.shape
    tm = TM_PROJ
    tok = lambda b, i: (b, i, 0)
    lw3 = lambda b, i: (l, 0, 0)
    return pl.pallas_call(
        functools.partial(_outproj_kernel, alpha=alpha),
        grid=(B, S // tm),
        in_specs=[
            pl.BlockSpec((1, tm, A_WIDTH), tok),
            pl.BlockSpec((1, tm, B_WIDTH), tok),
            pl.BlockSpec((1, tm, D), tok),
            pl.BlockSpec((1, 6, D), lambda b, i: (b, 0, 0)),
            pl.BlockSpec((1, D, D), lw3),
            pl.BlockSpec((1, 1, D), lw3),
            pl.BlockSpec((1, 1, D), lw3),
            pl.BlockSpec((1, N_EXPERTS, D), lw3),
            pl.BlockSpec((1, N_EXPERTS, tm), lw3),
        ],
        out_specs=[pl.BlockSpec((1, tm, D), tok), pl.BlockSpec((1, tm, D), tok),
                   pl.BlockSpec((1, tm, 2 * N_EXPERTS), tok)],
        out_shape=[jax.ShapeDtypeStruct((B, S, D), F32), jax.ShapeDtypeStruct((B, S, D), BF16),
                   jax.ShapeDtypeStruct((B, S, 2 * N_EXPERTS), BF16)],
        compiler_params=_cparams(("arbitrary", "arbitrary")),
        name="outproj_router",
    )(ya, yb, x, mod, wo, ln_g, ln_b, wrT, rbias)


def _moe_kernel(u_ref, gates_ref, x1_ref, mod_ref, wgu_ref, wd_ref, sgu_ref, sd_ref, lng_ref, lnb_ref,
                out_ref, acc_ref, *, alpha):
    s = pl.program_id(1)
    u = u_ref[...]

    def hidden(wgu):
        hgu = jnp.dot(u, wgu, preferred_element_type=F32)
        return _silu(hgu[:, :EXPERT_DIM]) * hgu[:, EXPERT_DIM:]

    @pl.when(s == 0)
    def _():
        acc_ref[...] = jnp.dot(hidden(sgu_ref[0]).astype(BF16), sd_ref[0], preferred_element_type=F32)

    rows = lax.broadcasted_iota(I32, (2 * N_EXPERTS, MOE_EXPERTS_PER_STEP * EXPERT_DIM), 0) & (N_EXPERTS - 1)
    cols = lax.broadcasted_iota(I32, (2 * N_EXPERTS, MOE_EXPERTS_PER_STEP * EXPERT_DIM), 1)
    onehot = jnp.where(rows == s * MOE_EXPERTS_PER_STEP + cols // EXPERT_DIM, 1.0, 0.0).astype(BF16)
    gate = jnp.dot(gates_ref[...], onehot, preferred_element_type=F32)
    h = jnp.concatenate(
        [(hidden(wgu_ref[0, k]) * gate[:, k * EXPERT_DIM:(k + 1) * EXPERT_DIM]).astype(BF16)
         for k in range(MOE_EXPERTS_PER_STEP)], axis=1)
    wd = wd_ref[0].reshape(MOE_EXPERTS_PER_STEP * EXPERT_DIM, wd_ref.shape[-1])
    acc_ref[...] += jnp.dot(h, wd, preferred_element_type=F32)

    @pl.when(s == pl.num_programs(1) - 1)
    def _():
        g2 = mod_ref[0, 5:6, :]
        out_ref[...] = _layernorm(alpha * x1_ref[...] + (1.0 + g2) * acc_ref[...], lng_ref[0], lnb_ref[0])


def _moe(l, u2, gates, x1, mod, wgu, wd, sgu, sd, ln_g, ln_b, alpha, seq):
    T, D = u2.shape
    tm = TM_MOE
    assert seq % tm == 0
    tok = lambda i, e: (i, 0)
    lw3 = lambda i, e: (l, 0, 0)
    return pl.pallas_call(
        functools.partial(_moe_kernel, alpha=alpha),
        grid=(T // tm, N_EXPERTS // MOE_EXPERTS_PER_STEP),
        in_specs=[
            pl.BlockSpec((tm, D), tok),
            pl.BlockSpec((tm, 2 * N_EXPERTS), tok),
            pl.BlockSpec((tm, D), tok),
            pl.BlockSpec((1, 6, D), lambda i, e: ((i * tm) // seq, 0, 0)),
            pl.BlockSpec((1, MOE_EXPERTS_PER_STEP, D, 2 * EXPERT_DIM), lambda i, e: (l, e, 0, 0)),
            pl.BlockSpec((1, MOE_EXPERTS_PER_STEP, EXPERT_DIM, D), lambda i, e: (l, e, 0, 0)),
            pl.BlockSpec((1, D, 2 * SHARED_DIM), lw3),
            pl.BlockSpec((1, SHARED_DIM, D), lw3),
            pl.BlockSpec((1, 1, D), lw3),
            pl.BlockSpec((1, 1, D), lw3),
        ],
        out_specs=pl.BlockSpec((tm, D), tok),
        out_shape=jax.ShapeDtypeStruct((T, D), F32),
        scratch_shapes=[pltpu.VMEM((tm, D), F32)],
        compiler_params=_cparams(("arbitrary", "arbitrary")),
        name="moe_dense",
    )(u2, gates, x1, mod, wgu, wd, sgu, sd, ln_g, ln_b)


def _prepare_params(w_in, kv_norm_g, w_uk, w_uv, hgrn_lb, w_out, w_router, router_bias,
                    w_gate, w_up, w_down, ws_gate, ws_up, ws_down):
    L = w_in.shape[0]
    sizes = (A_WIDTH, KV_RANK, IDX_HEADS * IDX_DIM, IDX_DIM, IDX_HEADS, B_FDIM, B_FDIM, B_WIDTH, B_WIDTH)
    offs = np.concatenate([[0], np.cumsum(sizes)])
    seg = lambda i: w_in[:, :, offs[i]:offs[i + 1]]
    w_aq, w_ckv, w_iq, w_ik, w_iw, w_hq, w_hf, w_hi, w_hg = (seg(i) for i in range(9))
    zik = jnp.zeros_like(w_ik)
    wp = jnp.concatenate([w_aq, w_ckv, w_iq, w_ik, zik, zik, w_ik, w_hq, w_hf, w_hg, w_hi], axis=-1).astype(BF16)
    assert wp.shape[-1] == _C_END
    eye = jnp.eye(A_HEADS, dtype=F32)
    wblk = (jnp.einsum('lhdr,hg->lhdgr', w_uk * ATTN_SCALE, eye)
            .reshape(L, A_WIDTH, A_HEADS * KV_RANK).astype(BF16))
    p = dict(
        wp=wp, wblk=wblk,
        wckvT=jnp.swapaxes(w_ckv, 1, 2).astype(BF16),
        wiwT=jnp.swapaxes(w_iw, 1, 2).astype(BF16),
        gkv=kv_norm_g.reshape(L, 1, KV_RANK),
        gkvT=jnp.broadcast_to(kv_norm_g[:, :, None], (L, KV_RANK, TM_PROJ)),
        wuvT=jnp.swapaxes(w_uv, 2, 3).astype(BF16),
        wo=w_out.astype(BF16),
        wrT=jnp.swapaxes(w_router, 1, 2).astype(BF16),
        rbias=jnp.broadcast_to(router_bias[:, :, None], (L, N_EXPERTS, TM_PROJ)),
        wgu=jnp.concatenate([w_gate, w_up], axis=-1).astype(BF16),
        wd=w_down.astype(BF16),
        sgu=jnp.concatenate([ws_gate, ws_up], axis=-1).astype(BF16),
        sd=ws_down.astype(BF16),
    )
    lbs = jnp.cumsum(jax.nn.softmax(hgrn_lb.astype(F32), axis=0), axis=0)
    lbs = jnp.clip(lbs - lbs[0:1], 0.0, 1.0 - 1e-6)
    p["llb"] = jnp.log(lbs).reshape(L, 1, B_FDIM)
    p["l1m"] = jnp.log1p(-lbs).reshape(L, 1, B_FDIM)
    return p


def kernel(x, c, w_ada, b_ada, w_in, kv_norm_g, w_uk, w_uv, rel_bias, hgrn_lb, gnorm_g, w_out, ln1_g, ln1_b,
           w_router, router_bias, w_gate, w_up, w_down, ws_gate, ws_up, ws_down, ln2_g, ln2_b):
    B, S, D = x.shape
    L = w_in.shape[0]
    alpha = (2 * L) ** 0.25
    p = _prepare_params(w_in, kv_norm_g, w_uk, w_uv, hgrn_lb, w_out, w_router, router_bias,
                        w_gate, w_up, w_down, ws_gate, ws_up, ws_down)
    mods = _adaln(c, w_ada, b_ada).reshape(L, B, 6, D)
    bn = _bias_tile(rel_bias)
    gn = gnorm_g.reshape(L, 1, B_VAL_DIM)
    ln1g, ln1b = ln1_g.reshape(L, 1, D), ln1_b.reshape(L, 1, D)
    ln2g, ln2b = ln2_g.reshape(L, 1, D), ln2_b.reshape(L, 1, D)
    for l in range(L):
        mod = mods[l]
        (qlat, ckv, ckvT, iq, ikA, ikB, iwT, hq, hk, hlf, hv, hgate) = _inproj(
            l, x, mod, p["wp"], p["wblk"], p["wckvT"], p["wiwT"], p["gkv"], p["gkvT"], p["llb"], p["l1m"])
        ya = _dsa(l, iq, iwT, qlat, ikA, ikB, ckv, ckvT, bn, p["wuvT"])
        yb = _hgrn(l, hq, hk, hlf, hv, hgate, gn)
        x1, u2, gates = _outproj(l, ya, yb, x, mod, p["wo"], ln1g, ln1b, p["wrT"], p["rbias"], alpha)
        x = _moe(l, u2.reshape(B * S, D), gates.reshape(B * S, 2 * N_EXPERTS), x1.reshape(B * S, D), mod,
                 p["wgu"], p["wd"], p["sgu"], p["sd"], ln2g, ln2b, alpha, S).reshape(B, S, D)
    return x
```

```python
import functools
import math

import numpy as np
import jax
import jax.numpy as jnp
from jax import lax
from jax.experimental import pallas as pl
from jax.experimental.pallas import tpu as pltpu

F32 = jnp.float32
BF16 = jnp.bfloat16
I32 = jnp.int32
I16 = jnp.int16

D_MODEL = 1024
CHUNK = 64
A_HEADS = 8
A_HEAD_DIM = 64
A_WIDTH = A_HEADS * A_HEAD_DIM
KV_RANK = 128
IDX_HEADS = 8
IDX_DIM = 64
IDX_TOPK_MAX = 256
IDX_W_SCALE = (IDX_HEADS ** -0.5) * (IDX_DIM ** -0.5)
ATTN_SCALE = A_HEAD_DIM ** -0.5
LOG2E = math.log2(math.e)
KV_EXT = KV_RANK + 16
NUM_BUCKETS = 32
MAX_DISTANCE = 128
B_HEADS = 4
B_KEY_DIM = 128
B_VAL_DIM = 128
B_WIDTH = B_HEADS * B_VAL_DIM
B_FDIM = B_HEADS * B_KEY_DIM
N_EXPERTS = 64
TOP_K = 8
N_GROUPS = 8
TOPK_GROUPS = 4
EXPERT_DIM = 256
SHARED_DIM = 256
ROUTED_SCALE = 2.5
LN_EPS = 1e-5
RMS_EPS = 1e-6

LANES = 128
SUBLANES = 8
VMEM_LIMIT_BYTES = 56 * 1024 * 1024

INT_MIN = -(2 ** 31)
NEG_INF = float("-inf")

TM_PROJ = 512
TQ = 128
UNIT = 512
NEAR = 2 * TQ
COUNT_ACCS = 8
TM_MOE = 1024
MOE_EXPERTS_PER_STEP = 4

_C_AQ, _C_CKV, _C_IQ, _C_IKA, _C_IKB, _C_HQ, _C_HF, _C_HG, _C_HI, _C_END = (
    0, 512, 640, 1152, 1280, 1408, 1920, 2432, 2944, 3456)


def _silu(v):
    return v * (1.0 / (1.0 + jnp.exp(-v)))


def _nt_dot(a, b):
    return lax.dot_general(a, b, (((1,), (1,)), ((), ())), preferred_element_type=F32)


def _cparams(sem):
    return pltpu.CompilerParams(dimension_semantics=sem, vmem_limit_bytes=VMEM_LIMIT_BYTES)


def _adaln_kernel(c_ref, w_ref, b_ref, o_ref):
    cond = _silu(c_ref[...])
    o_ref[0] = jnp.dot(cond.astype(BF16), w_ref[0].astype(BF16), preferred_element_type=F32) + b_ref[0]


def _adaln(c, w_ada, b_ada):
    L, D, D6 = w_ada.shape
    B = c.shape[0]
    nb = D6 // D
    return pl.pallas_call(
        _adaln_kernel,
        grid=(L, nb),
        in_specs=[
            pl.BlockSpec((B, D), lambda l, j: (0, 0)),
            pl.BlockSpec((1, D, D), lambda l, j: (l, 0, j)),
            pl.BlockSpec((1, 1, D), lambda l, j: (l, 0, j)),
        ],
        out_specs=pl.BlockSpec((1, B, D), lambda l, j: (l, 0, j)),
        out_shape=jax.ShapeDtypeStruct((L, B, D6), F32),
        compiler_params=_cparams(("arbitrary", "arbitrary")),
        name="adaln_mod",
    )(c, w_ada, b_ada.reshape(L, 1, D6))


_T5_NB = NUM_BUCKETS // 2
_T5_EXACT = _T5_NB // 2
_T5_THRESHOLDS = tuple(
    int(math.ceil(_T5_EXACT * (MAX_DISTANCE / _T5_EXACT) ** (j / (_T5_NB - _T5_EXACT)) - 1e-9))
    for j in range(1, _T5_NB - _T5_EXACT))
FAR_BUCKET = _T5_NB - 1
assert _T5_THRESHOLDS[-1] <= TQ, "keys further than one query block behind must share the far bucket"


def _bias_kernel(rb_ref, o_ref):
    kr = lax.broadcasted_iota(I32, (NEAR + TQ, TQ), 0)
    ql = lax.broadcasted_iota(I32, (NEAR + TQ, TQ), 1)
    rel = kr - TQ - ql
    n = jnp.abs(rel)
    large = jnp.full(rel.shape, _T5_EXACT, I32)
    for t in _T5_THRESHOLDS:
        large = large + (n >= t).astype(I32)
    bucket = jnp.where(rel > 0, _T5_NB, 0) + jnp.where(n < _T5_EXACT, n, large)
    for h in range(A_HEADS):
        acc = jnp.zeros(rel.shape, F32)
        for bk in range(NUM_BUCKETS):
            acc = jnp.where(bucket == bk, rb_ref[bk, h], acc)
        o_ref[h] = (acc - rb_ref[FAR_BUCKET, h]) * LOG2E


def _bias_tile(rel_bias):
    return pl.pallas_call(
        _bias_kernel,
        in_specs=[pl.BlockSpec(memory_space=pltpu.SMEM)],
        out_specs=pl.BlockSpec(memory_space=pltpu.VMEM),
        out_shape=jax.ShapeDtypeStruct((A_HEADS, NEAR + TQ, TQ), F32),
        name="rel_bias_tile",
    )(rel_bias)


def _inproj_kernel(x_ref, mod_ref, wp_ref, wblk_ref, wckvT_ref, wiwT_ref, gkv_ref, gkvT_ref, llb_ref, l1m_ref,
                   qlat_ref, ckv_ref, ckvT_ref, iq_ref, ikA_ref, ikB_ref, iwT_ref,
                   hq_ref, hk_ref, hlf_ref, hv_ref, hgate_ref):
    x = x_ref[0]
    sh1 = mod_ref[0, 0:1, :]
    sc1 = mod_ref[0, 1:2, :]
    u = (x * (1.0 + sc1) + sh1).astype(BF16)
    z = jnp.dot(u, wp_ref[0], preferred_element_type=F32)

    ql = jnp.dot(z[:, _C_AQ:_C_CKV].astype(BF16), wblk_ref[0], preferred_element_type=F32)
    for h in range(A_HEADS):
        qlat_ref[0, h] = ql[:, h * KV_RANK:(h + 1) * KV_RANK].astype(BF16)

    zc = z[:, _C_CKV:_C_IQ]
    inv = lax.rsqrt(jnp.mean(zc * zc, axis=-1, keepdims=True) + RMS_EPS)
    ckv_ref[0] = (zc * inv * gkv_ref[0]).astype(BF16)
    zt = _nt_dot(wckvT_ref[0], u)
    inv_t = lax.rsqrt(jnp.mean(zt * zt, axis=0, keepdims=True) + RMS_EPS)
    ckvT_ref[0, 0:KV_RANK, :] = (zt * inv_t * gkvT_ref[0]).astype(BF16)
    ckvT_ref[0, KV_RANK:KV_EXT, :] = jnp.ones((KV_EXT - KV_RANK, zt.shape[1]), BF16)

    for p in range(IDX_HEADS // 2):
        iq_ref[0, p] = z[:, _C_IQ + p * LANES:_C_IQ + (p + 1) * LANES].astype(BF16)
    ikA_ref[0] = z[:, _C_IKA:_C_IKB].astype(BF16)
    ikB_ref[0] = z[:, _C_IKB:_C_HQ].astype(BF16)
    iwT_ref[0] = _nt_dot(wiwT_ref[0], u) * IDX_W_SCALE

    hq_ref[0] = _silu(z[:, _C_HQ:_C_HF])
    zf = z[:, _C_HF:_C_HG]
    log_sig = jnp.minimum(zf, 0.0) - jnp.log1p(jnp.exp(-jnp.abs(zf)))
    a = llb_ref[0]
    c = l1m_ref[0] + log_sig
    logf = jnp.maximum(a, c) + jnp.log1p(jnp.exp(-jnp.abs(a - c)))
    hlf_ref[0] = logf
    hk_ref[0] = 1.0 - jnp.exp(logf)
    hgate_ref[0] = _silu(z[:, _C_HG:_C_HI])
    hv_ref[0] = z[:, _C_HI:_C_END].astype(BF16)


def _inproj(l, x, mod, wp, wblk, wckvT, wiwT, gkv, gkvT, llb, l1m):
    B, S, D = x.shape
    tm = TM_PROJ
    grid = (B, S // tm)
    lw3 = lambda b, i: (l, 0, 0)
    tok = lambda b, i: (b, i, 0)
    tokT = lambda b, i: (b, 0, i)
    hd4 = lambda b, i: (b, 0, i, 0)
    outs = [
        (jax.ShapeDtypeStruct((B, A_HEADS, S, KV_RANK), BF16), pl.BlockSpec((1, A_HEADS, tm, KV_RANK), hd4)),
        (jax.ShapeDtypeStruct((B, S, KV_RANK), BF16), pl.BlockSpec((1, tm, KV_RANK), tok)),
        (jax.ShapeDtypeStruct((B, KV_EXT, S), BF16), pl.BlockSpec((1, KV_EXT, tm), tokT)),
        (jax.ShapeDtypeStruct((B, IDX_HEADS // 2, S, LANES), BF16), pl.BlockSpec((1, IDX_HEADS // 2, tm, LANES), hd4)),
        (jax.ShapeDtypeStruct((B, S, LANES), BF16), pl.BlockSpec((1, tm, LANES), tok)),
        (jax.ShapeDtypeStruct((B, S, LANES), BF16), pl.BlockSpec((1, tm, LANES), tok)),
        (jax.ShapeDtypeStruct((B, IDX_HEADS, S), F32), pl.BlockSpec((1, IDX_HEADS, tm), tokT)),
        (jax.ShapeDtypeStruct((B, S, B_FDIM), F32), pl.BlockSpec((1, tm, B_FDIM), tok)),
        (jax.ShapeDtypeStruct((B, S, B_FDIM), F32), pl.BlockSpec((1, tm, B_FDIM), tok)),
        (jax.ShapeDtypeStruct((B, S, B_FDIM), F32), pl.BlockSpec((1, tm, B_FDIM), tok)),
        (jax.ShapeDtypeStruct((B, S, B_WIDTH), BF16), pl.BlockSpec((1, tm, B_WIDTH), tok)),
        (jax.ShapeDtypeStruct((B, S, B_WIDTH), F32), pl.BlockSpec((1, tm, B_WIDTH), tok)),
    ]
    return pl.pallas_call(
        _inproj_kernel,
        grid=grid,
        in_specs=[
            pl.BlockSpec((1, tm, D), tok),
            pl.BlockSpec((1, 6, D), lambda b, i: (b, 0, 0)),
            pl.BlockSpec((1, D, _C_END), lw3),
            pl.BlockSpec((1, A_WIDTH, A_HEADS * KV_RANK), lw3),
            pl.BlockSpec((1, KV_RANK, D), lw3),
            pl.BlockSpec((1, IDX_HEADS, D), lw3),
            pl.BlockSpec((1, 1, KV_RANK), lw3),
            pl.BlockSpec((1, KV_RANK, tm), lw3),
            pl.BlockSpec((1, 1, B_FDIM), lw3),
            pl.BlockSpec((1, 1, B_FDIM), lw3),
        ],
        out_specs=[o[1] for o in outs],
        out_shape=[o[0] for o in outs],
        compiler_params=_cparams(("arbitrary", "arbitrary")),
        name="inproj",
    )(x, mod, wp, wblk, wckvT, wiwT, gkv, gkvT, llb, l1m)


def _dsa_kernel(iq_ref, iwT_ref, qlat_ref, ikA_ref, ikB_ref, ckv_ref, ckvT_ref, bn_ref, wuvT_ref, out_ref,
                sc_ref, hi_ref, lo_ref, madd_ref, maddn_ref, la_ref, lb_ref, pma_ref, pmb_ref, ot_ref, yaT_ref,
                *, k_sel, n_idx_bits):
    j = pl.program_id(1)
    q0 = j * TQ
    nk = q0 + TQ
    nunit = (nk + UNIT - 1) // UNIT
    near0 = pl.multiple_of(jnp.maximum(nk - NEAR, 0), TQ)
    bn_row0 = pl.multiple_of(jnp.where(j == 0, TQ, 0), TQ)
    lane = lax.broadcasted_iota(I32, (1, TQ), 1)
    limit = (((q0 + lane) >> 6) + 1) << 6
    row_iota = lax.broadcasted_iota(I32, (UNIT, TQ), 0)

    def unit_rows(u):
        return pl.ds(pl.multiple_of(u * UNIT, UNIT), UNIT)

    iqs = iq_ref[0].reshape(IDX_HEADS // 2 * TQ, LANES)
    iw = iwT_ref[0]

    def score_unit(u, carry):
        rows = unit_rows(u)
        xe = _nt_dot(ikA_ref[0, rows, :], iqs)
        xo = _nt_dot(ikB_ref[0, rows, :], iqs)
        acc = jnp.zeros((UNIT, TQ), F32)
        for p in range(IDX_HEADS // 2):
            acc = acc + iw[2 * p:2 * p + 1, :] * jnp.maximum(xe[:, p * TQ:(p + 1) * TQ], 0.0)
            acc = acc + iw[2 * p + 1:2 * p + 2, :] * jnp.maximum(xo[:, p * TQ:(p + 1) * TQ], 0.0)
        bits = lax.bitcast_convert_type(acc, I32)
        key = bits ^ ((bits >> 31) & 0x7FFFFFFF)
        key = jnp.where(row_iota + u * UNIT < limit, key, INT_MIN)
        sc_ref[rows, :] = key
        hi_ref[rows, :] = (key >> 16).astype(I16)
        return carry

    lax.fori_loop(0, nunit, score_unit, 0)

    def count_where(pred):
        def body(u, acc):
            hit = pred(sc_ref[unit_rows(u), :], u * UNIT).reshape(-1, COUNT_ACCS * SUBLANES, TQ)
            for s in range(hit.shape[0]):
                acc = jnp.where(hit[s], acc + 1, acc)
            return acc
        acc = lax.fori_loop(0, nunit, body, jnp.zeros((COUNT_ACCS * SUBLANES, TQ), I32))
        return jnp.sum(acc, axis=0, keepdims=True)

    def count16(ref, pred):
        rows16 = COUNT_ACCS * 2 * SUBLANES
        def body(u, acc):
            hit = pred(ref[unit_rows(u), :]).reshape(-1, rows16, TQ)
            for s in range(hit.shape[0]):
                acc = jnp.where(hit[s], acc + jnp.int16(1), acc)
            return acc
        acc = lax.fori_loop(0, nunit, body, jnp.zeros((rows16, TQ), I16))
        return jnp.sum(acc.astype(I32), axis=0, keepdims=True)

    def search16(ref, base):
        def body(i, carry):
            u_val, cnt_val = carry
            cand_u = u_val | (jnp.int32(1) << (15 - i))
            cand = (cand_u - 2 ** 15).astype(I16)
            cnt = base + count16(ref, lambda blk: blk >= cand)
            ok = cnt >= k_sel
            return jnp.where(ok, cand_u, u_val), jnp.where(ok, cnt, cnt_val)
        return lax.fori_loop(0, 16, body, (jnp.zeros((1, TQ), I32), jnp.zeros((1, TQ), I32)))

    u_hi, c_hi = search16(hi_ref, 0)
    t_hi = u_hi - 2 ** 15
    t_hi16 = t_hi.astype(I16)
    c_above = count16(hi_ref, lambda blk: blk > t_hi16)

    def low_half_unit(u, carry):
        rows = unit_rows(u)
        key = sc_ref[rows, :]
        lo_ref[rows, :] = jnp.where((key >> 16) == t_hi, (key & 0xFFFF) - 2 ** 15, -(2 ** 15)).astype(I16)
        return carry

    lax.fori_loop(0, nunit, low_half_unit, 0)
    u_lo, c_lo = search16(lo_ref, c_above)
    c_ge = jnp.where(u_lo > 0, c_lo, c_hi)
    thr = jnp.maximum((t_hi << 16) + u_lo, INT_MIN + 1)
    straddle = (c_ge > k_sel).astype(I32)

    def tie_bound():
        c_gt = count_where(lambda blk, r0: blk > thr)
        need = k_sel - c_gt

        def tie_body(i, j0):
            cand = j0 | (jnp.int32(1) << (n_idx_bits - 1 - i))
            cnt = count_where(lambda blk, r0: jnp.where(blk == thr, row_iota + r0, cand) < cand)
            return jnp.where(cnt < need, cand, j0)

        j0 = lax.fori_loop(0, n_idx_bits, tie_body, jnp.zeros((1, TQ), I32))
        return jnp.where(straddle > 0, j0 + 1, jnp.int32(2 ** n_idx_bits))

    jstar = lax.cond(jnp.max(straddle) > 0, tie_bound, lambda: jnp.full((1, TQ), 2 ** n_idx_bits, I32))

    def madd_unit(u, carry):
        rows = unit_rows(u)
        key = sc_ref[rows, :]
        tie_keep = jnp.where(row_iota + u * UNIT < jstar, 0.0, NEG_INF)
        madd_ref[rows, :] = jnp.where(key > thr, 0.0, jnp.where(key == thr, tie_keep, NEG_INF))
        return carry

    lax.fori_loop(0, nunit, madd_unit, 0)
    maddn_ref[...] = madd_ref[pl.ds(near0, NEAR), :]
    madd_ref[pl.ds(near0, NEAR), :] = jnp.full((NEAR, TQ), NEG_INF, F32)

    qall = qlat_ref[0].reshape(A_HEADS * TQ, KV_RANK)

    def col_max(v):
        return jnp.max(v.reshape(v.shape[0] // SUBLANES, SUBLANES, A_HEADS * TQ), axis=0)

    def fold(xl, part_max, ckv_t, m_old):
        m_new = jnp.maximum(m_old, jnp.max(part_max, axis=0, keepdims=True))
        m_use = jnp.where(m_new == NEG_INF, 0.0, m_new)
        p = jnp.exp2((xl - m_use).astype(BF16))
        ot_ref[...] = ot_ref[...] * jnp.exp2(m_old - m_use) + jnp.dot(ckv_t, p, preferred_element_type=F32)
        return m_new

    ot_ref[...] = jnp.zeros(ot_ref.shape, F32)
    near_rows = pl.ds(near0, NEAR)
    xn = _nt_dot(ckv_ref[0, near_rows, :], qall) + jnp.concatenate([maddn_ref[...]] * A_HEADS, axis=1)
    xn = xn + jnp.concatenate([bn_ref[h, pl.ds(bn_row0, NEAR), :] for h in range(A_HEADS)], axis=1)
    m_run = fold(xn, col_max(xn), ckvT_ref[0, :, near_rows], jnp.full((1, A_HEADS * TQ), NEG_INF, F32))

    last_unit = sc_ref.shape[0] // UNIT - 1

    def issue_logits(u, buf_ref, pm_ref):
        rows = unit_rows(jnp.minimum(u, last_unit))
        xl = _nt_dot(ckv_ref[0, rows, :], qall) + jnp.concatenate([madd_ref[rows, :]] * A_HEADS, axis=1)
        buf_ref[...] = xl
        pm_ref[...] = col_max(xl)

    def consume_logits(u, buf_ref, pm_ref, m_old):
        return fold(buf_ref[...], pm_ref[...], ckvT_ref[0, :, unit_rows(u)], m_old)

    @pl.when(nunit % 2 == 1)
    def _():
        madd_ref[unit_rows(nunit), :] = jnp.full((UNIT, TQ), NEG_INF, F32)

    issue_logits(0, la_ref, pma_ref)

    def pair_step(i, m_old):
        issue_logits(2 * i + 1, lb_ref, pmb_ref)
        m_mid = consume_logits(2 * i, la_ref, pma_ref, m_old)
        issue_logits(2 * i + 2, la_ref, pma_ref)
        return consume_logits(2 * i + 1, lb_ref, pmb_ref, m_mid)

    lax.fori_loop(0, (nunit + 1) // 2, pair_step, m_run)
    o_t = (ot_ref[0:KV_RANK, :] * (1.0 / ot_ref[KV_RANK:KV_RANK + 1, :])).astype(BF16)
    for h in range(A_HEADS):
        yaT_ref[h * A_HEAD_DIM:(h + 1) * A_HEAD_DIM, :] = jnp.dot(
            wuvT_ref[0, h], o_t[:, h * TQ:(h + 1) * TQ], preferred_element_type=F32)

    out_ref[0] = yaT_ref[...].T.astype(BF16)


def _dsa(l, iq, iwT, qlat, ikA, ikB, ckv, ckvT, bn, wuvT):
    B, S = ckv.shape[0], ckv.shape[1]
    assert S % (2 * UNIT) == 0 and UNIT % TQ == 0 and TQ % CHUNK == 0 and CHUNK == 64 and NEAR <= UNIT
    k_sel = min(IDX_TOPK_MAX, S // 4)
    n_idx_bits = int(math.log2(S))
    assert 2 ** n_idx_bits == S
    grid = (B, S // TQ)
    blk = lambda b, i: (b, 0, i, 0)
    full = lambda b, i: (b, 0, 0)
    kern = functools.partial(_dsa_kernel, k_sel=k_sel, n_idx_bits=n_idx_bits)
    return pl.pallas_call(
        kern,
        grid=grid,
        in_specs=[
            pl.BlockSpec((1, IDX_HEADS // 2, TQ, LANES), blk),
            pl.BlockSpec((1, IDX_HEADS, TQ), lambda b, i: (b, 0, i)),
            pl.BlockSpec((1, A_HEADS, TQ, KV_RANK), blk),
            pl.BlockSpec((1, S, LANES), full),
            pl.BlockSpec((1, S, LANES), full),
            pl.BlockSpec((1, S, KV_RANK), full),
            pl.BlockSpec((1, KV_EXT, S), full),
            pl.BlockSpec((A_HEADS, NEAR + TQ, TQ), lambda b, i: (0, 0, 0)),
            pl.BlockSpec((1, A_HEADS, A_HEAD_DIM, KV_RANK), lambda b, i: (l, 0, 0, 0)),
        ],
        out_specs=pl.BlockSpec((1, TQ, A_WIDTH), lambda b, i: (b, i, 0)),
        out_shape=jax.ShapeDtypeStruct((B, S, A_WIDTH), BF16),
        scratch_shapes=[
            pltpu.VMEM((S, TQ), I32),
            pltpu.VMEM((S, TQ), I16),
            pltpu.VMEM((S, TQ), I16),
            pltpu.VMEM((S, TQ), F32),
            pltpu.VMEM((NEAR, TQ), F32),
            pltpu.VMEM((UNIT, A_HEADS * TQ), F32),
            pltpu.VMEM((UNIT, A_HEADS * TQ), F32),
            pltpu.VMEM((SUBLANES, A_HEADS * TQ), F32),
            pltpu.VMEM((SUBLANES, A_HEADS * TQ), F32),
            pltpu.VMEM((KV_EXT, A_HEADS * TQ), F32),
            pltpu.VMEM((A_WIDTH, TQ), F32),
        ],
        compiler_params=_cparams(("arbitrary", "arbitrary")),
        name="dsa_attention",
    )(iq, iwT, qlat, ikA, ikB, ckv, ckvT, bn, wuvT)


def _hgrn_constants():
    c = CHUNK
    r = np.arange(c)[:, None]
    jj = np.arange(c)[None, :]
    mats = [(jj <= r), (jj > r)]
    masks = [np.eye(c, dtype=bool)]
    m = c // 2
    while m >= 1:
        start = (r // (2 * m)) * (2 * m)
        bd = start + m - 1
        upper = r > bd
        mats.append(np.where(upper, (jj > bd) & (jj <= r), (jj > r) & (jj <= bd)))
        same_parent = (r // (2 * m)) == (jj // (2 * m))
        masks.append(same_parent & upper & (jj <= (jj // (2 * m)) * (2 * m) + m - 1))
        m //= 2
    m_all = np.concatenate(mats, axis=0).astype(np.float32)
    total = np.zeros((c, c), np.int32)
    for mk in masks:
        total += mk
    assert (total == np.tril(np.ones((c, c), np.int32))).all()
    return np.concatenate([m_all] * 3, axis=1), np.stack(masks).astype(np.float32)


_HGRN_M3, _HGRN_MASKS = _hgrn_constants()
_HGRN_LEVELS = _HGRN_MASKS.shape[0] - 1
HGRN_STEP_CHUNKS = 2


def _hgrn_kernel(q_ref, k_ref, lf_ref, v_ref, gate_ref, m3_ref, mask_ref, gn_ref, out_ref, st_ref):
    @pl.when(pl.program_id(1) == 0)
    def _():
        st_ref[...] = jnp.zeros(st_ref.shape, F32)

    c = CHUNK
    for ci in range(HGRN_STEP_CHUNKS):
        rows = slice(ci * c, (ci + 1) * c)
        g = lf_ref[0, rows, :]
        g_hi = g.astype(BF16)
        r1 = g - g_hi.astype(F32)
        g_mid = r1.astype(BF16)
        g_lo = (r1 - g_mid.astype(F32)).astype(BF16)
        sums = jnp.dot(m3_ref[...], jnp.concatenate([g_hi, g_mid, g_lo], axis=0), preferred_element_type=F32)
        e_all = jnp.exp(sums)
        for h in range(B_HEADS):
            cols = slice(h * B_KEY_DIM, (h + 1) * B_KEY_DIM)
            qh = q_ref[0, rows, cols]
            kh = k_ref[0, rows, cols]
            vh = v_ref[0, rows, cols]
            att = mask_ref[0] * _nt_dot(qh.astype(BF16), kh.astype(BF16))
            for lv in range(_HGRN_LEVELS):
                e_l = e_all[(2 + lv) * c:(3 + lv) * c, cols]
                att = att + mask_ref[lv + 1] * _nt_dot((qh * e_l).astype(BF16), (kh * e_l).astype(BF16))
            e_b = e_all[0:c, cols]
            e_rem = e_all[c:2 * c, cols]
            st = st_ref[h]
            o = jnp.dot(att.astype(BF16), vh, preferred_element_type=F32)
            o = o + _nt_dot((qh * e_b).astype(BF16), st.astype(BF16))
            upd = lax.dot_general(vh, (kh * e_rem).astype(BF16), (((0,), (0,)), ((), ())),
                                  preferred_element_type=F32)
            st_ref[h] = st * e_b[c - 1:c, :] + upd
            o = o * lax.rsqrt(jnp.mean(o * o, axis=-1, keepdims=True) + RMS_EPS) * gn_ref[0]
            out_ref[0, rows, cols] = (o * gate_ref[0, rows, cols]).astype(BF16)


def _hgrn(l, hq, hk, hlf, hv, hgate, gnorm):
    B, S, W = hq.shape
    ts = CHUNK * HGRN_STEP_CHUNKS
    tok = lambda b, i: (b, i, 0)
    return pl.pallas_call(
        _hgrn_kernel,
        grid=(B, S // ts),
        in_specs=[
            pl.BlockSpec((1, ts, W), tok),
            pl.BlockSpec((1, ts, W), tok),
            pl.BlockSpec((1, ts, W), tok),
            pl.BlockSpec((1, ts, W), tok),
            pl.BlockSpec((1, ts, W), tok),
            pl.BlockSpec(_HGRN_M3.shape, lambda b, i: (0, 0)),
            pl.BlockSpec(_HGRN_MASKS.shape, lambda b, i: (0, 0, 0)),
            pl.BlockSpec((1, 1, B_VAL_DIM), lambda b, i: (l, 0, 0)),
        ],
        out_specs=pl.BlockSpec((1, ts, W), tok),
        out_shape=jax.ShapeDtypeStruct((B, S, W), BF16),
        scratch_shapes=[pltpu.VMEM((B_HEADS, B_VAL_DIM, B_KEY_DIM), F32)],
        compiler_params=_cparams(("arbitrary", "arbitrary")),
        name="hgrn2",
    )(hq, hk, hlf, hv, hgate, jnp.asarray(_HGRN_M3, BF16), jnp.asarray(_HGRN_MASKS), gnorm)


def _layernorm(v, g, b):
    mu = jnp.mean(v, axis=-1, keepdims=True)
    d = v - mu
    var = jnp.mean(d * d, axis=-1, keepdims=True)
    return d * lax.rsqrt(var + LN_EPS) * g + b


def _first_argmax(v, idx, axes, big):
    mx = v
    for ax in axes:
        mx = jnp.max(mx, axis=ax, keepdims=True)
    pos = jnp.where(v == mx, idx, big)
    for ax in axes:
        pos = jnp.min(pos, axis=ax, keepdims=True)
    return mx, pos


def _outproj_kernel(ya_ref, yb_ref, x_ref, mod_ref, wo_ref, lng_ref, lnb_ref, wrT_ref, rbias_ref,
                    x1_ref, u2_ref, gates_ref, *, alpha):
    y = jnp.dot(ya_ref[0], wo_ref[0, 0:A_WIDTH, :], preferred_element_type=F32)
    y = y + jnp.dot(yb_ref[0], wo_ref[0, A_WIDTH:, :], preferred_element_type=F32)
    g1 = mod_ref[0, 2:3, :]
    x1 = _layernorm(alpha * x_ref[0] + (1.0 + g1) * y, lng_ref[0], lnb_ref[0])
    x1_ref[0] = x1
    u2 = (x1 * (1.0 + mod_ref[0, 4:5, :]) + mod_ref[0, 3:4, :]).astype(BF16)
    u2_ref[0] = u2

    tm = u2.shape[0]
    gsz = N_EXPERTS // N_GROUPS
    scores = 1.0 / (1.0 + jnp.exp(-_nt_dot(wrT_ref[0], u2)))
    sel = (scores + rbias_ref[0]).reshape(N_GROUPS, gsz, tm)
    scores = scores.reshape(N_GROUPS, gsz, tm)
    i_m = lax.broadcasted_iota(I32, (N_GROUPS, gsz, tm), 1)
    i_g = lax.broadcasted_iota(I32, (N_GROUPS, 1, tm), 0)
    i_e = lax.broadcasted_iota(I32, (N_GROUPS, gsz, tm), 0) * gsz + i_m
    m1, p1 = _first_argmax(sel, i_m, (1,), gsz)
    m2 = jnp.max(jnp.where(i_m == p1, NEG_INF, sel), axis=1, keepdims=True)
    gs = m1 + m2
    gmask = jnp.zeros(gs.shape, F32)
    for _ in range(TOPK_GROUPS):
        _, pg = _first_argmax(gs, i_g, (0,), N_GROUPS)
        hit = i_g == pg
        gmask = jnp.where(hit, 1.0, gmask)
        gs = jnp.where(hit, NEG_INF, gs)
    cand = jnp.where(jnp.broadcast_to(gmask, sel.shape) > 0.0, sel, NEG_INF)
    w = jnp.zeros(sel.shape, F32)
    for _ in range(TOP_K):
        _, pe = _first_argmax(cand, i_e, (1, 0), N_EXPERTS)
        hit = i_e == pe
        w = jnp.where(hit, scores, w)
        cand = jnp.where(hit, NEG_INF, cand)
    wsum = jnp.sum(jnp.sum(w, axis=1, keepdims=True), axis=0, keepdims=True)
    gates = (w / wsum * ROUTED_SCALE).reshape(N_EXPERTS, tm)
    g_hi = gates.astype(BF16).astype(F32)
    g_lo = (gates - g_hi).astype(BF16).astype(F32)
    gates_ref[0] = jnp.concatenate([g_hi, g_lo], axis=0).T.astype(BF16)


def _outproj(l, ya, yb, x, mod, wo, ln_g, ln_b, wrT, rbias, alpha):
    B, S, D = x.shape
    tm = TM_PROJ
    tok = lambda b, i: (b, i, 0)
    lw3 = lambda b, i: (l, 0, 0)
    return pl.pallas_call(
        functools.partial(_outproj_kernel, alpha=alpha),
        grid=(B, S // tm),
        in_specs=[
            pl.BlockSpec((1, tm, A_WIDTH), tok),
            pl.BlockSpec((1, tm, B_WIDTH), tok),
            pl.BlockSpec((1, tm, D), tok),
            pl.BlockSpec((1, 6, D), lambda b, i: (b, 0, 0)),
            pl.BlockSpec((1, D, D), lw3),
            pl.BlockSpec((1, 1, D), lw3),
            pl.BlockSpec((1, 1, D), lw3),
            pl.BlockSpec((1, N_EXPERTS, D), lw3),
            pl.BlockSpec((1, N_EXPERTS, tm), lw3),
        ],
        out_specs=[pl.BlockSpec((1, tm, D), tok), pl.BlockSpec((1, tm, D), tok),
                   pl.BlockSpec((1, tm, 2 * N_EXPERTS), tok)],
        out_shape=[jax.ShapeDtypeStruct((B, S, D), F32), jax.ShapeDtypeStruct((B, S, D), BF16),
                   jax.ShapeDtypeStruct((B, S, 2 * N_EXPERTS), BF16)],
        compiler_params=_cparams(("arbitrary", "arbitrary")),
        name="outproj_router",
    )(ya, yb, x, mod, wo, ln_g, ln_b, wrT, rbias)


def _moe_kernel(u_ref, gates_ref, x1_ref, mod_ref, wgu_ref, wd_ref, sgu_ref, sd_ref, lng_ref, lnb_ref,
                out_ref, acc_ref, *, alpha):
    s = pl.program_id(1)
    u = u_ref[...]

    def hidden(wgu):
        hgu = jnp.dot(u, wgu, preferred_element_type=F32)
        return _silu(hgu[:, :EXPERT_DIM]) * hgu[:, EXPERT_DIM:]

    @pl.when(s == 0)
    def _():
        acc_ref[...] = jnp.dot(hidden(sgu_ref[0]).astype(BF16), sd_ref[0], preferred_element_type=F32)

    rows = lax.broadcasted_iota(I32, (2 * N_EXPERTS, MOE_EXPERTS_PER_STEP * EXPERT_DIM), 0) & (N_EXPERTS - 1)
    cols = lax.broadcasted_iota(I32, (2 * N_EXPERTS, MOE_EXPERTS_PER_STEP * EXPERT_DIM), 1)
    onehot = jnp.where(rows == s * MOE_EXPERTS_PER_STEP + cols // EXPERT_DIM, 1.0, 0.0).astype(BF16)
    gate = jnp.dot(gates_ref[...], onehot, preferred_element_type=F32)
    h = jnp.concatenate(
        [(hidden(wgu_ref[0, k]) * gate[:, k * EXPERT_DIM:(k + 1) * EXPERT_DIM]).astype(BF16)
         for k in range(MOE_EXPERTS_PER_STEP)], axis=1)
    wd = wd_ref[0].reshape(MOE_EXPERTS_PER_STEP * EXPERT_DIM, wd_ref.shape[-1])
    acc_ref[...] += jnp.dot(h, wd, preferred_element_type=F32)

    @pl.when(s == pl.num_programs(1) - 1)
    def _():
        g2 = mod_ref[0, 5:6, :]
        out_ref[...] = _layernorm(alpha * x1_ref[...] + (1.0 + g2) * acc_ref[...], lng_ref[0], lnb_ref[0])


def _moe(l, u2, gates, x1, mod, wgu, wd, sgu, sd, ln_g, ln_b, alpha, seq):
    T, D = u2.shape
    tm = TM_MOE
    assert seq % tm == 0
    tok = lambda i, e: (i, 0)
    lw3 = lambda i, e: (l, 0, 0)
    return pl.pallas_call(
        functools.partial(_moe_kernel, alpha=alpha),
        grid=(T // tm, N_EXPERTS // MOE_EXPERTS_PER_STEP),
        in_specs=[
            pl.BlockSpec((tm, D), tok),
            pl.BlockSpec((tm, 2 * N_EXPERTS), tok),
            pl.BlockSpec((tm, D), tok),
            pl.BlockSpec((1, 6, D), lambda i, e: ((i * tm) // seq, 0, 0)),
            pl.BlockSpec((1, MOE_EXPERTS_PER_STEP, D, 2 * EXPERT_DIM), lambda i, e: (l, e, 0, 0)),
            pl.BlockSpec((1, MOE_EXPERTS_PER_STEP, EXPERT_DIM, D), lambda i, e: (l, e, 0, 0)),
            pl.BlockSpec((1, D, 2 * SHARED_DIM), lw3),
            pl.BlockSpec((1, SHARED_DIM, D), lw3),
            pl.BlockSpec((1, 1, D), lw3),
            pl.BlockSpec((1, 1, D), lw3),
        ],
        out_specs=pl.BlockSpec((tm, D), tok),
        out_shape=jax.ShapeDtypeStruct((T, D), F32),
        scratch_shapes=[pltpu.VMEM((tm, D), F32)],
        compiler_params=_cparams(("arbitrary", "arbitrary")),
        name="moe_dense",
    )(u2, gates, x1, mod, wgu, wd, sgu, sd, ln_g, ln_b)


def _prepare_params(w_in, kv_norm_g, w_uk, w_uv, hgrn_lb, w_out, w_router, router_bias,
                    w_gate, w_up, w_down, ws_gate, ws_up, ws_down):
    L = w_in.shape[0]
    sizes = (A_WIDTH, KV_RANK, IDX_HEADS * IDX_DIM, IDX_DIM, IDX_HEADS, B_FDIM, B_FDIM, B_WIDTH, B_WIDTH)
    offs = np.concatenate([[0], np.cumsum(sizes)])
    seg = lambda i: w_in[:, :, offs[i]:offs[i + 1]]
    w_aq, w_ckv, w_iq, w_ik, w_iw, w_hq, w_hf, w_hi, w_hg = (seg(i) for i in range(9))
    zik = jnp.zeros_like(w_ik)
    wp = jnp.concatenate([w_aq, w_ckv, w_iq, w_ik, zik, zik, w_ik, w_hq, w_hf, w_hg, w_hi], axis=-1).astype(BF16)
    assert wp.shape[-1] == _C_END
    eye = jnp.eye(A_HEADS, dtype=F32)
    wblk = (jnp.einsum('lhdr,hg->lhdgr', w_uk * (ATTN_SCALE * LOG2E), eye)
            .reshape(L, A_WIDTH, A_HEADS * KV_RANK).astype(BF16))
    p = dict(
        wp=wp, wblk=wblk,
        wckvT=jnp.swapaxes(w_ckv, 1, 2).astype(BF16),
        wiwT=jnp.swapaxes(w_iw, 1, 2).astype(BF16),
        gkv=kv_norm_g.reshape(L, 1, KV_RANK),
        gkvT=jnp.broadcast_to(kv_norm_g[:, :, None], (L, KV_RANK, TM_PROJ)),
        wuvT=jnp.swapaxes(w_uv, 2, 3).astype(BF16),
        wo=w_out.astype(BF16),
        wrT=jnp.swapaxes(w_router, 1, 2).astype(BF16),
        rbias=jnp.broadcast_to(router_bias[:, :, None], (L, N_EXPERTS, TM_PROJ)),
        wgu=jnp.concatenate([w_gate, w_up], axis=-1).astype(BF16),
        wd=w_down.astype(BF16),
        sgu=jnp.concatenate([ws_gate, ws_up], axis=-1).astype(BF16),
        sd=ws_down.astype(BF16),
    )
    lbs = jnp.cumsum(jax.nn.softmax(hgrn_lb.astype(F32), axis=0), axis=0)
    lbs = jnp.clip(lbs - lbs[0:1], 0.0, 1.0 - 1e-6)
    p["llb"] = jnp.log(lbs).reshape(L, 1, B_FDIM)
    p["l1m"] = jnp.log1p(-lbs).reshape(L, 1, B_FDIM)
    return p


def kernel(x, c, w_ada, b_ada, w_in, kv_norm_g, w_uk, w_uv, rel_bias, hgrn_lb, gnorm_g, w_out, ln1_g, ln1_b,
           w_router, router_bias, w_gate, w_up, w_down, ws_gate, ws_up, ws_down, ln2_g, ln2_b):
    B, S, D = x.shape
    L = w_in.shape[0]
    alpha = (2 * L) ** 0.25
    p = _prepare_params(w_in, kv_norm_g, w_uk, w_uv, hgrn_lb, w_out, w_router, router_bias,
                        w_gate, w_up, w_down, ws_gate, ws_up, ws_down)
    mods = _adaln(c, w_ada, b_ada).reshape(L, B, 6, D)
    bn = _bias_tile(rel_bias)
    gn = gnorm_g.reshape(L, 1, B_VAL_DIM)
    ln1g, ln1b = ln1_g.reshape(L, 1, D), ln1_b.reshape(L, 1, D)
    ln2g, ln2b = ln2_g.reshape(L, 1, D), ln2_b.reshape(L, 1, D)
    for l in range(L):
        mod = mods[l]
        (qlat, ckv, ckvT, iq, ikA, ikB, iwT, hq, hk, hlf, hv, hgate) = _inproj(
            l, x, mod, p["wp"], p["wblk"], p["wckvT"], p["wiwT"], p["gkv"], p["gkvT"], p["llb"], p["l1m"])
        ya = _dsa(l, iq, iwT, qlat, ikA, ikB, ckv, ckvT, bn, p["wuvT"])
        yb = _hgrn(l, hq, hk, hlf, hv, hgate, gn)
        x1, u2, gates = _outproj(l, ya, yb, x, mod, p["wo"], ln1g, ln1b, p["wrT"], p["rbias"], alpha)
        x = _moe(l, u2.reshape(B * S, D), gates.reshape(B * S, 2 * N_EXPERTS), x1.reshape(B * S, D), mod,
                 p["wgu"], p["wd"], p["sgu"], p["sd"], ln2g, ln2b, alpha, S).reshape(B, S, D)
    return x
```

```python
import functools
import math

import numpy as np
import jax
import jax.numpy as jnp
from jax import lax
from jax.experimental import pallas as pl
from jax.experimental.pallas import tpu as pltpu

F32 = jnp.float32
BF16 = jnp.bfloat16
I32 = jnp.int32

D_MODEL = 1024
CHUNK = 64
A_HEADS = 8
A_HEAD_DIM = 64
A_WIDTH = A_HEADS * A_HEAD_DIM
KV_RANK = 128
IDX_HEADS = 8
IDX_DIM = 64
IDX_TOPK_MAX = 256
IDX_W_SCALE = (IDX_HEADS ** -0.5) * (IDX_DIM ** -0.5)
ATTN_SCALE = A_HEAD_DIM ** -0.5
LOG2E = math.log2(math.e)
KV_EXT = KV_RANK + 16
NUM_BUCKETS = 32
MAX_DISTANCE = 128
B_HEADS = 4
B_KEY_DIM = 128
B_VAL_DIM = 128
B_WIDTH = B_HEADS * B_VAL_DIM
B_FDIM = B_HEADS * B_KEY_DIM
N_EXPERTS = 64
TOP_K = 8
N_GROUPS = 8
TOPK_GROUPS = 4
EXPERT_DIM = 256
SHARED_DIM = 256
ROUTED_SCALE = 2.5
LN_EPS = 1e-5
RMS_EPS = 1e-6

LANES = 128
SUBLANES = 8
VMEM_LIMIT_BYTES = 56 * 1024 * 1024

INT_MIN = -(2 ** 31)
NEG_INF = float("-inf")

TM_PROJ = 512
TQ = 128
UNIT = 512
NEAR = 2 * TQ
COUNT_ACCS = 8
PLANE_ROWS = 32 * SUBLANES
TM_MOE = 1024
MOE_EXPERTS_PER_STEP = 4

_C_AQ, _C_CKV, _C_IQ, _C_IKA, _C_IKB, _C_HQ, _C_HF, _C_HG, _C_HI, _C_END = (
    0, 512, 640, 1152, 1280, 1408, 1920, 2432, 2944, 3456)


def _silu(v):
    return v * (1.0 / (1.0 + jnp.exp(-v)))


def _nt_dot(a, b):
    return lax.dot_general(a, b, (((1,), (1,)), ((), ())), preferred_element_type=F32)


def _cparams(sem):
    return pltpu.CompilerParams(dimension_semantics=sem, vmem_limit_bytes=VMEM_LIMIT_BYTES)


def _adaln_kernel(c_ref, w_ref, b_ref, o_ref):
    cond = _silu(c_ref[...])
    o_ref[0] = jnp.dot(cond.astype(BF16), w_ref[0].astype(BF16), preferred_element_type=F32) + b_ref[0]


def _adaln(c, w_ada, b_ada):
    L, D, D6 = w_ada.shape
    B = c.shape[0]
    nb = D6 // D
    return pl.pallas_call(
        _adaln_kernel,
        grid=(L, nb),
        in_specs=[
            pl.BlockSpec((B, D), lambda l, j: (0, 0)),
            pl.BlockSpec((1, D, D), lambda l, j: (l, 0, j)),
            pl.BlockSpec((1, 1, D), lambda l, j: (l, 0, j)),
        ],
        out_specs=pl.BlockSpec((1, B, D), lambda l, j: (l, 0, j)),
        out_shape=jax.ShapeDtypeStruct((L, B, D6), F32),
        compiler_params=_cparams(("arbitrary", "arbitrary")),
        name="adaln_mod",
    )(c, w_ada, b_ada.reshape(L, 1, D6))


_T5_NB = NUM_BUCKETS // 2
_T5_EXACT = _T5_NB // 2
_T5_THRESHOLDS = tuple(
    int(math.ceil(_T5_EXACT * (MAX_DISTANCE / _T5_EXACT) ** (j / (_T5_NB - _T5_EXACT)) - 1e-9))
    for j in range(1, _T5_NB - _T5_EXACT))
FAR_BUCKET = _T5_NB - 1
assert _T5_THRESHOLDS[-1] <= TQ, "keys further than one query block behind must share the far bucket"


def _bias_kernel(rb_ref, o_ref):
    kr = lax.broadcasted_iota(I32, (NEAR + TQ, TQ), 0)
    ql = lax.broadcasted_iota(I32, (NEAR + TQ, TQ), 1)
    rel = kr - TQ - ql
    n = jnp.abs(rel)
    large = jnp.full(rel.shape, _T5_EXACT, I32)
    for t in _T5_THRESHOLDS:
        large = large + (n >= t).astype(I32)
    bucket = jnp.where(rel > 0, _T5_NB, 0) + jnp.where(n < _T5_EXACT, n, large)
    for h in range(A_HEADS):
        acc = jnp.zeros(rel.shape, F32)
        for bk in range(NUM_BUCKETS):
            acc = jnp.where(bucket == bk, rb_ref[bk, h], acc)
        o_ref[h] = (acc - rb_ref[FAR_BUCKET, h]) * LOG2E


def _bias_tile(rel_bias):
    return pl.pallas_call(
        _bias_kernel,
        in_specs=[pl.BlockSpec(memory_space=pltpu.SMEM)],
        out_specs=pl.BlockSpec(memory_space=pltpu.VMEM),
        out_shape=jax.ShapeDtypeStruct((A_HEADS, NEAR + TQ, TQ), F32),
        name="rel_bias_tile",
    )(rel_bias)


def _inproj_kernel(x_ref, mod_ref, wp_ref, wblk_ref, wckvT_ref, wiwT_ref, gkv_ref, gkvT_ref, llb_ref, l1m_ref,
                   qlat_ref, ckv_ref, ckvT_ref, iq_ref, ikA_ref, ikB_ref, iwT_ref,
                   hq_ref, hk_ref, hlf_ref, hv_ref, hgate_ref):
    x = x_ref[0]
    sh1 = mod_ref[0, 0:1, :]
    sc1 = mod_ref[0, 1:2, :]
    u = (x * (1.0 + sc1) + sh1).astype(BF16)
    z = jnp.dot(u, wp_ref[0], preferred_element_type=F32)

    ql = jnp.dot(z[:, _C_AQ:_C_CKV].astype(BF16), wblk_ref[0], preferred_element_type=F32)
    for h in range(A_HEADS):
        qlat_ref[0, h] = ql[:, h * KV_RANK:(h + 1) * KV_RANK].astype(BF16)

    zc = z[:, _C_CKV:_C_IQ]
    inv = lax.rsqrt(jnp.mean(zc * zc, axis=-1, keepdims=True) + RMS_EPS)
    ckv_ref[0] = (zc * inv * gkv_ref[0]).astype(BF16)
    zt = _nt_dot(wckvT_ref[0], u)
    inv_t = lax.rsqrt(jnp.mean(zt * zt, axis=0, keepdims=True) + RMS_EPS)
    ckvT_ref[0, 0:KV_RANK, :] = (zt * inv_t * gkvT_ref[0]).astype(BF16)
    ckvT_ref[0, KV_RANK:KV_EXT, :] = jnp.ones((KV_EXT - KV_RANK, zt.shape[1]), BF16)

    for p in range(IDX_HEADS // 2):
        iq_ref[0, p] = z[:, _C_IQ + p * LANES:_C_IQ + (p + 1) * LANES].astype(BF16)
    ikA_ref[0] = z[:, _C_IKA:_C_IKB].astype(BF16)
    ikB_ref[0] = z[:, _C_IKB:_C_HQ].astype(BF16)
    iwT_ref[0] = _nt_dot(wiwT_ref[0], u) * IDX_W_SCALE

    hq_ref[0] = _silu(z[:, _C_HQ:_C_HF])
    zf = z[:, _C_HF:_C_HG]
    log_sig = jnp.minimum(zf, 0.0) - jnp.log1p(jnp.exp(-jnp.abs(zf)))
    a = llb_ref[0]
    c = l1m_ref[0] + log_sig
    logf = jnp.maximum(a, c) + jnp.log1p(jnp.exp(-jnp.abs(a - c)))
    hlf_ref[0] = logf
    hk_ref[0] = 1.0 - jnp.exp(logf)
    hgate_ref[0] = _silu(z[:, _C_HG:_C_HI])
    hv_ref[0] = z[:, _C_HI:_C_END].astype(BF16)


def _inproj(l, x, mod, wp, wblk, wckvT, wiwT, gkv, gkvT, llb, l1m):
    B, S, D = x.shape
    tm = TM_PROJ
    grid = (B, S // tm)
    lw3 = lambda b, i: (l, 0, 0)
    tok = lambda b, i: (b, i, 0)
    tokT = lambda b, i: (b, 0, i)
    hd4 = lambda b, i: (b, 0, i, 0)
    outs = [
        (jax.ShapeDtypeStruct((B, A_HEADS, S, KV_RANK), BF16), pl.BlockSpec((1, A_HEADS, tm, KV_RANK), hd4)),
        (jax.ShapeDtypeStruct((B, S, KV_RANK), BF16), pl.BlockSpec((1, tm, KV_RANK), tok)),
        (jax.ShapeDtypeStruct((B, KV_EXT, S), BF16), pl.BlockSpec((1, KV_EXT, tm), tokT)),
        (jax.ShapeDtypeStruct((B, IDX_HEADS // 2, S, LANES), BF16), pl.BlockSpec((1, IDX_HEADS // 2, tm, LANES), hd4)),
        (jax.ShapeDtypeStruct((B, S, LANES), BF16), pl.BlockSpec((1, tm, LANES), tok)),
        (jax.ShapeDtypeStruct((B, S, LANES), BF16), pl.BlockSpec((1, tm, LANES), tok)),
        (jax.ShapeDtypeStruct((B, IDX_HEADS, S), F32), pl.BlockSpec((1, IDX_HEADS, tm), tokT)),
        (jax.ShapeDtypeStruct((B, S, B_FDIM), F32), pl.BlockSpec((1, tm, B_FDIM), tok)),
        (jax.ShapeDtypeStruct((B, S, B_FDIM), F32), pl.BlockSpec((1, tm, B_FDIM), tok)),
        (jax.ShapeDtypeStruct((B, S, B_FDIM), F32), pl.BlockSpec((1, tm, B_FDIM), tok)),
        (jax.ShapeDtypeStruct((B, S, B_WIDTH), BF16), pl.BlockSpec((1, tm, B_WIDTH), tok)),
        (jax.ShapeDtypeStruct((B, S, B_WIDTH), F32), pl.BlockSpec((1, tm, B_WIDTH), tok)),
    ]
    return pl.pallas_call(
        _inproj_kernel,
        grid=grid,
        in_specs=[
            pl.BlockSpec((1, tm, D), tok),
            pl.BlockSpec((1, 6, D), lambda b, i: (b, 0, 0)),
            pl.BlockSpec((1, D, _C_END), lw3),
            pl.BlockSpec((1, A_WIDTH, A_HEADS * KV_RANK), lw3),
            pl.BlockSpec((1, KV_RANK, D), lw3),
            pl.BlockSpec((1, IDX_HEADS, D), lw3),
            pl.BlockSpec((1, 1, KV_RANK), lw3),
            pl.BlockSpec((1, KV_RANK, tm), lw3),
            pl.BlockSpec((1, 1, B_FDIM), lw3),
            pl.BlockSpec((1, 1, B_FDIM), lw3),
        ],
        out_specs=[o[1] for o in outs],
        out_shape=[o[0] for o in outs],
        compiler_params=_cparams(("arbitrary", "arbitrary")),
        name="inproj",
    )(x, mod, wp, wblk, wckvT, wiwT, gkv, gkvT, llb, l1m)


def _dsa_kernel(iq_ref, iwT_ref, qlat_ref, ikA_ref, ikB_ref, ckv_ref, ckvT_ref, bn_ref, wuvT_ref, out_ref,
                sc_ref, plane_ref, madd_ref, maddn_ref, la_ref, lb_ref, pma_ref, pmb_ref, ot_ref, yaT_ref,
                *, k_sel, n_idx_bits):
    j = pl.program_id(1)
    q0 = j * TQ
    nk = q0 + TQ
    nunit = (nk + UNIT - 1) // UNIT
    near0 = pl.multiple_of(jnp.maximum(nk - NEAR, 0), TQ)
    bn_row0 = pl.multiple_of(jnp.where(j == 0, TQ, 0), TQ)
    lane = lax.broadcasted_iota(I32, (1, TQ), 1)
    limit = (((q0 + lane) >> 6) + 1) << 6
    row_iota = lax.broadcasted_iota(I32, (UNIT, TQ), 0)

    def unit_rows(u):
        return pl.ds(pl.multiple_of(u * UNIT, UNIT), UNIT)

    iqs = iq_ref[0].reshape(IDX_HEADS // 2 * TQ, LANES)
    iw = iwT_ref[0]

    def score_unit(u, carry):
        rows = unit_rows(u)
        xe = _nt_dot(ikA_ref[0, rows, :], iqs)
        xo = _nt_dot(ikB_ref[0, rows, :], iqs)
        acc = jnp.zeros((UNIT, TQ), F32)
        for p in range(IDX_HEADS // 2):
            acc = acc + iw[2 * p:2 * p + 1, :] * jnp.maximum(xe[:, p * TQ:(p + 1) * TQ], 0.0)
            acc = acc + iw[2 * p + 1:2 * p + 2, :] * jnp.maximum(xo[:, p * TQ:(p + 1) * TQ], 0.0)
        bits = lax.bitcast_convert_type(acc, I32)
        key = bits ^ ((bits >> 31) & 0x7FFFFFFF)
        key = jnp.where(row_iota + u * UNIT < limit, key, INT_MIN)
        sc_ref[rows, :] = key
        return carry

    lax.fori_loop(0, nunit, score_unit, 0)

    ngroups = (nk + PLANE_ROWS - 1) // PLANE_ROWS

    def plane_group(g, carry):
        rows = pl.ds(pl.multiple_of(g * PLANE_ROWS, PLANE_ROWS), PLANE_ROWS)
        words = (sc_ref[rows, :] ^ INT_MIN).reshape(32, SUBLANES, TQ)
        w = [words[i] for i in range(32)]
        j, m = 16, 0x0000FFFF
        while j:
            mask = np.int32(np.uint32(m).view(np.int32))
            k = 0
            while k < 32:
                t = (w[k] ^ lax.shift_right_logical(w[k + j], jnp.full(w[k].shape, j, I32))) & mask
                w[k] = w[k] ^ t
                w[k + j] = w[k + j] ^ (t << j)
                k = (k + j + 1) & ~j
            j >>= 1
            m = (m ^ (m << j)) & 0xFFFFFFFF
        for i in range(32):
            plane_ref[i, pl.ds(g * SUBLANES, SUBLANES), :] = w[i]
        return carry

    lax.fori_loop(0, ngroups, plane_group, 0)

    n_words = sc_ref.shape[0] // PLANE_ROWS * SUBLANES
    group_of_word = lax.broadcasted_iota(I32, (n_words, TQ), 0) // SUBLANES

    def bit_step(i, carry):
        alive, above, t_off, c_ge = carry
        hit = alive & plane_ref[i]
        cnt = above + jnp.sum(lax.population_count(hit), axis=0, keepdims=True)
        ok = cnt >= k_sel
        alive = jnp.where(ok, hit, alive ^ hit)
        above = jnp.where(ok, above, cnt)
        t_off = jnp.where(ok, t_off | (jnp.int32(1) << (31 - i)), t_off)
        return alive, above, t_off, jnp.where(ok, cnt, c_ge)

    zero_row = jnp.zeros((1, TQ), I32)
    _, _, t_off, c_ge = lax.fori_loop(
        0, 32, bit_step,
        (jnp.where(group_of_word < ngroups, jnp.int32(-1), jnp.int32(0)), zero_row, zero_row, zero_row))
    thr = jnp.maximum(t_off ^ INT_MIN, INT_MIN + 1)
    straddle = (c_ge > k_sel).astype(I32)

    def count_where(pred):
        def body(u, acc):
            hit = pred(sc_ref[unit_rows(u), :], u * UNIT).reshape(-1, COUNT_ACCS * SUBLANES, TQ)
            for s in range(hit.shape[0]):
                acc = jnp.where(hit[s], acc + 1, acc)
            return acc
        acc = lax.fori_loop(0, nunit, body, jnp.zeros((COUNT_ACCS * SUBLANES, TQ), I32))
        return jnp.sum(acc, axis=0, keepdims=True)

    def tie_bound():
        c_gt = count_where(lambda blk, r0: blk > thr)
        need = k_sel - c_gt

        def tie_body(i, j0):
            cand = j0 | (jnp.int32(1) << (n_idx_bits - 1 - i))
            cnt = count_where(lambda blk, r0: jnp.where(blk == thr, row_iota + r0, cand) < cand)
            return jnp.where(cnt < need, cand, j0)

        j0 = lax.fori_loop(0, n_idx_bits, tie_body, jnp.zeros((1, TQ), I32))
        return jnp.where(straddle > 0, j0 + 1, jnp.int32(2 ** n_idx_bits))

    jstar = lax.cond(jnp.max(straddle) > 0, tie_bound, lambda: jnp.full((1, TQ), 2 ** n_idx_bits, I32))

    def madd_unit(u, carry):
        rows = unit_rows(u)
        key = sc_ref[rows, :]
        tie_keep = jnp.where(row_iota + u * UNIT < jstar, 0.0, NEG_INF)
        madd_ref[rows, :] = jnp.where(key > thr, 0.0, jnp.where(key == thr, tie_keep, NEG_INF))
        return carry

    lax.fori_loop(0, nunit, madd_unit, 0)
    maddn_ref[...] = madd_ref[pl.ds(near0, NEAR), :]
    madd_ref[pl.ds(near0, NEAR), :] = jnp.full((NEAR, TQ), NEG_INF, F32)

    qall = qlat_ref[0].reshape(A_HEADS * TQ, KV_RANK)

    def col_max(v):
        return jnp.max(v.reshape(v.shape[0] // SUBLANES, SUBLANES, A_HEADS * TQ), axis=0)

    def fold(xl, part_max, ckv_t, m_old):
        m_new = jnp.maximum(m_old, jnp.max(part_max, axis=0, keepdims=True))
        m_use = jnp.where(m_new == NEG_INF, 0.0, m_new)
        p = jnp.exp2((xl - m_use).astype(BF16))
        ot_ref[...] = ot_ref[...] * jnp.exp2(m_old - m_use) + jnp.dot(ckv_t, p, preferred_element_type=F32)
        return m_new

    ot_ref[...] = jnp.zeros(ot_ref.shape, F32)
    near_rows = pl.ds(near0, NEAR)
    xn = _nt_dot(ckv_ref[0, near_rows, :], qall) + jnp.concatenate([maddn_ref[...]] * A_HEADS, axis=1)
    xn = xn + jnp.concatenate([bn_ref[h, pl.ds(bn_row0, NEAR), :] for h in range(A_HEADS)], axis=1)
    m_run = fold(xn, col_max(xn), ckvT_ref[0, :, near_rows], jnp.full((1, A_HEADS * TQ), NEG_INF, F32))

    last_unit = sc_ref.shape[0] // UNIT - 1

    def issue_logits(u, buf_ref, pm_ref):
        rows = unit_rows(jnp.minimum(u, last_unit))
        xl = _nt_dot(ckv_ref[0, rows, :], qall) + jnp.concatenate([madd_ref[rows, :]] * A_HEADS, axis=1)
        buf_ref[...] = xl
        pm_ref[...] = col_max(xl)

    def consume_logits(u, buf_ref, pm_ref, m_old):
        return fold(buf_ref[...], pm_ref[...], ckvT_ref[0, :, unit_rows(u)], m_old)

    @pl.when(nunit % 2 == 1)
    def _():
        madd_ref[unit_rows(nunit), :] = jnp.full((UNIT, TQ), NEG_INF, F32)

    issue_logits(0, la_ref, pma_ref)

    def pair_step(i, m_old):
        issue_logits(2 * i + 1, lb_ref, pmb_ref)
        m_mid = consume_logits(2 * i, la_ref, pma_ref, m_old)
        issue_logits(2 * i + 2, la_ref, pma_ref)
        return consume_logits(2 * i + 1, lb_ref, pmb_ref, m_mid)

    lax.fori_loop(0, (nunit + 1) // 2, pair_step, m_run)
    o_t = (ot_ref[0:KV_RANK, :] * (1.0 / ot_ref[KV_RANK:KV_RANK + 1, :])).astype(BF16)
    for h in range(A_HEADS):
        yaT_ref[h * A_HEAD_DIM:(h + 1) * A_HEAD_DIM, :] = jnp.dot(
            wuvT_ref[0, h], o_t[:, h * TQ:(h + 1) * TQ], preferred_element_type=F32)

    out_ref[0] = yaT_ref[...].T.astype(BF16)


def _dsa(l, iq, iwT, qlat, ikA, ikB, ckv, ckvT, bn, wuvT):
    B, S = ckv.shape[0], ckv.shape[1]
    assert S % (2 * UNIT) == 0 and UNIT % TQ == 0 and TQ % CHUNK == 0 and CHUNK == 64 and NEAR <= UNIT
    k_sel = min(IDX_TOPK_MAX, S // 4)
    n_idx_bits = int(math.log2(S))
    assert 2 ** n_idx_bits == S
    grid = (B, S // TQ)
    blk = lambda b, i: (b, 0, i, 0)
    full = lambda b, i: (b, 0, 0)
    kern = functools.partial(_dsa_kernel, k_sel=k_sel, n_idx_bits=n_idx_bits)
    return pl.pallas_call(
        kern,
        grid=grid,
        in_specs=[
            pl.BlockSpec((1, IDX_HEADS // 2, TQ, LANES), blk),
            pl.BlockSpec((1, IDX_HEADS, TQ), lambda b, i: (b, 0, i)),
            pl.BlockSpec((1, A_HEADS, TQ, KV_RANK), blk),
            pl.BlockSpec((1, S, LANES), full),
            pl.BlockSpec((1, S, LANES), full),
            pl.BlockSpec((1, S, KV_RANK), full),
            pl.BlockSpec((1, KV_EXT, S), full),
            pl.BlockSpec((A_HEADS, NEAR + TQ, TQ), lambda b, i: (0, 0, 0)),
            pl.BlockSpec((1, A_HEADS, A_HEAD_DIM, KV_RANK), lambda b, i: (l, 0, 0, 0)),
        ],
        out_specs=pl.BlockSpec((1, TQ, A_WIDTH), lambda b, i: (b, i, 0)),
        out_shape=jax.ShapeDtypeStruct((B, S, A_WIDTH), BF16),
        scratch_shapes=[
            pltpu.VMEM((S, TQ), I32),
            pltpu.VMEM((32, S // PLANE_ROWS * SUBLANES, TQ), I32),
            pltpu.VMEM((S, TQ), F32),
            pltpu.VMEM((NEAR, TQ), F32),
            pltpu.VMEM((UNIT, A_HEADS * TQ), F32),
            pltpu.VMEM((UNIT, A_HEADS * TQ), F32),
            pltpu.VMEM((SUBLANES, A_HEADS * TQ), F32),
            pltpu.VMEM((SUBLANES, A_HEADS * TQ), F32),
            pltpu.VMEM((KV_EXT, A_HEADS * TQ), F32),
            pltpu.VMEM((A_WIDTH, TQ), F32),
        ],
        compiler_params=_cparams(("arbitrary", "arbitrary")),
        name="dsa_attention",
    )(iq, iwT, qlat, ikA, ikB, ckv, ckvT, bn, wuvT)


def _hgrn_constants():
    c = CHUNK
    r = np.arange(c)[:, None]
    jj = np.arange(c)[None, :]
    mats = [(jj <= r), (jj > r)]
    masks = [np.eye(c, dtype=bool)]
    m = c // 2
    while m >= 1:
        start = (r // (2 * m)) * (2 * m)
        bd = start + m - 1
        upper = r > bd
        mats.append(np.where(upper, (jj > bd) & (jj <= r), (jj > r) & (jj <= bd)))
        same_parent = (r // (2 * m)) == (jj // (2 * m))
        masks.append(same_parent & upper & (jj <= (jj // (2 * m)) * (2 * m) + m - 1))
        m //= 2
    m_all = np.concatenate(mats, axis=0).astype(np.float32)
    total = np.zeros((c, c), np.int32)
    for mk in masks:
        total += mk
    assert (total == np.tril(np.ones((c, c), np.int32))).all()
    return np.concatenate([m_all] * 3, axis=1), np.stack(masks).astype(np.float32)


_HGRN_M3, _HGRN_MASKS = _hgrn_constants()
_HGRN_LEVELS = _HGRN_MASKS.shape[0] - 1
HGRN_STEP_CHUNKS = 2


def _hgrn_kernel(q_ref, k_ref, lf_ref, v_ref, gate_ref, m3_ref, mask_ref, gn_ref, out_ref, st_ref):
    @pl.when(pl.program_id(1) == 0)
    def _():
        st_ref[...] = jnp.zeros(st_ref.shape, F32)

    c = CHUNK
    for ci in range(HGRN_STEP_CHUNKS):
        rows = slice(ci * c, (ci + 1) * c)
        g = lf_ref[0, rows, :]
        g_hi = g.astype(BF16)
        r1 = g - g_hi.astype(F32)
        g_mid = r1.astype(BF16)
        g_lo = (r1 - g_mid.astype(F32)).astype(BF16)
        sums = jnp.dot(m3_ref[...], jnp.concatenate([g_hi, g_mid, g_lo], axis=0), preferred_element_type=F32)
        e_all = jnp.exp(sums)
        for h in range(B_HEADS):
            cols = slice(h * B_KEY_DIM, (h + 1) * B_KEY_DIM)
            qh = q_ref[0, rows, cols]
            kh = k_ref[0, rows, cols]
            vh = v_ref[0, rows, cols]
            att = mask_ref[0] * _nt_dot(qh.astype(BF16), kh.astype(BF16))
            for lv in range(_HGRN_LEVELS):
                e_l = e_all[(2 + lv) * c:(3 + lv) * c, cols]
                att = att + mask_ref[lv + 1] * _nt_dot((qh * e_l).astype(BF16), (kh * e_l).astype(BF16))
            e_b = e_all[0:c, cols]
            e_rem = e_all[c:2 * c, cols]
            st = st_ref[h]
            o = jnp.dot(att.astype(BF16), vh, preferred_element_type=F32)
            o = o + _nt_dot((qh * e_b).astype(BF16), st.astype(BF16))
            upd = lax.dot_general(vh, (kh * e_rem).astype(BF16), (((0,), (0,)), ((), ())),
                                  preferred_element_type=F32)
            st_ref[h] = st * e_b[c - 1:c, :] + upd
            o = o * lax.rsqrt(jnp.mean(o * o, axis=-1, keepdims=True) + RMS_EPS) * gn_ref[0]
            out_ref[0, rows, cols] = (o * gate_ref[0, rows, cols]).astype(BF16)


def _hgrn(l, hq, hk, hlf, hv, hgate, gnorm):
    B, S, W = hq.shape
    ts = CHUNK * HGRN_STEP_CHUNKS
    tok = lambda b, i: (b, i, 0)
    return pl.pallas_call(
        _hgrn_kernel,
        grid=(B, S // ts),
        in_specs=[
            pl.BlockSpec((1, ts, W), tok),
            pl.BlockSpec((1, ts, W), tok),
            pl.BlockSpec((1, ts, W), tok),
            pl.BlockSpec((1, ts, W), tok),
            pl.BlockSpec((1, ts, W), tok),
            pl.BlockSpec(_HGRN_M3.shape, lambda b, i: (0, 0)),
            pl.BlockSpec(_HGRN_MASKS.shape, lambda b, i: (0, 0, 0)),
            pl.BlockSpec((1, 1, B_VAL_DIM), lambda b, i: (l, 0, 0)),
        ],
        out_specs=pl.BlockSpec((1, ts, W), tok),
        out_shape=jax.ShapeDtypeStruct((B, S, W), BF16),
        scratch_shapes=[pltpu.VMEM((B_HEADS, B_VAL_DIM, B_KEY_DIM), F32)],
        compiler_params=_cparams(("arbitrary", "arbitrary")),
        name="hgrn2",
    )(hq, hk, hlf, hv, hgate, jnp.asarray(_HGRN_M3, BF16), jnp.asarray(_HGRN_MASKS), gnorm)


def _layernorm(v, g, b):
    mu = jnp.mean(v, axis=-1, keepdims=True)
    d = v - mu
    var = jnp.mean(d * d, axis=-1, keepdims=True)
    return d * lax.rsqrt(var + LN_EPS) * g + b


def _first_argmax(v, idx, axes, big):
    mx = v
    for ax in axes:
        mx = jnp.max(mx, axis=ax, keepdims=True)
    pos = jnp.where(v == mx, idx, big)
    for ax in axes:
        pos = jnp.min(pos, axis=ax, keepdims=True)
    return mx, pos


def _outproj_kernel(ya_ref, yb_ref, x_ref, mod_ref, wo_ref, lng_ref, lnb_ref, wrT_ref, rbias_ref,
                    x1_ref, u2_ref, gates_ref, *, alpha):
    y = jnp.dot(ya_ref[0], wo_ref[0, 0:A_WIDTH, :], preferred_element_type=F32)
    y = y + jnp.dot(yb_ref[0], wo_ref[0, A_WIDTH:, :], preferred_element_type=F32)
    g1 = mod_ref[0, 2:3, :]
    x1 = _layernorm(alpha * x_ref[0] + (1.0 + g1) * y, lng_ref[0], lnb_ref[0])
    x1_ref[0] = x1
    u2 = (x1 * (1.0 + mod_ref[0, 4:5, :]) + mod_ref[0, 3:4, :]).astype(BF16)
    u2_ref[0] = u2

    tm = u2.shape[0]
    gsz = N_EXPERTS // N_GROUPS
    scores = 1.0 / (1.0 + jnp.exp(-_nt_dot(wrT_ref[0], u2)))
    sel = (scores + rbias_ref[0]).reshape(N_GROUPS, gsz, tm)
    scores = scores.reshape(N_GROUPS, gsz, tm)
    i_m = lax.broadcasted_iota(I32, (N_GROUPS, gsz, tm), 1)
    i_g = lax.broadcasted_iota(I32, (N_GROUPS, 1, tm), 0)
    i_e = lax.broadcasted_iota(I32, (N_GROUPS, gsz, tm), 0) * gsz + i_m
    m1, p1 = _first_argmax(sel, i_m, (1,), gsz)
    m2 = jnp.max(jnp.where(i_m == p1, NEG_INF, sel), axis=1, keepdims=True)
    gs = m1 + m2
    gmask = jnp.zeros(gs.shape, F32)
    for _ in range(TOPK_GROUPS):
        _, pg = _first_argmax(gs, i_g, (0,), N_GROUPS)
        hit = i_g == pg
        gmask = jnp.where(hit, 1.0, gmask)
        gs = jnp.where(hit, NEG_INF, gs)
    cand = jnp.where(jnp.broadcast_to(gmask, sel.shape) > 0.0, sel, NEG_INF)
    w = jnp.zeros(sel.shape, F32)
    for _ in range(TOP_K):
        _, pe = _first_argmax(cand, i_e, (1, 0), N_EXPERTS)
        hit = i_e == pe
        w = jnp.where(hit, scores, w)
        cand = jnp.where(hit, NEG_INF, cand)
    wsum = jnp.sum(jnp.sum(w, axis=1, keepdims=True), axis=0, keepdims=True)
    gates = (w / wsum * ROUTED_SCALE).reshape(N_EXPERTS, tm)
    g_hi = gates.astype(BF16).astype(F32)
    g_lo = (gates - g_hi).astype(BF16).astype(F32)
    gates_ref[0] = jnp.concatenate([g_hi, g_lo], axis=0).T.astype(BF16)


def _outproj(l, ya, yb, x, mod, wo, ln_g, ln_b, wrT, rbias, alpha):
    B, S, D = x.shape
    tm = TM_PROJ
    tok = lambda b, i: (b, i, 0)
    lw3 = lambda b, i: (l, 0, 0)
    return pl.pallas_call(
        functools.partial(_outproj_kernel, alpha=alpha),
        grid=(B, S // tm),
        in_specs=[
            pl.BlockSpec((1, tm, A_WIDTH), tok),
            pl.BlockSpec((1, tm, B_WIDTH), tok),
            pl.BlockSpec((1, tm, D), tok),
            pl.BlockSpec((1, 6, D), lambda b, i: (b, 0, 0)),
            pl.BlockSpec((1, D, D), lw3),
            pl.BlockSpec((1, 1, D), lw3),
            pl.BlockSpec((1, 1, D), lw3),
            pl.BlockSpec((1, N_EXPERTS, D), lw3),
            pl.BlockSpec((1, N_EXPERTS, tm), lw3),
        ],
        out_specs=[pl.BlockSpec((1, tm, D), tok), pl.BlockSpec((1, tm, D), tok),
                   pl.BlockSpec((1, tm, 2 * N_EXPERTS), tok)],
        out_shape=[jax.ShapeDtypeStruct((B, S, D), F32), jax.ShapeDtypeStruct((B, S, D), BF16),
                   jax.ShapeDtypeStruct((B, S, 2 * N_EXPERTS), BF16)],
        compiler_params=_cparams(("arbitrary", "arbitrary")),
        name="outproj_router",
    )(ya, yb, x, mod, wo, ln_g, ln_b, wrT, rbias)


def _moe_kernel(u_ref, gates_ref, x1_ref, mod_ref, wgu_ref, wd_ref, sgu_ref, sd_ref, lng_ref, lnb_ref,
                out_ref, acc_ref, *, alpha):
    s = pl.program_id(1)
    u = u_ref[...]

    def hidden(wgu):
        hgu = jnp.dot(u, wgu, preferred_element_type=F32)
        return _silu(hgu[:, :EXPERT_DIM]) * hgu[:, EXPERT_DIM:]

    @pl.when(s == 0)
    def _():
        acc_ref[...] = jnp.dot(hidden(sgu_ref[0]).astype(BF16), sd_ref[0], preferred_element_type=F32)

    rows = lax.broadcasted_iota(I32, (2 * N_EXPERTS, MOE_EXPERTS_PER_STEP * EXPERT_DIM), 0) & (N_EXPERTS - 1)
    cols = lax.broadcasted_iota(I32, (2 * N_EXPERTS, MOE_EXPERTS_PER_STEP * EXPERT_DIM), 1)
    onehot = jnp.where(rows == s * MOE_EXPERTS_PER_STEP + cols // EXPERT_DIM, 1.0, 0.0).astype(BF16)
    gate = jnp.dot(gates_ref[...], onehot, preferred_element_type=F32)
    h = jnp.concatenate(
        [(hidden(wgu_ref[0, k]) * gate[:, k * EXPERT_DIM:(k + 1) * EXPERT_DIM]).astype(BF16)
         for k in range(MOE_EXPERTS_PER_STEP)], axis=1)
    wd = wd_ref[0].reshape(MOE_EXPERTS_PER_STEP * EXPERT_DIM, wd_ref.shape[-1])
    acc_ref[...] += jnp.dot(h, wd, preferred_element_type=F32)

    @pl.when(s == pl.num_programs(1) - 1)
    def _():
        g2 = mod_ref[0, 5:6, :]
        out_ref[...] = _layernorm(alpha * x1_ref[...] + (1.0 + g2) * acc_ref[...], lng_ref[0], lnb_ref[0])


def _moe(l, u2, gates, x1, mod, wgu, wd, sgu, sd, ln_g, ln_b, alpha, seq):
    T, D = u2.shape
    tm = TM_MOE
    assert seq % tm == 0
    tok = lambda i, e: (i, 0)
    lw3 = lambda i, e: (l, 0, 0)
    return pl.pallas_call(
        functools.partial(_moe_kernel, alpha=alpha),
        grid=(T // tm, N_EXPERTS // MOE_EXPERTS_PER_STEP),
        in_specs=[
            pl.BlockSpec((tm, D), tok),
            pl.BlockSpec((tm, 2 * N_EXPERTS), tok),
            pl.BlockSpec((tm, D), tok),
            pl.BlockSpec((1, 6, D), lambda i, e: ((i * tm) // seq, 0, 0)),
            pl.BlockSpec((1, MOE_EXPERTS_PER_STEP, D, 2 * EXPERT_DIM), lambda i, e: (l, e, 0, 0)),
            pl.BlockSpec((1, MOE_EXPERTS_PER_STEP, EXPERT_DIM, D), lambda i, e: (l, e, 0, 0)),
            pl.BlockSpec((1, D, 2 * SHARED_DIM), lw3),
            pl.BlockSpec((1, SHARED_DIM, D), lw3),
            pl.BlockSpec((1, 1, D), lw3),
            pl.BlockSpec((1, 1, D), lw3),
        ],
        out_specs=pl.BlockSpec((tm, D), tok),
        out_shape=jax.ShapeDtypeStruct((T, D), F32),
        scratch_shapes=[pltpu.VMEM((tm, D), F32)],
        compiler_params=_cparams(("arbitrary", "arbitrary")),
        name="moe_dense",
    )(u2, gates, x1, mod, wgu, wd, sgu, sd, ln_g, ln_b)


def _prepare_params(w_in, kv_norm_g, w_uk, w_uv, hgrn_lb, w_out, w_router, router_bias,
                    w_gate, w_up, w_down, ws_gate, ws_up, ws_down):
    L = w_in.shape[0]
    sizes = (A_WIDTH, KV_RANK, IDX_HEADS * IDX_DIM, IDX_DIM, IDX_HEADS, B_FDIM, B_FDIM, B_WIDTH, B_WIDTH)
    offs = np.concatenate([[0], np.cumsum(sizes)])
    seg = lambda i: w_in[:, :, offs[i]:offs[i + 1]]
    w_aq, w_ckv, w_iq, w_ik, w_iw, w_hq, w_hf, w_hi, w_hg = (seg(i) for i in range(9))
    zik = jnp.zeros_like(w_ik)
    wp = jnp.concatenate([w_aq, w_ckv, w_iq, w_ik, zik, zik, w_ik, w_hq, w_hf, w_hg, w_hi], axis=-1).astype(BF16)
    assert wp.shape[-1] == _C_END
    eye = jnp.eye(A_HEADS, dtype=F32)
    wblk = (jnp.einsum('lhdr,hg->lhdgr', w_uk * (ATTN_SCALE * LOG2E), eye)
            .reshape(L, A_WIDTH, A_HEADS * KV_RANK).astype(BF16))
    p = dict(
        wp=wp, wblk=wblk,
        wckvT=jnp.swapaxes(w_ckv, 1, 2).astype(BF16),
        wiwT=jnp.swapaxes(w_iw, 1, 2).astype(BF16),
        gkv=kv_norm_g.reshape(L, 1, KV_RANK),
        gkvT=jnp.broadcast_to(kv_norm_g[:, :, None], (L, KV_RANK, TM_PROJ)),
        wuvT=jnp.swapaxes(w_uv, 2, 3).astype(BF16),
        wo=w_out.astype(BF16),
        wrT=jnp.swapaxes(w_router, 1, 2).astype(BF16),
        rbias=jnp.broadcast_to(router_bias[:, :, None], (L, N_EXPERTS, TM_PROJ)),
        wgu=jnp.concatenate([w_gate, w_up], axis=-1).astype(BF16),
        wd=w_down.astype(BF16),
        sgu=jnp.concatenate([ws_gate, ws_up], axis=-1).astype(BF16),
        sd=ws_down.astype(BF16),
    )
    lbs = jnp.cumsum(jax.nn.softmax(hgrn_lb.astype(F32), axis=0), axis=0)
    lbs = jnp.clip(lbs - lbs[0:1], 0.0, 1.0 - 1e-6)
    p["llb"] = jnp.log(lbs).reshape(L, 1, B_FDIM)
    p["l1m"] = jnp.log1p(-lbs).reshape(L, 1, B_FDIM)
    return p


def kernel(x, c, w_ada, b_ada, w_in, kv_norm_g, w_uk, w_uv, rel_bias, hgrn_lb, gnorm_g, w_out, ln1_g, ln1_b,
           w_router, router_bias, w_gate, w_up, w_down, ws_gate, ws_up, ws_down, ln2_g, ln2_b):
    B, S, D = x.shape
    L = w_in.shape[0]
    alpha = (2 * L) ** 0.25
    p = _prepare_params(w_in, kv_norm_g, w_uk, w_uv, hgrn_lb, w_out, w_router, router_bias,
                        w_gate, w_up, w_down, ws_gate, ws_up, ws_down)
    mods = _adaln(c, w_ada, b_ada).reshape(L, B, 6, D)
    bn = _bias_tile(rel_bias)
    gn = gnorm_g.reshape(L, 1, B_VAL_DIM)
    ln1g, ln1b = ln1_g.reshape(L, 1, D), ln1_b.reshape(L, 1, D)
    ln2g, ln2b = ln2_g.reshape(L, 1, D), ln2_b.reshape(L, 1, D)
    for l in range(L):
        mod = mods[l]
        (qlat, ckv, ckvT, iq, ikA, ikB, iwT, hq, hk, hlf, hv, hgate) = _inproj(
            l, x, mod, p["wp"], p["wblk"], p["wckvT"], p["wiwT"], p["gkv"], p["gkvT"], p["llb"], p["l1m"])
        ya = _dsa(l, iq, iwT, qlat, ikA, ikB, ckv, ckvT, bn, p["wuvT"])
        yb = _hgrn(l, hq, hk, hlf, hv, hgate, gn)
        x1, u2, gates = _outproj(l, ya, yb, x, mod, p["wo"], ln1g, ln1b, p["wrT"], p["rbias"], alpha)
        x = _moe(l, u2.reshape(B * S, D), gates.reshape(B * S, 2 * N_EXPERTS), x1.reshape(B * S, D), mod,
                 p["wgu"], p["wd"], p["sgu"], p["sd"], ln2g, ln2b, alpha, S).reshape(B, S, D)
    return x
```

```python
import functools
import math

import numpy as np
import jax
import jax.numpy as jnp
from jax import lax
from jax.experimental import pallas as pl
from jax.experimental.pallas import tpu as pltpu

F32 = jnp.float32
BF16 = jnp.bfloat16
I32 = jnp.int32

D_MODEL = 1024
CHUNK = 64
A_HEADS = 8
A_HEAD_DIM = 64
A_WIDTH = A_HEADS * A_HEAD_DIM
KV_RANK = 128
IDX_HEADS = 8
IDX_DIM = 64
IDX_TOPK_MAX = 256
IDX_W_SCALE = (IDX_HEADS ** -0.5) * (IDX_DIM ** -0.5)
ATTN_SCALE = A_HEAD_DIM ** -0.5
LOG2E = math.log2(math.e)
KV_EXT = KV_RANK + 16
NUM_BUCKETS = 32
MAX_DISTANCE = 128
B_HEADS = 4
B_KEY_DIM = 128
B_VAL_DIM = 128
B_WIDTH = B_HEADS * B_VAL_DIM
B_FDIM = B_HEADS * B_KEY_DIM
N_EXPERTS = 64
TOP_K = 8
N_GROUPS = 8
TOPK_GROUPS = 4
EXPERT_DIM = 256
SHARED_DIM = 256
ROUTED_SCALE = 2.5
LN_EPS = 1e-5
RMS_EPS = 1e-6

LANES = 128
SUBLANES = 8
VMEM_LIMIT_BYTES = 56 * 1024 * 1024

INT_MIN = -(2 ** 31)
NEG_INF = float("-inf")

TM_PROJ = 512
TQ = 128
UNIT = 512
NEAR = 2 * TQ
COUNT_ACCS = 8
PLANE_ROWS = 32 * SUBLANES
TM_MOE = 1024
MOE_EXPERTS_PER_STEP = 4
MOE_GROUP = 256
MOE_CAP = 64
MOE_EXPERT_ROWS = 2048

_C_AQ, _C_CKV, _C_IQ, _C_IKA, _C_IKB, _C_HQ, _C_HF, _C_HG, _C_HI, _C_END = (
    0, 512, 640, 1152, 1280, 1408, 1920, 2432, 2944, 3456)


def _silu(v):
    return v * (1.0 / (1.0 + jnp.exp(-v)))


def _nt_dot(a, b):
    return lax.dot_general(a, b, (((1,), (1,)), ((), ())), preferred_element_type=F32)


def _cparams(sem):
    return pltpu.CompilerParams(dimension_semantics=sem, vmem_limit_bytes=VMEM_LIMIT_BYTES)


def _adaln_kernel(c_ref, w_ref, b_ref, o_ref):
    cond = _silu(c_ref[...])
    o_ref[0] = jnp.dot(cond.astype(BF16), w_ref[0].astype(BF16), preferred_element_type=F32) + b_ref[0]


def _adaln(c, w_ada, b_ada):
    L, D, D6 = w_ada.shape
    B = c.shape[0]
    nb = D6 // D
    return pl.pallas_call(
        _adaln_kernel,
        grid=(L, nb),
        in_specs=[
            pl.BlockSpec((B, D), lambda l, j: (0, 0)),
            pl.BlockSpec((1, D, D), lambda l, j: (l, 0, j)),
            pl.BlockSpec((1, 1, D), lambda l, j: (l, 0, j)),
        ],
        out_specs=pl.BlockSpec((1, B, D), lambda l, j: (l, 0, j)),
        out_shape=jax.ShapeDtypeStruct((L, B, D6), F32),
        compiler_params=_cparams(("arbitrary", "arbitrary")),
        name="adaln_mod",
    )(c, w_ada, b_ada.reshape(L, 1, D6))


_T5_NB = NUM_BUCKETS // 2
_T5_EXACT = _T5_NB // 2
_T5_THRESHOLDS = tuple(
    int(math.ceil(_T5_EXACT * (MAX_DISTANCE / _T5_EXACT) ** (j / (_T5_NB - _T5_EXACT)) - 1e-9))
    for j in range(1, _T5_NB - _T5_EXACT))
FAR_BUCKET = _T5_NB - 1
assert _T5_THRESHOLDS[-1] <= TQ, "keys further than one query block behind must share the far bucket"


def _bias_kernel(rb_ref, o_ref):
    kr = lax.broadcasted_iota(I32, (NEAR + TQ, TQ), 0)
    ql = lax.broadcasted_iota(I32, (NEAR + TQ, TQ), 1)
    rel = kr - TQ - ql
    n = jnp.abs(rel)
    large = jnp.full(rel.shape, _T5_EXACT, I32)
    for t in _T5_THRESHOLDS:
        large = large + (n >= t).astype(I32)
    bucket = jnp.where(rel > 0, _T5_NB, 0) + jnp.where(n < _T5_EXACT, n, large)
    for h in range(A_HEADS):
        acc = jnp.zeros(rel.shape, F32)
        for bk in range(NUM_BUCKETS):
            acc = jnp.where(bucket == bk, rb_ref[bk, h], acc)
        o_ref[h] = (acc - rb_ref[FAR_BUCKET, h]) * LOG2E


def _bias_tile(rel_bias):
    return pl.pallas_call(
        _bias_kernel,
        in_specs=[pl.BlockSpec(memory_space=pltpu.SMEM)],
        out_specs=pl.BlockSpec(memory_space=pltpu.VMEM),
        out_shape=jax.ShapeDtypeStruct((A_HEADS, NEAR + TQ, TQ), F32),
        name="rel_bias_tile",
    )(rel_bias)


def _inproj_kernel(x_ref, mod_ref, wp_ref, wblk_ref, wckvT_ref, wiwT_ref, gkv_ref, gkvT_ref, llb_ref, l1m_ref,
                   qlat_ref, ckv_ref, ckvT_ref, iq_ref, ikA_ref, ikB_ref, iwT_ref,
                   hq_ref, hk_ref, hlf_ref, hv_ref, hgate_ref):
    x = x_ref[0]
    sh1 = mod_ref[0, 0:1, :]
    sc1 = mod_ref[0, 1:2, :]
    u = (x * (1.0 + sc1) + sh1).astype(BF16)
    z = jnp.dot(u, wp_ref[0], preferred_element_type=F32)

    ql = jnp.dot(z[:, _C_AQ:_C_CKV].astype(BF16), wblk_ref[0], preferred_element_type=F32)
    for h in range(A_HEADS):
        qlat_ref[0, h] = ql[:, h * KV_RANK:(h + 1) * KV_RANK].astype(BF16)

    zc = z[:, _C_CKV:_C_IQ]
    inv = lax.rsqrt(jnp.mean(zc * zc, axis=-1, keepdims=True) + RMS_EPS)
    ckv_ref[0] = (zc * inv * gkv_ref[0]).astype(BF16)
    zt = _nt_dot(wckvT_ref[0], u)
    inv_t = lax.rsqrt(jnp.mean(zt * zt, axis=0, keepdims=True) + RMS_EPS)
    ckvT_ref[0, 0:KV_RANK, :] = (zt * inv_t * gkvT_ref[0]).astype(BF16)
    ckvT_ref[0, KV_RANK:KV_EXT, :] = jnp.ones((KV_EXT - KV_RANK, zt.shape[1]), BF16)

    for p in range(IDX_HEADS // 2):
        iq_ref[0, p] = z[:, _C_IQ + p * LANES:_C_IQ + (p + 1) * LANES].astype(BF16)
    ikA_ref[0] = z[:, _C_IKA:_C_IKB].astype(BF16)
    ikB_ref[0] = z[:, _C_IKB:_C_HQ].astype(BF16)
    iwT_ref[0] = _nt_dot(wiwT_ref[0], u) * IDX_W_SCALE

    hq_ref[0] = _silu(z[:, _C_HQ:_C_HF])
    zf = z[:, _C_HF:_C_HG]
    log_sig = jnp.minimum(zf, 0.0) - jnp.log1p(jnp.exp(-jnp.abs(zf)))
    a = llb_ref[0]
    c = l1m_ref[0] + log_sig
    logf = jnp.maximum(a, c) + jnp.log1p(jnp.exp(-jnp.abs(a - c)))
    hlf_ref[0] = logf
    hk_ref[0] = 1.0 - jnp.exp(logf)
    hgate_ref[0] = _silu(z[:, _C_HG:_C_HI])
    hv_ref[0] = z[:, _C_HI:_C_END].astype(BF16)


def _inproj(l, x, mod, wp, wblk, wckvT, wiwT, gkv, gkvT, llb, l1m):
    B, S, D = x.shape
    tm = TM_PROJ
    grid = (B, S // tm)
    lw3 = lambda b, i: (l, 0, 0)
    tok = lambda b, i: (b, i, 0)
    tokT = lambda b, i: (b, 0, i)
    hd4 = lambda b, i: (b, 0, i, 0)
    outs = [
        (jax.ShapeDtypeStruct((B, A_HEADS, S, KV_RANK), BF16), pl.BlockSpec((1, A_HEADS, tm, KV_RANK), hd4)),
        (jax.ShapeDtypeStruct((B, S, KV_RANK), BF16), pl.BlockSpec((1, tm, KV_RANK), tok)),
        (jax.ShapeDtypeStruct((B, KV_EXT, S), BF16), pl.BlockSpec((1, KV_EXT, tm), tokT)),
        (jax.ShapeDtypeStruct((B, IDX_HEADS // 2, S, LANES), BF16), pl.BlockSpec((1, IDX_HEADS // 2, tm, LANES), hd4)),
        (jax.ShapeDtypeStruct((B, S, LANES), BF16), pl.BlockSpec((1, tm, LANES), tok)),
        (jax.ShapeDtypeStruct((B, S, LANES), BF16), pl.BlockSpec((1, tm, LANES), tok)),
        (jax.ShapeDtypeStruct((B, IDX_HEADS, S), F32), pl.BlockSpec((1, IDX_HEADS, tm), tokT)),
        (jax.ShapeDtypeStruct((B, S, B_FDIM), F32), pl.BlockSpec((1, tm, B_FDIM), tok)),
        (jax.ShapeDtypeStruct((B, S, B_FDIM), F32), pl.BlockSpec((1, tm, B_FDIM), tok)),
        (jax.ShapeDtypeStruct((B, S, B_FDIM), F32), pl.BlockSpec((1, tm, B_FDIM), tok)),
        (jax.ShapeDtypeStruct((B, S, B_WIDTH), BF16), pl.BlockSpec((1, tm, B_WIDTH), tok)),
        (jax.ShapeDtypeStruct((B, S, B_WIDTH), F32), pl.BlockSpec((1, tm, B_WIDTH), tok)),
    ]
    return pl.pallas_call(
        _inproj_kernel,
        grid=grid,
        in_specs=[
            pl.BlockSpec((1, tm, D), tok),
            pl.BlockSpec((1, 6, D), lambda b, i: (b, 0, 0)),
            pl.BlockSpec((1, D, _C_END), lw3),
            pl.BlockSpec((1, A_WIDTH, A_HEADS * KV_RANK), lw3),
            pl.BlockSpec((1, KV_RANK, D), lw3),
            pl.BlockSpec((1, IDX_HEADS, D), lw3),
            pl.BlockSpec((1, 1, KV_RANK), lw3),
            pl.BlockSpec((1, KV_RANK, tm), lw3),
            pl.BlockSpec((1, 1, B_FDIM), lw3),
            pl.BlockSpec((1, 1, B_FDIM), lw3),
        ],
        out_specs=[o[1] for o in outs],
        out_shape=[o[0] for o in outs],
        compiler_params=_cparams(("arbitrary", "arbitrary")),
        name="inproj",
    )(x, mod, wp, wblk, wckvT, wiwT, gkv, gkvT, llb, l1m)


def _dsa_kernel(iq_ref, iwT_ref, qlat_ref, ikA_ref, ikB_ref, ckv_ref, ckvT_ref, bn_ref, wuvT_ref, out_ref,
                sc_ref, plane_ref, madd_ref, maddn_ref, la_ref, lb_ref, pma_ref, pmb_ref, ot_ref, yaT_ref,
                *, k_sel, n_idx_bits):
    j = pl.program_id(1)
    q0 = j * TQ
    nk = q0 + TQ
    nunit = (nk + UNIT - 1) // UNIT
    near0 = pl.multiple_of(jnp.maximum(nk - NEAR, 0), TQ)
    bn_row0 = pl.multiple_of(jnp.where(j == 0, TQ, 0), TQ)
    lane = lax.broadcasted_iota(I32, (1, TQ), 1)
    limit = (((q0 + lane) >> 6) + 1) << 6
    row_iota = lax.broadcasted_iota(I32, (UNIT, TQ), 0)

    def unit_rows(u):
        return pl.ds(pl.multiple_of(u * UNIT, UNIT), UNIT)

    iqs = iq_ref[0].reshape(IDX_HEADS // 2 * TQ, LANES)
    iw = iwT_ref[0]

    def score_unit(u, carry):
        rows = unit_rows(u)
        xe = _nt_dot(ikA_ref[0, rows, :], iqs)
        xo = _nt_dot(ikB_ref[0, rows, :], iqs)
        acc = jnp.zeros((UNIT, TQ), F32)
        for p in range(IDX_HEADS // 2):
            acc = acc + iw[2 * p:2 * p + 1, :] * jnp.maximum(xe[:, p * TQ:(p + 1) * TQ], 0.0)
            acc = acc + iw[2 * p + 1:2 * p + 2, :] * jnp.maximum(xo[:, p * TQ:(p + 1) * TQ], 0.0)
        bits = lax.bitcast_convert_type(acc, I32)
        key = bits ^ ((bits >> 31) & 0x7FFFFFFF)
        key = jnp.where(row_iota + u * UNIT < limit, key, INT_MIN)
        sc_ref[rows, :] = key
        return carry

    lax.fori_loop(0, nunit, score_unit, 0)

    ngroups = (nk + PLANE_ROWS - 1) // PLANE_ROWS

    def plane_group(g, carry):
        rows = pl.ds(pl.multiple_of(g * PLANE_ROWS, PLANE_ROWS), PLANE_ROWS)
        words = (sc_ref[rows, :] ^ INT_MIN).reshape(32, SUBLANES, TQ)
        w = [words[i] for i in range(32)]
        j, m = 16, 0x0000FFFF
        while j:
            mask = np.int32(np.uint32(m).view(np.int32))
            k = 0
            while k < 32:
                t = (w[k] ^ lax.shift_right_logical(w[k + j], jnp.full(w[k].shape, j, I32))) & mask
                w[k] = w[k] ^ t
                w[k + j] = w[k + j] ^ (t << j)
                k = (k + j + 1) & ~j
            j >>= 1
            m = (m ^ (m << j)) & 0xFFFFFFFF
        for i in range(32):
            plane_ref[i, pl.ds(g * SUBLANES, SUBLANES), :] = w[i]
        return carry

    lax.fori_loop(0, ngroups, plane_group, 0)

    n_words = sc_ref.shape[0] // PLANE_ROWS * SUBLANES
    group_of_word = lax.broadcasted_iota(I32, (n_words, TQ), 0) // SUBLANES

    def bit_step(i, carry):
        alive, above, t_off, c_ge = carry
        hit = alive & plane_ref[i]
        cnt = above + jnp.sum(lax.population_count(hit), axis=0, keepdims=True)
        ok = cnt >= k_sel
        alive = jnp.where(ok, hit, alive ^ hit)
        above = jnp.where(ok, above, cnt)
        t_off = jnp.where(ok, t_off | (jnp.int32(1) << (31 - i)), t_off)
        return alive, above, t_off, jnp.where(ok, cnt, c_ge)

    zero_row = jnp.zeros((1, TQ), I32)
    _, _, t_off, c_ge = lax.fori_loop(
        0, 32, bit_step,
        (jnp.where(group_of_word < ngroups, jnp.int32(-1), jnp.int32(0)), zero_row, zero_row, zero_row))
    thr = jnp.maximum(t_off ^ INT_MIN, INT_MIN + 1)
    straddle = (c_ge > k_sel).astype(I32)

    def count_where(pred):
        def body(u, acc):
            hit = pred(sc_ref[unit_rows(u), :], u * UNIT).reshape(-1, COUNT_ACCS * SUBLANES, TQ)
            for s in range(hit.shape[0]):
                acc = jnp.where(hit[s], acc + 1, acc)
            return acc
        acc = lax.fori_loop(0, nunit, body, jnp.zeros((COUNT_ACCS * SUBLANES, TQ), I32))
        return jnp.sum(acc, axis=0, keepdims=True)

    def tie_bound():
        c_gt = count_where(lambda blk, r0: blk > thr)
        need = k_sel - c_gt

        def tie_body(i, j0):
            cand = j0 | (jnp.int32(1) << (n_idx_bits - 1 - i))
            cnt = count_where(lambda blk, r0: jnp.where(blk == thr, row_iota + r0, cand) < cand)
            return jnp.where(cnt < need, cand, j0)

        j0 = lax.fori_loop(0, n_idx_bits, tie_body, jnp.zeros((1, TQ), I32))
        return jnp.where(straddle > 0, j0 + 1, jnp.int32(2 ** n_idx_bits))

    jstar = lax.cond(jnp.max(straddle) > 0, tie_bound, lambda: jnp.full((1, TQ), 2 ** n_idx_bits, I32))

    def madd_unit(u, carry):
        rows = unit_rows(u)
        key = sc_ref[rows, :]
        tie_keep = jnp.where(row_iota + u * UNIT < jstar, 0.0, NEG_INF)
        madd_ref[rows, :] = jnp.where(key > thr, 0.0, jnp.where(key == thr, tie_keep, NEG_INF))
        return carry

    lax.fori_loop(0, nunit, madd_unit, 0)
    maddn_ref[...] = madd_ref[pl.ds(near0, NEAR), :]
    madd_ref[pl.ds(near0, NEAR), :] = jnp.full((NEAR, TQ), NEG_INF, F32)

    qall = qlat_ref[0].reshape(A_HEADS * TQ, KV_RANK)

    def col_max(v):
        return jnp.max(v.reshape(v.shape[0] // SUBLANES, SUBLANES, A_HEADS * TQ), axis=0)

    def fold(xl, part_max, ckv_t, m_old):
        m_new = jnp.maximum(m_old, jnp.max(part_max, axis=0, keepdims=True))
        m_use = jnp.where(m_new == NEG_INF, 0.0, m_new)
        p = jnp.exp2((xl - m_use).astype(BF16))
        ot_ref[...] = ot_ref[...] * jnp.exp2(m_old - m_use) + jnp.dot(ckv_t, p, preferred_element_type=F32)
        return m_new

    ot_ref[...] = jnp.zeros(ot_ref.shape, F32)
    near_rows = pl.ds(near0, NEAR)
    xn = _nt_dot(ckv_ref[0, near_rows, :], qall) + jnp.concatenate([maddn_ref[...]] * A_HEADS, axis=1)
    xn = xn + jnp.concatenate([bn_ref[h, pl.ds(bn_row0, NEAR), :] for h in range(A_HEADS)], axis=1)
    m_run = fold(xn, col_max(xn), ckvT_ref[0, :, near_rows], jnp.full((1, A_HEADS * TQ), NEG_INF, F32))

    last_unit = sc_ref.shape[0] // UNIT - 1

    def issue_logits(u, buf_ref, pm_ref):
        rows = unit_rows(jnp.minimum(u, last_unit))
        xl = _nt_dot(ckv_ref[0, rows, :], qall) + jnp.concatenate([madd_ref[rows, :]] * A_HEADS, axis=1)
        buf_ref[...] = xl
        pm_ref[...] = col_max(xl)

    def consume_logits(u, buf_ref, pm_ref, m_old):
        return fold(buf_ref[...], pm_ref[...], ckvT_ref[0, :, unit_rows(u)], m_old)

    @pl.when(nunit % 2 == 1)
    def _():
        madd_ref[unit_rows(nunit), :] = jnp.full((UNIT, TQ), NEG_INF, F32)

    issue_logits(0, la_ref, pma_ref)

    def pair_step(i, m_old):
        issue_logits(2 * i + 1, lb_ref, pmb_ref)
        m_mid = consume_logits(2 * i, la_ref, pma_ref, m_old)
        issue_logits(2 * i + 2, la_ref, pma_ref)
        return consume_logits(2 * i + 1, lb_ref, pmb_ref, m_mid)

    lax.fori_loop(0, (nunit + 1) // 2, pair_step, m_run)
    o_t = (ot_ref[0:KV_RANK, :] * (1.0 / ot_ref[KV_RANK:KV_RANK + 1, :])).astype(BF16)
    for h in range(A_HEADS):
        yaT_ref[h * A_HEAD_DIM:(h + 1) * A_HEAD_DIM, :] = jnp.dot(
            wuvT_ref[0, h], o_t[:, h * TQ:(h + 1) * TQ], preferred_element_type=F32)

    out_ref[0] = yaT_ref[...].T.astype(BF16)


def _dsa(l, iq, iwT, qlat, ikA, ikB, ckv, ckvT, bn, wuvT):
    B, S = ckv.shape[0], ckv.shape[1]
    assert S % (2 * UNIT) == 0 and UNIT % TQ == 0 and TQ % CHUNK == 0 and CHUNK == 64 and NEAR <= UNIT
    k_sel = min(IDX_TOPK_MAX, S // 4)
    n_idx_bits = int(math.log2(S))
    assert 2 ** n_idx_bits == S
    grid = (B, S // TQ)
    blk = lambda b, i: (b, 0, i, 0)
    full = lambda b, i: (b, 0, 0)
    kern = functools.partial(_dsa_kernel, k_sel=k_sel, n_idx_bits=n_idx_bits)
    return pl.pallas_call(
        kern,
        grid=grid,
        in_specs=[
            pl.BlockSpec((1, IDX_HEADS // 2, TQ, LANES), blk),
            pl.BlockSpec((1, IDX_HEADS, TQ), lambda b, i: (b, 0, i)),
            pl.BlockSpec((1, A_HEADS, TQ, KV_RANK), blk),
            pl.BlockSpec((1, S, LANES), full),
            pl.BlockSpec((1, S, LANES), full),
            pl.BlockSpec((1, S, KV_RANK), full),
            pl.BlockSpec((1, KV_EXT, S), full),
            pl.BlockSpec((A_HEADS, NEAR + TQ, TQ), lambda b, i: (0, 0, 0)),
            pl.BlockSpec((1, A_HEADS, A_HEAD_DIM, KV_RANK), lambda b, i: (l, 0, 0, 0)),
        ],
        out_specs=pl.BlockSpec((1, TQ, A_WIDTH), lambda b, i: (b, i, 0)),
        out_shape=jax.ShapeDtypeStruct((B, S, A_WIDTH), BF16),
        scratch_shapes=[
            pltpu.VMEM((S, TQ), I32),
            pltpu.VMEM((32, S // PLANE_ROWS * SUBLANES, TQ), I32),
            pltpu.VMEM((S, TQ), F32),
            pltpu.VMEM((NEAR, TQ), F32),
            pltpu.VMEM((UNIT, A_HEADS * TQ), F32),
            pltpu.VMEM((UNIT, A_HEADS * TQ), F32),
            pltpu.VMEM((SUBLANES, A_HEADS * TQ), F32),
            pltpu.VMEM((SUBLANES, A_HEADS * TQ), F32),
            pltpu.VMEM((KV_EXT, A_HEADS * TQ), F32),
            pltpu.VMEM((A_WIDTH, TQ), F32),
        ],
        compiler_params=_cparams(("arbitrary", "arbitrary")),
        name="dsa_attention",
    )(iq, iwT, qlat, ikA, ikB, ckv, ckvT, bn, wuvT)


def _hgrn_constants():
    c = CHUNK
    r = np.arange(c)[:, None]
    jj = np.arange(c)[None, :]
    mats = [(jj <= r), (jj > r)]
    masks = [np.eye(c, dtype=bool)]
    m = c // 2
    while m >= 1:
        start = (r // (2 * m)) * (2 * m)
        bd = start + m - 1
        upper = r > bd
        mats.append(np.where(upper, (jj > bd) & (jj <= r), (jj > r) & (jj <= bd)))
        same_parent = (r // (2 * m)) == (jj // (2 * m))
        masks.append(same_parent & upper & (jj <= (jj // (2 * m)) * (2 * m) + m - 1))
        m //= 2
    m_all = np.concatenate(mats, axis=0).astype(np.float32)
    total = np.zeros((c, c), np.int32)
    for mk in masks:
        total += mk
    assert (total == np.tril(np.ones((c, c), np.int32))).all()
    return np.concatenate([m_all] * 3, axis=1), np.stack(masks).astype(np.float32)


_HGRN_M3, _HGRN_MASKS = _hgrn_constants()
_HGRN_LEVELS = _HGRN_MASKS.shape[0] - 1
HGRN_STEP_CHUNKS = 2


def _hgrn_kernel(q_ref, k_ref, lf_ref, v_ref, gate_ref, m3_ref, mask_ref, gn_ref, out_ref, st_ref):
    @pl.when(pl.program_id(1) == 0)
    def _():
        st_ref[...] = jnp.zeros(st_ref.shape, F32)

    c = CHUNK
    for ci in range(HGRN_STEP_CHUNKS):
        rows = slice(ci * c, (ci + 1) * c)
        g = lf_ref[0, rows, :]
        g_hi = g.astype(BF16)
        r1 = g - g_hi.astype(F32)
        g_mid = r1.astype(BF16)
        g_lo = (r1 - g_mid.astype(F32)).astype(BF16)
        sums = jnp.dot(m3_ref[...], jnp.concatenate([g_hi, g_mid, g_lo], axis=0), preferred_element_type=F32)
        e_all = jnp.exp(sums)
        for h in range(B_HEADS):
            cols = slice(h * B_KEY_DIM, (h + 1) * B_KEY_DIM)
            qh = q_ref[0, rows, cols]
            kh = k_ref[0, rows, cols]
            vh = v_ref[0, rows, cols]
            att = mask_ref[0] * _nt_dot(qh.astype(BF16), kh.astype(BF16))
            for lv in range(_HGRN_LEVELS):
                e_l = e_all[(2 + lv) * c:(3 + lv) * c, cols]
                att = att + mask_ref[lv + 1] * _nt_dot((qh * e_l).astype(BF16), (kh * e_l).astype(BF16))
            e_b = e_all[0:c, cols]
            e_rem = e_all[c:2 * c, cols]
            st = st_ref[h]
            o = jnp.dot(att.astype(BF16), vh, preferred_element_type=F32)
            o = o + _nt_dot((qh * e_b).astype(BF16), st.astype(BF16))
            upd = lax.dot_general(vh, (kh * e_rem).astype(BF16), (((0,), (0,)), ((), ())),
                                  preferred_element_type=F32)
            st_ref[h] = st * e_b[c - 1:c, :] + upd
            o = o * lax.rsqrt(jnp.mean(o * o, axis=-1, keepdims=True) + RMS_EPS) * gn_ref[0]
            out_ref[0, rows, cols] = (o * gate_ref[0, rows, cols]).astype(BF16)


def _hgrn(l, hq, hk, hlf, hv, hgate, gnorm):
    B, S, W = hq.shape
    ts = CHUNK * HGRN_STEP_CHUNKS
    tok = lambda b, i: (b, i, 0)
    return pl.pallas_call(
        _hgrn_kernel,
        grid=(B, S // ts),
        in_specs=[
            pl.BlockSpec((1, ts, W), tok),
            pl.BlockSpec((1, ts, W), tok),
            pl.BlockSpec((1, ts, W), tok),
            pl.BlockSpec((1, ts, W), tok),
            pl.BlockSpec((1, ts, W), tok),
            pl.BlockSpec(_HGRN_M3.shape, lambda b, i: (0, 0)),
            pl.BlockSpec(_HGRN_MASKS.shape, lambda b, i: (0, 0, 0)),
            pl.BlockSpec((1, 1, B_VAL_DIM), lambda b, i: (l, 0, 0)),
        ],
        out_specs=pl.BlockSpec((1, ts, W), tok),
        out_shape=jax.ShapeDtypeStruct((B, S, W), BF16),
        scratch_shapes=[pltpu.VMEM((B_HEADS, B_VAL_DIM, B_KEY_DIM), F32)],
        compiler_params=_cparams(("arbitrary", "arbitrary")),
        name="hgrn2",
    )(hq, hk, hlf, hv, hgate, jnp.asarray(_HGRN_M3, BF16), jnp.asarray(_HGRN_MASKS), gnorm)


def _layernorm(v, g, b):
    mu = jnp.mean(v, axis=-1, keepdims=True)
    d = v - mu
    var = jnp.mean(d * d, axis=-1, keepdims=True)
    return d * lax.rsqrt(var + LN_EPS) * g + b


def _first_argmax(v, idx, axes, big):
    mx = v
    for ax in axes:
        mx = jnp.max(mx, axis=ax, keepdims=True)
    pos = jnp.where(v == mx, idx, big)
    for ax in axes:
        pos = jnp.min(pos, axis=ax, keepdims=True)
    return mx, pos


def _outproj_kernel(ya_ref, yb_ref, x_ref, mod_ref, wo_ref, lng_ref, lnb_ref, wrT_ref, rbias_ref, tri_ref,
                    x1_ref, u2_ref, gates_ref, slotsT_ref, route_ref, *, alpha):
    y = jnp.dot(ya_ref[0], wo_ref[0, 0:A_WIDTH, :], preferred_element_type=F32)
    y = y + jnp.dot(yb_ref[0], wo_ref[0, A_WIDTH:, :], preferred_element_type=F32)
    g1 = mod_ref[0, 2:3, :]
    x1 = _layernorm(alpha * x_ref[0] + (1.0 + g1) * y, lng_ref[0], lnb_ref[0])
    x1_ref[0] = x1
    u2 = (x1 * (1.0 + mod_ref[0, 4:5, :]) + mod_ref[0, 3:4, :]).astype(BF16)
    u2_ref[0] = u2

    tm = u2.shape[0]
    gsz = N_EXPERTS // N_GROUPS
    scores = 1.0 / (1.0 + jnp.exp(-_nt_dot(wrT_ref[0], u2)))
    sel = (scores + rbias_ref[0]).reshape(N_GROUPS, gsz, tm)
    scores = scores.reshape(N_GROUPS, gsz, tm)
    i_m = lax.broadcasted_iota(I32, (N_GROUPS, gsz, tm), 1)
    i_g = lax.broadcasted_iota(I32, (N_GROUPS, 1, tm), 0)
    i_e = lax.broadcasted_iota(I32, (N_GROUPS, gsz, tm), 0) * gsz + i_m
    m1, p1 = _first_argmax(sel, i_m, (1,), gsz)
    m2 = jnp.max(jnp.where(i_m == p1, NEG_INF, sel), axis=1, keepdims=True)
    gs = m1 + m2
    gmask = jnp.zeros(gs.shape, F32)
    for _ in range(TOPK_GROUPS):
        _, pg = _first_argmax(gs, i_g, (0,), N_GROUPS)
        hit = i_g == pg
        gmask = jnp.where(hit, 1.0, gmask)
        gs = jnp.where(hit, NEG_INF, gs)
    cand = jnp.where(jnp.broadcast_to(gmask, sel.shape) > 0.0, sel, NEG_INF)
    w = jnp.zeros(sel.shape, F32)
    chosen = jnp.zeros(sel.shape, F32)
    picks = []
    for _ in range(TOP_K):
        _, pe = _first_argmax(cand, i_e, (1, 0), N_EXPERTS)
        hit = i_e == pe
        w = jnp.where(hit, scores, w)
        chosen = jnp.where(hit, 1.0, chosen)
        cand = jnp.where(hit, NEG_INF, cand)
        picks.append(pe)
    wsum = jnp.sum(jnp.sum(w, axis=1, keepdims=True), axis=0, keepdims=True)
    gates3 = w / wsum * ROUTED_SCALE
    gates = gates3.reshape(N_EXPERTS, tm)
    g_hi = gates.astype(BF16).astype(F32)
    g_lo = (gates - g_hi).astype(BF16).astype(F32)
    gates_ref[0] = jnp.concatenate([g_hi, g_lo], axis=0).T.astype(BF16)

    chosen2 = chosen.reshape(N_EXPERTS, tm).astype(BF16)
    rank = jnp.concatenate(
        [jnp.dot(chosen2[:, g * MOE_GROUP:(g + 1) * MOE_GROUP], tri_ref[...], preferred_element_type=F32)
         for g in range(tm // MOE_GROUP)], axis=1).reshape(N_GROUPS, gsz, tm)
    slot_rows, gate_rows = [], []
    for pe in picks:
        mine = i_e == pe
        rk = jnp.sum(jnp.sum(jnp.where(mine, rank, 0.0), axis=1, keepdims=True), axis=0, keepdims=True)
        gk = jnp.sum(jnp.sum(jnp.where(mine, gates3, 0.0), axis=1, keepdims=True), axis=0, keepdims=True)
        slot = jnp.where(rk < MOE_CAP, pe.astype(F32) * MOE_CAP + rk, -1.0)
        slot_rows.append(slot.reshape(1, tm))
        gate_rows.append(gk.reshape(1, tm))
    slots = jnp.concatenate(slot_rows, axis=0)
    slotsT_ref[0] = slots.astype(I32)
    info = jnp.concatenate([slots, jnp.concatenate(gate_rows, axis=0),
                            jnp.zeros((LANES - 2 * TOP_K, tm), F32)], axis=0)
    route_ref[0] = info.T


def _outproj(l, ya, yb, x, mod, wo, ln_g, ln_b, wrT, rbias, alpha):
    B, S, D = x.shape
    tm = TM_PROJ
    tok = lambda b, i: (b, i, 0)
    lw3 = lambda b, i: (l, 0, 0)
    return pl.pallas_call(
        functools.partial(_outproj_kernel, alpha=alpha),
        grid=(B, S // tm),
        in_specs=[
            pl.BlockSpec((1, tm, A_WIDTH), tok),
            pl.BlockSpec((1, tm, B_WIDTH), tok),
            pl.BlockSpec((1, tm, D), tok),
            pl.BlockSpec((1, 6, D), lambda b, i: (b, 0, 0)),
            pl.BlockSpec((1, D, D), lw3),
            pl.BlockSpec((1, 1, D), lw3),
            pl.BlockSpec((1, 1, D), lw3),
            pl.BlockSpec((1, N_EXPERTS, D), lw3),
            pl.BlockSpec((1, N_EXPERTS, tm), lw3),
            pl.BlockSpec((MOE_GROUP, MOE_GROUP), lambda b, i: (0, 0)),
        ],
        out_specs=[pl.BlockSpec((1, tm, D), tok), pl.BlockSpec((1, tm, D), tok),
                   pl.BlockSpec((1, tm, 2 * N_EXPERTS), tok),
                   pl.BlockSpec((1, TOP_K, tm), lambda b, i: (b, 0, i)),
                   pl.BlockSpec((1, tm, LANES), tok)],
        out_shape=[jax.ShapeDtypeStruct((B, S, D), F32), jax.ShapeDtypeStruct((B, S, D), BF16),
                   jax.ShapeDtypeStruct((B, S, 2 * N_EXPERTS), BF16),
                   jax.ShapeDtypeStruct((B, TOP_K, S), I32),
                   jax.ShapeDtypeStruct((B, S, LANES), F32)],
        compiler_params=_cparams(("arbitrary", "arbitrary")),
        name="outproj_router",
    )(ya, yb, x, mod, wo, ln_g, ln_b, wrT, rbias,
      jnp.asarray(np.triu(np.ones((MOE_GROUP, MOE_GROUP), np.float32), 1), BF16))


DISPATCH_ROWS = 512


def _dispatch_kernel(u_ref, slotsT_ref, x_ref):
    u = u_ref[...]
    slots = slotsT_ref[0]
    experts_per_chunk = DISPATCH_ROWS // MOE_CAP
    for c in range(N_EXPERTS * MOE_CAP // DISPATCH_ROWS):
        row = lax.broadcasted_iota(I32, (DISPATCH_ROWS, MOE_GROUP), 0) + c * DISPATCH_ROWS
        onehot = jnp.zeros((DISPATCH_ROWS, MOE_GROUP), F32)
        for k in range(TOP_K):
            onehot = jnp.where(row == slots[k:k + 1, :], 1.0, onehot)
        xs = jnp.dot(onehot.astype(BF16), u, preferred_element_type=F32).astype(BF16)
        x_ref[c * experts_per_chunk:(c + 1) * experts_per_chunk] = xs.reshape(experts_per_chunk, MOE_CAP, -1)


def _dispatch(u2, slotsT):
    T, D = u2.shape
    ng = T // MOE_GROUP
    gps = slotsT.shape[-1] // MOE_GROUP
    return pl.pallas_call(
        _dispatch_kernel,
        grid=(ng,),
        in_specs=[pl.BlockSpec((MOE_GROUP, D), lambda g: (g, 0)),
                  pl.BlockSpec((1, TOP_K, MOE_GROUP), lambda g: (g // gps, 0, g % gps))],
        out_specs=pl.BlockSpec((N_EXPERTS, MOE_CAP, D), lambda g: (0, g, 0)),
        out_shape=jax.ShapeDtypeStruct((N_EXPERTS, ng * MOE_CAP, D), BF16),
        compiler_params=_cparams(("arbitrary",)),
        name="moe_dispatch",
    )(u2, slotsT)


def _expert_kernel(x_ref, wgu_ref, wd_ref, y_ref):
    hgu = jnp.dot(x_ref[0], wgu_ref[0, 0], preferred_element_type=F32)
    h = _silu(hgu[:, :EXPERT_DIM]) * hgu[:, EXPERT_DIM:]
    y_ref[0] = jnp.dot(h.astype(BF16), wd_ref[0, 0], preferred_element_type=F32).astype(BF16)


def _experts(l, xs, wgu, wd):
    E, R, D = xs.shape
    tr = min(R, MOE_EXPERT_ROWS)
    assert R % tr == 0
    return pl.pallas_call(
        _expert_kernel,
        grid=(E, R // tr),
        in_specs=[pl.BlockSpec((1, tr, D), lambda e, i: (e, i, 0)),
                  pl.BlockSpec((1, 1, D, 2 * EXPERT_DIM), lambda e, i: (l, e, 0, 0)),
                  pl.BlockSpec((1, 1, EXPERT_DIM, D), lambda e, i: (l, e, 0, 0))],
        out_specs=pl.BlockSpec((1, tr, D), lambda e, i: (e, i, 0)),
        out_shape=jax.ShapeDtypeStruct((E, R, D), BF16),
        compiler_params=_cparams(("arbitrary", "arbitrary")),
        name="moe_experts",
    )(xs, wgu, wd)


def _combine_kernel(y_ref, route_ref, u_ref, x1_ref, mod_ref, sgu_ref, sd_ref, lng_ref, lnb_ref, out_ref, *, alpha):
    ys = y_ref[...].reshape(N_EXPERTS * MOE_CAP, -1)
    route = route_ref[...]
    lane = lax.broadcasted_iota(I32, (MOE_GROUP, N_EXPERTS * MOE_CAP), 1)
    pick = jnp.zeros((MOE_GROUP, N_EXPERTS * MOE_CAP), F32)
    for k in range(TOP_K):
        slot = route[:, k:k + 1].astype(I32)
        pick = jnp.where(lane == slot, route[:, TOP_K + k:TOP_K + k + 1], pick)
    routed = jnp.dot(pick.astype(BF16), ys, preferred_element_type=F32)
    hgu = jnp.dot(u_ref[...], sgu_ref[0], preferred_element_type=F32)
    hs = _silu(hgu[:, :SHARED_DIM]) * hgu[:, SHARED_DIM:]
    y = routed + jnp.dot(hs.astype(BF16), sd_ref[0], preferred_element_type=F32)
    g2 = mod_ref[0, 5:6, :]
    out_ref[...] = _layernorm(alpha * x1_ref[...] + (1.0 + g2) * y, lng_ref[0], lnb_ref[0])


def _combine(l, ys, route, u2, x1, mod, sgu, sd, ln_g, ln_b, alpha, seq):
    T, D = u2.shape
    tok = lambda g: (g, 0)
    lw3 = lambda g: (l, 0, 0)
    return pl.pallas_call(
        functools.partial(_combine_kernel, alpha=alpha),
        grid=(T // MOE_GROUP,),
        in_specs=[
            pl.BlockSpec((N_EXPERTS, MOE_CAP, D), lambda g: (0, g, 0)),
            pl.BlockSpec((MOE_GROUP, LANES), tok),
            pl.BlockSpec((MOE_GROUP, D), tok),
            pl.BlockSpec((MOE_GROUP, D), tok),
            pl.BlockSpec((1, 6, D), lambda g: ((g * MOE_GROUP) // seq, 0, 0)),
            pl.BlockSpec((1, D, 2 * SHARED_DIM), lw3),
            pl.BlockSpec((1, SHARED_DIM, D), lw3),
            pl.BlockSpec((1, 1, D), lw3),
            pl.BlockSpec((1, 1, D), lw3),
        ],
        out_specs=pl.BlockSpec((MOE_GROUP, D), tok),
        out_shape=jax.ShapeDtypeStruct((T, D), F32),
        compiler_params=_cparams(("arbitrary",)),
        name="moe_combine",
    )(ys, route, u2, x1, mod, sgu, sd, ln_g, ln_b)


def _moe_kernel(u_ref, gates_ref, x1_ref, mod_ref, wgu_ref, wd_ref, sgu_ref, sd_ref, lng_ref, lnb_ref,
                out_ref, acc_ref, *, alpha):
    s = pl.program_id(1)
    u = u_ref[...]

    def hidden(wgu):
        hgu = jnp.dot(u, wgu, preferred_element_type=F32)
        return _silu(hgu[:, :EXPERT_DIM]) * hgu[:, EXPERT_DIM:]

    @pl.when(s == 0)
    def _():
        acc_ref[...] = jnp.dot(hidden(sgu_ref[0]).astype(BF16), sd_ref[0], preferred_element_type=F32)

    rows = lax.broadcasted_iota(I32, (2 * N_EXPERTS, MOE_EXPERTS_PER_STEP * EXPERT_DIM), 0) & (N_EXPERTS - 1)
    cols = lax.broadcasted_iota(I32, (2 * N_EXPERTS, MOE_EXPERTS_PER_STEP * EXPERT_DIM), 1)
    onehot = jnp.where(rows == s * MOE_EXPERTS_PER_STEP + cols // EXPERT_DIM, 1.0, 0.0).astype(BF16)
    gate = jnp.dot(gates_ref[...], onehot, preferred_element_type=F32)
    h = jnp.concatenate(
        [(hidden(wgu_ref[0, k]) * gate[:, k * EXPERT_DIM:(k + 1) * EXPERT_DIM]).astype(BF16)
         for k in range(MOE_EXPERTS_PER_STEP)], axis=1)
    wd = wd_ref[0].reshape(MOE_EXPERTS_PER_STEP * EXPERT_DIM, wd_ref.shape[-1])
    acc_ref[...] += jnp.dot(h, wd, preferred_element_type=F32)

    @pl.when(s == pl.num_programs(1) - 1)
    def _():
        g2 = mod_ref[0, 5:6, :]
        out_ref[...] = _layernorm(alpha * x1_ref[...] + (1.0 + g2) * acc_ref[...], lng_ref[0], lnb_ref[0])


def _moe(l, u2, gates, x1, mod, wgu, wd, sgu, sd, ln_g, ln_b, alpha, seq):
    T, D = u2.shape
    tm = TM_MOE
    assert seq % tm == 0
    tok = lambda i, e: (i, 0)
    lw3 = lambda i, e: (l, 0, 0)
    return pl.pallas_call(
        functools.partial(_moe_kernel, alpha=alpha),
        grid=(T // tm, N_EXPERTS // MOE_EXPERTS_PER_STEP),
        in_specs=[
            pl.BlockSpec((tm, D), tok),
            pl.BlockSpec((tm, 2 * N_EXPERTS), tok),
            pl.BlockSpec((tm, D), tok),
            pl.BlockSpec((1, 6, D), lambda i, e: ((i * tm) // seq, 0, 0)),
            pl.BlockSpec((1, MOE_EXPERTS_PER_STEP, D, 2 * EXPERT_DIM), lambda i, e: (l, e, 0, 0)),
            pl.BlockSpec((1, MOE_EXPERTS_PER_STEP, EXPERT_DIM, D), lambda i, e: (l, e, 0, 0)),
            pl.BlockSpec((1, D, 2 * SHARED_DIM), lw3),
            pl.BlockSpec((1, SHARED_DIM, D), lw3),
            pl.BlockSpec((1, 1, D), lw3),
            pl.BlockSpec((1, 1, D), lw3),
        ],
        out_specs=pl.BlockSpec((tm, D), tok),
        out_shape=jax.ShapeDtypeStruct((T, D), F32),
        scratch_shapes=[pltpu.VMEM((tm, D), F32)],
        compiler_params=_cparams(("arbitrary", "arbitrary")),
        name="moe_dense",
    )(u2, gates, x1, mod, wgu, wd, sgu, sd, ln_g, ln_b)


def _prepare_params(w_in, kv_norm_g, w_uk, w_uv, hgrn_lb, w_out, w_router, router_bias,
                    w_gate, w_up, w_down, ws_gate, ws_up, ws_down):
    L = w_in.shape[0]
    sizes = (A_WIDTH, KV_RANK, IDX_HEADS * IDX_DIM, IDX_DIM, IDX_HEADS, B_FDIM, B_FDIM, B_WIDTH, B_WIDTH)
    offs = np.concatenate([[0], np.cumsum(sizes)])
    seg = lambda i: w_in[:, :, offs[i]:offs[i + 1]]
    w_aq, w_ckv, w_iq, w_ik, w_iw, w_hq, w_hf, w_hi, w_hg = (seg(i) for i in range(9))
    zik = jnp.zeros_like(w_ik)
    wp = jnp.concatenate([w_aq, w_ckv, w_iq, w_ik, zik, zik, w_ik, w_hq, w_hf, w_hg, w_hi], axis=-1).astype(BF16)
    assert wp.shape[-1] == _C_END
    eye = jnp.eye(A_HEADS, dtype=F32)
    wblk = (jnp.einsum('lhdr,hg->lhdgr', w_uk * (ATTN_SCALE * LOG2E), eye)
            .reshape(L, A_WIDTH, A_HEADS * KV_RANK).astype(BF16))
    p = dict(
        wp=wp, wblk=wblk,
        wckvT=jnp.swapaxes(w_ckv, 1, 2).astype(BF16),
        wiwT=jnp.swapaxes(w_iw, 1, 2).astype(BF16),
        gkv=kv_norm_g.reshape(L, 1, KV_RANK),
        gkvT=jnp.broadcast_to(kv_norm_g[:, :, None], (L, KV_RANK, TM_PROJ)),
        wuvT=jnp.swapaxes(w_uv, 2, 3).astype(BF16),
        wo=w_out.astype(BF16),
        wrT=jnp.swapaxes(w_router, 1, 2).astype(BF16),
        rbias=jnp.broadcast_to(router_bias[:, :, None], (L, N_EXPERTS, TM_PROJ)),
        wgu=jnp.concatenate([w_gate, w_up], axis=-1).astype(BF16),
        wd=w_down.astype(BF16),
        sgu=jnp.concatenate([ws_gate, ws_up], axis=-1).astype(BF16),
        sd=ws_down.astype(BF16),
    )
    lbs = jnp.cumsum(jax.nn.softmax(hgrn_lb.astype(F32), axis=0), axis=0)
    lbs = jnp.clip(lbs - lbs[0:1], 0.0, 1.0 - 1e-6)
    p["llb"] = jnp.log(lbs).reshape(L, 1, B_FDIM)
    p["l1m"] = jnp.log1p(-lbs).reshape(L, 1, B_FDIM)
    return p


def kernel(x, c, w_ada, b_ada, w_in, kv_norm_g, w_uk, w_uv, rel_bias, hgrn_lb, gnorm_g, w_out, ln1_g, ln1_b,
           w_router, router_bias, w_gate, w_up, w_down, ws_gate, ws_up, ws_down, ln2_g, ln2_b):
    B, S, D = x.shape
    L = w_in.shape[0]
    alpha = (2 * L) ** 0.25
    p = _prepare_params(w_in, kv_norm_g, w_uk, w_uv, hgrn_lb, w_out, w_router, router_bias,
                        w_gate, w_up, w_down, ws_gate, ws_up, ws_down)
    mods = _adaln(c, w_ada, b_ada).reshape(L, B, 6, D)
    bn = _bias_tile(rel_bias)
    gn = gnorm_g.reshape(L, 1, B_VAL_DIM)
    ln1g, ln1b = ln1_g.reshape(L, 1, D), ln1_b.reshape(L, 1, D)
    ln2g, ln2b = ln2_g.reshape(L, 1, D), ln2_b.reshape(L, 1, D)
    for l in range(L):
        mod = mods[l]
        (qlat, ckv, ckvT, iq, ikA, ikB, iwT, hq, hk, hlf, hv, hgate) = _inproj(
            l, x, mod, p["wp"], p["wblk"], p["wckvT"], p["wiwT"], p["gkv"], p["gkvT"], p["llb"], p["l1m"])
        ya = _dsa(l, iq, iwT, qlat, ikA, ikB, ckv, ckvT, bn, p["wuvT"])
        yb = _hgrn(l, hq, hk, hlf, hv, hgate, gn)
        x1, u2, gates, slots_t, route = _outproj(l, ya, yb, x, mod, p["wo"], ln1g, ln1b, p["wrT"], p["rbias"], alpha)
        u2f, x1f = u2.reshape(B * S, D), x1.reshape(B * S, D)

        def moe_sparse(l=l, mod=mod, u2f=u2f, x1f=x1f, slots_t=slots_t, route=route):
            ys = _experts(l, _dispatch(u2f, slots_t), p["wgu"], p["wd"])
            return _combine(l, ys, route.reshape(B * S, LANES), u2f, x1f, mod, p["sgu"], p["sd"], ln2g, ln2b, alpha, S)

        def moe_dense(l=l, mod=mod, u2f=u2f, x1f=x1f, gates=gates):
            return _moe(l, u2f, gates.reshape(B * S, 2 * N_EXPERTS), x1f, mod,
                        p["wgu"], p["wd"], p["sgu"], p["sd"], ln2g, ln2b, alpha, S)

        x = lax.cond(jnp.any(slots_t < 0), moe_dense, moe_sparse).reshape(B, S, D)
    return x
```

```python
import functools
import math

import numpy as np
import jax
import jax.numpy as jnp
from jax import lax
from jax.experimental import pallas as pl
from jax.experimental.pallas import tpu as pltpu

F32 = jnp.float32
BF16 = jnp.bfloat16
I32 = jnp.int32

D_MODEL = 1024
CHUNK = 64
A_HEADS = 8
A_HEAD_DIM = 64
A_WIDTH = A_HEADS * A_HEAD_DIM
KV_RANK = 128
IDX_HEADS = 8
IDX_DIM = 64
IDX_TOPK_MAX = 256
IDX_W_SCALE = (IDX_HEADS ** -0.5) * (IDX_DIM ** -0.5)
ATTN_SCALE = A_HEAD_DIM ** -0.5
LOG2E = math.log2(math.e)
KV_EXT = KV_RANK + 16
NUM_BUCKETS = 32
MAX_DISTANCE = 128
B_HEADS = 4
B_KEY_DIM = 128
B_VAL_DIM = 128
B_WIDTH = B_HEADS * B_VAL_DIM
B_FDIM = B_HEADS * B_KEY_DIM
N_EXPERTS = 64
TOP_K = 8
N_GROUPS = 8
TOPK_GROUPS = 4
EXPERT_DIM = 256
SHARED_DIM = 256
ROUTED_SCALE = 2.5
LN_EPS = 1e-5
RMS_EPS = 1e-6

LANES = 128
SUBLANES = 8
VMEM_LIMIT_BYTES = 56 * 1024 * 1024

INT_MIN = -(2 ** 31)
NEG_INF = float("-inf")

TM_PROJ = 512
TQ = 128
UNIT = 512
NEAR = 2 * TQ
COUNT_ACCS = 8
PLANE_ROWS = 32 * SUBLANES
TM_MOE = 1024
MOE_EXPERTS_PER_STEP = 4
MOE_GROUP = 256
MOE_CAP = 80
MOE_EXPERT_ROWS = 2048

_C_AQ, _C_CKV, _C_IQ, _C_IKA, _C_IKB, _C_HQ, _C_HF, _C_HG, _C_HI, _C_END = (
    0, 512, 640, 1152, 1280, 1408, 1920, 2432, 2944, 3456)


def _silu(v):
    return v * (1.0 / (1.0 + jnp.exp(-v)))


def _nt_dot(a, b):
    return lax.dot_general(a, b, (((1,), (1,)), ((), ())), preferred_element_type=F32)


def _cparams(sem):
    return pltpu.CompilerParams(dimension_semantics=sem, vmem_limit_bytes=VMEM_LIMIT_BYTES)


def _adaln_kernel(c_ref, w_ref, b_ref, o_ref):
    cond = _silu(c_ref[...])
    o_ref[0] = jnp.dot(cond.astype(BF16), w_ref[0].astype(BF16), preferred_element_type=F32) + b_ref[0]


def _adaln(c, w_ada, b_ada):
    L, D, D6 = w_ada.shape
    B = c.shape[0]
    nb = D6 // D
    return pl.pallas_call(
        _adaln_kernel,
        grid=(L, nb),
        in_specs=[
            pl.BlockSpec((B, D), lambda l, j: (0, 0)),
            pl.BlockSpec((1, D, D), lambda l, j: (l, 0, j)),
            pl.BlockSpec((1, 1, D), lambda l, j: (l, 0, j)),
        ],
        out_specs=pl.BlockSpec((1, B, D), lambda l, j: (l, 0, j)),
        out_shape=jax.ShapeDtypeStruct((L, B, D6), F32),
        compiler_params=_cparams(("arbitrary", "arbitrary")),
        name="adaln_mod",
    )(c, w_ada, b_ada.reshape(L, 1, D6))


_T5_NB = NUM_BUCKETS // 2
_T5_EXACT = _T5_NB // 2
_T5_THRESHOLDS = tuple(
    int(math.ceil(_T5_EXACT * (MAX_DISTANCE / _T5_EXACT) ** (j / (_T5_NB - _T5_EXACT)) - 1e-9))
    for j in range(1, _T5_NB - _T5_EXACT))
FAR_BUCKET = _T5_NB - 1
assert _T5_THRESHOLDS[-1] <= TQ, "keys further than one query block behind must share the far bucket"


def _bias_kernel(rb_ref, o_ref):
    kr = lax.broadcasted_iota(I32, (NEAR + TQ, TQ), 0)
    ql = lax.broadcasted_iota(I32, (NEAR + TQ, TQ), 1)
    rel = kr - TQ - ql
    n = jnp.abs(rel)
    large = jnp.full(rel.shape, _T5_EXACT, I32)
    for t in _T5_THRESHOLDS:
        large = large + (n >= t).astype(I32)
    bucket = jnp.where(rel > 0, _T5_NB, 0) + jnp.where(n < _T5_EXACT, n, large)
    for h in range(A_HEADS):
        acc = jnp.zeros(rel.shape, F32)
        for bk in range(NUM_BUCKETS):
            acc = jnp.where(bucket == bk, rb_ref[bk, h], acc)
        o_ref[h] = (acc - rb_ref[FAR_BUCKET, h]) * LOG2E


def _bias_tile(rel_bias):
    return pl.pallas_call(
        _bias_kernel,
        in_specs=[pl.BlockSpec(memory_space=pltpu.SMEM)],
        out_specs=pl.BlockSpec(memory_space=pltpu.VMEM),
        out_shape=jax.ShapeDtypeStruct((A_HEADS, NEAR + TQ, TQ), F32),
        name="rel_bias_tile",
    )(rel_bias)


def _inproj_kernel(x_ref, mod_ref, wp_ref, wblk_ref, wckvT_ref, wiwT_ref, gkv_ref, gkvT_ref, llb_ref, l1m_ref,
                   qlat_ref, ckv_ref, ckvT_ref, iq_ref, ikA_ref, ikB_ref, iwT_ref,
                   hq_ref, hk_ref, hlf_ref, hv_ref, hgate_ref):
    x = x_ref[0]
    sh1 = mod_ref[0, 0:1, :]
    sc1 = mod_ref[0, 1:2, :]
    u = (x * (1.0 + sc1) + sh1).astype(BF16)
    z = jnp.dot(u, wp_ref[0], preferred_element_type=F32)

    ql = jnp.dot(z[:, _C_AQ:_C_CKV].astype(BF16), wblk_ref[0], preferred_element_type=F32)
    for h in range(A_HEADS):
        qlat_ref[0, h] = ql[:, h * KV_RANK:(h + 1) * KV_RANK].astype(BF16)

    zc = z[:, _C_CKV:_C_IQ]
    inv = lax.rsqrt(jnp.mean(zc * zc, axis=-1, keepdims=True) + RMS_EPS)
    ckv_ref[0] = (zc * inv * gkv_ref[0]).astype(BF16)
    zt = _nt_dot(wckvT_ref[0], u)
    inv_t = lax.rsqrt(jnp.mean(zt * zt, axis=0, keepdims=True) + RMS_EPS)
    ckvT_ref[0, 0:KV_RANK, :] = (zt * inv_t * gkvT_ref[0]).astype(BF16)
    ckvT_ref[0, KV_RANK:KV_EXT, :] = jnp.ones((KV_EXT - KV_RANK, zt.shape[1]), BF16)

    for p in range(IDX_HEADS // 2):
        iq_ref[0, p] = z[:, _C_IQ + p * LANES:_C_IQ + (p + 1) * LANES].astype(BF16)
    ikA_ref[0] = z[:, _C_IKA:_C_IKB].astype(BF16)
    ikB_ref[0] = z[:, _C_IKB:_C_HQ].astype(BF16)
    iwT_ref[0] = _nt_dot(wiwT_ref[0], u) * IDX_W_SCALE

    hq_ref[0] = _silu(z[:, _C_HQ:_C_HF])
    zf = z[:, _C_HF:_C_HG]
    log_sig = jnp.minimum(zf, 0.0) - jnp.log1p(jnp.exp(-jnp.abs(zf)))
    a = llb_ref[0]
    c = l1m_ref[0] + log_sig
    logf = jnp.maximum(a, c) + jnp.log1p(jnp.exp(-jnp.abs(a - c)))
    hlf_ref[0] = logf
    hk_ref[0] = 1.0 - jnp.exp(logf)
    hgate_ref[0] = _silu(z[:, _C_HG:_C_HI])
    hv_ref[0] = z[:, _C_HI:_C_END].astype(BF16)


def _inproj(l, x, mod, wp, wblk, wckvT, wiwT, gkv, gkvT, llb, l1m):
    B, S, D = x.shape
    tm = TM_PROJ
    grid = (B, S // tm)
    lw3 = lambda b, i: (l, 0, 0)
    tok = lambda b, i: (b, i, 0)
    tokT = lambda b, i: (b, 0, i)
    hd4 = lambda b, i: (b, 0, i, 0)
    outs = [
        (jax.ShapeDtypeStruct((B, A_HEADS, S, KV_RANK), BF16), pl.BlockSpec((1, A_HEADS, tm, KV_RANK), hd4)),
        (jax.ShapeDtypeStruct((B, S, KV_RANK), BF16), pl.BlockSpec((1, tm, KV_RANK), tok)),
        (jax.ShapeDtypeStruct((B, KV_EXT, S), BF16), pl.BlockSpec((1, KV_EXT, tm), tokT)),
        (jax.ShapeDtypeStruct((B, IDX_HEADS // 2, S, LANES), BF16), pl.BlockSpec((1, IDX_HEADS // 2, tm, LANES), hd4)),
        (jax.ShapeDtypeStruct((B, S, LANES), BF16), pl.BlockSpec((1, tm, LANES), tok)),
        (jax.ShapeDtypeStruct((B, S, LANES), BF16), pl.BlockSpec((1, tm, LANES), tok)),
        (jax.ShapeDtypeStruct((B, IDX_HEADS, S), F32), pl.BlockSpec((1, IDX_HEADS, tm), tokT)),
        (jax.ShapeDtypeStruct((B, S, B_FDIM), F32), pl.BlockSpec((1, tm, B_FDIM), tok)),
        (jax.ShapeDtypeStruct((B, S, B_FDIM), F32), pl.BlockSpec((1, tm, B_FDIM), tok)),
        (jax.ShapeDtypeStruct((B, S, B_FDIM), F32), pl.BlockSpec((1, tm, B_FDIM), tok)),
        (jax.ShapeDtypeStruct((B, S, B_WIDTH), BF16), pl.BlockSpec((1, tm, B_WIDTH), tok)),
        (jax.ShapeDtypeStruct((B, S, B_WIDTH), F32), pl.BlockSpec((1, tm, B_WIDTH), tok)),
    ]
    return pl.pallas_call(
        _inproj_kernel,
        grid=grid,
        in_specs=[
            pl.BlockSpec((1, tm, D), tok),
            pl.BlockSpec((1, 6, D), lambda b, i: (b, 0, 0)),
            pl.BlockSpec((1, D, _C_END), lw3),
            pl.BlockSpec((1, A_WIDTH, A_HEADS * KV_RANK), lw3),
            pl.BlockSpec((1, KV_RANK, D), lw3),
            pl.BlockSpec((1, IDX_HEADS, D), lw3),
            pl.BlockSpec((1, 1, KV_RANK), lw3),
            pl.BlockSpec((1, KV_RANK, tm), lw3),
            pl.BlockSpec((1, 1, B_FDIM), lw3),
            pl.BlockSpec((1, 1, B_FDIM), lw3),
        ],
        out_specs=[o[1] for o in outs],
        out_shape=[o[0] for o in outs],
        compiler_params=_cparams(("arbitrary", "arbitrary")),
        name="inproj",
    )(x, mod, wp, wblk, wckvT, wiwT, gkv, gkvT, llb, l1m)


def _dsa_kernel(iq_ref, iwT_ref, qlat_ref, ikA_ref, ikB_ref, ckv_ref, ckvT_ref, bn_ref, wuvT_ref, out_ref,
                sc_ref, plane_ref, madd_ref, maddn_ref, la_ref, lb_ref, pma_ref, pmb_ref, ot_ref, yaT_ref,
                *, k_sel, n_idx_bits):
    j = pl.program_id(1)
    q0 = j * TQ
    nk = q0 + TQ
    nunit = (nk + UNIT - 1) // UNIT
    near0 = pl.multiple_of(jnp.maximum(nk - NEAR, 0), TQ)
    bn_row0 = pl.multiple_of(jnp.where(j == 0, TQ, 0), TQ)
    lane = lax.broadcasted_iota(I32, (1, TQ), 1)
    limit = (((q0 + lane) >> 6) + 1) << 6
    row_iota = lax.broadcasted_iota(I32, (UNIT, TQ), 0)

    def unit_rows(u):
        return pl.ds(pl.multiple_of(u * UNIT, UNIT), UNIT)

    iqs = iq_ref[0].reshape(IDX_HEADS // 2 * TQ, LANES)
    iw = iwT_ref[0]

    def score_unit(u, carry):
        rows = unit_rows(u)
        xe = _nt_dot(ikA_ref[0, rows, :], iqs)
        xo = _nt_dot(ikB_ref[0, rows, :], iqs)
        acc = jnp.zeros((UNIT, TQ), F32)
        for p in range(IDX_HEADS // 2):
            acc = acc + iw[2 * p:2 * p + 1, :] * jnp.maximum(xe[:, p * TQ:(p + 1) * TQ], 0.0)
            acc = acc + iw[2 * p + 1:2 * p + 2, :] * jnp.maximum(xo[:, p * TQ:(p + 1) * TQ], 0.0)
        bits = lax.bitcast_convert_type(acc, I32)
        key = bits ^ ((bits >> 31) & 0x7FFFFFFF)
        key = jnp.where(row_iota + u * UNIT < limit, key, INT_MIN)
        sc_ref[rows, :] = key
        return carry

    lax.fori_loop(0, nunit, score_unit, 0)

    ngroups = (nk + PLANE_ROWS - 1) // PLANE_ROWS

    def plane_group(g, carry):
        rows = pl.ds(pl.multiple_of(g * PLANE_ROWS, PLANE_ROWS), PLANE_ROWS)
        words = (sc_ref[rows, :] ^ INT_MIN).reshape(32, SUBLANES, TQ)
        w = [words[i] for i in range(32)]
        j, m = 16, 0x0000FFFF
        while j:
            mask = np.int32(np.uint32(m).view(np.int32))
            k = 0
            while k < 32:
                t = (w[k] ^ lax.shift_right_logical(w[k + j], jnp.full(w[k].shape, j, I32))) & mask
                w[k] = w[k] ^ t
                w[k + j] = w[k + j] ^ (t << j)
                k = (k + j + 1) & ~j
            j >>= 1
            m = (m ^ (m << j)) & 0xFFFFFFFF
        for i in range(32):
            plane_ref[i, pl.ds(g * SUBLANES, SUBLANES), :] = w[i]
        return carry

    lax.fori_loop(0, ngroups, plane_group, 0)

    n_words = sc_ref.shape[0] // PLANE_ROWS * SUBLANES
    group_of_word = lax.broadcasted_iota(I32, (n_words, TQ), 0) // SUBLANES

    def bit_step(i, carry):
        alive, above, t_off, c_ge = carry
        hit = alive & plane_ref[i]
        cnt = above + jnp.sum(lax.population_count(hit), axis=0, keepdims=True)
        ok = cnt >= k_sel
        alive = jnp.where(ok, hit, alive ^ hit)
        above = jnp.where(ok, above, cnt)
        t_off = jnp.where(ok, t_off | (jnp.int32(1) << (31 - i)), t_off)
        return alive, above, t_off, jnp.where(ok, cnt, c_ge)

    zero_row = jnp.zeros((1, TQ), I32)
    _, _, t_off, c_ge = lax.fori_loop(
        0, 32, bit_step,
        (jnp.where(group_of_word < ngroups, jnp.int32(-1), jnp.int32(0)), zero_row, zero_row, zero_row))
    thr = jnp.maximum(t_off ^ INT_MIN, INT_MIN + 1)
    straddle = (c_ge > k_sel).astype(I32)

    def count_where(pred):
        def body(u, acc):
            hit = pred(sc_ref[unit_rows(u), :], u * UNIT).reshape(-1, COUNT_ACCS * SUBLANES, TQ)
            for s in range(hit.shape[0]):
                acc = jnp.where(hit[s], acc + 1, acc)
            return acc
        acc = lax.fori_loop(0, nunit, body, jnp.zeros((COUNT_ACCS * SUBLANES, TQ), I32))
        return jnp.sum(acc, axis=0, keepdims=True)

    def tie_bound():
        c_gt = count_where(lambda blk, r0: blk > thr)
        need = k_sel - c_gt

        def tie_body(i, j0):
            cand = j0 | (jnp.int32(1) << (n_idx_bits - 1 - i))
            cnt = count_where(lambda blk, r0: jnp.where(blk == thr, row_iota + r0, cand) < cand)
            return jnp.where(cnt < need, cand, j0)

        j0 = lax.fori_loop(0, n_idx_bits, tie_body, jnp.zeros((1, TQ), I32))
        return jnp.where(straddle > 0, j0 + 1, jnp.int32(2 ** n_idx_bits))

    jstar = lax.cond(jnp.max(straddle) > 0, tie_bound, lambda: jnp.full((1, TQ), 2 ** n_idx_bits, I32))

    def madd_unit(u, carry):
        rows = unit_rows(u)
        key = sc_ref[rows, :]
        tie_keep = jnp.where(row_iota + u * UNIT < jstar, 0.0, NEG_INF)
        madd_ref[rows, :] = jnp.where(key > thr, 0.0, jnp.where(key == thr, tie_keep, NEG_INF))
        return carry

    lax.fori_loop(0, nunit, madd_unit, 0)
    maddn_ref[...] = madd_ref[pl.ds(near0, NEAR), :]
    madd_ref[pl.ds(near0, NEAR), :] = jnp.full((NEAR, TQ), NEG_INF, F32)

    qall = qlat_ref[0].reshape(A_HEADS * TQ, KV_RANK)

    def col_max(v):
        return jnp.max(v.reshape(v.shape[0] // SUBLANES, SUBLANES, A_HEADS * TQ), axis=0)

    def fold(xl, part_max, ckv_t, m_old):
        m_new = jnp.maximum(m_old, jnp.max(part_max, axis=0, keepdims=True))
        m_use = jnp.where(m_new == NEG_INF, 0.0, m_new)
        p = jnp.exp2((xl - m_use).astype(BF16))
        ot_ref[...] = ot_ref[...] * jnp.exp2(m_old - m_use) + jnp.dot(ckv_t, p, preferred_element_type=F32)
        return m_new

    ot_ref[...] = jnp.zeros(ot_ref.shape, F32)
    near_rows = pl.ds(near0, NEAR)
    xn = _nt_dot(ckv_ref[0, near_rows, :], qall) + jnp.concatenate([maddn_ref[...]] * A_HEADS, axis=1)
    xn = xn + jnp.concatenate([bn_ref[h, pl.ds(bn_row0, NEAR), :] for h in range(A_HEADS)], axis=1)
    m_run = fold(xn, col_max(xn), ckvT_ref[0, :, near_rows], jnp.full((1, A_HEADS * TQ), NEG_INF, F32))

    last_unit = sc_ref.shape[0] // UNIT - 1

    def issue_logits(u, buf_ref, pm_ref):
        rows = unit_rows(jnp.minimum(u, last_unit))
        xl = _nt_dot(ckv_ref[0, rows, :], qall) + jnp.concatenate([madd_ref[rows, :]] * A_HEADS, axis=1)
        buf_ref[...] = xl
        pm_ref[...] = col_max(xl)

    def consume_logits(u, buf_ref, pm_ref, m_old):
        return fold(buf_ref[...], pm_ref[...], ckvT_ref[0, :, unit_rows(u)], m_old)

    @pl.when(nunit % 2 == 1)
    def _():
        madd_ref[unit_rows(nunit), :] = jnp.full((UNIT, TQ), NEG_INF, F32)

    issue_logits(0, la_ref, pma_ref)

    def pair_step(i, m_old):
        issue_logits(2 * i + 1, lb_ref, pmb_ref)
        m_mid = consume_logits(2 * i, la_ref, pma_ref, m_old)
        issue_logits(2 * i + 2, la_ref, pma_ref)
        return consume_logits(2 * i + 1, lb_ref, pmb_ref, m_mid)

    lax.fori_loop(0, (nunit + 1) // 2, pair_step, m_run)
    o_t = (ot_ref[0:KV_RANK, :] * (1.0 / ot_ref[KV_RANK:KV_RANK + 1, :])).astype(BF16)
    for h in range(A_HEADS):
        yaT_ref[h * A_HEAD_DIM:(h + 1) * A_HEAD_DIM, :] = jnp.dot(
            wuvT_ref[0, h], o_t[:, h * TQ:(h + 1) * TQ], preferred_element_type=F32)

    out_ref[0] = yaT_ref[...].T.astype(BF16)


def _dsa(l, iq, iwT, qlat, ikA, ikB, ckv, ckvT, bn, wuvT):
    B, S = ckv.shape[0], ckv.shape[1]
    assert S % (2 * UNIT) == 0 and UNIT % TQ == 0 and TQ % CHUNK == 0 and CHUNK == 64 and NEAR <= UNIT
    k_sel = min(IDX_TOPK_MAX, S // 4)
    n_idx_bits = int(math.log2(S))
    assert 2 ** n_idx_bits == S
    grid = (B, S // TQ)
    blk = lambda b, i: (b, 0, i, 0)
    full = lambda b, i: (b, 0, 0)
    kern = functools.partial(_dsa_kernel, k_sel=k_sel, n_idx_bits=n_idx_bits)
    return pl.pallas_call(
        kern,
        grid=grid,
        in_specs=[
            pl.BlockSpec((1, IDX_HEADS // 2, TQ, LANES), blk),
            pl.BlockSpec((1, IDX_HEADS, TQ), lambda b, i: (b, 0, i)),
            pl.BlockSpec((1, A_HEADS, TQ, KV_RANK), blk),
            pl.BlockSpec((1, S, LANES), full),
            pl.BlockSpec((1, S, LANES), full),
            pl.BlockSpec((1, S, KV_RANK), full),
            pl.BlockSpec((1, KV_EXT, S), full),
            pl.BlockSpec((A_HEADS, NEAR + TQ, TQ), lambda b, i: (0, 0, 0)),
            pl.BlockSpec((1, A_HEADS, A_HEAD_DIM, KV_RANK), lambda b, i: (l, 0, 0, 0)),
        ],
        out_specs=pl.BlockSpec((1, TQ, A_WIDTH), lambda b, i: (b, i, 0)),
        out_shape=jax.ShapeDtypeStruct((B, S, A_WIDTH), BF16),
        scratch_shapes=[
            pltpu.VMEM((S, TQ), I32),
            pltpu.VMEM((32, S // PLANE_ROWS * SUBLANES, TQ), I32),
            pltpu.VMEM((S, TQ), F32),
            pltpu.VMEM((NEAR, TQ), F32),
            pltpu.VMEM((UNIT, A_HEADS * TQ), F32),
            pltpu.VMEM((UNIT, A_HEADS * TQ), F32),
            pltpu.VMEM((SUBLANES, A_HEADS * TQ), F32),
            pltpu.VMEM((SUBLANES, A_HEADS * TQ), F32),
            pltpu.VMEM((KV_EXT, A_HEADS * TQ), F32),
            pltpu.VMEM((A_WIDTH, TQ), F32),
        ],
        compiler_params=_cparams(("arbitrary", "arbitrary")),
        name="dsa_attention",
    )(iq, iwT, qlat, ikA, ikB, ckv, ckvT, bn, wuvT)


def _hgrn_constants():
    c = CHUNK
    r = np.arange(c)[:, None]
    jj = np.arange(c)[None, :]
    mats = [(jj <= r), (jj > r)]
    masks = [np.eye(c, dtype=bool)]
    m = c // 2
    while m >= 1:
        start = (r // (2 * m)) * (2 * m)
        bd = start + m - 1
        upper = r > bd
        mats.append(np.where(upper, (jj > bd) & (jj <= r), (jj > r) & (jj <= bd)))
        same_parent = (r // (2 * m)) == (jj // (2 * m))
        masks.append(same_parent & upper & (jj <= (jj // (2 * m)) * (2 * m) + m - 1))
        m //= 2
    m_all = np.concatenate(mats, axis=0).astype(np.float32)
    total = np.zeros((c, c), np.int32)
    for mk in masks:
        total += mk
    assert (total == np.tril(np.ones((c, c), np.int32))).all()
    return np.concatenate([m_all] * 3, axis=1), np.stack(masks).astype(np.float32)


_HGRN_M3, _HGRN_MASKS = _hgrn_constants()
_HGRN_LEVELS = _HGRN_MASKS.shape[0] - 1
HGRN_STEP_CHUNKS = 2


def _hgrn_kernel(q_ref, k_ref, lf_ref, v_ref, gate_ref, m3_ref, mask_ref, gn_ref, out_ref, st_ref):
    @pl.when(pl.program_id(1) == 0)
    def _():
        st_ref[...] = jnp.zeros(st_ref.shape, F32)

    c = CHUNK
    for ci in range(HGRN_STEP_CHUNKS):
        rows = slice(ci * c, (ci + 1) * c)
        g = lf_ref[0, rows, :]
        g_hi = g.astype(BF16)
        r1 = g - g_hi.astype(F32)
        g_mid = r1.astype(BF16)
        g_lo = (r1 - g_mid.astype(F32)).astype(BF16)
        sums = jnp.dot(m3_ref[...], jnp.concatenate([g_hi, g_mid, g_lo], axis=0), preferred_element_type=F32)
        e_all = jnp.exp(sums)
        for h in range(B_HEADS):
            cols = slice(h * B_KEY_DIM, (h + 1) * B_KEY_DIM)
            qh = q_ref[0, rows, cols]
            kh = k_ref[0, rows, cols]
            vh = v_ref[0, rows, cols]
            att = mask_ref[0] * _nt_dot(qh.astype(BF16), kh.astype(BF16))
            for lv in range(_HGRN_LEVELS):
                e_l = e_all[(2 + lv) * c:(3 + lv) * c, cols]
                att = att + mask_ref[lv + 1] * _nt_dot((qh * e_l).astype(BF16), (kh * e_l).astype(BF16))
            e_b = e_all[0:c, cols]
            e_rem = e_all[c:2 * c, cols]
            st = st_ref[h]
            o = jnp.dot(att.astype(BF16), vh, preferred_element_type=F32)
            o = o + _nt_dot((qh * e_b).astype(BF16), st.astype(BF16))
            upd = lax.dot_general(vh, (kh * e_rem).astype(BF16), (((0,), (0,)), ((), ())),
                                  preferred_element_type=F32)
            st_ref[h] = st * e_b[c - 1:c, :] + upd
            o = o * lax.rsqrt(jnp.mean(o * o, axis=-1, keepdims=True) + RMS_EPS) * gn_ref[0]
            out_ref[0, rows, cols] = (o * gate_ref[0, rows, cols]).astype(BF16)


def _hgrn(l, hq, hk, hlf, hv, hgate, gnorm):
    B, S, W = hq.shape
    ts = CHUNK * HGRN_STEP_CHUNKS
    tok = lambda b, i: (b, i, 0)
    return pl.pallas_call(
        _hgrn_kernel,
        grid=(B, S // ts),
        in_specs=[
            pl.BlockSpec((1, ts, W), tok),
            pl.BlockSpec((1, ts, W), tok),
            pl.BlockSpec((1, ts, W), tok),
            pl.BlockSpec((1, ts, W), tok),
            pl.BlockSpec((1, ts, W), tok),
            pl.BlockSpec(_HGRN_M3.shape, lambda b, i: (0, 0)),
            pl.BlockSpec(_HGRN_MASKS.shape, lambda b, i: (0, 0, 0)),
            pl.BlockSpec((1, 1, B_VAL_DIM), lambda b, i: (l, 0, 0)),
        ],
        out_specs=pl.BlockSpec((1, ts, W), tok),
        out_shape=jax.ShapeDtypeStruct((B, S, W), BF16),
        scratch_shapes=[pltpu.VMEM((B_HEADS, B_VAL_DIM, B_KEY_DIM), F32)],
        compiler_params=_cparams(("arbitrary", "arbitrary")),
        name="hgrn2",
    )(hq, hk, hlf, hv, hgate, jnp.asarray(_HGRN_M3, BF16), jnp.asarray(_HGRN_MASKS), gnorm)


def _layernorm(v, g, b):
    mu = jnp.mean(v, axis=-1, keepdims=True)
    d = v - mu
    var = jnp.mean(d * d, axis=-1, keepdims=True)
    return d * lax.rsqrt(var + LN_EPS) * g + b


def _first_argmax(v, idx, axes, big):
    mx = v
    for ax in axes:
        mx = jnp.max(mx, axis=ax, keepdims=True)
    pos = jnp.where(v == mx, idx, big)
    for ax in axes:
        pos = jnp.min(pos, axis=ax, keepdims=True)
    return mx, pos


def _outproj_kernel(ya_ref, yb_ref, x_ref, mod_ref, wo_ref, lng_ref, lnb_ref, wrT_ref, rbias_ref, tri_ref,
                    x1_ref, u2_ref, gates_ref, slotsT_ref, route_ref, *, alpha):
    y = jnp.dot(ya_ref[0], wo_ref[0, 0:A_WIDTH, :], preferred_element_type=F32)
    y = y + jnp.dot(yb_ref[0], wo_ref[0, A_WIDTH:, :], preferred_element_type=F32)
    g1 = mod_ref[0, 2:3, :]
    x1 = _layernorm(alpha * x_ref[0] + (1.0 + g1) * y, lng_ref[0], lnb_ref[0])
    x1_ref[0] = x1
    u2 = (x1 * (1.0 + mod_ref[0, 4:5, :]) + mod_ref[0, 3:4, :]).astype(BF16)
    u2_ref[0] = u2

    tm = u2.shape[0]
    gsz = N_EXPERTS // N_GROUPS
    scores = 1.0 / (1.0 + jnp.exp(-_nt_dot(wrT_ref[0], u2)))
    sel = (scores + rbias_ref[0]).reshape(N_GROUPS, gsz, tm)
    scores = scores.reshape(N_GROUPS, gsz, tm)
    i_m = lax.broadcasted_iota(I32, (N_GROUPS, gsz, tm), 1)
    i_g = lax.broadcasted_iota(I32, (N_GROUPS, 1, tm), 0)
    i_e = lax.broadcasted_iota(I32, (N_GROUPS, gsz, tm), 0) * gsz + i_m
    m1, p1 = _first_argmax(sel, i_m, (1,), gsz)
    m2 = jnp.max(jnp.where(i_m == p1, NEG_INF, sel), axis=1, keepdims=True)
    gs = m1 + m2
    gmask = jnp.zeros(gs.shape, F32)
    for _ in range(TOPK_GROUPS):
        _, pg = _first_argmax(gs, i_g, (0,), N_GROUPS)
        hit = i_g == pg
        gmask = jnp.where(hit, 1.0, gmask)
        gs = jnp.where(hit, NEG_INF, gs)
    cand = jnp.where(jnp.broadcast_to(gmask, sel.shape) > 0.0, sel, NEG_INF)
    w = jnp.zeros(sel.shape, F32)
    chosen = jnp.zeros(sel.shape, F32)
    picks = []
    for _ in range(TOP_K):
        _, pe = _first_argmax(cand, i_e, (1, 0), N_EXPERTS)
        hit = i_e == pe
        w = jnp.where(hit, scores, w)
        chosen = jnp.where(hit, 1.0, chosen)
        cand = jnp.where(hit, NEG_INF, cand)
        picks.append(pe)
    wsum = jnp.sum(jnp.sum(w, axis=1, keepdims=True), axis=0, keepdims=True)
    gates3 = w / wsum * ROUTED_SCALE
    gates = gates3.reshape(N_EXPERTS, tm)
    g_hi = gates.astype(BF16).astype(F32)
    g_lo = (gates - g_hi).astype(BF16).astype(F32)
    gates_ref[0] = jnp.concatenate([g_hi, g_lo], axis=0).T.astype(BF16)

    chosen2 = chosen.reshape(N_EXPERTS, tm).astype(BF16)
    rank = jnp.concatenate(
        [jnp.dot(chosen2[:, g * MOE_GROUP:(g + 1) * MOE_GROUP], tri_ref[...], preferred_element_type=F32)
         for g in range(tm // MOE_GROUP)], axis=1).reshape(N_GROUPS, gsz, tm)
    slot_rows, gate_rows = [], []
    for pe in picks:
        mine = i_e == pe
        rk = jnp.sum(jnp.sum(jnp.where(mine, rank, 0.0), axis=1, keepdims=True), axis=0, keepdims=True)
        gk = jnp.sum(jnp.sum(jnp.where(mine, gates3, 0.0), axis=1, keepdims=True), axis=0, keepdims=True)
        slot = jnp.where(rk < MOE_CAP, pe.astype(F32) * MOE_CAP + rk, -1.0)
        slot_rows.append(slot.reshape(1, tm))
        gate_rows.append(gk.reshape(1, tm))
    slots = jnp.concatenate(slot_rows, axis=0)
    slotsT_ref[0] = slots.astype(I32)
    info = jnp.concatenate([slots, jnp.concatenate(gate_rows, axis=0),
                            jnp.zeros((LANES - 2 * TOP_K, tm), F32)], axis=0)
    route_ref[0] = info.T


def _outproj(l, ya, yb, x, mod, wo, ln_g, ln_b, wrT, rbias, alpha):
    B, S, D = x.shape
    tm = TM_PROJ
    tok = lambda b, i: (b, i, 0)
    lw3 = lambda b, i: (l, 0, 0)
    return pl.pallas_call(
        functools.partial(_outproj_kernel, alpha=alpha),
        grid=(B, S // tm),
        in_specs=[
            pl.BlockSpec((1, tm, A_WIDTH), tok),
            pl.BlockSpec((1, tm, B_WIDTH), tok),
            pl.BlockSpec((1, tm, D), tok),
            pl.BlockSpec((1, 6, D), lambda b, i: (b, 0, 0)),
            pl.BlockSpec((1, D, D), lw3),
            pl.BlockSpec((1, 1, D), lw3),
            pl.BlockSpec((1, 1, D), lw3),
            pl.BlockSpec((1, N_EXPERTS, D), lw3),
            pl.BlockSpec((1, N_EXPERTS, tm), lw3),
            pl.BlockSpec((MOE_GROUP, MOE_GROUP), lambda b, i: (0, 0)),
        ],
        out_specs=[pl.BlockSpec((1, tm, D), tok), pl.BlockSpec((1, tm, D), tok),
                   pl.BlockSpec((1, tm, 2 * N_EXPERTS), tok),
                   pl.BlockSpec((1, TOP_K, tm), lambda b, i: (b, 0, i)),
                   pl.BlockSpec((1, tm, LANES), tok)],
        out_shape=[jax.ShapeDtypeStruct((B, S, D), F32), jax.ShapeDtypeStruct((B, S, D), BF16),
                   jax.ShapeDtypeStruct((B, S, 2 * N_EXPERTS), BF16),
                   jax.ShapeDtypeStruct((B, TOP_K, S), I32),
                   jax.ShapeDtypeStruct((B, S, LANES), F32)],
        compiler_params=_cparams(("arbitrary", "arbitrary")),
        name="outproj_router",
    )(ya, yb, x, mod, wo, ln_g, ln_b, wrT, rbias,
      jnp.asarray(np.triu(np.ones((MOE_GROUP, MOE_GROUP), np.float32), 1), BF16))


DISPATCH_ROWS = 8 * MOE_CAP


def _dispatch_kernel(u_ref, slotsT_ref, x_ref):
    u = u_ref[...]
    slots = slotsT_ref[0]
    experts_per_chunk = DISPATCH_ROWS // MOE_CAP
    for c in range(N_EXPERTS * MOE_CAP // DISPATCH_ROWS):
        row = lax.broadcasted_iota(I32, (DISPATCH_ROWS, MOE_GROUP), 0) + c * DISPATCH_ROWS
        onehot = jnp.zeros((DISPATCH_ROWS, MOE_GROUP), F32)
        for k in range(TOP_K):
            onehot = jnp.where(row == slots[k:k + 1, :], 1.0, onehot)
        xs = jnp.dot(onehot.astype(BF16), u, preferred_element_type=F32).astype(BF16)
        x_ref[c * experts_per_chunk:(c + 1) * experts_per_chunk] = xs.reshape(experts_per_chunk, MOE_CAP, -1)


def _dispatch(u2, slotsT):
    T, D = u2.shape
    ng = T // MOE_GROUP
    gps = slotsT.shape[-1] // MOE_GROUP
    return pl.pallas_call(
        _dispatch_kernel,
        grid=(ng,),
        in_specs=[pl.BlockSpec((MOE_GROUP, D), lambda g: (g, 0)),
                  pl.BlockSpec((1, TOP_K, MOE_GROUP), lambda g: (g // gps, 0, g % gps))],
        out_specs=pl.BlockSpec((N_EXPERTS, MOE_CAP, D), lambda g: (0, g, 0)),
        out_shape=jax.ShapeDtypeStruct((N_EXPERTS, ng * MOE_CAP, D), BF16),
        compiler_params=_cparams(("arbitrary",)),
        name="moe_dispatch",
    )(u2, slotsT)


def _expert_kernel(x_ref, wgu_ref, wd_ref, y_ref):
    hgu = jnp.dot(x_ref[0], wgu_ref[0, 0], preferred_element_type=F32)
    h = _silu(hgu[:, :EXPERT_DIM]) * hgu[:, EXPERT_DIM:]
    y_ref[0] = jnp.dot(h.astype(BF16), wd_ref[0, 0], preferred_element_type=F32).astype(BF16)


def _experts(l, xs, wgu, wd):
    E, R, D = xs.shape
    tr = min(R, MOE_EXPERT_ROWS)
    assert R % tr == 0
    return pl.pallas_call(
        _expert_kernel,
        grid=(E, R // tr),
        in_specs=[pl.BlockSpec((1, tr, D), lambda e, i: (e, i, 0)),
                  pl.BlockSpec((1, 1, D, 2 * EXPERT_DIM), lambda e, i: (l, e, 0, 0)),
                  pl.BlockSpec((1, 1, EXPERT_DIM, D), lambda e, i: (l, e, 0, 0))],
        out_specs=pl.BlockSpec((1, tr, D), lambda e, i: (e, i, 0)),
        out_shape=jax.ShapeDtypeStruct((E, R, D), BF16),
        compiler_params=_cparams(("arbitrary", "arbitrary")),
        name="moe_experts",
    )(xs, wgu, wd)


def _combine_kernel(y_ref, route_ref, u_ref, x1_ref, mod_ref, sgu_ref, sd_ref, lng_ref, lnb_ref, out_ref, *, alpha):
    ys = y_ref[...].reshape(N_EXPERTS * MOE_CAP, -1)
    route = route_ref[...]
    lane = lax.broadcasted_iota(I32, (MOE_GROUP, N_EXPERTS * MOE_CAP), 1)
    pick = jnp.zeros((MOE_GROUP, N_EXPERTS * MOE_CAP), F32)
    for k in range(TOP_K):
        slot = route[:, k:k + 1].astype(I32)
        pick = jnp.where(lane == slot, route[:, TOP_K + k:TOP_K + k + 1], pick)
    routed = jnp.dot(pick.astype(BF16), ys, preferred_element_type=F32)
    hgu = jnp.dot(u_ref[...], sgu_ref[0], preferred_element_type=F32)
    hs = _silu(hgu[:, :SHARED_DIM]) * hgu[:, SHARED_DIM:]
    y = routed + jnp.dot(hs.astype(BF16), sd_ref[0], preferred_element_type=F32)
    g2 = mod_ref[0, 5:6, :]
    out_ref[...] = _layernorm(alpha * x1_ref[...] + (1.0 + g2) * y, lng_ref[0], lnb_ref[0])


def _combine(l, ys, route, u2, x1, mod, sgu, sd, ln_g, ln_b, alpha, seq):
    T, D = u2.shape
    tok = lambda g: (g, 0)
    lw3 = lambda g: (l, 0, 0)
    return pl.pallas_call(
        functools.partial(_combine_kernel, alpha=alpha),
        grid=(T // MOE_GROUP,),
        in_specs=[
            pl.BlockSpec((N_EXPERTS, MOE_CAP, D), lambda g: (0, g, 0)),
            pl.BlockSpec((MOE_GROUP, LANES), tok),
            pl.BlockSpec((MOE_GROUP, D), tok),
            pl.BlockSpec((MOE_GROUP, D), tok),
            pl.BlockSpec((1, 6, D), lambda g: ((g * MOE_GROUP) // seq, 0, 0)),
            pl.BlockSpec((1, D, 2 * SHARED_DIM), lw3),
            pl.BlockSpec((1, SHARED_DIM, D), lw3),
            pl.BlockSpec((1, 1, D), lw3),
            pl.BlockSpec((1, 1, D), lw3),
        ],
        out_specs=pl.BlockSpec((MOE_GROUP, D), tok),
        out_shape=jax.ShapeDtypeStruct((T, D), F32),
        compiler_params=_cparams(("arbitrary",)),
        name="moe_combine",
    )(ys, route, u2, x1, mod, sgu, sd, ln_g, ln_b)


def _moe_kernel(u_ref, gates_ref, x1_ref, mod_ref, wgu_ref, wd_ref, sgu_ref, sd_ref, lng_ref, lnb_ref,
                out_ref, acc_ref, *, alpha):
    s = pl.program_id(1)
    u = u_ref[...]

    def hidden(wgu):
        hgu = jnp.dot(u, wgu, preferred_element_type=F32)
        return _silu(hgu[:, :EXPERT_DIM]) * hgu[:, EXPERT_DIM:]

    @pl.when(s == 0)
    def _():
        acc_ref[...] = jnp.dot(hidden(sgu_ref[0]).astype(BF16), sd_ref[0], preferred_element_type=F32)

    rows = lax.broadcasted_iota(I32, (2 * N_EXPERTS, MOE_EXPERTS_PER_STEP * EXPERT_DIM), 0) & (N_EXPERTS - 1)
    cols = lax.broadcasted_iota(I32, (2 * N_EXPERTS, MOE_EXPERTS_PER_STEP * EXPERT_DIM), 1)
    onehot = jnp.where(rows == s * MOE_EXPERTS_PER_STEP + cols // EXPERT_DIM, 1.0, 0.0).astype(BF16)
    gate = jnp.dot(gates_ref[...], onehot, preferred_element_type=F32)
    h = jnp.concatenate(
        [(hidden(wgu_ref[0, k]) * gate[:, k * EXPERT_DIM:(k + 1) * EXPERT_DIM]).astype(BF16)
         for k in range(MOE_EXPERTS_PER_STEP)], axis=1)
    wd = wd_ref[0].reshape(MOE_EXPERTS_PER_STEP * EXPERT_DIM, wd_ref.shape[-1])
    acc_ref[...] += jnp.dot(h, wd, preferred_element_type=F32)

    @pl.when(s == pl.num_programs(1) - 1)
    def _():
        g2 = mod_ref[0, 5:6, :]
        out_ref[...] = _layernorm(alpha * x1_ref[...] + (1.0 + g2) * acc_ref[...], lng_ref[0], lnb_ref[0])


def _moe(l, u2, gates, x1, mod, wgu, wd, sgu, sd, ln_g, ln_b, alpha, seq):
    T, D = u2.shape
    tm = TM_MOE
    assert seq % tm == 0
    tok = lambda i, e: (i, 0)
    lw3 = lambda i, e: (l, 0, 0)
    return pl.pallas_call(
        functools.partial(_moe_kernel, alpha=alpha),
        grid=(T // tm, N_EXPERTS // MOE_EXPERTS_PER_STEP),
        in_specs=[
            pl.BlockSpec((tm, D), tok),
            pl.BlockSpec((tm, 2 * N_EXPERTS), tok),
            pl.BlockSpec((tm, D), tok),
            pl.BlockSpec((1, 6, D), lambda i, e: ((i * tm) // seq, 0, 0)),
            pl.BlockSpec((1, MOE_EXPERTS_PER_STEP, D, 2 * EXPERT_DIM), lambda i, e: (l, e, 0, 0)),
            pl.BlockSpec((1, MOE_EXPERTS_PER_STEP, EXPERT_DIM, D), lambda i, e: (l, e, 0, 0)),
            pl.BlockSpec((1, D, 2 * SHARED_DIM), lw3),
            pl.BlockSpec((1, SHARED_DIM, D), lw3),
            pl.BlockSpec((1, 1, D), lw3),
            pl.BlockSpec((1, 1, D), lw3),
        ],
        out_specs=pl.BlockSpec((tm, D), tok),
        out_shape=jax.ShapeDtypeStruct((T, D), F32),
        scratch_shapes=[pltpu.VMEM((tm, D), F32)],
        compiler_params=_cparams(("arbitrary", "arbitrary")),
        name="moe_dense",
    )(u2, gates, x1, mod, wgu, wd, sgu, sd, ln_g, ln_b)


def _prepare_params(w_in, kv_norm_g, w_uk, w_uv, hgrn_lb, w_out, w_router, router_bias,
                    w_gate, w_up, w_down, ws_gate, ws_up, ws_down):
    L = w_in.shape[0]
    sizes = (A_WIDTH, KV_RANK, IDX_HEADS * IDX_DIM, IDX_DIM, IDX_HEADS, B_FDIM, B_FDIM, B_WIDTH, B_WIDTH)
    offs = np.concatenate([[0], np.cumsum(sizes)])
    seg = lambda i: w_in[:, :, offs[i]:offs[i + 1]]
    w_aq, w_ckv, w_iq, w_ik, w_iw, w_hq, w_hf, w_hi, w_hg = (seg(i) for i in range(9))
    zik = jnp.zeros_like(w_ik)
    wp = jnp.concatenate([w_aq, w_ckv, w_iq, w_ik, zik, zik, w_ik, w_hq, w_hf, w_hg, w_hi], axis=-1).astype(BF16)
    assert wp.shape[-1] == _C_END
    eye = jnp.eye(A_HEADS, dtype=F32)
    wblk = (jnp.einsum('lhdr,hg->lhdgr', w_uk * (ATTN_SCALE * LOG2E), eye)
            .reshape(L, A_WIDTH, A_HEADS * KV_RANK).astype(BF16))
    p = dict(
        wp=wp, wblk=wblk,
        wckvT=jnp.swapaxes(w_ckv, 1, 2).astype(BF16),
        wiwT=jnp.swapaxes(w_iw, 1, 2).astype(BF16),
        gkv=kv_norm_g.reshape(L, 1, KV_RANK),
        gkvT=jnp.broadcast_to(kv_norm_g[:, :, None], (L, KV_RANK, TM_PROJ)),
        wuvT=jnp.swapaxes(w_uv, 2, 3).astype(BF16),
        wo=w_out.astype(BF16),
        wrT=jnp.swapaxes(w_router, 1, 2).astype(BF16),
        rbias=jnp.broadcast_to(router_bias[:, :, None], (L, N_EXPERTS, TM_PROJ)),
        wgu=jnp.concatenate([w_gate, w_up], axis=-1).astype(BF16),
        wd=w_down.astype(BF16),
        sgu=jnp.concatenate([ws_gate, ws_up], axis=-1).astype(BF16),
        sd=ws_down.astype(BF16),
    )
    lbs = jnp.cumsum(jax.nn.softmax(hgrn_lb.astype(F32), axis=0), axis=0)
    lbs = jnp.clip(lbs - lbs[0:1], 0.0, 1.0 - 1e-6)
    p["llb"] = jnp.log(lbs).reshape(L, 1, B_FDIM)
    p["l1m"] = jnp.log1p(-lbs).reshape(L, 1, B_FDIM)
    return p


def kernel(x, c, w_ada, b_ada, w_in, kv_norm_g, w_uk, w_uv, rel_bias, hgrn_lb, gnorm_g, w_out, ln1_g, ln1_b,
           w_router, router_bias, w_gate, w_up, w_down, ws_gate, ws_up, ws_down, ln2_g, ln2_b):
    B, S, D = x.shape
    L = w_in.shape[0]
    alpha = (2 * L) ** 0.25
    p = _prepare_params(w_in, kv_norm_g, w_uk, w_uv, hgrn_lb, w_out, w_router, router_bias,
                        w_gate, w_up, w_down, ws_gate, ws_up, ws_down)
    mods = _adaln(c, w_ada, b_ada).reshape(L, B, 6, D)
    bn = _bias_tile(rel_bias)
    gn = gnorm_g.reshape(L, 1, B_VAL_DIM)
    ln1g, ln1b = ln1_g.reshape(L, 1, D), ln1_b.reshape(L, 1, D)
    ln2g, ln2b = ln2_g.reshape(L, 1, D), ln2_b.reshape(L, 1, D)
    for l in range(L):
        mod = mods[l]
        (qlat, ckv, ckvT, iq, ikA, ikB, iwT, hq, hk, hlf, hv, hgate) = _inproj(
            l, x, mod, p["wp"], p["wblk"], p["wckvT"], p["wiwT"], p["gkv"], p["gkvT"], p["llb"], p["l1m"])
        ya = _dsa(l, iq, iwT, qlat, ikA, ikB, ckv, ckvT, bn, p["wuvT"])
        yb = _hgrn(l, hq, hk, hlf, hv, hgate, gn)
        x1, u2, gates, slots_t, route = _outproj(l, ya, yb, x, mod, p["wo"], ln1g, ln1b, p["wrT"], p["rbias"], alpha)
        u2f, x1f = u2.reshape(B * S, D), x1.reshape(B * S, D)

        def moe_sparse(l=l, mod=mod, u2f=u2f, x1f=x1f, slots_t=slots_t, route=route):
            ys = _experts(l, _dispatch(u2f, slots_t), p["wgu"], p["wd"])
            return _combine(l, ys, route.reshape(B * S, LANES), u2f, x1f, mod, p["sgu"], p["sd"], ln2g, ln2b, alpha, S)

        def moe_dense(l=l, mod=mod, u2f=u2f, x1f=x1f, gates=gates):
            return _moe(l, u2f, gates.reshape(B * S, 2 * N_EXPERTS), x1f, mod,
                        p["wgu"], p["wd"], p["sgu"], p["sd"], ln2g, ln2b, alpha, S)

        x = lax.cond(jnp.any(slots_t < 0), moe_dense, moe_sparse).reshape(B, S, D)
    return x
```

```python
import functools
import math

import numpy as np
import jax
import jax.numpy as jnp
from jax import lax
from jax.experimental import pallas as pl
from jax.experimental.pallas import tpu as pltpu

F32 = jnp.float32
BF16 = jnp.bfloat16
I32 = jnp.int32

D_MODEL = 1024
CHUNK = 64
A_HEADS = 8
A_HEAD_DIM = 64
A_WIDTH = A_HEADS * A_HEAD_DIM
KV_RANK = 128
IDX_HEADS = 8
IDX_DIM = 64
IDX_TOPK_MAX = 256
IDX_W_SCALE = (IDX_HEADS ** -0.5) * (IDX_DIM ** -0.5)
ATTN_SCALE = A_HEAD_DIM ** -0.5
LOG2E = math.log2(math.e)
KV_EXT = KV_RANK + 16
NUM_BUCKETS = 32
MAX_DISTANCE = 128
B_HEADS = 4
B_KEY_DIM = 128
B_VAL_DIM = 128
B_WIDTH = B_HEADS * B_VAL_DIM
B_FDIM = B_HEADS * B_KEY_DIM
N_EXPERTS = 64
TOP_K = 8
N_GROUPS = 8
TOPK_GROUPS = 4
EXPERT_DIM = 256
SHARED_DIM = 256
ROUTED_SCALE = 2.5
LN_EPS = 1e-5
RMS_EPS = 1e-6

LANES = 128
SUBLANES = 8
VMEM_LIMIT_BYTES = 56 * 1024 * 1024

INT_MIN = -(2 ** 31)
NEG_INF = float("-inf")

TM_PROJ = 512
TQ = 128
UNIT = 512
NEAR = 2 * TQ
COUNT_ACCS = 8
PLANE_ROWS = 32 * SUBLANES
TM_MOE = 1024
MOE_EXPERTS_PER_STEP = 4
MOE_GROUP = 256
MOE_CAP = 80
MOE_EXPERT_ROWS = 2048

_C_AQ, _C_CKV, _C_IQ, _C_IKA, _C_IKB, _C_HQ, _C_HF, _C_HG, _C_HI, _C_END = (
    0, 512, 640, 1152, 1280, 1408, 1920, 2432, 2944, 3456)


def _silu(v):
    return v * (1.0 / (1.0 + jnp.exp(-v)))


def _nt_dot(a, b):
    return lax.dot_general(a, b, (((1,), (1,)), ((), ())), preferred_element_type=F32)


def _cparams(sem):
    return pltpu.CompilerParams(dimension_semantics=sem, vmem_limit_bytes=VMEM_LIMIT_BYTES)


def _adaln_kernel(c_ref, w_ref, b_ref, o_ref):
    cond = _silu(c_ref[...])
    o_ref[0] = jnp.dot(cond.astype(BF16), w_ref[0].astype(BF16), preferred_element_type=F32) + b_ref[0]


def _adaln(c, w_ada, b_ada):
    L, D, D6 = w_ada.shape
    B = c.shape[0]
    nb = D6 // D
    return pl.pallas_call(
        _adaln_kernel,
        grid=(L, nb),
        in_specs=[
            pl.BlockSpec((B, D), lambda l, j: (0, 0)),
            pl.BlockSpec((1, D, D), lambda l, j: (l, 0, j)),
            pl.BlockSpec((1, 1, D), lambda l, j: (l, 0, j)),
        ],
        out_specs=pl.BlockSpec((1, B, D), lambda l, j: (l, 0, j)),
        out_shape=jax.ShapeDtypeStruct((L, B, D6), F32),
        compiler_params=_cparams(("arbitrary", "arbitrary")),
        name="adaln_mod",
    )(c, w_ada, b_ada.reshape(L, 1, D6))


_T5_NB = NUM_BUCKETS // 2
_T5_EXACT = _T5_NB // 2
_T5_THRESHOLDS = tuple(
    int(math.ceil(_T5_EXACT * (MAX_DISTANCE / _T5_EXACT) ** (j / (_T5_NB - _T5_EXACT)) - 1e-9))
    for j in range(1, _T5_NB - _T5_EXACT))
FAR_BUCKET = _T5_NB - 1
assert _T5_THRESHOLDS[-1] <= TQ, "keys further than one query block behind must share the far bucket"


def _bias_kernel(rb_ref, o_ref):
    kr = lax.broadcasted_iota(I32, (NEAR + TQ, TQ), 0)
    ql = lax.broadcasted_iota(I32, (NEAR + TQ, TQ), 1)
    rel = kr - TQ - ql
    n = jnp.abs(rel)
    large = jnp.full(rel.shape, _T5_EXACT, I32)
    for t in _T5_THRESHOLDS:
        large = large + (n >= t).astype(I32)
    bucket = jnp.where(rel > 0, _T5_NB, 0) + jnp.where(n < _T5_EXACT, n, large)
    for h in range(A_HEADS):
        acc = jnp.zeros(rel.shape, F32)
        for bk in range(NUM_BUCKETS):
            acc = jnp.where(bucket == bk, rb_ref[bk, h], acc)
        o_ref[h] = (acc - rb_ref[FAR_BUCKET, h]) * LOG2E


def _bias_tile(rel_bias):
    return pl.pallas_call(
        _bias_kernel,
        in_specs=[pl.BlockSpec(memory_space=pltpu.SMEM)],
        out_specs=pl.BlockSpec(memory_space=pltpu.VMEM),
        out_shape=jax.ShapeDtypeStruct((A_HEADS, NEAR + TQ, TQ), F32),
        name="rel_bias_tile",
    )(rel_bias)


def _inproj_kernel(x_ref, mod_ref, wp_ref, wblk_ref, wckvT_ref, wiwT_ref, gkv_ref, gkvT_ref, llb_ref, l1m_ref,
                   qlat_ref, ckv_ref, ckvT_ref, iq_ref, ikA_ref, ikB_ref, iwT_ref,
                   hq_ref, hk_ref, hlf_ref, hv_ref, hgate_ref):
    x = x_ref[0]
    sh1 = mod_ref[0, 0:1, :]
    sc1 = mod_ref[0, 1:2, :]
    u = (x * (1.0 + sc1) + sh1).astype(BF16)
    z = jnp.dot(u, wp_ref[0], preferred_element_type=F32)

    ql = jnp.dot(z[:, _C_AQ:_C_CKV].astype(BF16), wblk_ref[0], preferred_element_type=F32)
    for h in range(A_HEADS):
        qlat_ref[0, h] = ql[:, h * KV_RANK:(h + 1) * KV_RANK].astype(BF16)

    zc = z[:, _C_CKV:_C_IQ]
    inv = lax.rsqrt(jnp.mean(zc * zc, axis=-1, keepdims=True) + RMS_EPS)
    ckv_ref[0] = (zc * inv * gkv_ref[0]).astype(BF16)
    zt = _nt_dot(wckvT_ref[0], u)
    inv_t = lax.rsqrt(jnp.mean(zt * zt, axis=0, keepdims=True) + RMS_EPS)
    ckvT_ref[0, 0:KV_RANK, :] = (zt * inv_t * gkvT_ref[0]).astype(BF16)
    ckvT_ref[0, KV_RANK:KV_EXT, :] = jnp.ones((KV_EXT - KV_RANK, zt.shape[1]), BF16)

    for p in range(IDX_HEADS // 2):
        iq_ref[0, p] = z[:, _C_IQ + p * LANES:_C_IQ + (p + 1) * LANES].astype(BF16)
    ikA_ref[0] = z[:, _C_IKA:_C_IKB].astype(BF16)
    ikB_ref[0] = z[:, _C_IKB:_C_HQ].astype(BF16)
    iwT_ref[0] = _nt_dot(wiwT_ref[0], u) * IDX_W_SCALE

    hq_ref[0] = _silu(z[:, _C_HQ:_C_HF])
    zf = z[:, _C_HF:_C_HG]
    log_sig = jnp.minimum(zf, 0.0) - jnp.log1p(jnp.exp(-jnp.abs(zf)))
    a = llb_ref[0]
    c = l1m_ref[0] + log_sig
    logf = jnp.maximum(a, c) + jnp.log1p(jnp.exp(-jnp.abs(a - c)))
    hlf_ref[0] = logf
    hk_ref[0] = 1.0 - jnp.exp(logf)
    hgate_ref[0] = _silu(z[:, _C_HG:_C_HI])
    hv_ref[0] = z[:, _C_HI:_C_END].astype(BF16)


def _inproj(l, x, mod, wp, wblk, wckvT, wiwT, gkv, gkvT, llb, l1m):
    B, S, D = x.shape
    tm = TM_PROJ
    grid = (B, S // tm)
    lw3 = lambda b, i: (l, 0, 0)
    tok = lambda b, i: (b, i, 0)
    tokT = lambda b, i: (b, 0, i)
    hd4 = lambda b, i: (b, 0, i, 0)
    outs = [
        (jax.ShapeDtypeStruct((B, A_HEADS, S, KV_RANK), BF16), pl.BlockSpec((1, A_HEADS, tm, KV_RANK), hd4)),
        (jax.ShapeDtypeStruct((B, S, KV_RANK), BF16), pl.BlockSpec((1, tm, KV_RANK), tok)),
        (jax.ShapeDtypeStruct((B, KV_EXT, S), BF16), pl.BlockSpec((1, KV_EXT, tm), tokT)),
        (jax.ShapeDtypeStruct((B, IDX_HEADS // 2, S, LANES), BF16), pl.BlockSpec((1, IDX_HEADS // 2, tm, LANES), hd4)),
        (jax.ShapeDtypeStruct((B, S, LANES), BF16), pl.BlockSpec((1, tm, LANES), tok)),
        (jax.ShapeDtypeStruct((B, S, LANES), BF16), pl.BlockSpec((1, tm, LANES), tok)),
        (jax.ShapeDtypeStruct((B, IDX_HEADS, S), F32), pl.BlockSpec((1, IDX_HEADS, tm), tokT)),
        (jax.ShapeDtypeStruct((B, S, B_FDIM), F32), pl.BlockSpec((1, tm, B_FDIM), tok)),
        (jax.ShapeDtypeStruct((B, S, B_FDIM), F32), pl.BlockSpec((1, tm, B_FDIM), tok)),
        (jax.ShapeDtypeStruct((B, S, B_FDIM), F32), pl.BlockSpec((1, tm, B_FDIM), tok)),
        (jax.ShapeDtypeStruct((B, S, B_WIDTH), BF16), pl.BlockSpec((1, tm, B_WIDTH), tok)),
        (jax.ShapeDtypeStruct((B, S, B_WIDTH), F32), pl.BlockSpec((1, tm, B_WIDTH), tok)),
    ]
    return pl.pallas_call(
        _inproj_kernel,
        grid=grid,
        in_specs=[
            pl.BlockSpec((1, tm, D), tok),
            pl.BlockSpec((1, 6, D), lambda b, i: (b, 0, 0)),
            pl.BlockSpec((1, D, _C_END), lw3),
            pl.BlockSpec((1, A_WIDTH, A_HEADS * KV_RANK), lw3),
            pl.BlockSpec((1, KV_RANK, D), lw3),
            pl.BlockSpec((1, IDX_HEADS, D), lw3),
            pl.BlockSpec((1, 1, KV_RANK), lw3),
            pl.BlockSpec((1, KV_RANK, tm), lw3),
            pl.BlockSpec((1, 1, B_FDIM), lw3),
            pl.BlockSpec((1, 1, B_FDIM), lw3),
        ],
        out_specs=[o[1] for o in outs],
        out_shape=[o[0] for o in outs],
        compiler_params=_cparams(("arbitrary", "arbitrary")),
        name="inproj",
    )(x, mod, wp, wblk, wckvT, wiwT, gkv, gkvT, llb, l1m)


def _dsa_kernel(iq_ref, iwT_ref, qlat_ref, ikA_ref, ikB_ref, ckv_ref, ckvT_ref, bn_ref, wuvT_ref, out_ref,
                sc_ref, plane_ref, madd_ref, maddn_ref, la_ref, lb_ref, pma_ref, pmb_ref, ot_ref, yaT_ref,
                *, k_sel, n_idx_bits):
    j = pl.program_id(1)
    q0 = j * TQ
    nk = q0 + TQ
    nunit = (nk + UNIT - 1) // UNIT
    near0 = pl.multiple_of(jnp.maximum(nk - NEAR, 0), TQ)
    bn_row0 = pl.multiple_of(jnp.where(j == 0, TQ, 0), TQ)
    lane = lax.broadcasted_iota(I32, (1, TQ), 1)
    limit = (((q0 + lane) >> 6) + 1) << 6
    row_iota = lax.broadcasted_iota(I32, (UNIT, TQ), 0)

    def unit_rows(u):
        return pl.ds(pl.multiple_of(u * UNIT, UNIT), UNIT)

    iqs = iq_ref[0].reshape(IDX_HEADS // 2 * TQ, LANES)
    iw = iwT_ref[0]

    def score_unit(u, carry):
        rows = unit_rows(u)
        xe = _nt_dot(ikA_ref[0, rows, :], iqs)
        xo = _nt_dot(ikB_ref[0, rows, :], iqs)
        acc = jnp.zeros((UNIT, TQ), F32)
        for p in range(IDX_HEADS // 2):
            acc = acc + iw[2 * p:2 * p + 1, :] * jnp.maximum(xe[:, p * TQ:(p + 1) * TQ], 0.0)
            acc = acc + iw[2 * p + 1:2 * p + 2, :] * jnp.maximum(xo[:, p * TQ:(p + 1) * TQ], 0.0)
        bits = lax.bitcast_convert_type(acc, I32)
        key = bits ^ ((bits >> 31) & 0x7FFFFFFF)
        key = jnp.where(row_iota + u * UNIT < limit, key, INT_MIN)
        sc_ref[rows, :] = key
        return carry

    lax.fori_loop(0, nunit, score_unit, 0)

    ngroups = (nk + PLANE_ROWS - 1) // PLANE_ROWS

    def plane_group(g, carry):
        rows = pl.ds(pl.multiple_of(g * PLANE_ROWS, PLANE_ROWS), PLANE_ROWS)
        words = (sc_ref[rows, :] ^ INT_MIN).reshape(32, SUBLANES, TQ)
        w = [words[i] for i in range(32)]
        j, m = 16, 0x0000FFFF
        while j:
            mask = np.int32(np.uint32(m).view(np.int32))
            k = 0
            while k < 32:
                t = (w[k] ^ lax.shift_right_logical(w[k + j], jnp.full(w[k].shape, j, I32))) & mask
                w[k] = w[k] ^ t
                w[k + j] = w[k + j] ^ (t << j)
                k = (k + j + 1) & ~j
            j >>= 1
            m = (m ^ (m << j)) & 0xFFFFFFFF
        for i in range(32):
            plane_ref[i, pl.ds(g * SUBLANES, SUBLANES), :] = w[i]
        return carry

    lax.fori_loop(0, ngroups, plane_group, 0)

    n_words = sc_ref.shape[0] // PLANE_ROWS * SUBLANES
    group_of_word = lax.broadcasted_iota(I32, (n_words, TQ), 0) // SUBLANES

    def bit_step(i, carry):
        alive, above, t_off, c_ge = carry
        hit = alive & plane_ref[i]
        cnt = above + jnp.sum(lax.population_count(hit), axis=0, keepdims=True)
        ok = cnt >= k_sel
        alive = jnp.where(ok, hit, alive ^ hit)
        above = jnp.where(ok, above, cnt)
        t_off = jnp.where(ok, t_off | (jnp.int32(1) << (31 - i)), t_off)
        return alive, above, t_off, jnp.where(ok, cnt, c_ge)

    zero_row = jnp.zeros((1, TQ), I32)
    _, _, t_off, c_ge = lax.fori_loop(
        0, 32, bit_step,
        (jnp.where(group_of_word < ngroups, jnp.int32(-1), jnp.int32(0)), zero_row, zero_row, zero_row))
    thr = jnp.maximum(t_off ^ INT_MIN, INT_MIN + 1)
    straddle = (c_ge > k_sel).astype(I32)

    def count_where(pred):
        def body(u, acc):
            hit = pred(sc_ref[unit_rows(u), :], u * UNIT).reshape(-1, COUNT_ACCS * SUBLANES, TQ)
            for s in range(hit.shape[0]):
                acc = jnp.where(hit[s], acc + 1, acc)
            return acc
        acc = lax.fori_loop(0, nunit, body, jnp.zeros((COUNT_ACCS * SUBLANES, TQ), I32))
        return jnp.sum(acc, axis=0, keepdims=True)

    def tie_bound():
        c_gt = count_where(lambda blk, r0: blk > thr)
        need = k_sel - c_gt

        def tie_body(i, j0):
            cand = j0 | (jnp.int32(1) << (n_idx_bits - 1 - i))
            cnt = count_where(lambda blk, r0: jnp.where(blk == thr, row_iota + r0, cand) < cand)
            return jnp.where(cnt < need, cand, j0)

        j0 = lax.fori_loop(0, n_idx_bits, tie_body, jnp.zeros((1, TQ), I32))
        return jnp.where(straddle > 0, j0 + 1, jnp.int32(2 ** n_idx_bits))

    jstar = lax.cond(jnp.max(straddle) > 0, tie_bound, lambda: jnp.full((1, TQ), 2 ** n_idx_bits, I32))

    def madd_unit(u, carry):
        rows = unit_rows(u)
        key = sc_ref[rows, :]
        tie_keep = jnp.where(row_iota + u * UNIT < jstar, 0.0, NEG_INF)
        madd_ref[rows, :] = jnp.where(key > thr, 0.0, jnp.where(key == thr, tie_keep, NEG_INF))
        return carry

    lax.fori_loop(0, nunit, madd_unit, 0)
    maddn_ref[...] = madd_ref[pl.ds(near0, NEAR), :]
    madd_ref[pl.ds(near0, NEAR), :] = jnp.full((NEAR, TQ), NEG_INF, F32)

    qall = qlat_ref[0].reshape(A_HEADS * TQ, KV_RANK)

    def col_max(v):
        return jnp.max(v.reshape(v.shape[0] // SUBLANES, SUBLANES, A_HEADS * TQ), axis=0)

    def fold(xl, part_max, ckv_t, m_old):
        m_new = jnp.maximum(m_old, jnp.max(part_max, axis=0, keepdims=True))
        m_use = jnp.where(m_new == NEG_INF, 0.0, m_new)
        p = jnp.exp2((xl - m_use).astype(BF16))
        ot_ref[...] = ot_ref[...] * jnp.exp2(m_old - m_use) + jnp.dot(ckv_t, p, preferred_element_type=F32)
        return m_new

    ot_ref[...] = jnp.zeros(ot_ref.shape, F32)
    near_rows = pl.ds(near0, NEAR)
    xn = _nt_dot(ckv_ref[0, near_rows, :], qall) + jnp.concatenate([maddn_ref[...]] * A_HEADS, axis=1)
    xn = xn + jnp.concatenate([bn_ref[h, pl.ds(bn_row0, NEAR), :] for h in range(A_HEADS)], axis=1)
    m_run = fold(xn, col_max(xn), ckvT_ref[0, :, near_rows], jnp.full((1, A_HEADS * TQ), NEG_INF, F32))

    last_unit = sc_ref.shape[0] // UNIT - 1

    def issue_logits(u, buf_ref, pm_ref):
        rows = unit_rows(jnp.minimum(u, last_unit))
        xl = _nt_dot(ckv_ref[0, rows, :], qall) + jnp.concatenate([madd_ref[rows, :]] * A_HEADS, axis=1)
        buf_ref[...] = xl
        pm_ref[...] = col_max(xl)

    def consume_logits(u, buf_ref, pm_ref, m_old):
        return fold(buf_ref[...], pm_ref[...], ckvT_ref[0, :, unit_rows(u)], m_old)

    @pl.when(nunit % 2 == 1)
    def _():
        madd_ref[unit_rows(nunit), :] = jnp.full((UNIT, TQ), NEG_INF, F32)

    issue_logits(0, la_ref, pma_ref)

    def pair_step(i, m_old):
        issue_logits(2 * i + 1, lb_ref, pmb_ref)
        m_mid = consume_logits(2 * i, la_ref, pma_ref, m_old)
        issue_logits(2 * i + 2, la_ref, pma_ref)
        return consume_logits(2 * i + 1, lb_ref, pmb_ref, m_mid)

    lax.fori_loop(0, (nunit + 1) // 2, pair_step, m_run)
    o_t = (ot_ref[0:KV_RANK, :] * (1.0 / ot_ref[KV_RANK:KV_RANK + 1, :])).astype(BF16)
    for h in range(A_HEADS):
        yaT_ref[h * A_HEAD_DIM:(h + 1) * A_HEAD_DIM, :] = jnp.dot(
            wuvT_ref[0, h], o_t[:, h * TQ:(h + 1) * TQ], preferred_element_type=F32)

    out_ref[0] = yaT_ref[...].T.astype(BF16)


def _dsa(l, iq, iwT, qlat, ikA, ikB, ckv, ckvT, bn, wuvT):
    B, S = ckv.shape[0], ckv.shape[1]
    assert S % (2 * UNIT) == 0 and UNIT % TQ == 0 and TQ % CHUNK == 0 and CHUNK == 64 and NEAR <= UNIT
    k_sel = min(IDX_TOPK_MAX, S // 4)
    n_idx_bits = int(math.log2(S))
    assert 2 ** n_idx_bits == S
    grid = (B, S // TQ)
    blk = lambda b, i: (b, 0, i, 0)
    full = lambda b, i: (b, 0, 0)
    kern = functools.partial(_dsa_kernel, k_sel=k_sel, n_idx_bits=n_idx_bits)
    return pl.pallas_call(
        kern,
        grid=grid,
        in_specs=[
            pl.BlockSpec((1, IDX_HEADS // 2, TQ, LANES), blk),
            pl.BlockSpec((1, IDX_HEADS, TQ), lambda b, i: (b, 0, i)),
            pl.BlockSpec((1, A_HEADS, TQ, KV_RANK), blk),
            pl.BlockSpec((1, S, LANES), full),
            pl.BlockSpec((1, S, LANES), full),
            pl.BlockSpec((1, S, KV_RANK), full),
            pl.BlockSpec((1, KV_EXT, S), full),
            pl.BlockSpec((A_HEADS, NEAR + TQ, TQ), lambda b, i: (0, 0, 0)),
            pl.BlockSpec((1, A_HEADS, A_HEAD_DIM, KV_RANK), lambda b, i: (l, 0, 0, 0)),
        ],
        out_specs=pl.BlockSpec((1, TQ, A_WIDTH), lambda b, i: (b, i, 0)),
        out_shape=jax.ShapeDtypeStruct((B, S, A_WIDTH), BF16),
        scratch_shapes=[
            pltpu.VMEM((S, TQ), I32),
            pltpu.VMEM((32, S // PLANE_ROWS * SUBLANES, TQ), I32),
            pltpu.VMEM((S, TQ), F32),
            pltpu.VMEM((NEAR, TQ), F32),
            pltpu.VMEM((UNIT, A_HEADS * TQ), F32),
            pltpu.VMEM((UNIT, A_HEADS * TQ), F32),
            pltpu.VMEM((SUBLANES, A_HEADS * TQ), F32),
            pltpu.VMEM((SUBLANES, A_HEADS * TQ), F32),
            pltpu.VMEM((KV_EXT, A_HEADS * TQ), F32),
            pltpu.VMEM((A_WIDTH, TQ), F32),
        ],
        compiler_params=_cparams(("arbitrary", "arbitrary")),
        name="dsa_attention",
    )(iq, iwT, qlat, ikA, ikB, ckv, ckvT, bn, wuvT)


def _hgrn_constants():
    c = CHUNK
    r = np.arange(c)[:, None]
    jj = np.arange(c)[None, :]
    mats = [(jj <= r), (jj > r)]
    masks = [np.eye(c, dtype=bool)]
    m = c // 2
    while m >= 1:
        start = (r // (2 * m)) * (2 * m)
        bd = start + m - 1
        upper = r > bd
        mats.append(np.where(upper, (jj > bd) & (jj <= r), (jj > r) & (jj <= bd)))
        same_parent = (r // (2 * m)) == (jj // (2 * m))
        masks.append(same_parent & upper & (jj <= (jj // (2 * m)) * (2 * m) + m - 1))
        m //= 2
    m_all = np.concatenate(mats, axis=0).astype(np.float32)
    total = np.zeros((c, c), np.int32)
    for mk in masks:
        total += mk
    assert (total == np.tril(np.ones((c, c), np.int32))).all()
    return np.concatenate([m_all] * 3, axis=1), np.stack(masks).astype(np.float32)


_HGRN_M3, _HGRN_MASKS = _hgrn_constants()
_HGRN_LEVELS = _HGRN_MASKS.shape[0] - 1
HGRN_STEP_CHUNKS = 2
HGRN_STEP_BATCH = 2


def _hgrn_kernel(q_ref, k_ref, lf_ref, v_ref, gate_ref, m3_ref, mask_ref, gn_ref, out_ref, st_ref):
    @pl.when(pl.program_id(1) == 0)
    def _():
        st_ref[...] = jnp.zeros(st_ref.shape, F32)

    c = CHUNK
    for bi, ci in [(b, ch) for ch in range(HGRN_STEP_CHUNKS) for b in range(HGRN_STEP_BATCH)]:
        rows = slice(ci * c, (ci + 1) * c)
        g = lf_ref[bi, rows, :]
        g_hi = g.astype(BF16)
        r1 = g - g_hi.astype(F32)
        g_mid = r1.astype(BF16)
        g_lo = (r1 - g_mid.astype(F32)).astype(BF16)
        sums = jnp.dot(m3_ref[...], jnp.concatenate([g_hi, g_mid, g_lo], axis=0), preferred_element_type=F32)
        e_all = jnp.exp(sums)
        for h in range(B_HEADS):
            cols = slice(h * B_KEY_DIM, (h + 1) * B_KEY_DIM)
            qh = q_ref[bi, rows, cols]
            kh = k_ref[bi, rows, cols]
            vh = v_ref[bi, rows, cols]
            att = mask_ref[0] * _nt_dot(qh.astype(BF16), kh.astype(BF16))
            for lv in range(_HGRN_LEVELS):
                e_l = e_all[(2 + lv) * c:(3 + lv) * c, cols]
                att = att + mask_ref[lv + 1] * _nt_dot((qh * e_l).astype(BF16), (kh * e_l).astype(BF16))
            e_b = e_all[0:c, cols]
            e_rem = e_all[c:2 * c, cols]
            st = st_ref[bi, h]
            o = jnp.dot(att.astype(BF16), vh, preferred_element_type=F32)
            o = o + _nt_dot((qh * e_b).astype(BF16), st.astype(BF16))
            upd = lax.dot_general(vh, (kh * e_rem).astype(BF16), (((0,), (0,)), ((), ())),
                                  preferred_element_type=F32)
            st_ref[bi, h] = st * e_b[c - 1:c, :] + upd
            o = o * lax.rsqrt(jnp.mean(o * o, axis=-1, keepdims=True) + RMS_EPS) * gn_ref[0]
            out_ref[bi, rows, cols] = (o * gate_ref[bi, rows, cols]).astype(BF16)


def _hgrn(l, hq, hk, hlf, hv, hgate, gnorm):
    B, S, W = hq.shape
    ts = CHUNK * HGRN_STEP_CHUNKS
    nb = HGRN_STEP_BATCH
    assert B % nb == 0
    tok = lambda b, i: (b, i, 0)
    return pl.pallas_call(
        _hgrn_kernel,
        grid=(B // nb, S // ts),
        in_specs=[
            pl.BlockSpec((nb, ts, W), tok),
            pl.BlockSpec((nb, ts, W), tok),
            pl.BlockSpec((nb, ts, W), tok),
            pl.BlockSpec((nb, ts, W), tok),
            pl.BlockSpec((nb, ts, W), tok),
            pl.BlockSpec(_HGRN_M3.shape, lambda b, i: (0, 0)),
            pl.BlockSpec(_HGRN_MASKS.shape, lambda b, i: (0, 0, 0)),
            pl.BlockSpec((1, 1, B_VAL_DIM), lambda b, i: (l, 0, 0)),
        ],
        out_specs=pl.BlockSpec((nb, ts, W), tok),
        out_shape=jax.ShapeDtypeStruct((B, S, W), BF16),
        scratch_shapes=[pltpu.VMEM((nb, B_HEADS, B_VAL_DIM, B_KEY_DIM), F32)],
        compiler_params=_cparams(("arbitrary", "arbitrary")),
        name="hgrn2",
    )(hq, hk, hlf, hv, hgate, jnp.asarray(_HGRN_M3, BF16), jnp.asarray(_HGRN_MASKS), gnorm)


def _layernorm(v, g, b):
    mu = jnp.mean(v, axis=-1, keepdims=True)
    d = v - mu
    var = jnp.mean(d * d, axis=-1, keepdims=True)
    return d * lax.rsqrt(var + LN_EPS) * g + b


def _first_argmax(v, idx, axes, big):
    mx = v
    for ax in axes:
        mx = jnp.max(mx, axis=ax, keepdims=True)
    pos = jnp.where(v == mx, idx, big)
    for ax in axes:
        pos = jnp.min(pos, axis=ax, keepdims=True)
    return mx, pos


def _outproj_kernel(ya_ref, yb_ref, x_ref, mod_ref, wo_ref, lng_ref, lnb_ref, wrT_ref, rbias_ref, tri_ref,
                    x1_ref, u2_ref, gates_ref, rank_ref, gatesT_ref, *, alpha):
    y = jnp.dot(ya_ref[0], wo_ref[0, 0:A_WIDTH, :], preferred_element_type=F32)
    y = y + jnp.dot(yb_ref[0], wo_ref[0, A_WIDTH:, :], preferred_element_type=F32)
    g1 = mod_ref[0, 2:3, :]
    x1 = _layernorm(alpha * x_ref[0] + (1.0 + g1) * y, lng_ref[0], lnb_ref[0])
    x1_ref[0] = x1
    u2 = (x1 * (1.0 + mod_ref[0, 4:5, :]) + mod_ref[0, 3:4, :]).astype(BF16)
    u2_ref[0] = u2

    tm = u2.shape[0]
    gsz = N_EXPERTS // N_GROUPS
    scores = 1.0 / (1.0 + jnp.exp(-_nt_dot(wrT_ref[0], u2)))
    sel = (scores + rbias_ref[0]).reshape(N_GROUPS, gsz, tm)
    scores = scores.reshape(N_GROUPS, gsz, tm)
    i_m = lax.broadcasted_iota(I32, (N_GROUPS, gsz, tm), 1)
    i_g = lax.broadcasted_iota(I32, (N_GROUPS, 1, tm), 0)
    i_e = lax.broadcasted_iota(I32, (N_GROUPS, gsz, tm), 0) * gsz + i_m
    m1, p1 = _first_argmax(sel, i_m, (1,), gsz)
    m2 = jnp.max(jnp.where(i_m == p1, NEG_INF, sel), axis=1, keepdims=True)
    gs = m1 + m2
    gmask = jnp.zeros(gs.shape, F32)
    for _ in range(TOPK_GROUPS):
        _, pg = _first_argmax(gs, i_g, (0,), N_GROUPS)
        hit = i_g == pg
        gmask = jnp.where(hit, 1.0, gmask)
        gs = jnp.where(hit, NEG_INF, gs)
    cand = jnp.where(jnp.broadcast_to(gmask, sel.shape) > 0.0, sel, NEG_INF)
    w = jnp.zeros(sel.shape, F32)
    chosen = jnp.zeros(sel.shape, F32)
    for _ in range(TOP_K):
        _, pe = _first_argmax(cand, i_e, (1, 0), N_EXPERTS)
        hit = i_e == pe
        w = jnp.where(hit, scores, w)
        chosen = jnp.where(hit, 1.0, chosen)
        cand = jnp.where(hit, NEG_INF, cand)
    wsum = jnp.sum(jnp.sum(w, axis=1, keepdims=True), axis=0, keepdims=True)
    gates = (w / wsum * ROUTED_SCALE).reshape(N_EXPERTS, tm)
    g_hi = gates.astype(BF16).astype(F32)
    g_lo = (gates - g_hi).astype(BF16).astype(F32)
    gates_ref[0] = jnp.concatenate([g_hi, g_lo], axis=0).T.astype(BF16)

    chosen2 = chosen.reshape(N_EXPERTS, tm)
    rank = jnp.concatenate(
        [jnp.dot(chosen2[:, g * MOE_GROUP:(g + 1) * MOE_GROUP].astype(BF16), tri_ref[...],
                 preferred_element_type=F32) for g in range(tm // MOE_GROUP)], axis=1)
    rank_ref[0] = jnp.where(chosen2 > 0.0, rank, -1.0).astype(I32)
    gatesT_ref[0] = gates


def _outproj(l, ya, yb, x, mod, wo, ln_g, ln_b, wrT, rbias, alpha):
    B, S, D = x.shape
    tm = TM_PROJ
    tok = lambda b, i: (b, i, 0)
    lw3 = lambda b, i: (l, 0, 0)
    return pl.pallas_call(
        functools.partial(_outproj_kernel, alpha=alpha),
        grid=(B, S // tm),
        in_specs=[
            pl.BlockSpec((1, tm, A_WIDTH), tok),
            pl.BlockSpec((1, tm, B_WIDTH), tok),
            pl.BlockSpec((1, tm, D), tok),
            pl.BlockSpec((1, 6, D), lambda b, i: (b, 0, 0)),
            pl.BlockSpec((1, D, D), lw3),
            pl.BlockSpec((1, 1, D), lw3),
            pl.BlockSpec((1, 1, D), lw3),
            pl.BlockSpec((1, N_EXPERTS, D), lw3),
            pl.BlockSpec((1, N_EXPERTS, tm), lw3),
            pl.BlockSpec((MOE_GROUP, MOE_GROUP), lambda b, i: (0, 0)),
        ],
        out_specs=[pl.BlockSpec((1, tm, D), tok), pl.BlockSpec((1, tm, D), tok),
                   pl.BlockSpec((1, tm, 2 * N_EXPERTS), tok),
                   pl.BlockSpec((1, N_EXPERTS, tm), lambda b, i: (b, 0, i)),
                   pl.BlockSpec((1, N_EXPERTS, tm), lambda b, i: (b, 0, i))],
        out_shape=[jax.ShapeDtypeStruct((B, S, D), F32), jax.ShapeDtypeStruct((B, S, D), BF16),
                   jax.ShapeDtypeStruct((B, S, 2 * N_EXPERTS), BF16),
                   jax.ShapeDtypeStruct((B, N_EXPERTS, S), I32),
                   jax.ShapeDtypeStruct((B, N_EXPERTS, S), F32)],
        compiler_params=_cparams(("arbitrary", "arbitrary")),
        name="outproj_router",
    )(ya, yb, x, mod, wo, ln_g, ln_b, wrT, rbias,
      jnp.asarray(np.triu(np.ones((MOE_GROUP, MOE_GROUP), np.float32), 1), BF16))


MOE_CHUNK_EXPERTS = 8


def _slot_onehot(rank_rows, values):
    row = lax.broadcasted_iota(I32, (MOE_CAP, rank_rows.shape[1]), 0)
    return jnp.concatenate(
        [jnp.where(row == rank_rows[e:e + 1, :], values[e:e + 1, :], 0.0) for e in range(rank_rows.shape[0])], axis=0)


def _dispatch_kernel(u_ref, rank_ref, x_ref):
    u = u_ref[...]
    ones = jnp.ones((MOE_CHUNK_EXPERTS, MOE_GROUP), F32)
    for c in range(N_EXPERTS // MOE_CHUNK_EXPERTS):
        es = slice(c * MOE_CHUNK_EXPERTS, (c + 1) * MOE_CHUNK_EXPERTS)
        onehot = _slot_onehot(rank_ref[0, es, :], ones)
        xs = jnp.dot(onehot.astype(BF16), u, preferred_element_type=F32).astype(BF16)
        x_ref[es] = xs.reshape(MOE_CHUNK_EXPERTS, MOE_CAP, -1)


def _dispatch(u2, rank):
    T, D = u2.shape
    ng = T // MOE_GROUP
    gps = rank.shape[-1] // MOE_GROUP
    return pl.pallas_call(
        _dispatch_kernel,
        grid=(ng,),
        in_specs=[pl.BlockSpec((MOE_GROUP, D), lambda g: (g, 0)),
                  pl.BlockSpec((1, N_EXPERTS, MOE_GROUP), lambda g: (g // gps, 0, g % gps))],
        out_specs=pl.BlockSpec((N_EXPERTS, MOE_CAP, D), lambda g: (0, g, 0)),
        out_shape=jax.ShapeDtypeStruct((N_EXPERTS, ng * MOE_CAP, D), BF16),
        compiler_params=_cparams(("arbitrary",)),
        name="moe_dispatch",
    )(u2, rank)


def _expert_kernel(x_ref, wgu_ref, wd_ref, y_ref):
    hgu = jnp.dot(x_ref[0], wgu_ref[0, 0], preferred_element_type=F32)
    h = _silu(hgu[:, :EXPERT_DIM]) * hgu[:, EXPERT_DIM:]
    y_ref[0] = jnp.dot(h.astype(BF16), wd_ref[0, 0], preferred_element_type=F32).astype(BF16)


def _experts(l, xs, wgu, wd):
    E, R, D = xs.shape
    tr = min(R, MOE_EXPERT_ROWS)
    assert R % tr == 0
    return pl.pallas_call(
        _expert_kernel,
        grid=(E, R // tr),
        in_specs=[pl.BlockSpec((1, tr, D), lambda e, i: (e, i, 0)),
                  pl.BlockSpec((1, 1, D, 2 * EXPERT_DIM), lambda e, i: (l, e, 0, 0)),
                  pl.BlockSpec((1, 1, EXPERT_DIM, D), lambda e, i: (l, e, 0, 0))],
        out_specs=pl.BlockSpec((1, tr, D), lambda e, i: (e, i, 0)),
        out_shape=jax.ShapeDtypeStruct((E, R, D), BF16),
        compiler_params=_cparams(("arbitrary", "arbitrary")),
        name="moe_experts",
    )(xs, wgu, wd)


def _combine_kernel(y_ref, rank_ref, gates_ref, u_ref, x1_ref, mod_ref, sgu_ref, sd_ref, lng_ref, lnb_ref,
                    out_ref, *, alpha):
    hgu = jnp.dot(u_ref[...], sgu_ref[0], preferred_element_type=F32)
    hs = _silu(hgu[:, :SHARED_DIM]) * hgu[:, SHARED_DIM:]
    y = jnp.dot(hs.astype(BF16), sd_ref[0], preferred_element_type=F32)
    for c in range(N_EXPERTS // MOE_CHUNK_EXPERTS):
        es = slice(c * MOE_CHUNK_EXPERTS, (c + 1) * MOE_CHUNK_EXPERTS)
        pick = _slot_onehot(rank_ref[0, es, :], gates_ref[0, es, :])
        ys = y_ref[es].reshape(MOE_CHUNK_EXPERTS * MOE_CAP, -1)
        y = y + lax.dot_general(pick.astype(BF16), ys, (((0,), (0,)), ((), ())), preferred_element_type=F32)
    g2 = mod_ref[0, 5:6, :]
    out_ref[...] = _layernorm(alpha * x1_ref[...] + (1.0 + g2) * y, lng_ref[0], lnb_ref[0])


def _combine(l, ys, rank, gates_t, u2, x1, mod, sgu, sd, ln_g, ln_b, alpha, seq):
    T, D = u2.shape
    tok = lambda g: (g, 0)
    lw3 = lambda g: (l, 0, 0)
    gps = seq // MOE_GROUP
    per_group = lambda g: (g // gps, 0, g % gps)
    return pl.pallas_call(
        functools.partial(_combine_kernel, alpha=alpha),
        grid=(T // MOE_GROUP,),
        in_specs=[
            pl.BlockSpec((N_EXPERTS, MOE_CAP, D), lambda g: (0, g, 0)),
            pl.BlockSpec((1, N_EXPERTS, MOE_GROUP), per_group),
            pl.BlockSpec((1, N_EXPERTS, MOE_GROUP), per_group),
            pl.BlockSpec((MOE_GROUP, D), tok),
            pl.BlockSpec((MOE_GROUP, D), tok),
            pl.BlockSpec((1, 6, D), lambda g: ((g * MOE_GROUP) // seq, 0, 0)),
            pl.BlockSpec((1, D, 2 * SHARED_DIM), lw3),
            pl.BlockSpec((1, SHARED_DIM, D), lw3),
            pl.BlockSpec((1, 1, D), lw3),
            pl.BlockSpec((1, 1, D), lw3),
        ],
        out_specs=pl.BlockSpec((MOE_GROUP, D), tok),
        out_shape=jax.ShapeDtypeStruct((T, D), F32),
        compiler_params=_cparams(("arbitrary",)),
        name="moe_combine",
    )(ys, rank, gates_t, u2, x1, mod, sgu, sd, ln_g, ln_b)


def _moe_kernel(u_ref, gates_ref, x1_ref, mod_ref, wgu_ref, wd_ref, sgu_ref, sd_ref, lng_ref, lnb_ref,
                out_ref, acc_ref, *, alpha):
    s = pl.program_id(1)
    u = u_ref[...]

    def hidden(wgu):
        hgu = jnp.dot(u, wgu, preferred_element_type=F32)
        return _silu(hgu[:, :EXPERT_DIM]) * hgu[:, EXPERT_DIM:]

    @pl.when(s == 0)
    def _():
        acc_ref[...] = jnp.dot(hidden(sgu_ref[0]).astype(BF16), sd_ref[0], preferred_element_type=F32)

    rows = lax.broadcasted_iota(I32, (2 * N_EXPERTS, MOE_EXPERTS_PER_STEP * EXPERT_DIM), 0) & (N_EXPERTS - 1)
    cols = lax.broadcasted_iota(I32, (2 * N_EXPERTS, MOE_EXPERTS_PER_STEP * EXPERT_DIM), 1)
    onehot = jnp.where(rows == s * MOE_EXPERTS_PER_STEP + cols // EXPERT_DIM, 1.0, 0.0).astype(BF16)
    gate = jnp.dot(gates_ref[...], onehot, preferred_element_type=F32)
    h = jnp.concatenate(
        [(hidden(wgu_ref[0, k]) * gate[:, k * EXPERT_DIM:(k + 1) * EXPERT_DIM]).astype(BF16)
         for k in range(MOE_EXPERTS_PER_STEP)], axis=1)
    wd = wd_ref[0].reshape(MOE_EXPERTS_PER_STEP * EXPERT_DIM, wd_ref.shape[-1])
    acc_ref[...] += jnp.dot(h, wd, preferred_element_type=F32)

    @pl.when(s == pl.num_programs(1) - 1)
    def _():
        g2 = mod_ref[0, 5:6, :]
        out_ref[...] = _layernorm(alpha * x1_ref[...] + (1.0 + g2) * acc_ref[...], lng_ref[0], lnb_ref[0])


def _moe(l, u2, gates, x1, mod, wgu, wd, sgu, sd, ln_g, ln_b, alpha, seq):
    T, D = u2.shape
    tm = TM_MOE
    assert seq % tm == 0
    tok = lambda i, e: (i, 0)
    lw3 = lambda i, e: (l, 0, 0)
    return pl.pallas_call(
        functools.partial(_moe_kernel, alpha=alpha),
        grid=(T // tm, N_EXPERTS // MOE_EXPERTS_PER_STEP),
        in_specs=[
            pl.BlockSpec((tm, D), tok),
            pl.BlockSpec((tm, 2 * N_EXPERTS), tok),
            pl.BlockSpec((tm, D), tok),
            pl.BlockSpec((1, 6, D), lambda i, e: ((i * tm) // seq, 0, 0)),
            pl.BlockSpec((1, MOE_EXPERTS_PER_STEP, D, 2 * EXPERT_DIM), lambda i, e: (l, e, 0, 0)),
            pl.BlockSpec((1, MOE_EXPERTS_PER_STEP, EXPERT_DIM, D), lambda i, e: (l, e, 0, 0)),
            pl.BlockSpec((1, D, 2 * SHARED_DIM), lw3),
            pl.BlockSpec((1, SHARED_DIM, D), lw3),
            pl.BlockSpec((1, 1, D), lw3),
            pl.BlockSpec((1, 1, D), lw3),
        ],
        out_specs=pl.BlockSpec((tm, D), tok),
        out_shape=jax.ShapeDtypeStruct((T, D), F32),
        scratch_shapes=[pltpu.VMEM((tm, D), F32)],
        compiler_params=_cparams(("arbitrary", "arbitrary")),
        name="moe_dense",
    )(u2, gates, x1, mod, wgu, wd, sgu, sd, ln_g, ln_b)


def _prepare_params(w_in, kv_norm_g, w_uk, w_uv, hgrn_lb, w_out, w_router, router_bias,
                    w_gate, w_up, w_down, ws_gate, ws_up, ws_down):
    L = w_in.shape[0]
    sizes = (A_WIDTH, KV_RANK, IDX_HEADS * IDX_DIM, IDX_DIM, IDX_HEADS, B_FDIM, B_FDIM, B_WIDTH, B_WIDTH)
    offs = np.concatenate([[0], np.cumsum(sizes)])
    seg = lambda i: w_in[:, :, offs[i]:offs[i + 1]]
    w_aq, w_ckv, w_iq, w_ik, w_iw, w_hq, w_hf, w_hi, w_hg = (seg(i) for i in range(9))
    zik = jnp.zeros_like(w_ik)
    wp = jnp.concatenate([w_aq, w_ckv, w_iq, w_ik, zik, zik, w_ik, w_hq, w_hf, w_hg, w_hi], axis=-1).astype(BF16)
    assert wp.shape[-1] == _C_END
    eye = jnp.eye(A_HEADS, dtype=F32)
    wblk = (jnp.einsum('lhdr,hg->lhdgr', w_uk * (ATTN_SCALE * LOG2E), eye)
            .reshape(L, A_WIDTH, A_HEADS * KV_RANK).astype(BF16))
    p = dict(
        wp=wp, wblk=wblk,
        wckvT=jnp.swapaxes(w_ckv, 1, 2).astype(BF16),
        wiwT=jnp.swapaxes(w_iw, 1, 2).astype(BF16),
        gkv=kv_norm_g.reshape(L, 1, KV_RANK),
        gkvT=jnp.broadcast_to(kv_norm_g[:, :, None], (L, KV_RANK, TM_PROJ)),
        wuvT=jnp.swapaxes(w_uv, 2, 3).astype(BF16),
        wo=w_out.astype(BF16),
        wrT=jnp.swapaxes(w_router, 1, 2).astype(BF16),
        rbias=jnp.broadcast_to(router_bias[:, :, None], (L, N_EXPERTS, TM_PROJ)),
        wgu=jnp.concatenate([w_gate, w_up], axis=-1).astype(BF16),
        wd=w_down.astype(BF16),
        sgu=jnp.concatenate([ws_gate, ws_up], axis=-1).astype(BF16),
        sd=ws_down.astype(BF16),
    )
    lbs = jnp.cumsum(jax.nn.softmax(hgrn_lb.astype(F32), axis=0), axis=0)
    lbs = jnp.clip(lbs - lbs[0:1], 0.0, 1.0 - 1e-6)
    p["llb"] = jnp.log(lbs).reshape(L, 1, B_FDIM)
    p["l1m"] = jnp.log1p(-lbs).reshape(L, 1, B_FDIM)
    return p


def kernel(x, c, w_ada, b_ada, w_in, kv_norm_g, w_uk, w_uv, rel_bias, hgrn_lb, gnorm_g, w_out, ln1_g, ln1_b,
           w_router, router_bias, w_gate, w_up, w_down, ws_gate, ws_up, ws_down, ln2_g, ln2_b):
    B, S, D = x.shape
    L = w_in.shape[0]
    alpha = (2 * L) ** 0.25
    p = _prepare_params(w_in, kv_norm_g, w_uk, w_uv, hgrn_lb, w_out, w_router, router_bias,
                        w_gate, w_up, w_down, ws_gate, ws_up, ws_down)
    mods = _adaln(c, w_ada, b_ada).reshape(L, B, 6, D)
    bn = _bias_tile(rel_bias)
    gn = gnorm_g.reshape(L, 1, B_VAL_DIM)
    ln1g, ln1b = ln1_g.reshape(L, 1, D), ln1_b.reshape(L, 1, D)
    ln2g, ln2b = ln2_g.reshape(L, 1, D), ln2_b.reshape(L, 1, D)
    for l in range(L):
        mod = mods[l]
        (qlat, ckv, ckvT, iq, ikA, ikB, iwT, hq, hk, hlf, hv, hgate) = _inproj(
            l, x, mod, p["wp"], p["wblk"], p["wckvT"], p["wiwT"], p["gkv"], p["gkvT"], p["llb"], p["l1m"])
        ya = _dsa(l, iq, iwT, qlat, ikA, ikB, ckv, ckvT, bn, p["wuvT"])
        yb = _hgrn(l, hq, hk, hlf, hv, hgate, gn)
        x1, u2, gates, rank, gates_t = _outproj(l, ya, yb, x, mod, p["wo"], ln1g, ln1b, p["wrT"], p["rbias"], alpha)
        u2f, x1f = u2.reshape(B * S, D), x1.reshape(B * S, D)

        def moe_sparse(l=l, mod=mod, u2f=u2f, x1f=x1f, rank=rank, gates_t=gates_t):
            ys = _experts(l, _dispatch(u2f, rank), p["wgu"], p["wd"])
            return _combine(l, ys, rank, gates_t, u2f, x1f, mod, p["sgu"], p["sd"], ln2g, ln2b, alpha, S)

        def moe_dense(l=l, mod=mod, u2f=u2f, x1f=x1f, gates=gates):
            return _moe(l, u2f, gates.reshape(B * S, 2 * N_EXPERTS), x1f, mod,
                        p["wgu"], p["wd"], p["sgu"], p["sd"], ln2g, ln2b, alpha, S)

        x = lax.cond(jnp.any(rank >= MOE_CAP), moe_dense, moe_sparse).reshape(B, S, D)
    return x
```

```python
import functools
import math

import numpy as np
import jax
import jax.numpy as jnp
from jax import lax
from jax.experimental import pallas as pl
from jax.experimental.pallas import tpu as pltpu

F32 = jnp.float32
BF16 = jnp.bfloat16
I32 = jnp.int32

D_MODEL = 1024
CHUNK = 64
A_HEADS = 8
A_HEAD_DIM = 64
A_WIDTH = A_HEADS * A_HEAD_DIM
KV_RANK = 128
IDX_HEADS = 8
IDX_DIM = 64
IDX_TOPK_MAX = 256
IDX_W_SCALE = (IDX_HEADS ** -0.5) * (IDX_DIM ** -0.5)
ATTN_SCALE = A_HEAD_DIM ** -0.5
LOG2E = math.log2(math.e)
KV_EXT = KV_RANK + 16
NUM_BUCKETS = 32
MAX_DISTANCE = 128
B_HEADS = 4
B_KEY_DIM = 128
B_VAL_DIM = 128
B_WIDTH = B_HEADS * B_VAL_DIM
B_FDIM = B_HEADS * B_KEY_DIM
N_EXPERTS = 64
TOP_K = 8
N_GROUPS = 8
TOPK_GROUPS = 4
EXPERT_DIM = 256
SHARED_DIM = 256
ROUTED_SCALE = 2.5
LN_EPS = 1e-5
RMS_EPS = 1e-6

LANES = 128
SUBLANES = 8
VMEM_LIMIT_BYTES = 56 * 1024 * 1024

INT_MIN = -(2 ** 31)
NEG_INF = float("-inf")

TM_PROJ = 512
TQ = 128
UNIT = 512
NEAR = 2 * TQ
COUNT_ACCS = 8
PLANE_ROWS = 32 * SUBLANES
TM_MOE = 1024
MOE_EXPERTS_PER_STEP = 4
MOE_GROUP = 256
MOE_CAP = 80
MOE_EXPERT_ROWS = 2048

_C_AQ, _C_CKV, _C_IQ, _C_IKA, _C_IKB, _C_HQ, _C_HF, _C_HG, _C_HI, _C_END = (
    0, 512, 640, 1152, 1280, 1408, 1920, 2432, 2944, 3456)


def _silu(v):
    return v * (1.0 / (1.0 + jnp.exp(-v)))


def _nt_dot(a, b):
    return lax.dot_general(a, b, (((1,), (1,)), ((), ())), preferred_element_type=F32)


def _cparams(sem):
    return pltpu.CompilerParams(dimension_semantics=sem, vmem_limit_bytes=VMEM_LIMIT_BYTES)


def _adaln_kernel(c_ref, w_ref, b_ref, o_ref):
    cond = _silu(c_ref[...])
    o_ref[0] = jnp.dot(cond.astype(BF16), w_ref[0].astype(BF16), preferred_element_type=F32) + b_ref[0]


def _adaln(c, w_ada, b_ada):
    L, D, D6 = w_ada.shape
    B = c.shape[0]
    nb = D6 // D
    return pl.pallas_call(
        _adaln_kernel,
        grid=(L, nb),
        in_specs=[
            pl.BlockSpec((B, D), lambda l, j: (0, 0)),
            pl.BlockSpec((1, D, D), lambda l, j: (l, 0, j)),
            pl.BlockSpec((1, 1, D), lambda l, j: (l, 0, j)),
        ],
        out_specs=pl.BlockSpec((1, B, D), lambda l, j: (l, 0, j)),
        out_shape=jax.ShapeDtypeStruct((L, B, D6), F32),
        compiler_params=_cparams(("arbitrary", "arbitrary")),
        name="adaln_mod",
    )(c, w_ada, b_ada.reshape(L, 1, D6))


_T5_NB = NUM_BUCKETS // 2
_T5_EXACT = _T5_NB // 2
_T5_THRESHOLDS = tuple(
    int(math.ceil(_T5_EXACT * (MAX_DISTANCE / _T5_EXACT) ** (j / (_T5_NB - _T5_EXACT)) - 1e-9))
    for j in range(1, _T5_NB - _T5_EXACT))
FAR_BUCKET = _T5_NB - 1
assert _T5_THRESHOLDS[-1] <= TQ, "keys further than one query block behind must share the far bucket"


def _bias_kernel(rb_ref, o_ref):
    kr = lax.broadcasted_iota(I32, (NEAR + TQ, TQ), 0)
    ql = lax.broadcasted_iota(I32, (NEAR + TQ, TQ), 1)
    rel = kr - TQ - ql
    n = jnp.abs(rel)
    large = jnp.full(rel.shape, _T5_EXACT, I32)
    for t in _T5_THRESHOLDS:
        large = large + (n >= t).astype(I32)
    bucket = jnp.where(rel > 0, _T5_NB, 0) + jnp.where(n < _T5_EXACT, n, large)
    for h in range(A_HEADS):
        acc = jnp.zeros(rel.shape, F32)
        for bk in range(NUM_BUCKETS):
            acc = jnp.where(bucket == bk, rb_ref[bk, h], acc)
        o_ref[h] = (acc - rb_ref[FAR_BUCKET, h]) * LOG2E


def _bias_tile(rel_bias):
    return pl.pallas_call(
        _bias_kernel,
        in_specs=[pl.BlockSpec(memory_space=pltpu.SMEM)],
        out_specs=pl.BlockSpec(memory_space=pltpu.VMEM),
        out_shape=jax.ShapeDtypeStruct((A_HEADS, NEAR + TQ, TQ), F32),
        name="rel_bias_tile",
    )(rel_bias)


def _inproj_kernel(x_ref, mod_ref, wp_ref, wblk_ref, wckvT_ref, wiwT_ref, gkv_ref, gkvT_ref, llb_ref, l1m_ref,
                   qlat_ref, ckv_ref, ckvT_ref, iq_ref, ikA_ref, ikB_ref, iwT_ref,
                   hq_ref, hk_ref, hlf_ref, hv_ref, hgate_ref):
    x = x_ref[0]
    sh1 = mod_ref[0, 0:1, :]
    sc1 = mod_ref[0, 1:2, :]
    u = (x * (1.0 + sc1) + sh1).astype(BF16)
    z = jnp.dot(u, wp_ref[0], preferred_element_type=F32)

    ql = jnp.dot(z[:, _C_AQ:_C_CKV].astype(BF16), wblk_ref[0], preferred_element_type=F32)
    for h in range(A_HEADS):
        qlat_ref[0, h] = ql[:, h * KV_RANK:(h + 1) * KV_RANK].astype(BF16)

    zc = z[:, _C_CKV:_C_IQ]
    inv = lax.rsqrt(jnp.mean(zc * zc, axis=-1, keepdims=True) + RMS_EPS)
    ckv_ref[0] = (zc * inv * gkv_ref[0]).astype(BF16)
    zt = _nt_dot(wckvT_ref[0], u)
    inv_t = lax.rsqrt(jnp.mean(zt * zt, axis=0, keepdims=True) + RMS_EPS)
    ckvT_ref[0, 0:KV_RANK, :] = (zt * inv_t * gkvT_ref[0]).astype(BF16)
    ckvT_ref[0, KV_RANK:KV_EXT, :] = jnp.ones((KV_EXT - KV_RANK, zt.shape[1]), BF16)

    for p in range(IDX_HEADS // 2):
        iq_ref[0, p] = z[:, _C_IQ + p * LANES:_C_IQ + (p + 1) * LANES].astype(BF16)
    ikA_ref[0] = z[:, _C_IKA:_C_IKB].astype(BF16)
    ikB_ref[0] = z[:, _C_IKB:_C_HQ].astype(BF16)
    iwT_ref[0] = _nt_dot(wiwT_ref[0], u) * IDX_W_SCALE

    hq_ref[0] = _silu(z[:, _C_HQ:_C_HF])
    zf = z[:, _C_HF:_C_HG]
    log_sig = jnp.minimum(zf, 0.0) - jnp.log1p(jnp.exp(-jnp.abs(zf)))
    a = llb_ref[0]
    c = l1m_ref[0] + log_sig
    logf = jnp.maximum(a, c) + jnp.log1p(jnp.exp(-jnp.abs(a - c)))
    hlf_ref[0] = logf
    hk_ref[0] = 1.0 - jnp.exp(logf)
    hgate_ref[0] = _silu(z[:, _C_HG:_C_HI])
    hv_ref[0] = z[:, _C_HI:_C_END].astype(BF16)


def _inproj(l, x, mod, wp, wblk, wckvT, wiwT, gkv, gkvT, llb, l1m):
    B, S, D = x.shape
    tm = TM_PROJ
    grid = (B, S // tm)
    lw3 = lambda b, i: (l, 0, 0)
    tok = lambda b, i: (b, i, 0)
    tokT = lambda b, i: (b, 0, i)
    hd4 = lambda b, i: (b, 0, i, 0)
    outs = [
        (jax.ShapeDtypeStruct((B, A_HEADS, S, KV_RANK), BF16), pl.BlockSpec((1, A_HEADS, tm, KV_RANK), hd4)),
        (jax.ShapeDtypeStruct((B, S, KV_RANK), BF16), pl.BlockSpec((1, tm, KV_RANK), tok)),
        (jax.ShapeDtypeStruct((B, KV_EXT, S), BF16), pl.BlockSpec((1, KV_EXT, tm), tokT)),
        (jax.ShapeDtypeStruct((B, IDX_HEADS // 2, S, LANES), BF16), pl.BlockSpec((1, IDX_HEADS // 2, tm, LANES), hd4)),
        (jax.ShapeDtypeStruct((B, S, LANES), BF16), pl.BlockSpec((1, tm, LANES), tok)),
        (jax.ShapeDtypeStruct((B, S, LANES), BF16), pl.BlockSpec((1, tm, LANES), tok)),
        (jax.ShapeDtypeStruct((B, IDX_HEADS, S), F32), pl.BlockSpec((1, IDX_HEADS, tm), tokT)),
        (jax.ShapeDtypeStruct((B, S, B_FDIM), F32), pl.BlockSpec((1, tm, B_FDIM), tok)),
        (jax.ShapeDtypeStruct((B, S, B_FDIM), F32), pl.BlockSpec((1, tm, B_FDIM), tok)),
        (jax.ShapeDtypeStruct((B, S, B_FDIM), F32), pl.BlockSpec((1, tm, B_FDIM), tok)),
        (jax.ShapeDtypeStruct((B, S, B_WIDTH), BF16), pl.BlockSpec((1, tm, B_WIDTH), tok)),
        (jax.ShapeDtypeStruct((B, S, B_WIDTH), F32), pl.BlockSpec((1, tm, B_WIDTH), tok)),
    ]
    return pl.pallas_call(
        _inproj_kernel,
        grid=grid,
        in_specs=[
            pl.BlockSpec((1, tm, D), tok),
            pl.BlockSpec((1, 6, D), lambda b, i: (b, 0, 0)),
            pl.BlockSpec((1, D, _C_END), lw3),
            pl.BlockSpec((1, A_WIDTH, A_HEADS * KV_RANK), lw3),
            pl.BlockSpec((1, KV_RANK, D), lw3),
            pl.BlockSpec((1, IDX_HEADS, D), lw3),
            pl.BlockSpec((1, 1, KV_RANK), lw3),
            pl.BlockSpec((1, KV_RANK, tm), lw3),
            pl.BlockSpec((1, 1, B_FDIM), lw3),
            pl.BlockSpec((1, 1, B_FDIM), lw3),
        ],
        out_specs=[o[1] for o in outs],
        out_shape=[o[0] for o in outs],
        compiler_params=_cparams(("arbitrary", "arbitrary")),
        name="inproj",
    )(x, mod, wp, wblk, wckvT, wiwT, gkv, gkvT, llb, l1m)


def _dsa_kernel(iq_ref, iwT_ref, qlat_ref, ikA_ref, ikB_ref, ckv_ref, ckvT_ref, bn_ref, wuvT_ref, out_ref,
                sc_ref, plane_ref, madd_ref, maddn_ref, la_ref, lb_ref, pma_ref, pmb_ref, ot_ref, yaT_ref,
                *, k_sel, n_idx_bits):
    j = pl.program_id(1)
    q0 = j * TQ
    nk = q0 + TQ
    nunit = (nk + UNIT - 1) // UNIT
    near0 = pl.multiple_of(jnp.maximum(nk - NEAR, 0), TQ)
    bn_row0 = pl.multiple_of(jnp.where(j == 0, TQ, 0), TQ)
    lane = lax.broadcasted_iota(I32, (1, TQ), 1)
    limit = (((q0 + lane) >> 6) + 1) << 6
    row_iota = lax.broadcasted_iota(I32, (UNIT, TQ), 0)

    def unit_rows(u):
        return pl.ds(pl.multiple_of(u * UNIT, UNIT), UNIT)

    iqs = iq_ref[0].reshape(IDX_HEADS // 2 * TQ, LANES)
    iw = iwT_ref[0]

    last_unit = sc_ref.shape[0] // UNIT - 1
    half = IDX_HEADS // 2 * TQ

    def issue_scores(u, buf_ref):
        rows = unit_rows(jnp.minimum(u, last_unit))
        buf_ref[:, 0:half] = _nt_dot(ikA_ref[0, rows, :], iqs)
        buf_ref[:, half:2 * half] = _nt_dot(ikB_ref[0, rows, :], iqs)

    def reduce_scores(u, buf_ref):
        acc = jnp.zeros((UNIT, TQ), F32)
        for p in range(IDX_HEADS // 2):
            acc = acc + iw[2 * p:2 * p + 1, :] * jnp.maximum(buf_ref[:, p * TQ:(p + 1) * TQ], 0.0)
            acc = acc + iw[2 * p + 1:2 * p + 2, :] * jnp.maximum(buf_ref[:, half + p * TQ:half + (p + 1) * TQ], 0.0)
        bits = lax.bitcast_convert_type(acc, I32)
        key = bits ^ ((bits >> 31) & 0x7FFFFFFF)
        sc_ref[unit_rows(u), :] = jnp.where(row_iota + u * UNIT < limit, key, INT_MIN)

    issue_scores(0, la_ref)

    def score_pair(i, carry):
        issue_scores(2 * i + 1, lb_ref)
        reduce_scores(2 * i, la_ref)
        issue_scores(2 * i + 2, la_ref)
        reduce_scores(2 * i + 1, lb_ref)
        return carry

    lax.fori_loop(0, nunit // 2, score_pair, 0)

    @pl.when(nunit % 2 == 1)
    def _():
        reduce_scores(nunit - 1, la_ref)

    ngroups = (nk + PLANE_ROWS - 1) // PLANE_ROWS

    def plane_group(g, carry):
        rows = pl.ds(pl.multiple_of(g * PLANE_ROWS, PLANE_ROWS), PLANE_ROWS)
        words = (sc_ref[rows, :] ^ INT_MIN).reshape(32, SUBLANES, TQ)
        w = [words[i] for i in range(32)]
        j, m = 16, 0x0000FFFF
        while j:
            mask = np.int32(np.uint32(m).view(np.int32))
            k = 0
            while k < 32:
                t = (w[k] ^ lax.shift_right_logical(w[k + j], jnp.full(w[k].shape, j, I32))) & mask
                w[k] = w[k] ^ t
                w[k + j] = w[k + j] ^ (t << j)
                k = (k + j + 1) & ~j
            j >>= 1
            m = (m ^ (m << j)) & 0xFFFFFFFF
        for i in range(32):
            plane_ref[i, pl.ds(g * SUBLANES, SUBLANES), :] = w[i]
        return carry

    lax.fori_loop(0, ngroups, plane_group, 0)

    n_words = sc_ref.shape[0] // PLANE_ROWS * SUBLANES
    group_of_word = lax.broadcasted_iota(I32, (n_words, TQ), 0) // SUBLANES

    def bit_step(i, carry):
        alive, above, t_off, c_ge = carry
        hit = alive & plane_ref[i]
        cnt = above + jnp.sum(lax.population_count(hit), axis=0, keepdims=True)
        ok = cnt >= k_sel
        alive = jnp.where(ok, hit, alive ^ hit)
        above = jnp.where(ok, above, cnt)
        t_off = jnp.where(ok, t_off | (jnp.int32(1) << (31 - i)), t_off)
        return alive, above, t_off, jnp.where(ok, cnt, c_ge)

    zero_row = jnp.zeros((1, TQ), I32)
    _, _, t_off, c_ge = lax.fori_loop(
        0, 32, bit_step,
        (jnp.where(group_of_word < ngroups, jnp.int32(-1), jnp.int32(0)), zero_row, zero_row, zero_row))
    thr = jnp.maximum(t_off ^ INT_MIN, INT_MIN + 1)
    straddle = (c_ge > k_sel).astype(I32)

    def count_where(pred):
        def body(u, acc):
            hit = pred(sc_ref[unit_rows(u), :], u * UNIT).reshape(-1, COUNT_ACCS * SUBLANES, TQ)
            for s in range(hit.shape[0]):
                acc = jnp.where(hit[s], acc + 1, acc)
            return acc
        acc = lax.fori_loop(0, nunit, body, jnp.zeros((COUNT_ACCS * SUBLANES, TQ), I32))
        return jnp.sum(acc, axis=0, keepdims=True)

    def tie_bound():
        c_gt = count_where(lambda blk, r0: blk > thr)
        need = k_sel - c_gt

        def tie_body(i, j0):
            cand = j0 | (jnp.int32(1) << (n_idx_bits - 1 - i))
            cnt = count_where(lambda blk, r0: jnp.where(blk == thr, row_iota + r0, cand) < cand)
            return jnp.where(cnt < need, cand, j0)

        j0 = lax.fori_loop(0, n_idx_bits, tie_body, jnp.zeros((1, TQ), I32))
        return jnp.where(straddle > 0, j0 + 1, jnp.int32(2 ** n_idx_bits))

    jstar = lax.cond(jnp.max(straddle) > 0, tie_bound, lambda: jnp.full((1, TQ), 2 ** n_idx_bits, I32))

    def madd_unit(u, carry):
        rows = unit_rows(u)
        key = sc_ref[rows, :]
        tie_keep = jnp.where(row_iota + u * UNIT < jstar, 0.0, NEG_INF)
        madd_ref[rows, :] = jnp.where(key > thr, 0.0, jnp.where(key == thr, tie_keep, NEG_INF))
        return carry

    lax.fori_loop(0, nunit, madd_unit, 0)
    maddn_ref[...] = madd_ref[pl.ds(near0, NEAR), :]
    madd_ref[pl.ds(near0, NEAR), :] = jnp.full((NEAR, TQ), NEG_INF, F32)

    qall = qlat_ref[0].reshape(A_HEADS * TQ, KV_RANK)

    def col_max(v):
        return jnp.max(v.reshape(v.shape[0] // SUBLANES, SUBLANES, A_HEADS * TQ), axis=0)

    def fold(xl, part_max, ckv_t, m_old):
        m_new = jnp.maximum(m_old, jnp.max(part_max, axis=0, keepdims=True))
        m_use = jnp.where(m_new == NEG_INF, 0.0, m_new)
        p = jnp.exp2((xl - m_use).astype(BF16))
        ot_ref[...] = ot_ref[...] * jnp.exp2(m_old - m_use) + jnp.dot(ckv_t, p, preferred_element_type=F32)
        return m_new

    ot_ref[...] = jnp.zeros(ot_ref.shape, F32)
    near_rows = pl.ds(near0, NEAR)
    xn = _nt_dot(ckv_ref[0, near_rows, :], qall) + jnp.concatenate([maddn_ref[...]] * A_HEADS, axis=1)
    xn = xn + jnp.concatenate([bn_ref[h, pl.ds(bn_row0, NEAR), :] for h in range(A_HEADS)], axis=1)
    m_run = fold(xn, col_max(xn), ckvT_ref[0, :, near_rows], jnp.full((1, A_HEADS * TQ), NEG_INF, F32))


    def issue_logits(u, buf_ref, pm_ref):
        rows = unit_rows(jnp.minimum(u, last_unit))
        xl = _nt_dot(ckv_ref[0, rows, :], qall) + jnp.concatenate([madd_ref[rows, :]] * A_HEADS, axis=1)
        buf_ref[...] = xl
        pm_ref[...] = col_max(xl)

    def consume_logits(u, buf_ref, pm_ref, m_old):
        return fold(buf_ref[...], pm_ref[...], ckvT_ref[0, :, unit_rows(u)], m_old)

    issue_logits(0, la_ref, pma_ref)

    def pair_step(i, m_old):
        issue_logits(2 * i + 1, lb_ref, pmb_ref)
        m_mid = consume_logits(2 * i, la_ref, pma_ref, m_old)
        issue_logits(2 * i + 2, la_ref, pma_ref)
        return consume_logits(2 * i + 1, lb_ref, pmb_ref, m_mid)

    m_run = lax.fori_loop(0, nunit // 2, pair_step, m_run)

    @pl.when(nunit % 2 == 1)
    def _():
        consume_logits(nunit - 1, la_ref, pma_ref, m_run)
    o_t = (ot_ref[0:KV_RANK, :] * (1.0 / ot_ref[KV_RANK:KV_RANK + 1, :])).astype(BF16)
    for h in range(A_HEADS):
        yaT_ref[h * A_HEAD_DIM:(h + 1) * A_HEAD_DIM, :] = jnp.dot(
            wuvT_ref[0, h], o_t[:, h * TQ:(h + 1) * TQ], preferred_element_type=F32)

    out_ref[0] = yaT_ref[...].T.astype(BF16)


def _dsa(l, iq, iwT, qlat, ikA, ikB, ckv, ckvT, bn, wuvT):
    B, S = ckv.shape[0], ckv.shape[1]
    assert S % (2 * UNIT) == 0 and UNIT % TQ == 0 and TQ % CHUNK == 0 and CHUNK == 64 and NEAR <= UNIT
    k_sel = min(IDX_TOPK_MAX, S // 4)
    n_idx_bits = int(math.log2(S))
    assert 2 ** n_idx_bits == S
    grid = (B, S // TQ)
    blk = lambda b, i: (b, 0, i, 0)
    full = lambda b, i: (b, 0, 0)
    kern = functools.partial(_dsa_kernel, k_sel=k_sel, n_idx_bits=n_idx_bits)
    return pl.pallas_call(
        kern,
        grid=grid,
        in_specs=[
            pl.BlockSpec((1, IDX_HEADS // 2, TQ, LANES), blk),
            pl.BlockSpec((1, IDX_HEADS, TQ), lambda b, i: (b, 0, i)),
            pl.BlockSpec((1, A_HEADS, TQ, KV_RANK), blk),
            pl.BlockSpec((1, S, LANES), full),
            pl.BlockSpec((1, S, LANES), full),
            pl.BlockSpec((1, S, KV_RANK), full),
            pl.BlockSpec((1, KV_EXT, S), full),
            pl.BlockSpec((A_HEADS, NEAR + TQ, TQ), lambda b, i: (0, 0, 0)),
            pl.BlockSpec((1, A_HEADS, A_HEAD_DIM, KV_RANK), lambda b, i: (l, 0, 0, 0)),
        ],
        out_specs=pl.BlockSpec((1, TQ, A_WIDTH), lambda b, i: (b, i, 0)),
        out_shape=jax.ShapeDtypeStruct((B, S, A_WIDTH), BF16),
        scratch_shapes=[
            pltpu.VMEM((S, TQ), I32),
            pltpu.VMEM((32, S // PLANE_ROWS * SUBLANES, TQ), I32),
            pltpu.VMEM((S, TQ), F32),
            pltpu.VMEM((NEAR, TQ), F32),
            pltpu.VMEM((UNIT, A_HEADS * TQ), F32),
            pltpu.VMEM((UNIT, A_HEADS * TQ), F32),
            pltpu.VMEM((SUBLANES, A_HEADS * TQ), F32),
            pltpu.VMEM((SUBLANES, A_HEADS * TQ), F32),
            pltpu.VMEM((KV_EXT, A_HEADS * TQ), F32),
            pltpu.VMEM((A_WIDTH, TQ), F32),
        ],
        compiler_params=_cparams(("arbitrary", "arbitrary")),
        name="dsa_attention",
    )(iq, iwT, qlat, ikA, ikB, ckv, ckvT, bn, wuvT)


def _hgrn_constants():
    c = CHUNK
    r = np.arange(c)[:, None]
    jj = np.arange(c)[None, :]
    mats = [(jj <= r), (jj > r)]
    masks = [np.eye(c, dtype=bool)]
    m = c // 2
    while m >= 1:
        start = (r // (2 * m)) * (2 * m)
        bd = start + m - 1
        upper = r > bd
        mats.append(np.where(upper, (jj > bd) & (jj <= r), (jj > r) & (jj <= bd)))
        same_parent = (r // (2 * m)) == (jj // (2 * m))
        masks.append(same_parent & upper & (jj <= (jj // (2 * m)) * (2 * m) + m - 1))
        m //= 2
    m_all = np.concatenate(mats, axis=0).astype(np.float32)
    total = np.zeros((c, c), np.int32)
    for mk in masks:
        total += mk
    assert (total == np.tril(np.ones((c, c), np.int32))).all()
    return np.concatenate([m_all] * 3, axis=1), np.stack(masks).astype(np.float32)


_HGRN_M3, _HGRN_MASKS = _hgrn_constants()
_HGRN_LEVELS = _HGRN_MASKS.shape[0] - 1
HGRN_STEP_CHUNKS = 2
HGRN_STEP_BATCH = 2


def _hgrn_kernel(q_ref, k_ref, lf_ref, v_ref, gate_ref, m3_ref, mask_ref, gn_ref, out_ref, st_ref):
    @pl.when(pl.program_id(1) == 0)
    def _():
        st_ref[...] = jnp.zeros(st_ref.shape, F32)

    c = CHUNK
    for bi, ci in [(b, ch) for ch in range(HGRN_STEP_CHUNKS) for b in range(HGRN_STEP_BATCH)]:
        rows = slice(ci * c, (ci + 1) * c)
        g = lf_ref[bi, rows, :]
        g_hi = g.astype(BF16)
        r1 = g - g_hi.astype(F32)
        g_mid = r1.astype(BF16)
        g_lo = (r1 - g_mid.astype(F32)).astype(BF16)
        sums = jnp.dot(m3_ref[...], jnp.concatenate([g_hi, g_mid, g_lo], axis=0), preferred_element_type=F32)
        e_all = jnp.exp(sums)
        for h in range(B_HEADS):
            cols = slice(h * B_KEY_DIM, (h + 1) * B_KEY_DIM)
            qh = q_ref[bi, rows, cols]
            kh = k_ref[bi, rows, cols]
            vh = v_ref[bi, rows, cols]
            att = mask_ref[0] * _nt_dot(qh.astype(BF16), kh.astype(BF16))
            for lv in range(_HGRN_LEVELS):
                e_l = e_all[(2 + lv) * c:(3 + lv) * c, cols]
                att = att + mask_ref[lv + 1] * _nt_dot((qh * e_l).astype(BF16), (kh * e_l).astype(BF16))
            e_b = e_all[0:c, cols]
            e_rem = e_all[c:2 * c, cols]
            st = st_ref[bi, h]
            o = jnp.dot(att.astype(BF16), vh, preferred_element_type=F32)
            o = o + _nt_dot((qh * e_b).astype(BF16), st.astype(BF16))
            upd = lax.dot_general(vh, (kh * e_rem).astype(BF16), (((0,), (0,)), ((), ())),
                                  preferred_element_type=F32)
            st_ref[bi, h] = st * e_b[c - 1:c, :] + upd
            o = o * lax.rsqrt(jnp.mean(o * o, axis=-1, keepdims=True) + RMS_EPS) * gn_ref[0]
            out_ref[bi, rows, cols] = (o * gate_ref[bi, rows, cols]).astype(BF16)


def _hgrn(l, hq, hk, hlf, hv, hgate, gnorm):
    B, S, W = hq.shape
    ts = CHUNK * HGRN_STEP_CHUNKS
    nb = HGRN_STEP_BATCH
    assert B % nb == 0
    tok = lambda b, i: (b, i, 0)
    return pl.pallas_call(
        _hgrn_kernel,
        grid=(B // nb, S // ts),
        in_specs=[
            pl.BlockSpec((nb, ts, W), tok),
            pl.BlockSpec((nb, ts, W), tok),
            pl.BlockSpec((nb, ts, W), tok),
            pl.BlockSpec((nb, ts, W), tok),
            pl.BlockSpec((nb, ts, W), tok),
            pl.BlockSpec(_HGRN_M3.shape, lambda b, i: (0, 0)),
            pl.BlockSpec(_HGRN_MASKS.shape, lambda b, i: (0, 0, 0)),
            pl.BlockSpec((1, 1, B_VAL_DIM), lambda b, i: (l, 0, 0)),
        ],
        out_specs=pl.BlockSpec((nb, ts, W), tok),
        out_shape=jax.ShapeDtypeStruct((B, S, W), BF16),
        scratch_shapes=[pltpu.VMEM((nb, B_HEADS, B_VAL_DIM, B_KEY_DIM), F32)],
        compiler_params=_cparams(("arbitrary", "arbitrary")),
        name="hgrn2",
    )(hq, hk, hlf, hv, hgate, jnp.asarray(_HGRN_M3, BF16), jnp.asarray(_HGRN_MASKS), gnorm)


def _layernorm(v, g, b):
    mu = jnp.mean(v, axis=-1, keepdims=True)
    d = v - mu
    var = jnp.mean(d * d, axis=-1, keepdims=True)
    return d * lax.rsqrt(var + LN_EPS) * g + b


def _first_argmax(v, idx, axes, big):
    mx = v
    for ax in axes:
        mx = jnp.max(mx, axis=ax, keepdims=True)
    pos = jnp.where(v == mx, idx, big)
    for ax in axes:
        pos = jnp.min(pos, axis=ax, keepdims=True)
    return mx, pos


def _outproj_kernel(ya_ref, yb_ref, x_ref, mod_ref, wo_ref, lng_ref, lnb_ref, wrT_ref, rbias_ref, tri_ref,
                    x1_ref, u2_ref, gates_ref, rank_ref, gatesT_ref, *, alpha):
    y = jnp.dot(ya_ref[0], wo_ref[0, 0:A_WIDTH, :], preferred_element_type=F32)
    y = y + jnp.dot(yb_ref[0], wo_ref[0, A_WIDTH:, :], preferred_element_type=F32)
    g1 = mod_ref[0, 2:3, :]
    x1 = _layernorm(alpha * x_ref[0] + (1.0 + g1) * y, lng_ref[0], lnb_ref[0])
    x1_ref[0] = x1
    u2 = (x1 * (1.0 + mod_ref[0, 4:5, :]) + mod_ref[0, 3:4, :]).astype(BF16)
    u2_ref[0] = u2

    tm = u2.shape[0]
    gsz = N_EXPERTS // N_GROUPS
    scores = 1.0 / (1.0 + jnp.exp(-_nt_dot(wrT_ref[0], u2)))
    sel = (scores + rbias_ref[0]).reshape(N_GROUPS, gsz, tm)
    scores = scores.reshape(N_GROUPS, gsz, tm)
    i_m = lax.broadcasted_iota(I32, (N_GROUPS, gsz, tm), 1)
    i_g = lax.broadcasted_iota(I32, (N_GROUPS, 1, tm), 0)
    i_e = lax.broadcasted_iota(I32, (N_GROUPS, gsz, tm), 0) * gsz + i_m
    m1, p1 = _first_argmax(sel, i_m, (1,), gsz)
    m2 = jnp.max(jnp.where(i_m == p1, NEG_INF, sel), axis=1, keepdims=True)
    gs = m1 + m2
    gmask = jnp.zeros(gs.shape, F32)
    for _ in range(TOPK_GROUPS):
        _, pg = _first_argmax(gs, i_g, (0,), N_GROUPS)
        hit = i_g == pg
        gmask = jnp.where(hit, 1.0, gmask)
        gs = jnp.where(hit, NEG_INF, gs)
    cand = jnp.where(jnp.broadcast_to(gmask, sel.shape) > 0.0, sel, NEG_INF)
    w = jnp.zeros(sel.shape, F32)
    chosen = jnp.zeros(sel.shape, F32)
    for _ in range(TOP_K):
        _, pe = _first_argmax(cand, i_e, (1, 0), N_EXPERTS)
        hit = i_e == pe
        w = jnp.where(hit, scores, w)
        chosen = jnp.where(hit, 1.0, chosen)
        cand = jnp.where(hit, NEG_INF, cand)
    wsum = jnp.sum(jnp.sum(w, axis=1, keepdims=True), axis=0, keepdims=True)
    gates = (w / wsum * ROUTED_SCALE).reshape(N_EXPERTS, tm)
    g_hi = gates.astype(BF16).astype(F32)
    g_lo = (gates - g_hi).astype(BF16).astype(F32)
    gates_ref[0] = jnp.concatenate([g_hi, g_lo], axis=0).T.astype(BF16)

    chosen2 = chosen.reshape(N_EXPERTS, tm)
    rank = jnp.concatenate(
        [jnp.dot(chosen2[:, g * MOE_GROUP:(g + 1) * MOE_GROUP].astype(BF16), tri_ref[...],
                 preferred_element_type=F32) for g in range(tm // MOE_GROUP)], axis=1)
    rank_ref[0] = jnp.where(chosen2 > 0.0, rank, -1.0).astype(I32)
    gatesT_ref[0] = gates


def _outproj(l, ya, yb, x, mod, wo, ln_g, ln_b, wrT, rbias, alpha):
    B, S, D = x.shape
    tm = TM_PROJ
    tok = lambda b, i: (b, i, 0)
    lw3 = lambda b, i: (l, 0, 0)
    return pl.pallas_call(
        functools.partial(_outproj_kernel, alpha=alpha),
        grid=(B, S // tm),
        in_specs=[
            pl.BlockSpec((1, tm, A_WIDTH), tok),
            pl.BlockSpec((1, tm, B_WIDTH), tok),
            pl.BlockSpec((1, tm, D), tok),
            pl.BlockSpec((1, 6, D), lambda b, i: (b, 0, 0)),
            pl.BlockSpec((1, D, D), lw3),
            pl.BlockSpec((1, 1, D), lw3),
            pl.BlockSpec((1, 1, D), lw3),
            pl.BlockSpec((1, N_EXPERTS, D), lw3),
            pl.BlockSpec((1, N_EXPERTS, tm), lw3),
            pl.BlockSpec((MOE_GROUP, MOE_GROUP), lambda b, i: (0, 0)),
        ],
        out_specs=[pl.BlockSpec((1, tm, D), tok), pl.BlockSpec((1, tm, D), tok),
                   pl.BlockSpec((1, tm, 2 * N_EXPERTS), tok),
                   pl.BlockSpec((1, N_EXPERTS, tm), lambda b, i: (b, 0, i)),
                   pl.BlockSpec((1, N_EXPERTS, tm), lambda b, i: (b, 0, i))],
        out_shape=[jax.ShapeDtypeStruct((B, S, D), F32), jax.ShapeDtypeStruct((B, S, D), BF16),
                   jax.ShapeDtypeStruct((B, S, 2 * N_EXPERTS), BF16),
                   jax.ShapeDtypeStruct((B, N_EXPERTS, S), I32),
                   jax.ShapeDtypeStruct((B, N_EXPERTS, S), F32)],
        compiler_params=_cparams(("arbitrary", "arbitrary")),
        name="outproj_router",
    )(ya, yb, x, mod, wo, ln_g, ln_b, wrT, rbias,
      jnp.asarray(np.triu(np.ones((MOE_GROUP, MOE_GROUP), np.float32), 1), BF16))


MOE_CHUNK_EXPERTS = 8


def _slot_onehot(rank_rows, values):
    row = lax.broadcasted_iota(I32, (MOE_CAP, rank_rows.shape[1]), 0)
    return jnp.concatenate(
        [jnp.where(row == rank_rows[e:e + 1, :], values[e:e + 1, :], 0.0) for e in range(rank_rows.shape[0])], axis=0)


def _dispatch_kernel(u_ref, rank_ref, x_ref):
    u = u_ref[...]
    ones = jnp.ones((MOE_CHUNK_EXPERTS, MOE_GROUP), F32)
    for c in range(N_EXPERTS // MOE_CHUNK_EXPERTS):
        es = slice(c * MOE_CHUNK_EXPERTS, (c + 1) * MOE_CHUNK_EXPERTS)
        onehot = _slot_onehot(rank_ref[0, es, :], ones)
        xs = jnp.dot(onehot.astype(BF16), u, preferred_element_type=F32).astype(BF16)
        x_ref[es] = xs.reshape(MOE_CHUNK_EXPERTS, MOE_CAP, -1)


def _dispatch(u2, rank):
    T, D = u2.shape
    ng = T // MOE_GROUP
    gps = rank.shape[-1] // MOE_GROUP
    return pl.pallas_call(
        _dispatch_kernel,
        grid=(ng,),
        in_specs=[pl.BlockSpec((MOE_GROUP, D), lambda g: (g, 0)),
                  pl.BlockSpec((1, N_EXPERTS, MOE_GROUP), lambda g: (g // gps, 0, g % gps))],
        out_specs=pl.BlockSpec((N_EXPERTS, MOE_CAP, D), lambda g: (0, g, 0)),
        out_shape=jax.ShapeDtypeStruct((N_EXPERTS, ng * MOE_CAP, D), BF16),
        compiler_params=_cparams(("arbitrary",)),
        name="moe_dispatch",
    )(u2, rank)


def _expert_kernel(x_ref, wgu_ref, wd_ref, y_ref):
    hgu = jnp.dot(x_ref[0], wgu_ref[0, 0], preferred_element_type=F32)
    h = _silu(hgu[:, :EXPERT_DIM]) * hgu[:, EXPERT_DIM:]
    y_ref[0] = jnp.dot(h.astype(BF16), wd_ref[0, 0], preferred_element_type=F32).astype(BF16)


def _experts(l, xs, wgu, wd):
    E, R, D = xs.shape
    tr = min(R, MOE_EXPERT_ROWS)
    assert R % tr == 0
    return pl.pallas_call(
        _expert_kernel,
        grid=(E, R // tr),
        in_specs=[pl.BlockSpec((1, tr, D), lambda e, i: (e, i, 0)),
                  pl.BlockSpec((1, 1, D, 2 * EXPERT_DIM), lambda e, i: (l, e, 0, 0)),
                  pl.BlockSpec((1, 1, EXPERT_DIM, D), lambda e, i: (l, e, 0, 0))],
        out_specs=pl.BlockSpec((1, tr, D), lambda e, i: (e, i, 0)),
        out_shape=jax.ShapeDtypeStruct((E, R, D), BF16),
        compiler_params=_cparams(("arbitrary", "arbitrary")),
        name="moe_experts",
    )(xs, wgu, wd)


def _combine_kernel(y_ref, rank_ref, gates_ref, u_ref, x1_ref, mod_ref, sgu_ref, sd_ref, lng_ref, lnb_ref,
                    out_ref, *, alpha):
    hgu = jnp.dot(u_ref[...], sgu_ref[0], preferred_element_type=F32)
    hs = _silu(hgu[:, :SHARED_DIM]) * hgu[:, SHARED_DIM:]
    y = jnp.dot(hs.astype(BF16), sd_ref[0], preferred_element_type=F32)
    for c in range(N_EXPERTS // MOE_CHUNK_EXPERTS):
        es = slice(c * MOE_CHUNK_EXPERTS, (c + 1) * MOE_CHUNK_EXPERTS)
        pick = _slot_onehot(rank_ref[0, es, :], gates_ref[0, es, :])
        ys = y_ref[es].reshape(MOE_CHUNK_EXPERTS * MOE_CAP, -1)
        y = y + lax.dot_general(pick.astype(BF16), ys, (((0,), (0,)), ((), ())), preferred_element_type=F32)
    g2 = mod_ref[0, 5:6, :]
    out_ref[...] = _layernorm(alpha * x1_ref[...] + (1.0 + g2) * y, lng_ref[0], lnb_ref[0])


def _combine(l, ys, rank, gates_t, u2, x1, mod, sgu, sd, ln_g, ln_b, alpha, seq):
    T, D = u2.shape
    tok = lambda g: (g, 0)
    lw3 = lambda g: (l, 0, 0)
    gps = seq // MOE_GROUP
    per_group = lambda g: (g // gps, 0, g % gps)
    return pl.pallas_call(
        functools.partial(_combine_kernel, alpha=alpha),
        grid=(T // MOE_GROUP,),
        in_specs=[
            pl.BlockSpec((N_EXPERTS, MOE_CAP, D), lambda g: (0, g, 0)),
            pl.BlockSpec((1, N_EXPERTS, MOE_GROUP), per_group),
            pl.BlockSpec((1, N_EXPERTS, MOE_GROUP), per_group),
            pl.BlockSpec((MOE_GROUP, D), tok),
            pl.BlockSpec((MOE_GROUP, D), tok),
            pl.BlockSpec((1, 6, D), lambda g: ((g * MOE_GROUP) // seq, 0, 0)),
            pl.BlockSpec((1, D, 2 * SHARED_DIM), lw3),
            pl.BlockSpec((1, SHARED_DIM, D), lw3),
            pl.BlockSpec((1, 1, D), lw3),
            pl.BlockSpec((1, 1, D), lw3),
        ],
        out_specs=pl.BlockSpec((MOE_GROUP, D), tok),
        out_shape=jax.ShapeDtypeStruct((T, D), F32),
        compiler_params=_cparams(("arbitrary",)),
        name="moe_combine",
    )(ys, rank, gates_t, u2, x1, mod, sgu, sd, ln_g, ln_b)


def _moe_kernel(u_ref, gates_ref, x1_ref, mod_ref, wgu_ref, wd_ref, sgu_ref, sd_ref, lng_ref, lnb_ref,
                out_ref, acc_ref, *, alpha):
    s = pl.program_id(1)
    u = u_ref[...]

    def hidden(wgu):
        hgu = jnp.dot(u, wgu, preferred_element_type=F32)
        return _silu(hgu[:, :EXPERT_DIM]) * hgu[:, EXPERT_DIM:]

    @pl.when(s == 0)
    def _():
        acc_ref[...] = jnp.dot(hidden(sgu_ref[0]).astype(BF16), sd_ref[0], preferred_element_type=F32)

    rows = lax.broadcasted_iota(I32, (2 * N_EXPERTS, MOE_EXPERTS_PER_STEP * EXPERT_DIM), 0) & (N_EXPERTS - 1)
    cols = lax.broadcasted_iota(I32, (2 * N_EXPERTS, MOE_EXPERTS_PER_STEP * EXPERT_DIM), 1)
    onehot = jnp.where(rows == s * MOE_EXPERTS_PER_STEP + cols // EXPERT_DIM, 1.0, 0.0).astype(BF16)
    gate = jnp.dot(gates_ref[...], onehot, preferred_element_type=F32)
    h = jnp.concatenate(
        [(hidden(wgu_ref[0, k]) * gate[:, k * EXPERT_DIM:(k + 1) * EXPERT_DIM]).astype(BF16)
         for k in range(MOE_EXPERTS_PER_STEP)], axis=1)
    wd = wd_ref[0].reshape(MOE_EXPERTS_PER_STEP * EXPERT_DIM, wd_ref.shape[-1])
    acc_ref[...] += jnp.dot(h, wd, preferred_element_type=F32)

    @pl.when(s == pl.num_programs(1) - 1)
    def _():
        g2 = mod_ref[0, 5:6, :]
        out_ref[...] = _layernorm(alpha * x1_ref[...] + (1.0 + g2) * acc_ref[...], lng_ref[0], lnb_ref[0])


def _moe(l, u2, gates, x1, mod, wgu, wd, sgu, sd, ln_g, ln_b, alpha, seq):
    T, D = u2.shape
    tm = TM_MOE
    assert seq % tm == 0
    tok = lambda i, e: (i, 0)
    lw3 = lambda i, e: (l, 0, 0)
    return pl.pallas_call(
        functools.partial(_moe_kernel, alpha=alpha),
        grid=(T // tm, N_EXPERTS // MOE_EXPERTS_PER_STEP),
        in_specs=[
            pl.BlockSpec((tm, D), tok),
            pl.BlockSpec((tm, 2 * N_EXPERTS), tok),
            pl.BlockSpec((tm, D), tok),
            pl.BlockSpec((1, 6, D), lambda i, e: ((i * tm) // seq, 0, 0)),
            pl.BlockSpec((1, MOE_EXPERTS_PER_STEP, D, 2 * EXPERT_DIM), lambda i, e: (l, e, 0, 0)),
            pl.BlockSpec((1, MOE_EXPERTS_PER_STEP, EXPERT_DIM, D), lambda i, e: (l, e, 0, 0)),
            pl.BlockSpec((1, D, 2 * SHARED_DIM), lw3),
            pl.BlockSpec((1, SHARED_DIM, D), lw3),
            pl.BlockSpec((1, 1, D), lw3),
            pl.BlockSpec((1, 1, D), lw3),
        ],
        out_specs=pl.BlockSpec((tm, D), tok),
        out_shape=jax.ShapeDtypeStruct((T, D), F32),
        scratch_shapes=[pltpu.VMEM((tm, D), F32)],
        compiler_params=_cparams(("arbitrary", "arbitrary")),
        name="moe_dense",
    )(u2, gates, x1, mod, wgu, wd, sgu, sd, ln_g, ln_b)


def _prepare_params(w_in, kv_norm_g, w_uk, w_uv, hgrn_lb, w_out, w_router, router_bias,
                    w_gate, w_up, w_down, ws_gate, ws_up, ws_down):
    L = w_in.shape[0]
    sizes = (A_WIDTH, KV_RANK, IDX_HEADS * IDX_DIM, IDX_DIM, IDX_HEADS, B_FDIM, B_FDIM, B_WIDTH, B_WIDTH)
    offs = np.concatenate([[0], np.cumsum(sizes)])
    seg = lambda i: w_in[:, :, offs[i]:offs[i + 1]]
    w_aq, w_ckv, w_iq, w_ik, w_iw, w_hq, w_hf, w_hi, w_hg = (seg(i) for i in range(9))
    zik = jnp.zeros_like(w_ik)
    wp = jnp.concatenate([w_aq, w_ckv, w_iq, w_ik, zik, zik, w_ik, w_hq, w_hf, w_hg, w_hi], axis=-1).astype(BF16)
    assert wp.shape[-1] == _C_END
    eye = jnp.eye(A_HEADS, dtype=F32)
    wblk = (jnp.einsum('lhdr,hg->lhdgr', w_uk * (ATTN_SCALE * LOG2E), eye)
            .reshape(L, A_WIDTH, A_HEADS * KV_RANK).astype(BF16))
    p = dict(
        wp=wp, wblk=wblk,
        wckvT=jnp.swapaxes(w_ckv, 1, 2).astype(BF16),
        wiwT=jnp.swapaxes(w_iw, 1, 2).astype(BF16),
        gkv=kv_norm_g.reshape(L, 1, KV_RANK),
        gkvT=jnp.broadcast_to(kv_norm_g[:, :, None], (L, KV_RANK, TM_PROJ)),
        wuvT=jnp.swapaxes(w_uv, 2, 3).astype(BF16),
        wo=w_out.astype(BF16),
        wrT=jnp.swapaxes(w_router, 1, 2).astype(BF16),
        rbias=jnp.broadcast_to(router_bias[:, :, None], (L, N_EXPERTS, TM_PROJ)),
        wgu=jnp.concatenate([w_gate, w_up], axis=-1).astype(BF16),
        wd=w_down.astype(BF16),
        sgu=jnp.concatenate([ws_gate, ws_up], axis=-1).astype(BF16),
        sd=ws_down.astype(BF16),
    )
    lbs = jnp.cumsum(jax.nn.softmax(hgrn_lb.astype(F32), axis=0), axis=0)
    lbs = jnp.clip(lbs - lbs[0:1], 0.0, 1.0 - 1e-6)
    p["llb"] = jnp.log(lbs).reshape(L, 1, B_FDIM)
    p["l1m"] = jnp.log1p(-lbs).reshape(L, 1, B_FDIM)
    return p


def kernel(x, c, w_ada, b_ada, w_in, kv_norm_g, w_uk, w_uv, rel_bias, hgrn_lb, gnorm_g, w_out, ln1_g, ln1_b,
           w_router, router_bias, w_gate, w_up, w_down, ws_gate, ws_up, ws_down, ln2_g, ln2_b):
    B, S, D = x.shape
    L = w_in.shape[0]
    alpha = (2 * L) ** 0.25
    p = _prepare_params(w_in, kv_norm_g, w_uk, w_uv, hgrn_lb, w_out, w_router, router_bias,
                        w_gate, w_up, w_down, ws_gate, ws_up, ws_down)
    mods = _adaln(c, w_ada, b_ada).reshape(L, B, 6, D)
    bn = _bias_tile(rel_bias)
    gn = gnorm_g.reshape(L, 1, B_VAL_DIM)
    ln1g, ln1b = ln1_g.reshape(L, 1, D), ln1_b.reshape(L, 1, D)
    ln2g, ln2b = ln2_g.reshape(L, 1, D), ln2_b.reshape(L, 1, D)
    for l in range(L):
        mod = mods[l]
        (qlat, ckv, ckvT, iq, ikA, ikB, iwT, hq, hk, hlf, hv, hgate) = _inproj(
            l, x, mod, p["wp"], p["wblk"], p["wckvT"], p["wiwT"], p["gkv"], p["gkvT"], p["llb"], p["l1m"])
        ya = _dsa(l, iq, iwT, qlat, ikA, ikB, ckv, ckvT, bn, p["wuvT"])
        yb = _hgrn(l, hq, hk, hlf, hv, hgate, gn)
        x1, u2, gates, rank, gates_t = _outproj(l, ya, yb, x, mod, p["wo"], ln1g, ln1b, p["wrT"], p["rbias"], alpha)
        u2f, x1f = u2.reshape(B * S, D), x1.reshape(B * S, D)

        def moe_sparse(l=l, mod=mod, u2f=u2f, x1f=x1f, rank=rank, gates_t=gates_t):
            ys = _experts(l, _dispatch(u2f, rank), p["wgu"], p["wd"])
            return _combine(l, ys, rank, gates_t, u2f, x1f, mod, p["sgu"], p["sd"], ln2g, ln2b, alpha, S)

        def moe_dense(l=l, mod=mod, u2f=u2f, x1f=x1f, gates=gates):
            return _moe(l, u2f, gates.reshape(B * S, 2 * N_EXPERTS), x1f, mod,
                        p["wgu"], p["wd"], p["sgu"], p["sd"], ln2g, ln2b, alpha, S)

        x = lax.cond(jnp.any(rank >= MOE_CAP), moe_dense, moe_sparse).reshape(B, S, D)
    return x
```

```python
import functools
import math

import numpy as np
import jax
import jax.numpy as jnp
from jax import lax
from jax.experimental import pallas as pl
from jax.experimental.pallas import tpu as pltpu

F32 = jnp.float32
BF16 = jnp.bfloat16
I32 = jnp.int32

D_MODEL = 1024
CHUNK = 64
A_HEADS = 8
A_HEAD_DIM = 64
A_WIDTH = A_HEADS * A_HEAD_DIM
KV_RANK = 128
IDX_HEADS = 8
IDX_DIM = 64
IDX_TOPK_MAX = 256
IDX_W_SCALE = (IDX_HEADS ** -0.5) * (IDX_DIM ** -0.5)
ATTN_SCALE = A_HEAD_DIM ** -0.5
LOG2E = math.log2(math.e)
KV_EXT = KV_RANK + 16
NUM_BUCKETS = 32
MAX_DISTANCE = 128
B_HEADS = 4
B_KEY_DIM = 128
B_VAL_DIM = 128
B_WIDTH = B_HEADS * B_VAL_DIM
B_FDIM = B_HEADS * B_KEY_DIM
N_EXPERTS = 64
TOP_K = 8
N_GROUPS = 8
TOPK_GROUPS = 4
EXPERT_DIM = 256
SHARED_DIM = 256
ROUTED_SCALE = 2.5
LN_EPS = 1e-5
RMS_EPS = 1e-6

LANES = 128
SUBLANES = 8
VMEM_LIMIT_BYTES = 56 * 1024 * 1024

INT_MIN = -(2 ** 31)
NEG_INF = float("-inf")

TM_PROJ = 512
TQ = 128
UNIT = 512
NEAR = 2 * TQ
COUNT_ACCS = 8
PLANE_ROWS = 32 * SUBLANES
TM_MOE = 1024
MOE_EXPERTS_PER_STEP = 4
MOE_GROUP = 256
MOE_CAP = 80
MOE_EXPERT_ROWS = 2048

_C_AQ, _C_CKV, _C_IQ, _C_IKA, _C_IKB, _C_HQ, _C_HF, _C_HG, _C_HI, _C_END = (
    0, 512, 640, 1152, 1280, 1408, 1920, 2432, 2944, 3456)


def _silu(v):
    return v * (1.0 / (1.0 + jnp.exp(-v)))


def _nt_dot(a, b):
    return lax.dot_general(a, b, (((1,), (1,)), ((), ())), preferred_element_type=F32)


def _cparams(sem):
    return pltpu.CompilerParams(dimension_semantics=sem, vmem_limit_bytes=VMEM_LIMIT_BYTES)


def _adaln_kernel(c_ref, w_ref, b_ref, o_ref):
    cond = _silu(c_ref[...])
    o_ref[0] = jnp.dot(cond.astype(BF16), w_ref[0].astype(BF16), preferred_element_type=F32) + b_ref[0]


def _adaln(c, w_ada, b_ada):
    L, D, D6 = w_ada.shape
    B = c.shape[0]
    nb = D6 // D
    return pl.pallas_call(
        _adaln_kernel,
        grid=(L, nb),
        in_specs=[
            pl.BlockSpec((B, D), lambda l, j: (0, 0)),
            pl.BlockSpec((1, D, D), lambda l, j: (l, 0, j)),
            pl.BlockSpec((1, 1, D), lambda l, j: (l, 0, j)),
        ],
        out_specs=pl.BlockSpec((1, B, D), lambda l, j: (l, 0, j)),
        out_shape=jax.ShapeDtypeStruct((L, B, D6), F32),
        compiler_params=_cparams(("arbitrary", "arbitrary")),
        name="adaln_mod",
    )(c, w_ada, b_ada.reshape(L, 1, D6))


_T5_NB = NUM_BUCKETS // 2
_T5_EXACT = _T5_NB // 2
_T5_THRESHOLDS = tuple(
    int(math.ceil(_T5_EXACT * (MAX_DISTANCE / _T5_EXACT) ** (j / (_T5_NB - _T5_EXACT)) - 1e-9))
    for j in range(1, _T5_NB - _T5_EXACT))
FAR_BUCKET = _T5_NB - 1
assert _T5_THRESHOLDS[-1] <= TQ, "keys further than one query block behind must share the far bucket"


def _bias_kernel(rb_ref, o_ref):
    kr = lax.broadcasted_iota(I32, (NEAR + TQ, TQ), 0)
    ql = lax.broadcasted_iota(I32, (NEAR + TQ, TQ), 1)
    rel = kr - TQ - ql
    n = jnp.abs(rel)
    large = jnp.full(rel.shape, _T5_EXACT, I32)
    for t in _T5_THRESHOLDS:
        large = large + (n >= t).astype(I32)
    bucket = jnp.where(rel > 0, _T5_NB, 0) + jnp.where(n < _T5_EXACT, n, large)
    for h in range(A_HEADS):
        acc = jnp.zeros(rel.shape, F32)
        for bk in range(NUM_BUCKETS):
            acc = jnp.where(bucket == bk, rb_ref[bk, h], acc)
        o_ref[h] = (acc - rb_ref[FAR_BUCKET, h]) * LOG2E


def _bias_tile(rel_bias):
    return pl.pallas_call(
        _bias_kernel,
        in_specs=[pl.BlockSpec(memory_space=pltpu.SMEM)],
        out_specs=pl.BlockSpec(memory_space=pltpu.VMEM),
        out_shape=jax.ShapeDtypeStruct((A_HEADS, NEAR + TQ, TQ), F32),
        name="rel_bias_tile",
    )(rel_bias)


def _inproj_kernel(x_ref, mod_ref, wp_ref, wblk_ref, wckvT_ref, wiwT_ref, gkv_ref, gkvT_ref, llb_ref, l1m_ref,
                   qlat_ref, ckv_ref, ckvT_ref, iq_ref, ikA_ref, ikB_ref, iwT_ref,
                   hq_ref, hk_ref, hlf_ref, hv_ref, hgate_ref):
    x = x_ref[0]
    sh1 = mod_ref[0, 0:1, :]
    sc1 = mod_ref[0, 1:2, :]
    u = (x * (1.0 + sc1) + sh1).astype(BF16)
    z = jnp.dot(u, wp_ref[0], preferred_element_type=F32)

    def proj(lo, hi):
        return z[:, lo:hi]

    ql = jnp.dot(proj(_C_AQ, _C_CKV).astype(BF16), wblk_ref[0], preferred_element_type=F32)
    for h in range(A_HEADS):
        qlat_ref[0, h] = ql[:, h * KV_RANK:(h + 1) * KV_RANK].astype(BF16)

    zc = proj(_C_CKV, _C_IQ)
    inv = lax.rsqrt(jnp.mean(zc * zc, axis=-1, keepdims=True) + RMS_EPS)
    ckv_ref[0] = (zc * inv * gkv_ref[0]).astype(BF16)
    zt = _nt_dot(wckvT_ref[0], u)
    inv_t = lax.rsqrt(jnp.mean(zt * zt, axis=0, keepdims=True) + RMS_EPS)
    ckvT_ref[0, 0:KV_RANK, :] = (zt * inv_t * gkvT_ref[0]).astype(BF16)
    ckvT_ref[0, KV_RANK:KV_EXT, :] = jnp.ones((KV_EXT - KV_RANK, zt.shape[1]), BF16)

    ziq = proj(_C_IQ, _C_IKA)
    for p in range(IDX_HEADS // 2):
        iq_ref[0, p] = ziq[:, p * LANES:(p + 1) * LANES].astype(BF16)
    ikA_ref[0] = proj(_C_IKA, _C_IKB).astype(BF16)
    ikB_ref[0] = proj(_C_IKB, _C_HQ).astype(BF16)
    iwT_ref[0] = _nt_dot(wiwT_ref[0], u) * IDX_W_SCALE

    hq_ref[0] = _silu(proj(_C_HQ, _C_HF))
    zf = proj(_C_HF, _C_HG)
    log_sig = jnp.minimum(zf, 0.0) - jnp.log1p(jnp.exp(-jnp.abs(zf)))
    a = llb_ref[0]
    c = l1m_ref[0] + log_sig
    logf = jnp.maximum(a, c) + jnp.log1p(jnp.exp(-jnp.abs(a - c)))
    hlf_ref[0] = logf
    hk_ref[0] = 1.0 - jnp.exp(logf)
    hgate_ref[0] = _silu(proj(_C_HG, _C_HI))
    hv_ref[0] = proj(_C_HI, _C_END).astype(BF16)


def _inproj(l, x, mod, wp, wblk, wckvT, wiwT, gkv, gkvT, llb, l1m):
    B, S, D = x.shape
    tm = TM_PROJ
    grid = (B, S // tm)
    lw3 = lambda b, i: (l, 0, 0)
    tok = lambda b, i: (b, i, 0)
    tokT = lambda b, i: (b, 0, i)
    hd4 = lambda b, i: (b, 0, i, 0)
    outs = [
        (jax.ShapeDtypeStruct((B, A_HEADS, S, KV_RANK), BF16), pl.BlockSpec((1, A_HEADS, tm, KV_RANK), hd4)),
        (jax.ShapeDtypeStruct((B, S, KV_RANK), BF16), pl.BlockSpec((1, tm, KV_RANK), tok)),
        (jax.ShapeDtypeStruct((B, KV_EXT, S), BF16), pl.BlockSpec((1, KV_EXT, tm), tokT)),
        (jax.ShapeDtypeStruct((B, IDX_HEADS // 2, S, LANES), BF16), pl.BlockSpec((1, IDX_HEADS // 2, tm, LANES), hd4)),
        (jax.ShapeDtypeStruct((B, S, LANES), BF16), pl.BlockSpec((1, tm, LANES), tok)),
        (jax.ShapeDtypeStruct((B, S, LANES), BF16), pl.BlockSpec((1, tm, LANES), tok)),
        (jax.ShapeDtypeStruct((B, IDX_HEADS, S), F32), pl.BlockSpec((1, IDX_HEADS, tm), tokT)),
        (jax.ShapeDtypeStruct((B, S, B_FDIM), F32), pl.BlockSpec((1, tm, B_FDIM), tok)),
        (jax.ShapeDtypeStruct((B, S, B_FDIM), F32), pl.BlockSpec((1, tm, B_FDIM), tok)),
        (jax.ShapeDtypeStruct((B, S, B_FDIM), F32), pl.BlockSpec((1, tm, B_FDIM), tok)),
        (jax.ShapeDtypeStruct((B, S, B_WIDTH), BF16), pl.BlockSpec((1, tm, B_WIDTH), tok)),
        (jax.ShapeDtypeStruct((B, S, B_WIDTH), F32), pl.BlockSpec((1, tm, B_WIDTH), tok)),
    ]
    return pl.pallas_call(
        _inproj_kernel,
        grid=grid,
        in_specs=[
            pl.BlockSpec((1, tm, D), tok),
            pl.BlockSpec((1, 6, D), lambda b, i: (b, 0, 0)),
            pl.BlockSpec((1, D, _C_END), lw3),
            pl.BlockSpec((1, A_WIDTH, A_HEADS * KV_RANK), lw3),
            pl.BlockSpec((1, KV_RANK, D), lw3),
            pl.BlockSpec((1, IDX_HEADS, D), lw3),
            pl.BlockSpec((1, 1, KV_RANK), lw3),
            pl.BlockSpec((1, KV_RANK, tm), lw3),
            pl.BlockSpec((1, 1, B_FDIM), lw3),
            pl.BlockSpec((1, 1, B_FDIM), lw3),
        ],
        out_specs=[o[1] for o in outs],
        out_shape=[o[0] for o in outs],
        compiler_params=_cparams(("arbitrary", "arbitrary")),
        name="inproj",
    )(x, mod, wp, wblk, wckvT, wiwT, gkv, gkvT, llb, l1m)


def _dsa_kernel(iq_ref, iwT_ref, qlat_ref, ikA_ref, ikB_ref, ckv_ref, ckvT_ref, bn_ref, wuvT_ref, out_ref,
                sc_ref, plane_ref, madd_ref, maddn_ref, la_ref, lb_ref, pma_ref, pmb_ref, ot_ref, yaT_ref,
                *, k_sel, n_idx_bits):
    j = pl.program_id(1)
    q0 = j * TQ
    nk = q0 + TQ
    nunit = (nk + UNIT - 1) // UNIT
    near0 = pl.multiple_of(jnp.maximum(nk - NEAR, 0), TQ)
    bn_row0 = pl.multiple_of(jnp.where(j == 0, TQ, 0), TQ)
    lane = lax.broadcasted_iota(I32, (1, TQ), 1)
    limit = (((q0 + lane) >> 6) + 1) << 6
    row_iota = lax.broadcasted_iota(I32, (UNIT, TQ), 0)

    def unit_rows(u):
        return pl.ds(pl.multiple_of(u * UNIT, UNIT), UNIT)

    iqs = iq_ref[0].reshape(IDX_HEADS // 2 * TQ, LANES)
    iw = iwT_ref[0]

    last_unit = sc_ref.shape[0] // UNIT - 1
    half = IDX_HEADS // 2 * TQ

    def issue_scores(u, buf_ref):
        rows = unit_rows(jnp.minimum(u, last_unit))
        buf_ref[:, 0:half] = _nt_dot(ikA_ref[0, rows, :], iqs)
        buf_ref[:, half:2 * half] = _nt_dot(ikB_ref[0, rows, :], iqs)

    def reduce_scores(u, buf_ref):
        acc = jnp.zeros((UNIT, TQ), F32)
        for p in range(IDX_HEADS // 2):
            acc = acc + iw[2 * p:2 * p + 1, :] * jnp.maximum(buf_ref[:, p * TQ:(p + 1) * TQ], 0.0)
            acc = acc + iw[2 * p + 1:2 * p + 2, :] * jnp.maximum(buf_ref[:, half + p * TQ:half + (p + 1) * TQ], 0.0)
        bits = lax.bitcast_convert_type(acc, I32)
        key = bits ^ ((bits >> 31) & 0x7FFFFFFF)
        sc_ref[unit_rows(u), :] = jnp.where(row_iota + u * UNIT < limit, key, INT_MIN)

    issue_scores(0, la_ref)

    def score_pair(i, carry):
        issue_scores(2 * i + 1, lb_ref)
        reduce_scores(2 * i, la_ref)
        issue_scores(2 * i + 2, la_ref)
        reduce_scores(2 * i + 1, lb_ref)
        return carry

    lax.fori_loop(0, nunit // 2, score_pair, 0)

    @pl.when(nunit % 2 == 1)
    def _():
        reduce_scores(nunit - 1, la_ref)

    ngroups = (nk + PLANE_ROWS - 1) // PLANE_ROWS

    def plane_group(g, carry):
        rows = pl.ds(pl.multiple_of(g * PLANE_ROWS, PLANE_ROWS), PLANE_ROWS)
        words = (sc_ref[rows, :] ^ INT_MIN).reshape(32, SUBLANES, TQ)
        w = [words[i] for i in range(32)]
        j, m = 16, 0x0000FFFF
        while j:
            mask = np.int32(np.uint32(m).view(np.int32))
            k = 0
            while k < 32:
                t = (w[k] ^ lax.shift_right_logical(w[k + j], jnp.full(w[k].shape, j, I32))) & mask
                w[k] = w[k] ^ t
                w[k + j] = w[k + j] ^ (t << j)
                k = (k + j + 1) & ~j
            j >>= 1
            m = (m ^ (m << j)) & 0xFFFFFFFF
        for i in range(32):
            plane_ref[i, pl.ds(g * SUBLANES, SUBLANES), :] = w[i]
        return carry

    lax.fori_loop(0, ngroups, plane_group, 0)

    n_words = sc_ref.shape[0] // PLANE_ROWS * SUBLANES
    group_of_word = lax.broadcasted_iota(I32, (n_words, TQ), 0) // SUBLANES

    def bit_step(i, carry):
        alive, above, t_off, c_ge = carry
        hit = alive & plane_ref[i]
        cnt = above + jnp.sum(lax.population_count(hit), axis=0, keepdims=True)
        ok = cnt >= k_sel
        alive = jnp.where(ok, hit, alive ^ hit)
        above = jnp.where(ok, above, cnt)
        t_off = jnp.where(ok, t_off | (jnp.int32(1) << (31 - i)), t_off)
        return alive, above, t_off, jnp.where(ok, cnt, c_ge)

    zero_row = jnp.zeros((1, TQ), I32)
    _, _, t_off, c_ge = lax.fori_loop(
        0, 32, bit_step,
        (jnp.where(group_of_word < ngroups, jnp.int32(-1), jnp.int32(0)), zero_row, zero_row, zero_row))
    thr = jnp.maximum(t_off ^ INT_MIN, INT_MIN + 1)
    straddle = (c_ge > k_sel).astype(I32)

    def count_where(pred):
        def body(u, acc):
            hit = pred(sc_ref[unit_rows(u), :], u * UNIT).reshape(-1, COUNT_ACCS * SUBLANES, TQ)
            for s in range(hit.shape[0]):
                acc = jnp.where(hit[s], acc + 1, acc)
            return acc
        acc = lax.fori_loop(0, nunit, body, jnp.zeros((COUNT_ACCS * SUBLANES, TQ), I32))
        return jnp.sum(acc, axis=0, keepdims=True)

    def tie_bound():
        c_gt = count_where(lambda blk, r0: blk > thr)
        need = k_sel - c_gt

        def tie_body(i, j0):
            cand = j0 | (jnp.int32(1) << (n_idx_bits - 1 - i))
            cnt = count_where(lambda blk, r0: jnp.where(blk == thr, row_iota + r0, cand) < cand)
            return jnp.where(cnt < need, cand, j0)

        j0 = lax.fori_loop(0, n_idx_bits, tie_body, jnp.zeros((1, TQ), I32))
        return jnp.where(straddle > 0, j0 + 1, jnp.int32(2 ** n_idx_bits))

    jstar = lax.cond(jnp.max(straddle) > 0, tie_bound, lambda: jnp.full((1, TQ), 2 ** n_idx_bits, I32))

    def madd_unit(u, carry):
        rows = unit_rows(u)
        key = sc_ref[rows, :]
        tie_keep = jnp.where(row_iota + u * UNIT < jstar, 0.0, NEG_INF)
        madd_ref[rows, :] = jnp.where(key > thr, 0.0, jnp.where(key == thr, tie_keep, NEG_INF))
        return carry

    lax.fori_loop(0, nunit, madd_unit, 0)
    maddn_ref[...] = madd_ref[pl.ds(near0, NEAR), :]
    madd_ref[pl.ds(near0, NEAR), :] = jnp.full((NEAR, TQ), NEG_INF, F32)

    qall = qlat_ref[0].reshape(A_HEADS * TQ, KV_RANK)

    def col_max(v):
        return jnp.max(v.reshape(v.shape[0] // SUBLANES, SUBLANES, A_HEADS * TQ), axis=0)

    def fold(xl, part_max, ckv_t, m_old):
        m_new = jnp.maximum(m_old, jnp.max(part_max, axis=0, keepdims=True))
        m_use = jnp.where(m_new == NEG_INF, 0.0, m_new)
        p = jnp.exp2((xl - m_use).astype(BF16))
        ot_ref[...] = ot_ref[...] * jnp.exp2(m_old - m_use) + jnp.dot(ckv_t, p, preferred_element_type=F32)
        return m_new

    ot_ref[...] = jnp.zeros(ot_ref.shape, F32)
    near_rows = pl.ds(near0, NEAR)
    xn = _nt_dot(ckv_ref[0, near_rows, :], qall) + jnp.concatenate([maddn_ref[...]] * A_HEADS, axis=1)
    xn = xn + jnp.concatenate([bn_ref[h, pl.ds(bn_row0, NEAR), :] for h in range(A_HEADS)], axis=1)
    m_run = fold(xn, col_max(xn), ckvT_ref[0, :, near_rows], jnp.full((1, A_HEADS * TQ), NEG_INF, F32))


    def issue_logits(u, buf_ref, pm_ref):
        rows = unit_rows(jnp.minimum(u, last_unit))
        xl = _nt_dot(ckv_ref[0, rows, :], qall) + jnp.concatenate([madd_ref[rows, :]] * A_HEADS, axis=1)
        buf_ref[...] = xl
        pm_ref[...] = col_max(xl)

    def consume_logits(u, buf_ref, pm_ref, m_old):
        return fold(buf_ref[...], pm_ref[...], ckvT_ref[0, :, unit_rows(u)], m_old)

    issue_logits(0, la_ref, pma_ref)

    def pair_step(i, m_old):
        issue_logits(2 * i + 1, lb_ref, pmb_ref)
        m_mid = consume_logits(2 * i, la_ref, pma_ref, m_old)
        issue_logits(2 * i + 2, la_ref, pma_ref)
        return consume_logits(2 * i + 1, lb_ref, pmb_ref, m_mid)

    m_run = lax.fori_loop(0, nunit // 2, pair_step, m_run)

    @pl.when(nunit % 2 == 1)
    def _():
        consume_logits(nunit - 1, la_ref, pma_ref, m_run)
    o_t = (ot_ref[0:KV_RANK, :] * (1.0 / ot_ref[KV_RANK:KV_RANK + 1, :])).astype(BF16)
    for h in range(A_HEADS):
        yaT_ref[h * A_HEAD_DIM:(h + 1) * A_HEAD_DIM, :] = jnp.dot(
            wuvT_ref[0, h], o_t[:, h * TQ:(h + 1) * TQ], preferred_element_type=F32)

    out_ref[0] = yaT_ref[...].T.astype(BF16)


def _dsa(l, iq, iwT, qlat, ikA, ikB, ckv, ckvT, bn, wuvT):
    B, S = ckv.shape[0], ckv.shape[1]
    assert S % (2 * UNIT) == 0 and UNIT % TQ == 0 and TQ % CHUNK == 0 and CHUNK == 64 and NEAR <= UNIT
    k_sel = min(IDX_TOPK_MAX, S // 4)
    n_idx_bits = int(math.log2(S))
    assert 2 ** n_idx_bits == S
    grid = (B, S // TQ)
    blk = lambda b, i: (b, 0, i, 0)
    full = lambda b, i: (b, 0, 0)
    kern = functools.partial(_dsa_kernel, k_sel=k_sel, n_idx_bits=n_idx_bits)
    return pl.pallas_call(
        kern,
        grid=grid,
        in_specs=[
            pl.BlockSpec((1, IDX_HEADS // 2, TQ, LANES), blk),
            pl.BlockSpec((1, IDX_HEADS, TQ), lambda b, i: (b, 0, i)),
            pl.BlockSpec((1, A_HEADS, TQ, KV_RANK), blk),
            pl.BlockSpec((1, S, LANES), full),
            pl.BlockSpec((1, S, LANES), full),
            pl.BlockSpec((1, S, KV_RANK), full),
            pl.BlockSpec((1, KV_EXT, S), full),
            pl.BlockSpec((A_HEADS, NEAR + TQ, TQ), lambda b, i: (0, 0, 0)),
            pl.BlockSpec((1, A_HEADS, A_HEAD_DIM, KV_RANK), lambda b, i: (l, 0, 0, 0)),
        ],
        out_specs=pl.BlockSpec((1, TQ, A_WIDTH), lambda b, i: (b, i, 0)),
        out_shape=jax.ShapeDtypeStruct((B, S, A_WIDTH), BF16),
        scratch_shapes=[
            pltpu.VMEM((S, TQ), I32),
            pltpu.VMEM((32, S // PLANE_ROWS * SUBLANES, TQ), I32),
            pltpu.VMEM((S, TQ), F32),
            pltpu.VMEM((NEAR, TQ), F32),
            pltpu.VMEM((UNIT, A_HEADS * TQ), F32),
            pltpu.VMEM((UNIT, A_HEADS * TQ), F32),
            pltpu.VMEM((SUBLANES, A_HEADS * TQ), F32),
            pltpu.VMEM((SUBLANES, A_HEADS * TQ), F32),
            pltpu.VMEM((KV_EXT, A_HEADS * TQ), F32),
            pltpu.VMEM((A_WIDTH, TQ), F32),
        ],
        compiler_params=_cparams(("arbitrary", "arbitrary")),
        name="dsa_attention",
    )(iq, iwT, qlat, ikA, ikB, ckv, ckvT, bn, wuvT)


def _hgrn_constants():
    c = CHUNK
    r = np.arange(c)[:, None]
    jj = np.arange(c)[None, :]
    mats = [(jj <= r), (jj > r)]
    masks = [np.eye(c, dtype=bool)]
    m = c // 2
    while m >= 1:
        start = (r // (2 * m)) * (2 * m)
        bd = start + m - 1
        upper = r > bd
        mats.append(np.where(upper, (jj > bd) & (jj <= r), (jj > r) & (jj <= bd)))
        same_parent = (r // (2 * m)) == (jj // (2 * m))
        masks.append(same_parent & upper & (jj <= (jj // (2 * m)) * (2 * m) + m - 1))
        m //= 2
    m_all = np.concatenate(mats, axis=0).astype(np.float32)
    total = np.zeros((c, c), np.int32)
    for mk in masks:
        total += mk
    assert (total == np.tril(np.ones((c, c), np.int32))).all()
    return np.concatenate([m_all] * 3, axis=1), np.stack(masks).astype(np.float32)


_HGRN_M3, _HGRN_MASKS = _hgrn_constants()
_HGRN_LEVELS = _HGRN_MASKS.shape[0] - 1
HGRN_STEP_CHUNKS = 2
HGRN_STEP_BATCH = 4


def _hgrn_kernel(q_ref, k_ref, lf_ref, v_ref, gate_ref, m3_ref, mask_ref, gn_ref, out_ref, st_ref):
    @pl.when(pl.program_id(1) == 0)
    def _():
        st_ref[...] = jnp.zeros(st_ref.shape, F32)

    c = CHUNK
    intra = {}
    for ci in range(HGRN_STEP_CHUNKS):
        rows = slice(ci * c, (ci + 1) * c)
        for bi in range(HGRN_STEP_BATCH):
            g = lf_ref[bi, rows, :]
            g_hi = g.astype(BF16)
            r1 = g - g_hi.astype(F32)
            g_mid = r1.astype(BF16)
            g_lo = (r1 - g_mid.astype(F32)).astype(BF16)
            sums = jnp.dot(m3_ref[...], jnp.concatenate([g_hi, g_mid, g_lo], axis=0), preferred_element_type=F32)
            e_all = jnp.exp(sums)
            for h in range(B_HEADS):
                cols = slice(h * B_KEY_DIM, (h + 1) * B_KEY_DIM)
                qh = q_ref[bi, rows, cols]
                kh = k_ref[bi, rows, cols]
                att = mask_ref[0] * _nt_dot(qh.astype(BF16), kh.astype(BF16))
                for lv in range(_HGRN_LEVELS):
                    e_l = e_all[(2 + lv) * c:(3 + lv) * c, cols]
                    att = att + mask_ref[lv + 1] * _nt_dot((qh * e_l).astype(BF16), (kh * e_l).astype(BF16))
                e_b = e_all[0:c, cols]
                intra[bi, ci, h] = (jnp.dot(att.astype(BF16), v_ref[bi, rows, cols], preferred_element_type=F32),
                                    (qh * e_b).astype(BF16), (kh * e_all[c:2 * c, cols]).astype(BF16),
                                    e_b[c - 1:c, :])
    for ci in range(HGRN_STEP_CHUNKS):
        rows = slice(ci * c, (ci + 1) * c)
        for bi in range(HGRN_STEP_BATCH):
            for h in range(B_HEADS):
                cols = slice(h * B_KEY_DIM, (h + 1) * B_KEY_DIM)
                o_intra, q_dec, k_rem, decay_all = intra[bi, ci, h]
                st = st_ref[bi, h]
                o = o_intra + _nt_dot(q_dec, st.astype(BF16))
                upd = lax.dot_general(v_ref[bi, rows, cols], k_rem, (((0,), (0,)), ((), ())),
                                      preferred_element_type=F32)
                st_ref[bi, h] = st * decay_all + upd
                o = o * lax.rsqrt(jnp.mean(o * o, axis=-1, keepdims=True) + RMS_EPS) * gn_ref[0]
                out_ref[bi, rows, cols] = (o * gate_ref[bi, rows, cols]).astype(BF16)


def _hgrn(l, hq, hk, hlf, hv, hgate, gnorm):
    B, S, W = hq.shape
    ts = CHUNK * HGRN_STEP_CHUNKS
    nb = HGRN_STEP_BATCH
    assert B % nb == 0
    tok = lambda b, i: (b, i, 0)
    return pl.pallas_call(
        _hgrn_kernel,
        grid=(B // nb, S // ts),
        in_specs=[
            pl.BlockSpec((nb, ts, W), tok),
            pl.BlockSpec((nb, ts, W), tok),
            pl.BlockSpec((nb, ts, W), tok),
            pl.BlockSpec((nb, ts, W), tok),
            pl.BlockSpec((nb, ts, W), tok),
            pl.BlockSpec(_HGRN_M3.shape, lambda b, i: (0, 0)),
            pl.BlockSpec(_HGRN_MASKS.shape, lambda b, i: (0, 0, 0)),
            pl.BlockSpec((1, 1, B_VAL_DIM), lambda b, i: (l, 0, 0)),
        ],
        out_specs=pl.BlockSpec((nb, ts, W), tok),
        out_shape=jax.ShapeDtypeStruct((B, S, W), BF16),
        scratch_shapes=[pltpu.VMEM((nb, B_HEADS, B_VAL_DIM, B_KEY_DIM), F32)],
        compiler_params=_cparams(("arbitrary", "arbitrary")),
        name="hgrn2",
    )(hq, hk, hlf, hv, hgate, jnp.asarray(_HGRN_M3, BF16), jnp.asarray(_HGRN_MASKS), gnorm)


def _layernorm(v, g, b):
    mu = jnp.mean(v, axis=-1, keepdims=True)
    d = v - mu
    var = jnp.mean(d * d, axis=-1, keepdims=True)
    return d * lax.rsqrt(var + LN_EPS) * g + b


def _first_argmax(v, idx, axes, big):
    mx = v
    for ax in axes:
        mx = jnp.max(mx, axis=ax, keepdims=True)
    pos = jnp.where(v == mx, idx, big)
    for ax in axes:
        pos = jnp.min(pos, axis=ax, keepdims=True)
    return mx, pos


def _outproj_kernel(ya_ref, yb_ref, x_ref, mod_ref, wo_ref, lng_ref, lnb_ref, wrT_ref, rbias_ref, tri_ref,
                    x1_ref, u2_ref, gates_ref, rank_ref, gatesT_ref, *, alpha):
    y = jnp.dot(ya_ref[0], wo_ref[0, 0:A_WIDTH, :], preferred_element_type=F32)
    y = y + jnp.dot(yb_ref[0], wo_ref[0, A_WIDTH:, :], preferred_element_type=F32)
    g1 = mod_ref[0, 2:3, :]
    x1 = _layernorm(alpha * x_ref[0] + (1.0 + g1) * y, lng_ref[0], lnb_ref[0])
    x1_ref[0] = x1
    u2 = (x1 * (1.0 + mod_ref[0, 4:5, :]) + mod_ref[0, 3:4, :]).astype(BF16)
    u2_ref[0] = u2

    tm = u2.shape[0]
    gsz = N_EXPERTS // N_GROUPS
    scores = 1.0 / (1.0 + jnp.exp(-_nt_dot(wrT_ref[0], u2)))
    sel = (scores + rbias_ref[0]).reshape(N_GROUPS, gsz, tm)
    scores = scores.reshape(N_GROUPS, gsz, tm)
    i_m = lax.broadcasted_iota(I32, (N_GROUPS, gsz, tm), 1)
    i_g = lax.broadcasted_iota(I32, (N_GROUPS, 1, tm), 0)
    i_e = lax.broadcasted_iota(I32, (N_GROUPS, gsz, tm), 0) * gsz + i_m
    m1, p1 = _first_argmax(sel, i_m, (1,), gsz)
    m2 = jnp.max(jnp.where(i_m == p1, NEG_INF, sel), axis=1, keepdims=True)
    gs = m1 + m2
    gmask = jnp.zeros(gs.shape, F32)
    for _ in range(TOPK_GROUPS):
        _, pg = _first_argmax(gs, i_g, (0,), N_GROUPS)
        hit = i_g == pg
        gmask = jnp.where(hit, 1.0, gmask)
        gs = jnp.where(hit, NEG_INF, gs)
    cand = jnp.where(jnp.broadcast_to(gmask, sel.shape) > 0.0, sel, NEG_INF)
    w = jnp.zeros(sel.shape, F32)
    chosen = jnp.zeros(sel.shape, F32)
    for _ in range(TOP_K):
        _, pe = _first_argmax(cand, i_e, (1, 0), N_EXPERTS)
        hit = i_e == pe
        w = jnp.where(hit, scores, w)
        chosen = jnp.where(hit, 1.0, chosen)
        cand = jnp.where(hit, NEG_INF, cand)
    wsum = jnp.sum(jnp.sum(w, axis=1, keepdims=True), axis=0, keepdims=True)
    gates = (w / wsum * ROUTED_SCALE).reshape(N_EXPERTS, tm)
    g_hi = gates.astype(BF16).astype(F32)
    g_lo = (gates - g_hi).astype(BF16).astype(F32)
    gates_ref[0] = jnp.concatenate([g_hi, g_lo], axis=0).T.astype(BF16)

    chosen2 = chosen.reshape(N_EXPERTS, tm)
    rank = jnp.concatenate(
        [jnp.dot(chosen2[:, g * MOE_GROUP:(g + 1) * MOE_GROUP].astype(BF16), tri_ref[...],
                 preferred_element_type=F32) for g in range(tm // MOE_GROUP)], axis=1)
    rank_ref[0] = jnp.where(chosen2 > 0.0, rank, -1.0).astype(I32)
    gatesT_ref[0] = gates


def _outproj(l, ya, yb, x, mod, wo, ln_g, ln_b, wrT, rbias, alpha):
    B, S, D = x.shape
    tm = TM_PROJ
    tok = lambda b, i: (b, i, 0)
    lw3 = lambda b, i: (l, 0, 0)
    return pl.pallas_call(
        functools.partial(_outproj_kernel, alpha=alpha),
        grid=(B, S // tm),
        in_specs=[
            pl.BlockSpec((1, tm, A_WIDTH), tok),
            pl.BlockSpec((1, tm, B_WIDTH), tok),
            pl.BlockSpec((1, tm, D), tok),
            pl.BlockSpec((1, 6, D), lambda b, i: (b, 0, 0)),
            pl.BlockSpec((1, D, D), lw3),
            pl.BlockSpec((1, 1, D), lw3),
            pl.BlockSpec((1, 1, D), lw3),
            pl.BlockSpec((1, N_EXPERTS, D), lw3),
            pl.BlockSpec((1, N_EXPERTS, tm), lw3),
            pl.BlockSpec((MOE_GROUP, MOE_GROUP), lambda b, i: (0, 0)),
        ],
        out_specs=[pl.BlockSpec((1, tm, D), tok), pl.BlockSpec((1, tm, D), tok),
                   pl.BlockSpec((1, tm, 2 * N_EXPERTS), tok),
                   pl.BlockSpec((1, N_EXPERTS, tm), lambda b, i: (b, 0, i)),
                   pl.BlockSpec((1, N_EXPERTS, tm), lambda b, i: (b, 0, i))],
        out_shape=[jax.ShapeDtypeStruct((B, S, D), F32), jax.ShapeDtypeStruct((B, S, D), BF16),
                   jax.ShapeDtypeStruct((B, S, 2 * N_EXPERTS), BF16),
                   jax.ShapeDtypeStruct((B, N_EXPERTS, S), I32),
                   jax.ShapeDtypeStruct((B, N_EXPERTS, S), F32)],
        compiler_params=_cparams(("arbitrary", "arbitrary")),
        name="outproj_router",
    )(ya, yb, x, mod, wo, ln_g, ln_b, wrT, rbias,
      jnp.asarray(np.triu(np.ones((MOE_GROUP, MOE_GROUP), np.float32), 1), BF16))


MOE_CHUNK_EXPERTS = 8


def _slot_onehot(rank_rows, values):
    row = lax.broadcasted_iota(I32, (MOE_CAP, rank_rows.shape[1]), 0)
    return jnp.concatenate(
        [jnp.where(row == rank_rows[e:e + 1, :], values[e:e + 1, :], 0.0) for e in range(rank_rows.shape[0])], axis=0)


def _dispatch_kernel(u_ref, rank_ref, x_ref):
    u = u_ref[...]
    ones = jnp.ones((MOE_CHUNK_EXPERTS, MOE_GROUP), F32)
    for c in range(N_EXPERTS // MOE_CHUNK_EXPERTS):
        es = slice(c * MOE_CHUNK_EXPERTS, (c + 1) * MOE_CHUNK_EXPERTS)
        onehot = _slot_onehot(rank_ref[0, es, :], ones)
        xs = jnp.dot(onehot.astype(BF16), u, preferred_element_type=F32).astype(BF16)
        x_ref[es] = xs.reshape(MOE_CHUNK_EXPERTS, MOE_CAP, -1)


def _dispatch(u2, rank):
    T, D = u2.shape
    ng = T // MOE_GROUP
    gps = rank.shape[-1] // MOE_GROUP
    return pl.pallas_call(
        _dispatch_kernel,
        grid=(ng,),
        in_specs=[pl.BlockSpec((MOE_GROUP, D), lambda g: (g, 0)),
                  pl.BlockSpec((1, N_EXPERTS, MOE_GROUP), lambda g: (g // gps, 0, g % gps))],
        out_specs=pl.BlockSpec((N_EXPERTS, MOE_CAP, D), lambda g: (0, g, 0)),
        out_shape=jax.ShapeDtypeStruct((N_EXPERTS, ng * MOE_CAP, D), BF16),
        compiler_params=_cparams(("arbitrary",)),
        name="moe_dispatch",
    )(u2, rank)


def _expert_kernel(x_ref, wgu_ref, wd_ref, y_ref):
    hgu = jnp.dot(x_ref[0], wgu_ref[0, 0], preferred_element_type=F32)
    h = _silu(hgu[:, :EXPERT_DIM]) * hgu[:, EXPERT_DIM:]
    y_ref[0] = jnp.dot(h.astype(BF16), wd_ref[0, 0], preferred_element_type=F32).astype(BF16)


def _experts(l, xs, wgu, wd):
    E, R, D = xs.shape
    tr = min(R, MOE_EXPERT_ROWS)
    assert R % tr == 0
    return pl.pallas_call(
        _expert_kernel,
        grid=(E, R // tr),
        in_specs=[pl.BlockSpec((1, tr, D), lambda e, i: (e, i, 0)),
                  pl.BlockSpec((1, 1, D, 2 * EXPERT_DIM), lambda e, i: (l, e, 0, 0)),
                  pl.BlockSpec((1, 1, EXPERT_DIM, D), lambda e, i: (l, e, 0, 0))],
        out_specs=pl.BlockSpec((1, tr, D), lambda e, i: (e, i, 0)),
        out_shape=jax.ShapeDtypeStruct((E, R, D), BF16),
        compiler_params=_cparams(("arbitrary", "arbitrary")),
        name="moe_experts",
    )(xs, wgu, wd)


def _combine_kernel(y_ref, rank_ref, gates_ref, u_ref, x1_ref, mod_ref, sgu_ref, sd_ref, lng_ref, lnb_ref,
                    out_ref, *, alpha):
    hgu = jnp.dot(u_ref[...], sgu_ref[0], preferred_element_type=F32)
    hs = _silu(hgu[:, :SHARED_DIM]) * hgu[:, SHARED_DIM:]
    y = jnp.dot(hs.astype(BF16), sd_ref[0], preferred_element_type=F32)
    for c in range(N_EXPERTS // MOE_CHUNK_EXPERTS):
        es = slice(c * MOE_CHUNK_EXPERTS, (c + 1) * MOE_CHUNK_EXPERTS)
        pick = _slot_onehot(rank_ref[0, es, :], gates_ref[0, es, :])
        ys = y_ref[es].reshape(MOE_CHUNK_EXPERTS * MOE_CAP, -1)
        y = y + lax.dot_general(pick.astype(BF16), ys, (((0,), (0,)), ((), ())), preferred_element_type=F32)
    g2 = mod_ref[0, 5:6, :]
    out_ref[...] = _layernorm(alpha * x1_ref[...] + (1.0 + g2) * y, lng_ref[0], lnb_ref[0])


def _combine(l, ys, rank, gates_t, u2, x1, mod, sgu, sd, ln_g, ln_b, alpha, seq):
    T, D = u2.shape
    tok = lambda g: (g, 0)
    lw3 = lambda g: (l, 0, 0)
    gps = seq // MOE_GROUP
    per_group = lambda g: (g // gps, 0, g % gps)
    return pl.pallas_call(
        functools.partial(_combine_kernel, alpha=alpha),
        grid=(T // MOE_GROUP,),
        in_specs=[
            pl.BlockSpec((N_EXPERTS, MOE_CAP, D), lambda g: (0, g, 0)),
            pl.BlockSpec((1, N_EXPERTS, MOE_GROUP), per_group),
            pl.BlockSpec((1, N_EXPERTS, MOE_GROUP), per_group),
            pl.BlockSpec((MOE_GROUP, D), tok),
            pl.BlockSpec((MOE_GROUP, D), tok),
            pl.BlockSpec((1, 6, D), lambda g: ((g * MOE_GROUP) // seq, 0, 0)),
            pl.BlockSpec((1, D, 2 * SHARED_DIM), lw3),
            pl.BlockSpec((1, SHARED_DIM, D), lw3),
            pl.BlockSpec((1, 1, D), lw3),
            pl.BlockSpec((1, 1, D), lw3),
        ],
        out_specs=pl.BlockSpec((MOE_GROUP, D), tok),
        out_shape=jax.ShapeDtypeStruct((T, D), F32),
        compiler_params=_cparams(("arbitrary",)),
        name="moe_combine",
    )(ys, rank, gates_t, u2, x1, mod, sgu, sd, ln_g, ln_b)


def _moe_kernel(u_ref, gates_ref, x1_ref, mod_ref, wgu_ref, wd_ref, sgu_ref, sd_ref, lng_ref, lnb_ref,
                out_ref, acc_ref, *, alpha):
    s = pl.program_id(1)
    u = u_ref[...]

    def hidden(wgu):
        hgu = jnp.dot(u, wgu, preferred_element_type=F32)
        return _silu(hgu[:, :EXPERT_DIM]) * hgu[:, EXPERT_DIM:]

    @pl.when(s == 0)
    def _():
        acc_ref[...] = jnp.dot(hidden(sgu_ref[0]).astype(BF16), sd_ref[0], preferred_element_type=F32)

    rows = lax.broadcasted_iota(I32, (2 * N_EXPERTS, MOE_EXPERTS_PER_STEP * EXPERT_DIM), 0) & (N_EXPERTS - 1)
    cols = lax.broadcasted_iota(I32, (2 * N_EXPERTS, MOE_EXPERTS_PER_STEP * EXPERT_DIM), 1)
    onehot = jnp.where(rows == s * MOE_EXPERTS_PER_STEP + cols // EXPERT_DIM, 1.0, 0.0).astype(BF16)
    gate = jnp.dot(gates_ref[...], onehot, preferred_element_type=F32)
    h = jnp.concatenate(
        [(hidden(wgu_ref[0, k]) * gate[:, k * EXPERT_DIM:(k + 1) * EXPERT_DIM]).astype(BF16)
         for k in range(MOE_EXPERTS_PER_STEP)], axis=1)
    wd = wd_ref[0].reshape(MOE_EXPERTS_PER_STEP * EXPERT_DIM, wd_ref.shape[-1])
    acc_ref[...] += jnp.dot(h, wd, preferred_element_type=F32)

    @pl.when(s == pl.num_programs(1) - 1)
    def _():
        g2 = mod_ref[0, 5:6, :]
        out_ref[...] = _layernorm(alpha * x1_ref[...] + (1.0 + g2) * acc_ref[...], lng_ref[0], lnb_ref[0])


def _moe(l, u2, gates, x1, mod, wgu, wd, sgu, sd, ln_g, ln_b, alpha, seq):
    T, D = u2.shape
    tm = TM_MOE
    assert seq % tm == 0
    tok = lambda i, e: (i, 0)
    lw3 = lambda i, e: (l, 0, 0)
    return pl.pallas_call(
        functools.partial(_moe_kernel, alpha=alpha),
        grid=(T // tm, N_EXPERTS // MOE_EXPERTS_PER_STEP),
        in_specs=[
            pl.BlockSpec((tm, D), tok),
            pl.BlockSpec((tm, 2 * N_EXPERTS), tok),
            pl.BlockSpec((tm, D), tok),
            pl.BlockSpec((1, 6, D), lambda i, e: ((i * tm) // seq, 0, 0)),
            pl.BlockSpec((1, MOE_EXPERTS_PER_STEP, D, 2 * EXPERT_DIM), lambda i, e: (l, e, 0, 0)),
            pl.BlockSpec((1, MOE_EXPERTS_PER_STEP, EXPERT_DIM, D), lambda i, e: (l, e, 0, 0)),
            pl.BlockSpec((1, D, 2 * SHARED_DIM), lw3),
            pl.BlockSpec((1, SHARED_DIM, D), lw3),
            pl.BlockSpec((1, 1, D), lw3),
            pl.BlockSpec((1, 1, D), lw3),
        ],
        out_specs=pl.BlockSpec((tm, D), tok),
        out_shape=jax.ShapeDtypeStruct((T, D), F32),
        scratch_shapes=[pltpu.VMEM((tm, D), F32)],
        compiler_params=_cparams(("arbitrary", "arbitrary")),
        name="moe_dense",
    )(u2, gates, x1, mod, wgu, wd, sgu, sd, ln_g, ln_b)


def _prepare_params(w_in, kv_norm_g, w_uk, w_uv, hgrn_lb, w_out, w_router, router_bias,
                    w_gate, w_up, w_down, ws_gate, ws_up, ws_down):
    L = w_in.shape[0]
    sizes = (A_WIDTH, KV_RANK, IDX_HEADS * IDX_DIM, IDX_DIM, IDX_HEADS, B_FDIM, B_FDIM, B_WIDTH, B_WIDTH)
    offs = np.concatenate([[0], np.cumsum(sizes)])
    seg = lambda i: w_in[:, :, offs[i]:offs[i + 1]]
    w_aq, w_ckv, w_iq, w_ik, w_iw, w_hq, w_hf, w_hi, w_hg = (seg(i) for i in range(9))
    zik = jnp.zeros_like(w_ik)
    wp = jnp.concatenate([w_aq, w_ckv, w_iq, w_ik, zik, zik, w_ik, w_hq, w_hf, w_hg, w_hi], axis=-1).astype(BF16)
    assert wp.shape[-1] == _C_END
    eye = jnp.eye(A_HEADS, dtype=F32)
    wblk = (jnp.einsum('lhdr,hg->lhdgr', w_uk * (ATTN_SCALE * LOG2E), eye)
            .reshape(L, A_WIDTH, A_HEADS * KV_RANK).astype(BF16))
    p = dict(
        wp=wp, wblk=wblk,
        wckvT=jnp.swapaxes(w_ckv, 1, 2).astype(BF16),
        wiwT=jnp.swapaxes(w_iw, 1, 2).astype(BF16),
        gkv=kv_norm_g.reshape(L, 1, KV_RANK),
        gkvT=jnp.broadcast_to(kv_norm_g[:, :, None], (L, KV_RANK, TM_PROJ)),
        wuvT=jnp.swapaxes(w_uv, 2, 3).astype(BF16),
        wo=w_out.astype(BF16),
        wrT=jnp.swapaxes(w_router, 1, 2).astype(BF16),
        rbias=jnp.broadcast_to(router_bias[:, :, None], (L, N_EXPERTS, TM_PROJ)),
        wgu=jnp.concatenate([w_gate, w_up], axis=-1).astype(BF16),
        wd=w_down.astype(BF16),
        sgu=jnp.concatenate([ws_gate, ws_up], axis=-1).astype(BF16),
        sd=ws_down.astype(BF16),
    )
    lbs = jnp.cumsum(jax.nn.softmax(hgrn_lb.astype(F32), axis=0), axis=0)
    lbs = jnp.clip(lbs - lbs[0:1], 0.0, 1.0 - 1e-6)
    p["llb"] = jnp.log(lbs).reshape(L, 1, B_FDIM)
    p["l1m"] = jnp.log1p(-lbs).reshape(L, 1, B_FDIM)
    return p


def kernel(x, c, w_ada, b_ada, w_in, kv_norm_g, w_uk, w_uv, rel_bias, hgrn_lb, gnorm_g, w_out, ln1_g, ln1_b,
           w_router, router_bias, w_gate, w_up, w_down, ws_gate, ws_up, ws_down, ln2_g, ln2_b):
    B, S, D = x.shape
    L = w_in.shape[0]
    alpha = (2 * L) ** 0.25
    p = _prepare_params(w_in, kv_norm_g, w_uk, w_uv, hgrn_lb, w_out, w_router, router_bias,
                        w_gate, w_up, w_down, ws_gate, ws_up, ws_down)
    mods = _adaln(c, w_ada, b_ada).reshape(L, B, 6, D)
    bn = _bias_tile(rel_bias)
    gn = gnorm_g.reshape(L, 1, B_VAL_DIM)
    ln1g, ln1b = ln1_g.reshape(L, 1, D), ln1_b.reshape(L, 1, D)
    ln2g, ln2b = ln2_g.reshape(L, 1, D), ln2_b.reshape(L, 1, D)
    for l in range(L):
        mod = mods[l]
        (qlat, ckv, ckvT, iq, ikA, ikB, iwT, hq, hk, hlf, hv, hgate) = _inproj(
            l, x, mod, p["wp"], p["wblk"], p["wckvT"], p["wiwT"], p["gkv"], p["gkvT"], p["llb"], p["l1m"])
        ya = _dsa(l, iq, iwT, qlat, ikA, ikB, ckv, ckvT, bn, p["wuvT"])
        yb = _hgrn(l, hq, hk, hlf, hv, hgate, gn)
        x1, u2, gates, rank, gates_t = _outproj(l, ya, yb, x, mod, p["wo"], ln1g, ln1b, p["wrT"], p["rbias"], alpha)
        u2f, x1f = u2.reshape(B * S, D), x1.reshape(B * S, D)

        def moe_sparse(l=l, mod=mod, u2f=u2f, x1f=x1f, rank=rank, gates_t=gates_t):
            ys = _experts(l, _dispatch(u2f, rank), p["wgu"], p["wd"])
            return _combine(l, ys, rank, gates_t, u2f, x1f, mod, p["sgu"], p["sd"], ln2g, ln2b, alpha, S)

        def moe_dense(l=l, mod=mod, u2f=u2f, x1f=x1f, gates=gates):
            return _moe(l, u2f, gates.reshape(B * S, 2 * N_EXPERTS), x1f, mod,
                        p["wgu"], p["wd"], p["sgu"], p["sd"], ln2g, ln2b, alpha, S)

        x = lax.cond(jnp.any(rank >= MOE_CAP), moe_dense, moe_sparse).reshape(B, S, D)
    return x
```

```python
import functools
import math

import numpy as np
import jax
import jax.numpy as jnp
from jax import lax
from jax.experimental import pallas as pl
from jax.experimental.pallas import tpu as pltpu

F32 = jnp.float32
BF16 = jnp.bfloat16
I32 = jnp.int32

D_MODEL = 1024
CHUNK = 64
A_HEADS = 8
A_HEAD_DIM = 64
A_WIDTH = A_HEADS * A_HEAD_DIM
KV_RANK = 128
IDX_HEADS = 8
IDX_DIM = 64
IDX_TOPK_MAX = 256
IDX_W_SCALE = (IDX_HEADS ** -0.5) * (IDX_DIM ** -0.5)
ATTN_SCALE = A_HEAD_DIM ** -0.5
LOG2E = math.log2(math.e)
KV_EXT = KV_RANK + 16
NUM_BUCKETS = 32
MAX_DISTANCE = 128
B_HEADS = 4
B_KEY_DIM = 128
B_VAL_DIM = 128
B_WIDTH = B_HEADS * B_VAL_DIM
B_FDIM = B_HEADS * B_KEY_DIM
N_EXPERTS = 64
TOP_K = 8
N_GROUPS = 8
TOPK_GROUPS = 4
EXPERT_DIM = 256
SHARED_DIM = 256
ROUTED_SCALE = 2.5
LN_EPS = 1e-5
RMS_EPS = 1e-6

LANES = 128
SUBLANES = 8
VMEM_LIMIT_BYTES = 56 * 1024 * 1024

INT_MIN = -(2 ** 31)
NEG_INF = float("-inf")

TM_PROJ = 512
TQ = 128
UNIT = 512
NEAR = 2 * TQ
COUNT_ACCS = 8
PLANE_ROWS = 32 * SUBLANES
TM_MOE = 1024
MOE_EXPERTS_PER_STEP = 4
MOE_GROUP = 256
MOE_CAP = 80
MOE_EXPERT_ROWS = 2048

_C_AQ, _C_CKV, _C_IQ, _C_IKA, _C_IKB, _C_HQ, _C_HF, _C_HG, _C_HI, _C_END = (
    0, 512, 640, 1152, 1280, 1408, 1920, 2432, 2944, 3456)


def _silu(v):
    return v * (1.0 / (1.0 + jnp.exp(-v)))


def _nt_dot(a, b):
    return lax.dot_general(a, b, (((1,), (1,)), ((), ())), preferred_element_type=F32)


def _cparams(sem):
    return pltpu.CompilerParams(dimension_semantics=sem, vmem_limit_bytes=VMEM_LIMIT_BYTES)


def _adaln_kernel(c_ref, w_ref, b_ref, o_ref):
    cond = _silu(c_ref[...])
    o_ref[0] = jnp.dot(cond.astype(BF16), w_ref[0].astype(BF16), preferred_element_type=F32) + b_ref[0]


def _adaln(c, w_ada, b_ada):
    L, D, D6 = w_ada.shape
    B = c.shape[0]
    nb = D6 // D
    return pl.pallas_call(
        _adaln_kernel,
        grid=(L, nb),
        in_specs=[
            pl.BlockSpec((B, D), lambda l, j: (0, 0)),
            pl.BlockSpec((1, D, D), lambda l, j: (l, 0, j)),
            pl.BlockSpec((1, 1, D), lambda l, j: (l, 0, j)),
        ],
        out_specs=pl.BlockSpec((1, B, D), lambda l, j: (l, 0, j)),
        out_shape=jax.ShapeDtypeStruct((L, B, D6), F32),
        compiler_params=_cparams(("arbitrary", "arbitrary")),
        name="adaln_mod",
    )(c, w_ada, b_ada.reshape(L, 1, D6))


_T5_NB = NUM_BUCKETS // 2
_T5_EXACT = _T5_NB // 2
_T5_THRESHOLDS = tuple(
    int(math.ceil(_T5_EXACT * (MAX_DISTANCE / _T5_EXACT) ** (j / (_T5_NB - _T5_EXACT)) - 1e-9))
    for j in range(1, _T5_NB - _T5_EXACT))
FAR_BUCKET = _T5_NB - 1
assert _T5_THRESHOLDS[-1] <= TQ, "keys further than one query block behind must share the far bucket"


def _bias_kernel(rb_ref, o_ref):
    kr = lax.broadcasted_iota(I32, (NEAR + TQ, TQ), 0)
    ql = lax.broadcasted_iota(I32, (NEAR + TQ, TQ), 1)
    rel = kr - TQ - ql
    n = jnp.abs(rel)
    large = jnp.full(rel.shape, _T5_EXACT, I32)
    for t in _T5_THRESHOLDS:
        large = large + (n >= t).astype(I32)
    bucket = jnp.where(rel > 0, _T5_NB, 0) + jnp.where(n < _T5_EXACT, n, large)
    for h in range(A_HEADS):
        acc = jnp.zeros(rel.shape, F32)
        for bk in range(NUM_BUCKETS):
            acc = jnp.where(bucket == bk, rb_ref[bk, h], acc)
        o_ref[h] = (acc - rb_ref[FAR_BUCKET, h]) * LOG2E


def _bias_tile(rel_bias):
    return pl.pallas_call(
        _bias_kernel,
        in_specs=[pl.BlockSpec(memory_space=pltpu.SMEM)],
        out_specs=pl.BlockSpec(memory_space=pltpu.VMEM),
        out_shape=jax.ShapeDtypeStruct((A_HEADS, NEAR + TQ, TQ), F32),
        name="rel_bias_tile",
    )(rel_bias)


def _inproj_kernel(x_ref, mod_ref, wp_ref, wblk_ref, wckvT_ref, wiwT_ref, gkv_ref, gkvT_ref, llb_ref, l1m_ref,
                   qlat_ref, ckv_ref, ckvT_ref, iq_ref, ikA_ref, ikB_ref, iwT_ref,
                   hq_ref, hk_ref, hlf_ref, hv_ref, hgate_ref):
    x = x_ref[0]
    sh1 = mod_ref[0, 0:1, :]
    sc1 = mod_ref[0, 1:2, :]
    u = (x * (1.0 + sc1) + sh1).astype(BF16)
    z = jnp.dot(u, wp_ref[0], preferred_element_type=F32)

    def proj(lo, hi):
        return z[:, lo:hi]

    ql = jnp.dot(proj(_C_AQ, _C_CKV).astype(BF16), wblk_ref[0], preferred_element_type=F32)
    for h in range(A_HEADS):
        qlat_ref[0, h] = ql[:, h * KV_RANK:(h + 1) * KV_RANK].astype(BF16)

    zc = proj(_C_CKV, _C_IQ)
    inv = lax.rsqrt(jnp.mean(zc * zc, axis=-1, keepdims=True) + RMS_EPS)
    ckv_ref[0] = (zc * inv * gkv_ref[0]).astype(BF16)
    zt = _nt_dot(wckvT_ref[0], u)
    inv_t = lax.rsqrt(jnp.mean(zt * zt, axis=0, keepdims=True) + RMS_EPS)
    ckvT_ref[0, 0:KV_RANK, :] = (zt * inv_t * gkvT_ref[0]).astype(BF16)
    ckvT_ref[0, KV_RANK:KV_EXT, :] = jnp.ones((KV_EXT - KV_RANK, zt.shape[1]), BF16)

    ziq = proj(_C_IQ, _C_IKA)
    for p in range(IDX_HEADS // 2):
        iq_ref[0, p] = ziq[:, p * LANES:(p + 1) * LANES].astype(BF16)
    ikA_ref[0] = proj(_C_IKA, _C_IKB).astype(BF16)
    ikB_ref[0] = proj(_C_IKB, _C_HQ).astype(BF16)
    iwT_ref[0] = _nt_dot(wiwT_ref[0], u) * IDX_W_SCALE

    hq_ref[0] = _silu(proj(_C_HQ, _C_HF))
    zf = proj(_C_HF, _C_HG)
    log_sig = jnp.minimum(zf, 0.0) - jnp.log1p(jnp.exp(-jnp.abs(zf)))
    a = llb_ref[0]
    c = l1m_ref[0] + log_sig
    logf = jnp.maximum(a, c) + jnp.log1p(jnp.exp(-jnp.abs(a - c)))
    hlf_ref[0] = logf
    hk_ref[0] = 1.0 - jnp.exp(logf)
    hgate_ref[0] = _silu(proj(_C_HG, _C_HI))
    hv_ref[0] = proj(_C_HI, _C_END).astype(BF16)


def _inproj(l, x, mod, wp, wblk, wckvT, wiwT, gkv, gkvT, llb, l1m):
    B, S, D = x.shape
    tm = TM_PROJ
    grid = (B, S // tm)
    lw3 = lambda b, i: (l, 0, 0)
    tok = lambda b, i: (b, i, 0)
    tokT = lambda b, i: (b, 0, i)
    hd4 = lambda b, i: (b, 0, i, 0)
    outs = [
        (jax.ShapeDtypeStruct((B, A_HEADS, S, KV_RANK), BF16), pl.BlockSpec((1, A_HEADS, tm, KV_RANK), hd4)),
        (jax.ShapeDtypeStruct((B, S, KV_RANK), BF16), pl.BlockSpec((1, tm, KV_RANK), tok)),
        (jax.ShapeDtypeStruct((B, KV_EXT, S), BF16), pl.BlockSpec((1, KV_EXT, tm), tokT)),
        (jax.ShapeDtypeStruct((B, IDX_HEADS // 2, S, LANES), BF16), pl.BlockSpec((1, IDX_HEADS // 2, tm, LANES), hd4)),
        (jax.ShapeDtypeStruct((B, S, LANES), BF16), pl.BlockSpec((1, tm, LANES), tok)),
        (jax.ShapeDtypeStruct((B, S, LANES), BF16), pl.BlockSpec((1, tm, LANES), tok)),
        (jax.ShapeDtypeStruct((B, IDX_HEADS, S), F32), pl.BlockSpec((1, IDX_HEADS, tm), tokT)),
        (jax.ShapeDtypeStruct((B, S, B_FDIM), F32), pl.BlockSpec((1, tm, B_FDIM), tok)),
        (jax.ShapeDtypeStruct((B, S, B_FDIM), F32), pl.BlockSpec((1, tm, B_FDIM), tok)),
        (jax.ShapeDtypeStruct((B, S, B_FDIM), F32), pl.BlockSpec((1, tm, B_FDIM), tok)),
        (jax.ShapeDtypeStruct((B, S, B_WIDTH), BF16), pl.BlockSpec((1, tm, B_WIDTH), tok)),
        (jax.ShapeDtypeStruct((B, S, B_WIDTH), F32), pl.BlockSpec((1, tm, B_WIDTH), tok)),
    ]
    return pl.pallas_call(
        _inproj_kernel,
        grid=grid,
        in_specs=[
            pl.BlockSpec((1, tm, D), tok),
            pl.BlockSpec((1, 6, D), lambda b, i: (b, 0, 0)),
            pl.BlockSpec((1, D, _C_END), lw3),
            pl.BlockSpec((1, A_WIDTH, A_HEADS * KV_RANK), lw3),
            pl.BlockSpec((1, KV_RANK, D), lw3),
            pl.BlockSpec((1, IDX_HEADS, D), lw3),
            pl.BlockSpec((1, 1, KV_RANK), lw3),
            pl.BlockSpec((1, KV_RANK, tm), lw3),
            pl.BlockSpec((1, 1, B_FDIM), lw3),
            pl.BlockSpec((1, 1, B_FDIM), lw3),
        ],
        out_specs=[o[1] for o in outs],
        out_shape=[o[0] for o in outs],
        compiler_params=_cparams(("arbitrary", "arbitrary")),
        name="inproj",
    )(x, mod, wp, wblk, wckvT, wiwT, gkv, gkvT, llb, l1m)


def _dsa_kernel(iq_ref, iwT_ref, qlat_ref, ikA_ref, ikB_ref, ckv_ref, ckvT_ref, bn_ref, wuvT_ref, out_ref,
                sc_ref, plane_ref, madd_ref, maddn_ref, la_ref, lb_ref, pma_ref, pmb_ref, ot_ref, yaT_ref,
                *, k_sel, n_idx_bits):
    j = pl.program_id(1)
    q0 = j * TQ
    nk = q0 + TQ
    nunit = (nk + UNIT - 1) // UNIT
    near0 = pl.multiple_of(jnp.maximum(nk - NEAR, 0), TQ)
    bn_row0 = pl.multiple_of(jnp.where(j == 0, TQ, 0), TQ)
    lane = lax.broadcasted_iota(I32, (1, TQ), 1)
    limit = (((q0 + lane) >> 6) + 1) << 6
    row_iota = lax.broadcasted_iota(I32, (UNIT, TQ), 0)

    def unit_rows(u):
        return pl.ds(pl.multiple_of(u * UNIT, UNIT), UNIT)

    iqs = iq_ref[0].reshape(IDX_HEADS // 2 * TQ, LANES)
    iw = iwT_ref[0]

    last_unit = sc_ref.shape[0] // UNIT - 1
    half = IDX_HEADS // 2 * TQ

    def issue_scores(u, buf_ref):
        rows = unit_rows(jnp.minimum(u, last_unit))
        buf_ref[:, 0:half] = _nt_dot(ikA_ref[0, rows, :], iqs)
        buf_ref[:, half:2 * half] = _nt_dot(ikB_ref[0, rows, :], iqs)

    def reduce_scores(u, buf_ref):
        acc = jnp.zeros((UNIT, TQ), F32)
        for p in range(IDX_HEADS // 2):
            acc = acc + iw[2 * p:2 * p + 1, :] * jnp.maximum(buf_ref[:, p * TQ:(p + 1) * TQ], 0.0)
            acc = acc + iw[2 * p + 1:2 * p + 2, :] * jnp.maximum(buf_ref[:, half + p * TQ:half + (p + 1) * TQ], 0.0)
        bits = lax.bitcast_convert_type(acc, I32)
        key = bits ^ ((bits >> 31) & 0x7FFFFFFF)
        sc_ref[unit_rows(u), :] = jnp.where(row_iota + u * UNIT < limit, key, INT_MIN)

    issue_scores(0, la_ref)

    def score_pair(i, carry):
        issue_scores(2 * i + 1, lb_ref)
        reduce_scores(2 * i, la_ref)
        issue_scores(2 * i + 2, la_ref)
        reduce_scores(2 * i + 1, lb_ref)
        return carry

    lax.fori_loop(0, nunit // 2, score_pair, 0)

    @pl.when(nunit % 2 == 1)
    def _():
        reduce_scores(nunit - 1, la_ref)

    ngroups = (nk + PLANE_ROWS - 1) // PLANE_ROWS

    def plane_group(g, carry):
        rows = pl.ds(pl.multiple_of(g * PLANE_ROWS, PLANE_ROWS), PLANE_ROWS)
        words = (sc_ref[rows, :] ^ INT_MIN).reshape(32, SUBLANES, TQ)
        w = [words[i] for i in range(32)]
        j, m = 16, 0x0000FFFF
        while j:
            mask = np.int32(np.uint32(m).view(np.int32))
            k = 0
            while k < 32:
                t = (w[k] ^ lax.shift_right_logical(w[k + j], jnp.full(w[k].shape, j, I32))) & mask
                w[k] = w[k] ^ t
                w[k + j] = w[k + j] ^ (t << j)
                k = (k + j + 1) & ~j
            j >>= 1
            m = (m ^ (m << j)) & 0xFFFFFFFF
        for i in range(32):
            plane_ref[i, pl.ds(g * SUBLANES, SUBLANES), :] = w[i]
        return carry

    lax.fori_loop(0, ngroups, plane_group, 0)

    n_words = sc_ref.shape[0] // PLANE_ROWS * SUBLANES
    group_of_word = lax.broadcasted_iota(I32, (n_words, TQ), 0) // SUBLANES

    def bit_step(i, carry):
        alive, above, t_off, c_ge = carry
        hit = alive & plane_ref[i]
        cnt = above + jnp.sum(lax.population_count(hit), axis=0, keepdims=True)
        ok = cnt >= k_sel
        alive = jnp.where(ok, hit, alive ^ hit)
        above = jnp.where(ok, above, cnt)
        t_off = jnp.where(ok, t_off | (jnp.int32(1) << (31 - i)), t_off)
        return alive, above, t_off, jnp.where(ok, cnt, c_ge)

    zero_row = jnp.zeros((1, TQ), I32)
    _, _, t_off, c_ge = lax.fori_loop(
        0, 32, bit_step,
        (jnp.where(group_of_word < ngroups, jnp.int32(-1), jnp.int32(0)), zero_row, zero_row, zero_row))
    thr = jnp.maximum(t_off ^ INT_MIN, INT_MIN + 1)
    straddle = (c_ge > k_sel).astype(I32)

    def count_where(pred):
        def body(u, acc):
            hit = pred(sc_ref[unit_rows(u), :], u * UNIT).reshape(-1, COUNT_ACCS * SUBLANES, TQ)
            for s in range(hit.shape[0]):
                acc = jnp.where(hit[s], acc + 1, acc)
            return acc
        acc = lax.fori_loop(0, nunit, body, jnp.zeros((COUNT_ACCS * SUBLANES, TQ), I32))
        return jnp.sum(acc, axis=0, keepdims=True)

    def tie_bound():
        c_gt = count_where(lambda blk, r0: blk > thr)
        need = k_sel - c_gt

        def tie_body(i, j0):
            cand = j0 | (jnp.int32(1) << (n_idx_bits - 1 - i))
            cnt = count_where(lambda blk, r0: jnp.where(blk == thr, row_iota + r0, cand) < cand)
            return jnp.where(cnt < need, cand, j0)

        j0 = lax.fori_loop(0, n_idx_bits, tie_body, jnp.zeros((1, TQ), I32))
        return jnp.where(straddle > 0, j0 + 1, jnp.int32(2 ** n_idx_bits))

    jstar = lax.cond(jnp.max(straddle) > 0, tie_bound, lambda: jnp.full((1, TQ), 2 ** n_idx_bits, I32))

    def madd_unit(u, carry):
        rows = unit_rows(u)
        key = sc_ref[rows, :]
        tie_keep = jnp.where(row_iota + u * UNIT < jstar, 0.0, NEG_INF)
        madd_ref[rows, :] = jnp.where(key > thr, 0.0, jnp.where(key == thr, tie_keep, NEG_INF))
        return carry

    lax.fori_loop(0, nunit, madd_unit, 0)
    maddn_ref[...] = madd_ref[pl.ds(near0, NEAR), :]
    madd_ref[pl.ds(near0, NEAR), :] = jnp.full((NEAR, TQ), NEG_INF, F32)

    qall = qlat_ref[0].reshape(A_HEADS * TQ, KV_RANK)

    def col_max(v):
        return jnp.max(v.reshape(v.shape[0] // SUBLANES, SUBLANES, A_HEADS * TQ), axis=0)

    def fold(xl, part_max, ckv_t, m_old):
        m_new = jnp.maximum(m_old, jnp.max(part_max, axis=0, keepdims=True))
        m_use = jnp.where(m_new == NEG_INF, 0.0, m_new)
        p = jnp.exp2(xl - m_use).astype(BF16)
        ot_ref[...] = ot_ref[...] * jnp.exp2(m_old - m_use) + jnp.dot(ckv_t, p, preferred_element_type=F32)
        return m_new

    ot_ref[...] = jnp.zeros(ot_ref.shape, F32)
    near_rows = pl.ds(near0, NEAR)
    xn = _nt_dot(ckv_ref[0, near_rows, :], qall) + jnp.concatenate([maddn_ref[...]] * A_HEADS, axis=1)
    xn = xn + jnp.concatenate([bn_ref[h, pl.ds(bn_row0, NEAR), :] for h in range(A_HEADS)], axis=1)
    m_run = fold(xn, col_max(xn), ckvT_ref[0, :, near_rows], jnp.full((1, A_HEADS * TQ), NEG_INF, F32))


    def issue_logits(u, buf_ref, pm_ref):
        rows = unit_rows(jnp.minimum(u, last_unit))
        xl = _nt_dot(ckv_ref[0, rows, :], qall) + jnp.concatenate([madd_ref[rows, :]] * A_HEADS, axis=1)
        buf_ref[...] = xl
        pm_ref[...] = col_max(xl)

    def consume_logits(u, buf_ref, pm_ref, m_old):
        return fold(buf_ref[...], pm_ref[...], ckvT_ref[0, :, unit_rows(u)], m_old)

    issue_logits(0, la_ref, pma_ref)

    def pair_step(i, m_old):
        issue_logits(2 * i + 1, lb_ref, pmb_ref)
        m_mid = consume_logits(2 * i, la_ref, pma_ref, m_old)
        issue_logits(2 * i + 2, la_ref, pma_ref)
        return consume_logits(2 * i + 1, lb_ref, pmb_ref, m_mid)

    m_run = lax.fori_loop(0, nunit // 2, pair_step, m_run)

    @pl.when(nunit % 2 == 1)
    def _():
        consume_logits(nunit - 1, la_ref, pma_ref, m_run)
    o_t = (ot_ref[0:KV_RANK, :] * (1.0 / ot_ref[KV_RANK:KV_RANK + 1, :])).astype(BF16)
    for h in range(A_HEADS):
        yaT_ref[h * A_HEAD_DIM:(h + 1) * A_HEAD_DIM, :] = jnp.dot(
            wuvT_ref[0, h], o_t[:, h * TQ:(h + 1) * TQ], preferred_element_type=F32)

    out_ref[0] = yaT_ref[...].T.astype(BF16)


def _dsa(l, iq, iwT, qlat, ikA, ikB, ckv, ckvT, bn, wuvT):
    B, S = ckv.shape[0], ckv.shape[1]
    assert S % (2 * UNIT) == 0 and UNIT % TQ == 0 and TQ % CHUNK == 0 and CHUNK == 64 and NEAR <= UNIT
    k_sel = min(IDX_TOPK_MAX, S // 4)
    n_idx_bits = int(math.log2(S))
    assert 2 ** n_idx_bits == S
    grid = (B, S // TQ)
    blk = lambda b, i: (b, 0, i, 0)
    full = lambda b, i: (b, 0, 0)
    kern = functools.partial(_dsa_kernel, k_sel=k_sel, n_idx_bits=n_idx_bits)
    return pl.pallas_call(
        kern,
        grid=grid,
        in_specs=[
            pl.BlockSpec((1, IDX_HEADS // 2, TQ, LANES), blk),
            pl.BlockSpec((1, IDX_HEADS, TQ), lambda b, i: (b, 0, i)),
            pl.BlockSpec((1, A_HEADS, TQ, KV_RANK), blk),
            pl.BlockSpec((1, S, LANES), full),
            pl.BlockSpec((1, S, LANES), full),
            pl.BlockSpec((1, S, KV_RANK), full),
            pl.BlockSpec((1, KV_EXT, S), full),
            pl.BlockSpec((A_HEADS, NEAR + TQ, TQ), lambda b, i: (0, 0, 0)),
            pl.BlockSpec((1, A_HEADS, A_HEAD_DIM, KV_RANK), lambda b, i: (l, 0, 0, 0)),
        ],
        out_specs=pl.BlockSpec((1, TQ, A_WIDTH), lambda b, i: (b, i, 0)),
        out_shape=jax.ShapeDtypeStruct((B, S, A_WIDTH), BF16),
        scratch_shapes=[
            pltpu.VMEM((S, TQ), I32),
            pltpu.VMEM((32, S // PLANE_ROWS * SUBLANES, TQ), I32),
            pltpu.VMEM((S, TQ), F32),
            pltpu.VMEM((NEAR, TQ), F32),
            pltpu.VMEM((UNIT, A_HEADS * TQ), F32),
            pltpu.VMEM((UNIT, A_HEADS * TQ), F32),
            pltpu.VMEM((SUBLANES, A_HEADS * TQ), F32),
            pltpu.VMEM((SUBLANES, A_HEADS * TQ), F32),
            pltpu.VMEM((KV_EXT, A_HEADS * TQ), F32),
            pltpu.VMEM((A_WIDTH, TQ), F32),
        ],
        compiler_params=_cparams(("arbitrary", "arbitrary")),
        name="dsa_attention",
    )(iq, iwT, qlat, ikA, ikB, ckv, ckvT, bn, wuvT)


def _hgrn_constants():
    c = CHUNK
    r = np.arange(c)[:, None]
    jj = np.arange(c)[None, :]
    mats = [(jj <= r), (jj > r)]
    masks = [np.eye(c, dtype=bool)]
    m = c // 2
    while m >= 1:
        start = (r // (2 * m)) * (2 * m)
        bd = start + m - 1
        upper = r > bd
        mats.append(np.where(upper, (jj > bd) & (jj <= r), (jj > r) & (jj <= bd)))
        same_parent = (r // (2 * m)) == (jj // (2 * m))
        masks.append(same_parent & upper & (jj <= (jj // (2 * m)) * (2 * m) + m - 1))
        m //= 2
    m_all = np.concatenate(mats, axis=0).astype(np.float32)
    total = np.zeros((c, c), np.int32)
    for mk in masks:
        total += mk
    assert (total == np.tril(np.ones((c, c), np.int32))).all()
    return np.concatenate([m_all] * 3, axis=1), np.stack(masks).astype(np.float32)


_HGRN_M3, _HGRN_MASKS = _hgrn_constants()
_HGRN_LEVELS = _HGRN_MASKS.shape[0] - 1
HGRN_STEP_CHUNKS = 2
HGRN_STEP_BATCH = 4


def _hgrn_kernel(q_ref, k_ref, lf_ref, v_ref, gate_ref, m3_ref, mask_ref, gn_ref, out_ref, st_ref):
    @pl.when(pl.program_id(1) == 0)
    def _():
        st_ref[...] = jnp.zeros(st_ref.shape, F32)

    c = CHUNK
    intra = {}
    for ci in range(HGRN_STEP_CHUNKS):
        rows = slice(ci * c, (ci + 1) * c)
        for bi in range(HGRN_STEP_BATCH):
            g = lf_ref[bi, rows, :]
            g_hi = g.astype(BF16)
            r1 = g - g_hi.astype(F32)
            g_mid = r1.astype(BF16)
            g_lo = (r1 - g_mid.astype(F32)).astype(BF16)
            sums = jnp.dot(m3_ref[...], jnp.concatenate([g_hi, g_mid, g_lo], axis=0), preferred_element_type=F32)
            e_all = jnp.exp(sums)
            for h in range(B_HEADS):
                cols = slice(h * B_KEY_DIM, (h + 1) * B_KEY_DIM)
                qh = q_ref[bi, rows, cols]
                kh = k_ref[bi, rows, cols]
                att = mask_ref[0] * _nt_dot(qh.astype(BF16), kh.astype(BF16))
                for lv in range(_HGRN_LEVELS):
                    e_l = e_all[(2 + lv) * c:(3 + lv) * c, cols]
                    att = att + mask_ref[lv + 1] * _nt_dot((qh * e_l).astype(BF16), (kh * e_l).astype(BF16))
                e_b = e_all[0:c, cols]
                intra[bi, ci, h] = (jnp.dot(att.astype(BF16), v_ref[bi, rows, cols], preferred_element_type=F32),
                                    (qh * e_b).astype(BF16), (kh * e_all[c:2 * c, cols]).astype(BF16),
                                    e_b[c - 1:c, :])
    for ci in range(HGRN_STEP_CHUNKS):
        rows = slice(ci * c, (ci + 1) * c)
        for bi in range(HGRN_STEP_BATCH):
            for h in range(B_HEADS):
                cols = slice(h * B_KEY_DIM, (h + 1) * B_KEY_DIM)
                o_intra, q_dec, k_rem, decay_all = intra[bi, ci, h]
                st = st_ref[bi, h]
                o = o_intra + _nt_dot(q_dec, st.astype(BF16))
                upd = lax.dot_general(v_ref[bi, rows, cols], k_rem, (((0,), (0,)), ((), ())),
                                      preferred_element_type=F32)
                st_ref[bi, h] = st * decay_all + upd
                o = o * lax.rsqrt(jnp.mean(o * o, axis=-1, keepdims=True) + RMS_EPS) * gn_ref[0]
                out_ref[bi, rows, cols] = (o * gate_ref[bi, rows, cols]).astype(BF16)


def _hgrn(l, hq, hk, hlf, hv, hgate, gnorm):
    B, S, W = hq.shape
    ts = CHUNK * HGRN_STEP_CHUNKS
    nb = HGRN_STEP_BATCH
    assert B % nb == 0
    tok = lambda b, i: (b, i, 0)
    return pl.pallas_call(
        _hgrn_kernel,
        grid=(B // nb, S // ts),
        in_specs=[
            pl.BlockSpec((nb, ts, W), tok),
            pl.BlockSpec((nb, ts, W), tok),
            pl.BlockSpec((nb, ts, W), tok),
            pl.BlockSpec((nb, ts, W), tok),
            pl.BlockSpec((nb, ts, W), tok),
            pl.BlockSpec(_HGRN_M3.shape, lambda b, i: (0, 0)),
            pl.BlockSpec(_HGRN_MASKS.shape, lambda b, i: (0, 0, 0)),
            pl.BlockSpec((1, 1, B_VAL_DIM), lambda b, i: (l, 0, 0)),
        ],
        out_specs=pl.BlockSpec((nb, ts, W), tok),
        out_shape=jax.ShapeDtypeStruct((B, S, W), BF16),
        scratch_shapes=[pltpu.VMEM((nb, B_HEADS, B_VAL_DIM, B_KEY_DIM), F32)],
        compiler_params=_cparams(("arbitrary", "arbitrary")),
        name="hgrn2",
    )(hq, hk, hlf, hv, hgate, jnp.asarray(_HGRN_M3, BF16), jnp.asarray(_HGRN_MASKS), gnorm)


def _layernorm(v, g, b):
    mu = jnp.mean(v, axis=-1, keepdims=True)
    d = v - mu
    var = jnp.mean(d * d, axis=-1, keepdims=True)
    return d * lax.rsqrt(var + LN_EPS) * g + b


def _first_argmax(v, idx, axes, big):
    mx = v
    for ax in axes:
        mx = jnp.max(mx, axis=ax, keepdims=True)
    pos = jnp.where(v == mx, idx, big)
    for ax in axes:
        pos = jnp.min(pos, axis=ax, keepdims=True)
    return mx, pos


def _outproj_kernel(ya_ref, yb_ref, x_ref, mod_ref, wo_ref, lng_ref, lnb_ref, wrT_ref, rbias_ref, tri_ref,
                    x1_ref, u2_ref, gates_ref, rank_ref, gatesT_ref, *, alpha):
    y = jnp.dot(ya_ref[0], wo_ref[0, 0:A_WIDTH, :], preferred_element_type=F32)
    y = y + jnp.dot(yb_ref[0], wo_ref[0, A_WIDTH:, :], preferred_element_type=F32)
    g1 = mod_ref[0, 2:3, :]
    x1 = _layernorm(alpha * x_ref[0] + (1.0 + g1) * y, lng_ref[0], lnb_ref[0])
    x1_ref[0] = x1
    u2 = (x1 * (1.0 + mod_ref[0, 4:5, :]) + mod_ref[0, 3:4, :]).astype(BF16)
    u2_ref[0] = u2

    tm = u2.shape[0]
    gsz = N_EXPERTS // N_GROUPS
    scores = 1.0 / (1.0 + jnp.exp(-_nt_dot(wrT_ref[0], u2)))
    sel = (scores + rbias_ref[0]).reshape(N_GROUPS, gsz, tm)
    scores = scores.reshape(N_GROUPS, gsz, tm)
    i_m = lax.broadcasted_iota(I32, (N_GROUPS, gsz, tm), 1)
    i_g = lax.broadcasted_iota(I32, (N_GROUPS, 1, tm), 0)
    i_e = lax.broadcasted_iota(I32, (N_GROUPS, gsz, tm), 0) * gsz + i_m
    m1, p1 = _first_argmax(sel, i_m, (1,), gsz)
    m2 = jnp.max(jnp.where(i_m == p1, NEG_INF, sel), axis=1, keepdims=True)
    gs = m1 + m2
    gmask = jnp.zeros(gs.shape, F32)
    for _ in range(TOPK_GROUPS):
        _, pg = _first_argmax(gs, i_g, (0,), N_GROUPS)
        hit = i_g == pg
        gmask = jnp.where(hit, 1.0, gmask)
        gs = jnp.where(hit, NEG_INF, gs)
    cand = jnp.where(jnp.broadcast_to(gmask, sel.shape) > 0.0, sel, NEG_INF)
    w = jnp.zeros(sel.shape, F32)
    chosen = jnp.zeros(sel.shape, F32)
    for _ in range(TOP_K):
        _, pe = _first_argmax(cand, i_e, (1, 0), N_EXPERTS)
        hit = i_e == pe
        w = jnp.where(hit, scores, w)
        chosen = jnp.where(hit, 1.0, chosen)
        cand = jnp.where(hit, NEG_INF, cand)
    wsum = jnp.sum(jnp.sum(w, axis=1, keepdims=True), axis=0, keepdims=True)
    gates = (w / wsum * ROUTED_SCALE).reshape(N_EXPERTS, tm)
    g_hi = gates.astype(BF16).astype(F32)
    g_lo = (gates - g_hi).astype(BF16).astype(F32)
    gates_ref[0] = jnp.concatenate([g_hi, g_lo], axis=0).T.astype(BF16)

    chosen2 = chosen.reshape(N_EXPERTS, tm)
    rank = jnp.concatenate(
        [jnp.dot(chosen2[:, g * MOE_GROUP:(g + 1) * MOE_GROUP].astype(BF16), tri_ref[...],
                 preferred_element_type=F32) for g in range(tm // MOE_GROUP)], axis=1)
    rank_ref[0] = jnp.where(chosen2 > 0.0, rank, -1.0).astype(I32)
    gatesT_ref[0] = gates


def _outproj(l, ya, yb, x, mod, wo, ln_g, ln_b, wrT, rbias, alpha):
    B, S, D = x.shape
    tm = TM_PROJ
    tok = lambda b, i: (b, i, 0)
    lw3 = lambda b, i: (l, 0, 0)
    return pl.pallas_call(
        functools.partial(_outproj_kernel, alpha=alpha),
        grid=(B, S // tm),
        in_specs=[
            pl.BlockSpec((1, tm, A_WIDTH), tok),
            pl.BlockSpec((1, tm, B_WIDTH), tok),
            pl.BlockSpec((1, tm, D), tok),
            pl.BlockSpec((1, 6, D), lambda b, i: (b, 0, 0)),
            pl.BlockSpec((1, D, D), lw3),
            pl.BlockSpec((1, 1, D), lw3),
            pl.BlockSpec((1, 1, D), lw3),
            pl.BlockSpec((1, N_EXPERTS, D), lw3),
            pl.BlockSpec((1, N_EXPERTS, tm), lw3),
            pl.BlockSpec((MOE_GROUP, MOE_GROUP), lambda b, i: (0, 0)),
        ],
        out_specs=[pl.BlockSpec((1, tm, D), tok), pl.BlockSpec((1, tm, D), tok),
                   pl.BlockSpec((1, tm, 2 * N_EXPERTS), tok),
                   pl.BlockSpec((1, N_EXPERTS, tm), lambda b, i: (b, 0, i)),
                   pl.BlockSpec((1, N_EXPERTS, tm), lambda b, i: (b, 0, i))],
        out_shape=[jax.ShapeDtypeStruct((B, S, D), F32), jax.ShapeDtypeStruct((B, S, D), BF16),
                   jax.ShapeDtypeStruct((B, S, 2 * N_EXPERTS), BF16),
                   jax.ShapeDtypeStruct((B, N_EXPERTS, S), I32),
                   jax.ShapeDtypeStruct((B, N_EXPERTS, S), F32)],
        compiler_params=_cparams(("arbitrary", "arbitrary")),
        name="outproj_router",
    )(ya, yb, x, mod, wo, ln_g, ln_b, wrT, rbias,
      jnp.asarray(np.triu(np.ones((MOE_GROUP, MOE_GROUP), np.float32), 1), BF16))


MOE_CHUNK_EXPERTS = 8


def _slot_onehot(rank_rows, values):
    row = lax.broadcasted_iota(I32, (MOE_CAP, rank_rows.shape[1]), 0)
    return jnp.concatenate(
        [jnp.where(row == rank_rows[e:e + 1, :], values[e:e + 1, :], 0.0) for e in range(rank_rows.shape[0])], axis=0)


def _dispatch_kernel(u_ref, rank_ref, x_ref):
    u = u_ref[...]
    ones = jnp.ones((MOE_CHUNK_EXPERTS, MOE_GROUP), F32)
    for c in range(N_EXPERTS // MOE_CHUNK_EXPERTS):
        es = slice(c * MOE_CHUNK_EXPERTS, (c + 1) * MOE_CHUNK_EXPERTS)
        onehot = _slot_onehot(rank_ref[0, es, :], ones)
        xs = jnp.dot(onehot.astype(BF16), u, preferred_element_type=F32).astype(BF16)
        x_ref[es] = xs.reshape(MOE_CHUNK_EXPERTS, MOE_CAP, -1)


def _dispatch(u2, rank):
    T, D = u2.shape
    ng = T // MOE_GROUP
    gps = rank.shape[-1] // MOE_GROUP
    return pl.pallas_call(
        _dispatch_kernel,
        grid=(ng,),
        in_specs=[pl.BlockSpec((MOE_GROUP, D), lambda g: (g, 0)),
                  pl.BlockSpec((1, N_EXPERTS, MOE_GROUP), lambda g: (g // gps, 0, g % gps))],
        out_specs=pl.BlockSpec((N_EXPERTS, MOE_CAP, D), lambda g: (0, g, 0)),
        out_shape=jax.ShapeDtypeStruct((N_EXPERTS, ng * MOE_CAP, D), BF16),
        compiler_params=_cparams(("arbitrary",)),
        name="moe_dispatch",
    )(u2, rank)


def _expert_kernel(x_ref, wgu_ref, wd_ref, y_ref):
    hgu = jnp.dot(x_ref[0], wgu_ref[0, 0], preferred_element_type=F32)
    h = _silu(hgu[:, :EXPERT_DIM]) * hgu[:, EXPERT_DIM:]
    y_ref[0] = jnp.dot(h.astype(BF16), wd_ref[0, 0], preferred_element_type=F32).astype(BF16)


def _experts(l, xs, wgu, wd):
    E, R, D = xs.shape
    tr = min(R, MOE_EXPERT_ROWS)
    assert R % tr == 0
    return pl.pallas_call(
        _expert_kernel,
        grid=(E, R // tr),
        in_specs=[pl.BlockSpec((1, tr, D), lambda e, i: (e, i, 0)),
                  pl.BlockSpec((1, 1, D, 2 * EXPERT_DIM), lambda e, i: (l, e, 0, 0)),
                  pl.BlockSpec((1, 1, EXPERT_DIM, D), lambda e, i: (l, e, 0, 0))],
        out_specs=pl.BlockSpec((1, tr, D), lambda e, i: (e, i, 0)),
        out_shape=jax.ShapeDtypeStruct((E, R, D), BF16),
        compiler_params=_cparams(("arbitrary", "arbitrary")),
        name="moe_experts",
    )(xs, wgu, wd)


def _combine_kernel(y_ref, rank_ref, gates_ref, u_ref, x1_ref, mod_ref, sgu_ref, sd_ref, lng_ref, lnb_ref,
                    out_ref, *, alpha):
    hgu = jnp.dot(u_ref[...], sgu_ref[0], preferred_element_type=F32)
    hs = _silu(hgu[:, :SHARED_DIM]) * hgu[:, SHARED_DIM:]
    y = jnp.dot(hs.astype(BF16), sd_ref[0], preferred_element_type=F32)
    for c in range(N_EXPERTS // MOE_CHUNK_EXPERTS):
        es = slice(c * MOE_CHUNK_EXPERTS, (c + 1) * MOE_CHUNK_EXPERTS)
        pick = _slot_onehot(rank_ref[0, es, :], gates_ref[0, es, :])
        ys = y_ref[es].reshape(MOE_CHUNK_EXPERTS * MOE_CAP, -1)
        y = y + lax.dot_general(pick.astype(BF16), ys, (((0,), (0,)), ((), ())), preferred_element_type=F32)
    g2 = mod_ref[0, 5:6, :]
    out_ref[...] = _layernorm(alpha * x1_ref[...] + (1.0 + g2) * y, lng_ref[0], lnb_ref[0])


def _combine(l, ys, rank, gates_t, u2, x1, mod, sgu, sd, ln_g, ln_b, alpha, seq):
    T, D = u2.shape
    tok = lambda g: (g, 0)
    lw3 = lambda g: (l, 0, 0)
    gps = seq // MOE_GROUP
    per_group = lambda g: (g // gps, 0, g % gps)
    return pl.pallas_call(
        functools.partial(_combine_kernel, alpha=alpha),
        grid=(T // MOE_GROUP,),
        in_specs=[
            pl.BlockSpec((N_EXPERTS, MOE_CAP, D), lambda g: (0, g, 0)),
            pl.BlockSpec((1, N_EXPERTS, MOE_GROUP), per_group),
            pl.BlockSpec((1, N_EXPERTS, MOE_GROUP), per_group),
            pl.BlockSpec((MOE_GROUP, D), tok),
            pl.BlockSpec((MOE_GROUP, D), tok),
            pl.BlockSpec((1, 6, D), lambda g: ((g * MOE_GROUP) // seq, 0, 0)),
            pl.BlockSpec((1, D, 2 * SHARED_DIM), lw3),
            pl.BlockSpec((1, SHARED_DIM, D), lw3),
            pl.BlockSpec((1, 1, D), lw3),
            pl.BlockSpec((1, 1, D), lw3),
        ],
        out_specs=pl.BlockSpec((MOE_GROUP, D), tok),
        out_shape=jax.ShapeDtypeStruct((T, D), F32),
        compiler_params=_cparams(("arbitrary",)),
        name="moe_combine",
    )(ys, rank, gates_t, u2, x1, mod, sgu, sd, ln_g, ln_b)


def _moe_kernel(u_ref, gates_ref, x1_ref, mod_ref, wgu_ref, wd_ref, sgu_ref, sd_ref, lng_ref, lnb_ref,
                out_ref, acc_ref, *, alpha):
    s = pl.program_id(1)
    u = u_ref[...]

    def hidden(wgu):
        hgu = jnp.dot(u, wgu, preferred_element_type=F32)
        return _silu(hgu[:, :EXPERT_DIM]) * hgu[:, EXPERT_DIM:]

    @pl.when(s == 0)
    def _():
        acc_ref[...] = jnp.dot(hidden(sgu_ref[0]).astype(BF16), sd_ref[0], preferred_element_type=F32)

    rows = lax.broadcasted_iota(I32, (2 * N_EXPERTS, MOE_EXPERTS_PER_STEP * EXPERT_DIM), 0) & (N_EXPERTS - 1)
    cols = lax.broadcasted_iota(I32, (2 * N_EXPERTS, MOE_EXPERTS_PER_STEP * EXPERT_DIM), 1)
    onehot = jnp.where(rows == s * MOE_EXPERTS_PER_STEP + cols // EXPERT_DIM, 1.0, 0.0).astype(BF16)
    gate = jnp.dot(gates_ref[...], onehot, preferred_element_type=F32)
    h = jnp.concatenate(
        [(hidden(wgu_ref[0, k]) * gate[:, k * EXPERT_DIM:(k + 1) * EXPERT_DIM]).astype(BF16)
         for k in range(MOE_EXPERTS_PER_STEP)], axis=1)
    wd = wd_ref[0].reshape(MOE_EXPERTS_PER_STEP * EXPERT_DIM, wd_ref.shape[-1])
    acc_ref[...] += jnp.dot(h, wd, preferred_element_type=F32)

    @pl.when(s == pl.num_programs(1) - 1)
    def _():
        g2 = mod_ref[0, 5:6, :]
        out_ref[...] = _layernorm(alpha * x1_ref[...] + (1.0 + g2) * acc_ref[...], lng_ref[0], lnb_ref[0])


def _moe(l, u2, gates, x1, mod, wgu, wd, sgu, sd, ln_g, ln_b, alpha, seq):
    T, D = u2.shape
    tm = TM_MOE
    assert seq % tm == 0
    tok = lambda i, e: (i, 0)
    lw3 = lambda i, e: (l, 0, 0)
    return pl.pallas_call(
        functools.partial(_moe_kernel, alpha=alpha),
        grid=(T // tm, N_EXPERTS // MOE_EXPERTS_PER_STEP),
        in_specs=[
            pl.BlockSpec((tm, D), tok),
            pl.BlockSpec((tm, 2 * N_EXPERTS), tok),
            pl.BlockSpec((tm, D), tok),
            pl.BlockSpec((1, 6, D), lambda i, e: ((i * tm) // seq, 0, 0)),
            pl.BlockSpec((1, MOE_EXPERTS_PER_STEP, D, 2 * EXPERT_DIM), lambda i, e: (l, e, 0, 0)),
            pl.BlockSpec((1, MOE_EXPERTS_PER_STEP, EXPERT_DIM, D), lambda i, e: (l, e, 0, 0)),
            pl.BlockSpec((1, D, 2 * SHARED_DIM), lw3),
            pl.BlockSpec((1, SHARED_DIM, D), lw3),
            pl.BlockSpec((1, 1, D), lw3),
            pl.BlockSpec((1, 1, D), lw3),
        ],
        out_specs=pl.BlockSpec((tm, D), tok),
        out_shape=jax.ShapeDtypeStruct((T, D), F32),
        scratch_shapes=[pltpu.VMEM((tm, D), F32)],
        compiler_params=_cparams(("arbitrary", "arbitrary")),
        name="moe_dense",
    )(u2, gates, x1, mod, wgu, wd, sgu, sd, ln_g, ln_b)


def _prepare_params(w_in, kv_norm_g, w_uk, w_uv, hgrn_lb, w_out, w_router, router_bias,
                    w_gate, w_up, w_down, ws_gate, ws_up, ws_down):
    L = w_in.shape[0]
    sizes = (A_WIDTH, KV_RANK, IDX_HEADS * IDX_DIM, IDX_DIM, IDX_HEADS, B_FDIM, B_FDIM, B_WIDTH, B_WIDTH)
    offs = np.concatenate([[0], np.cumsum(sizes)])
    seg = lambda i: w_in[:, :, offs[i]:offs[i + 1]]
    w_aq, w_ckv, w_iq, w_ik, w_iw, w_hq, w_hf, w_hi, w_hg = (seg(i) for i in range(9))
    zik = jnp.zeros_like(w_ik)
    wp = jnp.concatenate([w_aq, w_ckv, w_iq, w_ik, zik, zik, w_ik, w_hq, w_hf, w_hg, w_hi], axis=-1).astype(BF16)
    assert wp.shape[-1] == _C_END
    eye = jnp.eye(A_HEADS, dtype=F32)
    wblk = (jnp.einsum('lhdr,hg->lhdgr', w_uk * (ATTN_SCALE * LOG2E), eye)
            .reshape(L, A_WIDTH, A_HEADS * KV_RANK).astype(BF16))
    p = dict(
        wp=wp, wblk=wblk,
        wckvT=jnp.swapaxes(w_ckv, 1, 2).astype(BF16),
        wiwT=jnp.swapaxes(w_iw, 1, 2).astype(BF16),
        gkv=kv_norm_g.reshape(L, 1, KV_RANK),
        gkvT=jnp.broadcast_to(kv_norm_g[:, :, None], (L, KV_RANK, TM_PROJ)),
        wuvT=jnp.swapaxes(w_uv, 2, 3).astype(BF16),
        wo=w_out.astype(BF16),
        wrT=jnp.swapaxes(w_router, 1, 2).astype(BF16),
        rbias=jnp.broadcast_to(router_bias[:, :, None], (L, N_EXPERTS, TM_PROJ)),
        wgu=jnp.concatenate([w_gate, w_up], axis=-1).astype(BF16),
        wd=w_down.astype(BF16),
        sgu=jnp.concatenate([ws_gate, ws_up], axis=-1).astype(BF16),
        sd=ws_down.astype(BF16),
    )
    lbs = jnp.cumsum(jax.nn.softmax(hgrn_lb.astype(F32), axis=0), axis=0)
    lbs = jnp.clip(lbs - lbs[0:1], 0.0, 1.0 - 1e-6)
    p["llb"] = jnp.log(lbs).reshape(L, 1, B_FDIM)
    p["l1m"] = jnp.log1p(-lbs).reshape(L, 1, B_FDIM)
    return p


def kernel(x, c, w_ada, b_ada, w_in, kv_norm_g, w_uk, w_uv, rel_bias, hgrn_lb, gnorm_g, w_out, ln1_g, ln1_b,
           w_router, router_bias, w_gate, w_up, w_down, ws_gate, ws_up, ws_down, ln2_g, ln2_b):
    B, S, D = x.shape
    L = w_in.shape[0]
    alpha = (2 * L) ** 0.25
    p = _prepare_params(w_in, kv_norm_g, w_uk, w_uv, hgrn_lb, w_out, w_router, router_bias,
                        w_gate, w_up, w_down, ws_gate, ws_up, ws_down)
    mods = _adaln(c, w_ada, b_ada).reshape(L, B, 6, D)
    bn = _bias_tile(rel_bias)
    gn = gnorm_g.reshape(L, 1, B_VAL_DIM)
    ln1g, ln1b = ln1_g.reshape(L, 1, D), ln1_b.reshape(L, 1, D)
    ln2g, ln2b = ln2_g.reshape(L, 1, D), ln2_b.reshape(L, 1, D)
    for l in range(L):
        mod = mods[l]
        (qlat, ckv, ckvT, iq, ikA, ikB, iwT, hq, hk, hlf, hv, hgate) = _inproj(
            l, x, mod, p["wp"], p["wblk"], p["wckvT"], p["wiwT"], p["gkv"], p["gkvT"], p["llb"], p["l1m"])
        ya = _dsa(l, iq, iwT, qlat, ikA, ikB, ckv, ckvT, bn, p["wuvT"])
        yb = _hgrn(l, hq, hk, hlf, hv, hgate, gn)
        x1, u2, gates, rank, gates_t = _outproj(l, ya, yb, x, mod, p["wo"], ln1g, ln1b, p["wrT"], p["rbias"], alpha)
        u2f, x1f = u2.reshape(B * S, D), x1.reshape(B * S, D)

        def moe_sparse(l=l, mod=mod, u2f=u2f, x1f=x1f, rank=rank, gates_t=gates_t):
            ys = _experts(l, _dispatch(u2f, rank), p["wgu"], p["wd"])
            return _combine(l, ys, rank, gates_t, u2f, x1f, mod, p["sgu"], p["sd"], ln2g, ln2b, alpha, S)

        def moe_dense(l=l, mod=mod, u2f=u2f, x1f=x1f, gates=gates):
            return _moe(l, u2f, gates.reshape(B * S, 2 * N_EXPERTS), x1f, mod,
                        p["wgu"], p["wd"], p["sgu"], p["sd"], ln2g, ln2b, alpha, S)

        x = lax.cond(jnp.any(rank >= MOE_CAP), moe_dense, moe_sparse).reshape(B, S, D)
    return x
```

```python
import functools
import math

import numpy as np
import jax
import jax.numpy as jnp
from jax import lax
from jax.experimental import pallas as pl
from jax.experimental.pallas import tpu as pltpu

F32 = jnp.float32
BF16 = jnp.bfloat16
I32 = jnp.int32

D_MODEL = 1024
CHUNK = 64
A_HEADS = 8
A_HEAD_DIM = 64
A_WIDTH = A_HEADS * A_HEAD_DIM
KV_RANK = 128
IDX_HEADS = 8
IDX_DIM = 64
IDX_TOPK_MAX = 256
IDX_W_SCALE = (IDX_HEADS ** -0.5) * (IDX_DIM ** -0.5)
ATTN_SCALE = A_HEAD_DIM ** -0.5
LOG2E = math.log2(math.e)
KV_EXT = KV_RANK + 16
NUM_BUCKETS = 32
MAX_DISTANCE = 128
B_HEADS = 4
B_KEY_DIM = 128
B_VAL_DIM = 128
B_WIDTH = B_HEADS * B_VAL_DIM
B_FDIM = B_HEADS * B_KEY_DIM
N_EXPERTS = 64
TOP_K = 8
N_GROUPS = 8
TOPK_GROUPS = 4
EXPERT_DIM = 256
SHARED_DIM = 256
ROUTED_SCALE = 2.5
LN_EPS = 1e-5
RMS_EPS = 1e-6

LANES = 128
SUBLANES = 8
VMEM_LIMIT_BYTES = 56 * 1024 * 1024

INT_MIN = -(2 ** 31)
NEG_INF = float("-inf")

TM_PROJ = 512
TQ = 128
UNIT = 512
NEAR = 2 * TQ
COUNT_ACCS = 8
PLANE_ROWS = 32 * SUBLANES
TM_MOE = 1024
MOE_EXPERTS_PER_STEP = 4
MOE_GROUP = 256
MOE_CAP = 80
MOE_EXPERT_ROWS = 2048

_C_CKV, _C_IKA, _C_IKB, _C_HQ, _C_HF, _C_HG, _C_HI, _C_END = (0, 128, 256, 384, 896, 1408, 1920, 2432)
_R_AQ, _R_IQ, _R_CKV, _R_IW, _R_END = (0, 512, 1024, 1152, 1160)


def _silu(v):
    return v * (1.0 / (1.0 + jnp.exp(-v)))


def _nt_dot(a, b):
    return lax.dot_general(a, b, (((1,), (1,)), ((), ())), preferred_element_type=F32)


def _cparams(sem):
    return pltpu.CompilerParams(dimension_semantics=sem, vmem_limit_bytes=VMEM_LIMIT_BYTES)


def _adaln_kernel(c_ref, w_ref, b_ref, o_ref):
    cond = _silu(c_ref[...])
    o_ref[0] = jnp.dot(cond.astype(BF16), w_ref[0].astype(BF16), preferred_element_type=F32) + b_ref[0]


def _adaln(c, w_ada, b_ada):
    L, D, D6 = w_ada.shape
    B = c.shape[0]
    nb = D6 // D
    return pl.pallas_call(
        _adaln_kernel,
        grid=(L, nb),
        in_specs=[
            pl.BlockSpec((B, D), lambda l, j: (0, 0)),
            pl.BlockSpec((1, D, D), lambda l, j: (l, 0, j)),
            pl.BlockSpec((1, 1, D), lambda l, j: (l, 0, j)),
        ],
        out_specs=pl.BlockSpec((1, B, D), lambda l, j: (l, 0, j)),
        out_shape=jax.ShapeDtypeStruct((L, B, D6), F32),
        compiler_params=_cparams(("arbitrary", "arbitrary")),
        name="adaln_mod",
    )(c, w_ada, b_ada.reshape(L, 1, D6))


_T5_NB = NUM_BUCKETS // 2
_T5_EXACT = _T5_NB // 2
_T5_THRESHOLDS = tuple(
    int(math.ceil(_T5_EXACT * (MAX_DISTANCE / _T5_EXACT) ** (j / (_T5_NB - _T5_EXACT)) - 1e-9))
    for j in range(1, _T5_NB - _T5_EXACT))
FAR_BUCKET = _T5_NB - 1
assert _T5_THRESHOLDS[-1] <= TQ, "keys further than one query block behind must share the far bucket"


def _bias_kernel(rb_ref, o_ref):
    kr = lax.broadcasted_iota(I32, (NEAR + TQ, TQ), 0)
    ql = lax.broadcasted_iota(I32, (NEAR + TQ, TQ), 1)
    rel = kr - TQ - ql
    n = jnp.abs(rel)
    large = jnp.full(rel.shape, _T5_EXACT, I32)
    for t in _T5_THRESHOLDS:
        large = large + (n >= t).astype(I32)
    bucket = jnp.where(rel > 0, _T5_NB, 0) + jnp.where(n < _T5_EXACT, n, large)
    for h in range(A_HEADS):
        acc = jnp.zeros(rel.shape, F32)
        for bk in range(NUM_BUCKETS):
            acc = jnp.where(bucket == bk, rb_ref[bk, h], acc)
        o_ref[h] = (acc - rb_ref[FAR_BUCKET, h]) * LOG2E


def _bias_tile(rel_bias):
    return pl.pallas_call(
        _bias_kernel,
        in_specs=[pl.BlockSpec(memory_space=pltpu.SMEM)],
        out_specs=pl.BlockSpec(memory_space=pltpu.VMEM),
        out_shape=jax.ShapeDtypeStruct((A_HEADS, NEAR + TQ, TQ), F32),
        name="rel_bias_tile",
    )(rel_bias)


def _inproj_kernel(x_ref, mod_ref, wp_ref, wt_ref, wblkT_ref, gkv_ref, gkvT_ref, llb_ref, l1m_ref,
                   qlatT_ref, ckv_ref, ckvT_ref, iqT_ref, ikA_ref, ikB_ref, iwT_ref,
                   hq_ref, hk_ref, hlf_ref, hv_ref, hgate_ref):
    x = x_ref[0]
    sh1 = mod_ref[0, 0:1, :]
    sc1 = mod_ref[0, 1:2, :]
    u = (x * (1.0 + sc1) + sh1).astype(BF16)
    z = jnp.dot(u, wp_ref[0], preferred_element_type=F32)
    zt = _nt_dot(wt_ref[0], u)

    def proj(lo, hi):
        return z[:, lo:hi]

    qlatT_ref[0] = jnp.dot(wblkT_ref[0], zt[_R_AQ:_R_IQ].astype(BF16), preferred_element_type=F32).astype(BF16)

    zc = proj(_C_CKV, _C_IKA)
    inv = lax.rsqrt(jnp.mean(zc * zc, axis=-1, keepdims=True) + RMS_EPS)
    ckv_ref[0] = (zc * inv * gkv_ref[0]).astype(BF16)
    zct = zt[_R_CKV:_R_IW]
    inv_t = lax.rsqrt(jnp.mean(zct * zct, axis=0, keepdims=True) + RMS_EPS)
    ckvT_ref[0, 0:KV_RANK, :] = (zct * inv_t * gkvT_ref[0]).astype(BF16)
    ckvT_ref[0, KV_RANK:KV_EXT, :] = jnp.ones((KV_EXT - KV_RANK, zct.shape[1]), BF16)

    iqT_ref[0] = zt[_R_IQ:_R_CKV].astype(BF16)
    ikA_ref[0] = proj(_C_IKA, _C_IKB).astype(BF16)
    ikB_ref[0] = proj(_C_IKB, _C_HQ).astype(BF16)
    iwT_ref[0] = zt[_R_IW:_R_END] * IDX_W_SCALE

    hq_ref[0] = _silu(proj(_C_HQ, _C_HF))
    zf = proj(_C_HF, _C_HG)
    log_sig = jnp.minimum(zf, 0.0) - jnp.log1p(jnp.exp(-jnp.abs(zf)))
    a = llb_ref[0]
    c = l1m_ref[0] + log_sig
    logf = jnp.maximum(a, c) + jnp.log1p(jnp.exp(-jnp.abs(a - c)))
    hlf_ref[0] = logf
    hk_ref[0] = 1.0 - jnp.exp(logf)
    hgate_ref[0] = _silu(proj(_C_HG, _C_HI))
    hv_ref[0] = proj(_C_HI, _C_END).astype(BF16)


def _inproj(l, x, mod, wp, wt, wblkT, gkv, gkvT, llb, l1m):
    B, S, D = x.shape
    tm = TM_PROJ
    grid = (B, S // tm)
    lw3 = lambda b, i: (l, 0, 0)
    tok = lambda b, i: (b, i, 0)
    tokT = lambda b, i: (b, 0, i)
    hd4 = lambda b, i: (b, 0, i, 0)
    outs = [
        (jax.ShapeDtypeStruct((B, A_HEADS * KV_RANK, S), BF16), pl.BlockSpec((1, A_HEADS * KV_RANK, tm), tokT)),
        (jax.ShapeDtypeStruct((B, S, KV_RANK), BF16), pl.BlockSpec((1, tm, KV_RANK), tok)),
        (jax.ShapeDtypeStruct((B, KV_EXT, S), BF16), pl.BlockSpec((1, KV_EXT, tm), tokT)),
        (jax.ShapeDtypeStruct((B, IDX_HEADS * IDX_DIM, S), BF16), pl.BlockSpec((1, IDX_HEADS * IDX_DIM, tm), tokT)),
        (jax.ShapeDtypeStruct((B, S, LANES), BF16), pl.BlockSpec((1, tm, LANES), tok)),
        (jax.ShapeDtypeStruct((B, S, LANES), BF16), pl.BlockSpec((1, tm, LANES), tok)),
        (jax.ShapeDtypeStruct((B, IDX_HEADS, S), F32), pl.BlockSpec((1, IDX_HEADS, tm), tokT)),
        (jax.ShapeDtypeStruct((B, S, B_FDIM), F32), pl.BlockSpec((1, tm, B_FDIM), tok)),
        (jax.ShapeDtypeStruct((B, S, B_FDIM), F32), pl.BlockSpec((1, tm, B_FDIM), tok)),
        (jax.ShapeDtypeStruct((B, S, B_FDIM), F32), pl.BlockSpec((1, tm, B_FDIM), tok)),
        (jax.ShapeDtypeStruct((B, S, B_WIDTH), BF16), pl.BlockSpec((1, tm, B_WIDTH), tok)),
        (jax.ShapeDtypeStruct((B, S, B_WIDTH), F32), pl.BlockSpec((1, tm, B_WIDTH), tok)),
    ]
    return pl.pallas_call(
        _inproj_kernel,
        grid=grid,
        in_specs=[
            pl.BlockSpec((1, tm, D), tok),
            pl.BlockSpec((1, 6, D), lambda b, i: (b, 0, 0)),
            pl.BlockSpec((1, D, _C_END), lw3),
            pl.BlockSpec((1, _R_END, D), lw3),
            pl.BlockSpec((1, A_HEADS * KV_RANK, A_WIDTH), lw3),
            pl.BlockSpec((1, 1, KV_RANK), lw3),
            pl.BlockSpec((1, KV_RANK, tm), lw3),
            pl.BlockSpec((1, 1, B_FDIM), lw3),
            pl.BlockSpec((1, 1, B_FDIM), lw3),
        ],
        out_specs=[o[1] for o in outs],
        out_shape=[o[0] for o in outs],
        compiler_params=_cparams(("arbitrary", "arbitrary")),
        name="inproj",
    )(x, mod, wp, wt, wblkT, gkv, gkvT, llb, l1m)


def _dsa_kernel(iq_ref, iwT_ref, qlat_ref, ikA_ref, ikB_ref, ckv_ref, ckvT_ref, bn_ref, wuvT_ref, out_ref,
                sc_ref, plane_ref, madd_ref, maddn_ref, la_ref, lb_ref, pma_ref, pmb_ref, ot_ref, yaT_ref,
                *, k_sel, n_idx_bits):
    j = pl.program_id(1)
    q0 = j * TQ
    nk = q0 + TQ
    nunit = (nk + UNIT - 1) // UNIT
    near0 = pl.multiple_of(jnp.maximum(nk - NEAR, 0), TQ)
    bn_row0 = pl.multiple_of(jnp.where(j == 0, TQ, 0), TQ)
    lane = lax.broadcasted_iota(I32, (1, TQ), 1)
    limit = (((q0 + lane) >> 6) + 1) << 6
    row_iota = lax.broadcasted_iota(I32, (UNIT, TQ), 0)

    def unit_rows(u):
        return pl.ds(pl.multiple_of(u * UNIT, UNIT), UNIT)

    iqs = jnp.concatenate([iq_ref[0, p * LANES:(p + 1) * LANES, :] for p in range(IDX_HEADS // 2)], axis=1)
    iw = iwT_ref[0]

    last_unit = sc_ref.shape[0] // UNIT - 1
    half = IDX_HEADS // 2 * TQ

    def issue_scores(u, buf_ref):
        rows = unit_rows(jnp.minimum(u, last_unit))
        buf_ref[:, 0:half] = jnp.dot(ikA_ref[0, rows, :], iqs, preferred_element_type=F32)
        buf_ref[:, half:2 * half] = jnp.dot(ikB_ref[0, rows, :], iqs, preferred_element_type=F32)

    def reduce_scores(u, buf_ref):
        acc = jnp.zeros((UNIT, TQ), F32)
        for p in range(IDX_HEADS // 2):
            acc = acc + iw[2 * p:2 * p + 1, :] * jnp.maximum(buf_ref[:, p * TQ:(p + 1) * TQ], 0.0)
            acc = acc + iw[2 * p + 1:2 * p + 2, :] * jnp.maximum(buf_ref[:, half + p * TQ:half + (p + 1) * TQ], 0.0)
        bits = lax.bitcast_convert_type(acc, I32)
        key = bits ^ ((bits >> 31) & 0x7FFFFFFF)
        sc_ref[unit_rows(u), :] = jnp.where(row_iota + u * UNIT < limit, key, INT_MIN)

    issue_scores(0, la_ref)

    def score_pair(i, carry):
        issue_scores(2 * i + 1, lb_ref)
        reduce_scores(2 * i, la_ref)
        issue_scores(2 * i + 2, la_ref)
        reduce_scores(2 * i + 1, lb_ref)
        return carry

    lax.fori_loop(0, nunit // 2, score_pair, 0)

    @pl.when(nunit % 2 == 1)
    def _():
        reduce_scores(nunit - 1, la_ref)

    ngroups = (nk + PLANE_ROWS - 1) // PLANE_ROWS

    def plane_group(g, carry):
        rows = pl.ds(pl.multiple_of(g * PLANE_ROWS, PLANE_ROWS), PLANE_ROWS)
        words = (sc_ref[rows, :] ^ INT_MIN).reshape(32, SUBLANES, TQ)
        w = [words[i] for i in range(32)]
        j, m = 16, 0x0000FFFF
        while j:
            mask = np.int32(np.uint32(m).view(np.int32))
            k = 0
            while k < 32:
                t = (w[k] ^ lax.shift_right_logical(w[k + j], jnp.full(w[k].shape, j, I32))) & mask
                w[k] = w[k] ^ t
                w[k + j] = w[k + j] ^ (t << j)
                k = (k + j + 1) & ~j
            j >>= 1
            m = (m ^ (m << j)) & 0xFFFFFFFF
        for i in range(32):
            plane_ref[i, pl.ds(g * SUBLANES, SUBLANES), :] = w[i]
        return carry

    lax.fori_loop(0, ngroups, plane_group, 0)

    n_words = sc_ref.shape[0] // PLANE_ROWS * SUBLANES
    group_of_word = lax.broadcasted_iota(I32, (n_words, TQ), 0) // SUBLANES

    def bit_step(i, carry):
        alive, above, t_off, c_ge = carry
        hit = alive & plane_ref[i]
        cnt = above + jnp.sum(lax.population_count(hit), axis=0, keepdims=True)
        ok = cnt >= k_sel
        alive = jnp.where(ok, hit, alive ^ hit)
        above = jnp.where(ok, above, cnt)
        t_off = jnp.where(ok, t_off | (jnp.int32(1) << (31 - i)), t_off)
        return alive, above, t_off, jnp.where(ok, cnt, c_ge)

    zero_row = jnp.zeros((1, TQ), I32)
    _, _, t_off, c_ge = lax.fori_loop(
        0, 32, bit_step,
        (jnp.where(group_of_word < ngroups, jnp.int32(-1), jnp.int32(0)), zero_row, zero_row, zero_row))
    thr = jnp.maximum(t_off ^ INT_MIN, INT_MIN + 1)
    straddle = (c_ge > k_sel).astype(I32)

    def count_where(pred):
        def body(u, acc):
            hit = pred(sc_ref[unit_rows(u), :], u * UNIT).reshape(-1, COUNT_ACCS * SUBLANES, TQ)
            for s in range(hit.shape[0]):
                acc = jnp.where(hit[s], acc + 1, acc)
            return acc
        acc = lax.fori_loop(0, nunit, body, jnp.zeros((COUNT_ACCS * SUBLANES, TQ), I32))
        return jnp.sum(acc, axis=0, keepdims=True)

    def tie_bound():
        c_gt = count_where(lambda blk, r0: blk > thr)
        need = k_sel - c_gt

        def tie_body(i, j0):
            cand = j0 | (jnp.int32(1) << (n_idx_bits - 1 - i))
            cnt = count_where(lambda blk, r0: jnp.where(blk == thr, row_iota + r0, cand) < cand)
            return jnp.where(cnt < need, cand, j0)

        j0 = lax.fori_loop(0, n_idx_bits, tie_body, jnp.zeros((1, TQ), I32))
        return jnp.where(straddle > 0, j0 + 1, jnp.int32(2 ** n_idx_bits))

    jstar = lax.cond(jnp.max(straddle) > 0, tie_bound, lambda: jnp.full((1, TQ), 2 ** n_idx_bits, I32))

    def madd_unit(u, carry):
        rows = unit_rows(u)
        key = sc_ref[rows, :]
        tie_keep = jnp.where(row_iota + u * UNIT < jstar, 0.0, NEG_INF)
        madd_ref[rows, :] = jnp.where(key > thr, 0.0, jnp.where(key == thr, tie_keep, NEG_INF))
        return carry

    lax.fori_loop(0, nunit, madd_unit, 0)
    maddn_ref[...] = madd_ref[pl.ds(near0, NEAR), :]
    madd_ref[pl.ds(near0, NEAR), :] = jnp.full((NEAR, TQ), NEG_INF, F32)

    qall = jnp.concatenate([qlat_ref[0, h * KV_RANK:(h + 1) * KV_RANK, :] for h in range(A_HEADS)], axis=1)

    def col_max(v):
        return jnp.max(v.reshape(v.shape[0] // SUBLANES, SUBLANES, A_HEADS * TQ), axis=0)

    def fold(xl, part_max, ckv_t, m_old):
        m_new = jnp.maximum(m_old, jnp.max(part_max, axis=0, keepdims=True))
        m_use = jnp.where(m_new == NEG_INF, 0.0, m_new)
        p = jnp.exp2(xl - m_use).astype(BF16)
        ot_ref[...] = ot_ref[...] * jnp.exp2(m_old - m_use) + jnp.dot(ckv_t, p, preferred_element_type=F32)
        return m_new

    ot_ref[...] = jnp.zeros(ot_ref.shape, F32)
    near_rows = pl.ds(near0, NEAR)
    xn = jnp.dot(ckv_ref[0, near_rows, :], qall, preferred_element_type=F32)
    xn = xn + jnp.concatenate([maddn_ref[...]] * A_HEADS, axis=1)
    xn = xn + jnp.concatenate([bn_ref[h, pl.ds(bn_row0, NEAR), :] for h in range(A_HEADS)], axis=1)
    m_run = fold(xn, col_max(xn), ckvT_ref[0, :, near_rows], jnp.full((1, A_HEADS * TQ), NEG_INF, F32))


    def issue_logits(u, buf_ref, pm_ref):
        rows = unit_rows(jnp.minimum(u, last_unit))
        xl = jnp.dot(ckv_ref[0, rows, :], qall, preferred_element_type=F32)
        xl = xl + jnp.concatenate([madd_ref[rows, :]] * A_HEADS, axis=1)
        buf_ref[...] = xl
        pm_ref[...] = col_max(xl)

    def consume_logits(u, buf_ref, pm_ref, m_old):
        return fold(buf_ref[...], pm_ref[...], ckvT_ref[0, :, unit_rows(u)], m_old)

    issue_logits(0, la_ref, pma_ref)

    def pair_step(i, m_old):
        issue_logits(2 * i + 1, lb_ref, pmb_ref)
        m_mid = consume_logits(2 * i, la_ref, pma_ref, m_old)
        issue_logits(2 * i + 2, la_ref, pma_ref)
        return consume_logits(2 * i + 1, lb_ref, pmb_ref, m_mid)

    m_run = lax.fori_loop(0, nunit // 2, pair_step, m_run)

    @pl.when(nunit % 2 == 1)
    def _():
        consume_logits(nunit - 1, la_ref, pma_ref, m_run)
    o_t = (ot_ref[0:KV_RANK, :] * (1.0 / ot_ref[KV_RANK:KV_RANK + 1, :])).astype(BF16)
    for h in range(A_HEADS):
        yaT_ref[h * A_HEAD_DIM:(h + 1) * A_HEAD_DIM, :] = jnp.dot(
            wuvT_ref[0, h], o_t[:, h * TQ:(h + 1) * TQ], preferred_element_type=F32)

    out_ref[0] = yaT_ref[...].T.astype(BF16)


def _dsa(l, iq, iwT, qlat, ikA, ikB, ckv, ckvT, bn, wuvT):
    B, S = ckv.shape[0], ckv.shape[1]
    assert S % (2 * UNIT) == 0 and UNIT % TQ == 0 and TQ % CHUNK == 0 and CHUNK == 64 and NEAR <= UNIT
    k_sel = min(IDX_TOPK_MAX, S // 4)
    n_idx_bits = int(math.log2(S))
    assert 2 ** n_idx_bits == S
    grid = (B, S // TQ)
    blk = lambda b, i: (b, 0, i, 0)
    full = lambda b, i: (b, 0, 0)
    kern = functools.partial(_dsa_kernel, k_sel=k_sel, n_idx_bits=n_idx_bits)
    return pl.pallas_call(
        kern,
        grid=grid,
        in_specs=[
            pl.BlockSpec((1, IDX_HEADS * IDX_DIM, TQ), lambda b, i: (b, 0, i)),
            pl.BlockSpec((1, IDX_HEADS, TQ), lambda b, i: (b, 0, i)),
            pl.BlockSpec((1, A_HEADS * KV_RANK, TQ), lambda b, i: (b, 0, i)),
            pl.BlockSpec((1, S, LANES), full),
            pl.BlockSpec((1, S, LANES), full),
            pl.BlockSpec((1, S, KV_RANK), full),
            pl.BlockSpec((1, KV_EXT, S), full),
            pl.BlockSpec((A_HEADS, NEAR + TQ, TQ), lambda b, i: (0, 0, 0)),
            pl.BlockSpec((1, A_HEADS, A_HEAD_DIM, KV_RANK), lambda b, i: (l, 0, 0, 0)),
        ],
        out_specs=pl.BlockSpec((1, TQ, A_WIDTH), lambda b, i: (b, i, 0)),
        out_shape=jax.ShapeDtypeStruct((B, S, A_WIDTH), BF16),
        scratch_shapes=[
            pltpu.VMEM((S, TQ), I32),
            pltpu.VMEM((32, S // PLANE_ROWS * SUBLANES, TQ), I32),
            pltpu.VMEM((S, TQ), F32),
            pltpu.VMEM((NEAR, TQ), F32),
            pltpu.VMEM((UNIT, A_HEADS * TQ), F32),
            pltpu.VMEM((UNIT, A_HEADS * TQ), F32),
            pltpu.VMEM((SUBLANES, A_HEADS * TQ), F32),
            pltpu.VMEM((SUBLANES, A_HEADS * TQ), F32),
            pltpu.VMEM((KV_EXT, A_HEADS * TQ), F32),
            pltpu.VMEM((A_WIDTH, TQ), F32),
        ],
        compiler_params=_cparams(("arbitrary", "arbitrary")),
        name="dsa_attention",
    )(iq, iwT, qlat, ikA, ikB, ckv, ckvT, bn, wuvT)


def _hgrn_constants():
    c = CHUNK
    r = np.arange(c)[:, None]
    jj = np.arange(c)[None, :]
    mats = [(jj <= r), (jj > r)]
    masks = [np.eye(c, dtype=bool)]
    m = c // 2
    while m >= 1:
        start = (r // (2 * m)) * (2 * m)
        bd = start + m - 1
        upper = r > bd
        mats.append(np.where(upper, (jj > bd) & (jj <= r), (jj > r) & (jj <= bd)))
        same_parent = (r // (2 * m)) == (jj // (2 * m))
        masks.append(same_parent & upper & (jj <= (jj // (2 * m)) * (2 * m) + m - 1))
        m //= 2
    m_all = np.concatenate(mats, axis=0).astype(np.float32)
    total = np.zeros((c, c), np.int32)
    for mk in masks:
        total += mk
    assert (total == np.tril(np.ones((c, c), np.int32))).all()
    return np.concatenate([m_all] * 3, axis=1), np.stack(masks).astype(np.float32)


_HGRN_M3, _HGRN_MASKS = _hgrn_constants()
_HGRN_LEVELS = _HGRN_MASKS.shape[0] - 1
HGRN_STEP_CHUNKS = 2
HGRN_STEP_BATCH = 4


def _hgrn_kernel(q_ref, k_ref, lf_ref, v_ref, gate_ref, m3_ref, mask_ref, gn_ref, out_ref, st_ref):
    @pl.when(pl.program_id(1) == 0)
    def _():
        st_ref[...] = jnp.zeros(st_ref.shape, F32)

    c = CHUNK
    intra = {}
    for ci in range(HGRN_STEP_CHUNKS):
        rows = slice(ci * c, (ci + 1) * c)
        for bi in range(HGRN_STEP_BATCH):
            g = lf_ref[bi, rows, :]
            g_hi = g.astype(BF16)
            r1 = g - g_hi.astype(F32)
            g_mid = r1.astype(BF16)
            g_lo = (r1 - g_mid.astype(F32)).astype(BF16)
            sums = jnp.dot(m3_ref[...], jnp.concatenate([g_hi, g_mid, g_lo], axis=0), preferred_element_type=F32)
            e_all = jnp.exp(sums)
            for h in range(B_HEADS):
                cols = slice(h * B_KEY_DIM, (h + 1) * B_KEY_DIM)
                qh = q_ref[bi, rows, cols]
                kh = k_ref[bi, rows, cols]
                att = mask_ref[0] * _nt_dot(qh.astype(BF16), kh.astype(BF16))
                for lv in range(_HGRN_LEVELS):
                    e_l = e_all[(2 + lv) * c:(3 + lv) * c, cols]
                    att = att + mask_ref[lv + 1] * _nt_dot((qh * e_l).astype(BF16), (kh * e_l).astype(BF16))
                e_b = e_all[0:c, cols]
                intra[bi, ci, h] = (jnp.dot(att.astype(BF16), v_ref[bi, rows, cols], preferred_element_type=F32),
                                    (qh * e_b).astype(BF16), (kh * e_all[c:2 * c, cols]).astype(BF16),
                                    e_b[c - 1:c, :])
    for ci in range(HGRN_STEP_CHUNKS):
        rows = slice(ci * c, (ci + 1) * c)
        for bi in range(HGRN_STEP_BATCH):
            for h in range(B_HEADS):
                cols = slice(h * B_KEY_DIM, (h + 1) * B_KEY_DIM)
                o_intra, q_dec, k_rem, decay_all = intra[bi, ci, h]
                st = st_ref[bi, h]
                o = o_intra + _nt_dot(q_dec, st.astype(BF16))
                upd = lax.dot_general(v_ref[bi, rows, cols], k_rem, (((0,), (0,)), ((), ())),
                                      preferred_element_type=F32)
                st_ref[bi, h] = st * decay_all + upd
                o = o * lax.rsqrt(jnp.mean(o * o, axis=-1, keepdims=True) + RMS_EPS) * gn_ref[0]
                out_ref[bi, rows, cols] = (o * gate_ref[bi, rows, cols]).astype(BF16)


def _hgrn(l, hq, hk, hlf, hv, hgate, gnorm):
    B, S, W = hq.shape
    ts = CHUNK * HGRN_STEP_CHUNKS
    nb = HGRN_STEP_BATCH
    assert B % nb == 0
    tok = lambda b, i: (b, i, 0)
    return pl.pallas_call(
        _hgrn_kernel,
        grid=(B // nb, S // ts),
        in_specs=[
            pl.BlockSpec((nb, ts, W), tok),
            pl.BlockSpec((nb, ts, W), tok),
            pl.BlockSpec((nb, ts, W), tok),
            pl.BlockSpec((nb, ts, W), tok),
            pl.BlockSpec((nb, ts, W), tok),
            pl.BlockSpec(_HGRN_M3.shape, lambda b, i: (0, 0)),
            pl.BlockSpec(_HGRN_MASKS.shape, lambda b, i: (0, 0, 0)),
            pl.BlockSpec((1, 1, B_VAL_DIM), lambda b, i: (l, 0, 0)),
        ],
        out_specs=pl.BlockSpec((nb, ts, W), tok),
        out_shape=jax.ShapeDtypeStruct((B, S, W), BF16),
        scratch_shapes=[pltpu.VMEM((nb, B_HEADS, B_VAL_DIM, B_KEY_DIM), F32)],
        compiler_params=_cparams(("arbitrary", "arbitrary")),
        name="hgrn2",
    )(hq, hk, hlf, hv, hgate, jnp.asarray(_HGRN_M3, BF16), jnp.asarray(_HGRN_MASKS), gnorm)


def _layernorm(v, g, b):
    mu = jnp.mean(v, axis=-1, keepdims=True)
    d = v - mu
    var = jnp.mean(d * d, axis=-1, keepdims=True)
    return d * lax.rsqrt(var + LN_EPS) * g + b


def _first_argmax(v, idx, axes, big):
    mx = v
    for ax in axes:
        mx = jnp.max(mx, axis=ax, keepdims=True)
    pos = jnp.where(v == mx, idx, big)
    for ax in axes:
        pos = jnp.min(pos, axis=ax, keepdims=True)
    return mx, pos


def _outproj_kernel(ya_ref, yb_ref, x_ref, mod_ref, wo_ref, lng_ref, lnb_ref, wrT_ref, rbias_ref, tri_ref,
                    x1_ref, u2_ref, gates_ref, rank_ref, gatesT_ref, *, alpha):
    y = jnp.dot(ya_ref[0], wo_ref[0, 0:A_WIDTH, :], preferred_element_type=F32)
    y = y + jnp.dot(yb_ref[0], wo_ref[0, A_WIDTH:, :], preferred_element_type=F32)
    g1 = mod_ref[0, 2:3, :]
    x1 = _layernorm(alpha * x_ref[0] + (1.0 + g1) * y, lng_ref[0], lnb_ref[0])
    x1_ref[0] = x1
    u2 = (x1 * (1.0 + mod_ref[0, 4:5, :]) + mod_ref[0, 3:4, :]).astype(BF16)
    u2_ref[0] = u2

    tm = u2.shape[0]
    gsz = N_EXPERTS // N_GROUPS
    scores = 1.0 / (1.0 + jnp.exp(-_nt_dot(wrT_ref[0], u2)))
    sel = (scores + rbias_ref[0]).reshape(N_GROUPS, gsz, tm)
    scores = scores.reshape(N_GROUPS, gsz, tm)
    i_m = lax.broadcasted_iota(I32, (N_GROUPS, gsz, tm), 1)
    i_g = lax.broadcasted_iota(I32, (N_GROUPS, 1, tm), 0)
    i_e = lax.broadcasted_iota(I32, (N_GROUPS, gsz, tm), 0) * gsz + i_m
    m1, p1 = _first_argmax(sel, i_m, (1,), gsz)
    m2 = jnp.max(jnp.where(i_m == p1, NEG_INF, sel), axis=1, keepdims=True)
    gs = m1 + m2
    gmask = jnp.zeros(gs.shape, F32)
    for _ in range(TOPK_GROUPS):
        _, pg = _first_argmax(gs, i_g, (0,), N_GROUPS)
        hit = i_g == pg
        gmask = jnp.where(hit, 1.0, gmask)
        gs = jnp.where(hit, NEG_INF, gs)
    cand = jnp.where(jnp.broadcast_to(gmask, sel.shape) > 0.0, sel, NEG_INF)
    w = jnp.zeros(sel.shape, F32)
    chosen = jnp.zeros(sel.shape, F32)
    for _ in range(TOP_K):
        _, pe = _first_argmax(cand, i_e, (1, 0), N_EXPERTS)
        hit = i_e == pe
        w = jnp.where(hit, scores, w)
        chosen = jnp.where(hit, 1.0, chosen)
        cand = jnp.where(hit, NEG_INF, cand)
    wsum = jnp.sum(jnp.sum(w, axis=1, keepdims=True), axis=0, keepdims=True)
    gates = (w / wsum * ROUTED_SCALE).reshape(N_EXPERTS, tm)
    g_hi = gates.astype(BF16).astype(F32)
    g_lo = (gates - g_hi).astype(BF16).astype(F32)
    gates_ref[0] = jnp.concatenate([g_hi, g_lo], axis=0).T.astype(BF16)

    chosen2 = chosen.reshape(N_EXPERTS, tm)
    rank = jnp.concatenate(
        [jnp.dot(chosen2[:, g * MOE_GROUP:(g + 1) * MOE_GROUP].astype(BF16), tri_ref[...],
                 preferred_element_type=F32) for g in range(tm // MOE_GROUP)], axis=1)
    rank_ref[0] = jnp.where(chosen2 > 0.0, rank, -1.0).astype(I32)
    gatesT_ref[0] = gates


def _outproj(l, ya, yb, x, mod, wo, ln_g, ln_b, wrT, rbias, alpha):
    B, S, D = x.shape
    tm = TM_PROJ
    tok = lambda b, i: (b, i, 0)
    lw3 = lambda b, i: (l, 0, 0)
    return pl.pallas_call(
        functools.partial(_outproj_kernel, alpha=alpha),
        grid=(B, S // tm),
        in_specs=[
            pl.BlockSpec((1, tm, A_WIDTH), tok),
            pl.BlockSpec((1, tm, B_WIDTH), tok),
            pl.BlockSpec((1, tm, D), tok),
            pl.BlockSpec((1, 6, D), lambda b, i: (b, 0, 0)),
            pl.BlockSpec((1, D, D), lw3),
            pl.BlockSpec((1, 1, D), lw3),
            pl.BlockSpec((1, 1, D), lw3),
            pl.BlockSpec((1, N_EXPERTS, D), lw3),
            pl.BlockSpec((1, N_EXPERTS, tm), lw3),
            pl.BlockSpec((MOE_GROUP, MOE_GROUP), lambda b, i: (0, 0)),
        ],
        out_specs=[pl.BlockSpec((1, tm, D), tok), pl.BlockSpec((1, tm, D), tok),
                   pl.BlockSpec((1, tm, 2 * N_EXPERTS), tok),
                   pl.BlockSpec((1, N_EXPERTS, tm), lambda b, i: (b, 0, i)),
                   pl.BlockSpec((1, N_EXPERTS, tm), lambda b, i: (b, 0, i))],
        out_shape=[jax.ShapeDtypeStruct((B, S, D), F32), jax.ShapeDtypeStruct((B, S, D), BF16),
                   jax.ShapeDtypeStruct((B, S, 2 * N_EXPERTS), BF16),
                   jax.ShapeDtypeStruct((B, N_EXPERTS, S), I32),
                   jax.ShapeDtypeStruct((B, N_EXPERTS, S), F32)],
        compiler_params=_cparams(("arbitrary", "arbitrary")),
        name="outproj_router",
    )(ya, yb, x, mod, wo, ln_g, ln_b, wrT, rbias,
      jnp.asarray(np.triu(np.ones((MOE_GROUP, MOE_GROUP), np.float32), 1), BF16))


MOE_CHUNK_EXPERTS = 8


def _slot_onehot(rank_rows, values):
    row = lax.broadcasted_iota(I32, (MOE_CAP, rank_rows.shape[1]), 0)
    return jnp.concatenate(
        [jnp.where(row == rank_rows[e:e + 1, :], values[e:e + 1, :], 0.0) for e in range(rank_rows.shape[0])], axis=0)


def _dispatch_kernel(u_ref, rank_ref, x_ref):
    u = u_ref[...]
    ones = jnp.ones((MOE_CHUNK_EXPERTS, MOE_GROUP), F32)
    for c in range(N_EXPERTS // MOE_CHUNK_EXPERTS):
        es = slice(c * MOE_CHUNK_EXPERTS, (c + 1) * MOE_CHUNK_EXPERTS)
        onehot = _slot_onehot(rank_ref[0, es, :], ones)
        xs = jnp.dot(onehot.astype(BF16), u, preferred_element_type=F32).astype(BF16)
        x_ref[es] = xs.reshape(MOE_CHUNK_EXPERTS, MOE_CAP, -1)


def _dispatch(u2, rank):
    T, D = u2.shape
    ng = T // MOE_GROUP
    gps = rank.shape[-1] // MOE_GROUP
    return pl.pallas_call(
        _dispatch_kernel,
        grid=(ng,),
        in_specs=[pl.BlockSpec((MOE_GROUP, D), lambda g: (g, 0)),
                  pl.BlockSpec((1, N_EXPERTS, MOE_GROUP), lambda g: (g // gps, 0, g % gps))],
        out_specs=pl.BlockSpec((N_EXPERTS, MOE_CAP, D), lambda g: (0, g, 0)),
        out_shape=jax.ShapeDtypeStruct((N_EXPERTS, ng * MOE_CAP, D), BF16),
        compiler_params=_cparams(("arbitrary",)),
        name="moe_dispatch",
    )(u2, rank)


def _expert_kernel(x_ref, wgu_ref, wd_ref, y_ref):
    hgu = jnp.dot(x_ref[0], wgu_ref[0, 0], preferred_element_type=F32)
    h = _silu(hgu[:, :EXPERT_DIM]) * hgu[:, EXPERT_DIM:]
    y_ref[0] = jnp.dot(h.astype(BF16), wd_ref[0, 0], preferred_element_type=F32).astype(BF16)


def _experts(l, xs, wgu, wd):
    E, R, D = xs.shape
    tr = min(R, MOE_EXPERT_ROWS)
    assert R % tr == 0
    return pl.pallas_call(
        _expert_kernel,
        grid=(E, R // tr),
        in_specs=[pl.BlockSpec((1, tr, D), lambda e, i: (e, i, 0)),
                  pl.BlockSpec((1, 1, D, 2 * EXPERT_DIM), lambda e, i: (l, e, 0, 0)),
                  pl.BlockSpec((1, 1, EXPERT_DIM, D), lambda e, i: (l, e, 0, 0))],
        out_specs=pl.BlockSpec((1, tr, D), lambda e, i: (e, i, 0)),
        out_shape=jax.ShapeDtypeStruct((E, R, D), BF16),
        compiler_params=_cparams(("arbitrary", "arbitrary")),
        name="moe_experts",
    )(xs, wgu, wd)


def _combine_kernel(y_ref, rank_ref, gates_ref, u_ref, x1_ref, mod_ref, sgu_ref, sd_ref, lng_ref, lnb_ref,
                    out_ref, *, alpha):
    hgu = jnp.dot(u_ref[...], sgu_ref[0], preferred_element_type=F32)
    hs = _silu(hgu[:, :SHARED_DIM]) * hgu[:, SHARED_DIM:]
    y = jnp.dot(hs.astype(BF16), sd_ref[0], preferred_element_type=F32)
    for c in range(N_EXPERTS // MOE_CHUNK_EXPERTS):
        es = slice(c * MOE_CHUNK_EXPERTS, (c + 1) * MOE_CHUNK_EXPERTS)
        pick = _slot_onehot(rank_ref[0, es, :], gates_ref[0, es, :])
        ys = y_ref[es].reshape(MOE_CHUNK_EXPERTS * MOE_CAP, -1)
        y = y + lax.dot_general(pick.astype(BF16), ys, (((0,), (0,)), ((), ())), preferred_element_type=F32)
    g2 = mod_ref[0, 5:6, :]
    out_ref[...] = _layernorm(alpha * x1_ref[...] + (1.0 + g2) * y, lng_ref[0], lnb_ref[0])


def _combine(l, ys, rank, gates_t, u2, x1, mod, sgu, sd, ln_g, ln_b, alpha, seq):
    T, D = u2.shape
    tok = lambda g: (g, 0)
    lw3 = lambda g: (l, 0, 0)
    gps = seq // MOE_GROUP
    per_group = lambda g: (g // gps, 0, g % gps)
    return pl.pallas_call(
        functools.partial(_combine_kernel, alpha=alpha),
        grid=(T // MOE_GROUP,),
        in_specs=[
            pl.BlockSpec((N_EXPERTS, MOE_CAP, D), lambda g: (0, g, 0)),
            pl.BlockSpec((1, N_EXPERTS, MOE_GROUP), per_group),
            pl.BlockSpec((1, N_EXPERTS, MOE_GROUP), per_group),
            pl.BlockSpec((MOE_GROUP, D), tok),
            pl.BlockSpec((MOE_GROUP, D), tok),
            pl.BlockSpec((1, 6, D), lambda g: ((g * MOE_GROUP) // seq, 0, 0)),
            pl.BlockSpec((1, D, 2 * SHARED_DIM), lw3),
            pl.BlockSpec((1, SHARED_DIM, D), lw3),
            pl.BlockSpec((1, 1, D), lw3),
            pl.BlockSpec((1, 1, D), lw3),
        ],
        out_specs=pl.BlockSpec((MOE_GROUP, D), tok),
        out_shape=jax.ShapeDtypeStruct((T, D), F32),
        compiler_params=_cparams(("arbitrary",)),
        name="moe_combine",
    )(ys, rank, gates_t, u2, x1, mod, sgu, sd, ln_g, ln_b)


def _moe_kernel(u_ref, gates_ref, x1_ref, mod_ref, wgu_ref, wd_ref, sgu_ref, sd_ref, lng_ref, lnb_ref,
                out_ref, acc_ref, *, alpha):
    s = pl.program_id(1)
    u = u_ref[...]

    def hidden(wgu):
        hgu = jnp.dot(u, wgu, preferred_element_type=F32)
        return _silu(hgu[:, :EXPERT_DIM]) * hgu[:, EXPERT_DIM:]

    @pl.when(s == 0)
    def _():
        acc_ref[...] = jnp.dot(hidden(sgu_ref[0]).astype(BF16), sd_ref[0], preferred_element_type=F32)

    rows = lax.broadcasted_iota(I32, (2 * N_EXPERTS, MOE_EXPERTS_PER_STEP * EXPERT_DIM), 0) & (N_EXPERTS - 1)
    cols = lax.broadcasted_iota(I32, (2 * N_EXPERTS, MOE_EXPERTS_PER_STEP * EXPERT_DIM), 1)
    onehot = jnp.where(rows == s * MOE_EXPERTS_PER_STEP + cols // EXPERT_DIM, 1.0, 0.0).astype(BF16)
    gate = jnp.dot(gates_ref[...], onehot, preferred_element_type=F32)
    h = jnp.concatenate(
        [(hidden(wgu_ref[0, k]) * gate[:, k * EXPERT_DIM:(k + 1) * EXPERT_DIM]).astype(BF16)
         for k in range(MOE_EXPERTS_PER_STEP)], axis=1)
    wd = wd_ref[0].reshape(MOE_EXPERTS_PER_STEP * EXPERT_DIM, wd_ref.shape[-1])
    acc_ref[...] += jnp.dot(h, wd, preferred_element_type=F32)

    @pl.when(s == pl.num_programs(1) - 1)
    def _():
        g2 = mod_ref[0, 5:6, :]
        out_ref[...] = _layernorm(alpha * x1_ref[...] + (1.0 + g2) * acc_ref[...], lng_ref[0], lnb_ref[0])


def _moe(l, u2, gates, x1, mod, wgu, wd, sgu, sd, ln_g, ln_b, alpha, seq):
    T, D = u2.shape
    tm = TM_MOE
    assert seq % tm == 0
    tok = lambda i, e: (i, 0)
    lw3 = lambda i, e: (l, 0, 0)
    return pl.pallas_call(
        functools.partial(_moe_kernel, alpha=alpha),
        grid=(T // tm, N_EXPERTS // MOE_EXPERTS_PER_STEP),
        in_specs=[
            pl.BlockSpec((tm, D), tok),
            pl.BlockSpec((tm, 2 * N_EXPERTS), tok),
            pl.BlockSpec((tm, D), tok),
            pl.BlockSpec((1, 6, D), lambda i, e: ((i * tm) // seq, 0, 0)),
            pl.BlockSpec((1, MOE_EXPERTS_PER_STEP, D, 2 * EXPERT_DIM), lambda i, e: (l, e, 0, 0)),
            pl.BlockSpec((1, MOE_EXPERTS_PER_STEP, EXPERT_DIM, D), lambda i, e: (l, e, 0, 0)),
            pl.BlockSpec((1, D, 2 * SHARED_DIM), lw3),
            pl.BlockSpec((1, SHARED_DIM, D), lw3),
            pl.BlockSpec((1, 1, D), lw3),
            pl.BlockSpec((1, 1, D), lw3),
        ],
        out_specs=pl.BlockSpec((tm, D), tok),
        out_shape=jax.ShapeDtypeStruct((T, D), F32),
        scratch_shapes=[pltpu.VMEM((tm, D), F32)],
        compiler_params=_cparams(("arbitrary", "arbitrary")),
        name="moe_dense",
    )(u2, gates, x1, mod, wgu, wd, sgu, sd, ln_g, ln_b)


def _prepare_params(w_in, kv_norm_g, w_uk, w_uv, hgrn_lb, w_out, w_router, router_bias,
                    w_gate, w_up, w_down, ws_gate, ws_up, ws_down):
    L = w_in.shape[0]
    sizes = (A_WIDTH, KV_RANK, IDX_HEADS * IDX_DIM, IDX_DIM, IDX_HEADS, B_FDIM, B_FDIM, B_WIDTH, B_WIDTH)
    offs = np.concatenate([[0], np.cumsum(sizes)])
    seg = lambda i: w_in[:, :, offs[i]:offs[i + 1]]
    w_aq, w_ckv, w_iq, w_ik, w_iw, w_hq, w_hf, w_hi, w_hg = (seg(i) for i in range(9))
    zik = jnp.zeros_like(w_ik)
    wp = jnp.concatenate([w_ckv, w_ik, zik, zik, w_ik, w_hq, w_hf, w_hg, w_hi], axis=-1).astype(BF16)
    wt = jnp.swapaxes(jnp.concatenate([w_aq, w_iq, w_ckv, w_iw], axis=-1), 1, 2).astype(BF16)
    assert wp.shape[-1] == _C_END and wt.shape[1] == _R_END
    eye = jnp.eye(A_HEADS, dtype=F32)
    wblk = (jnp.einsum('lhdr,hg->lhdgr', w_uk * (ATTN_SCALE * LOG2E), eye)
            .reshape(L, A_WIDTH, A_HEADS * KV_RANK).astype(BF16))
    p = dict(
        wp=wp, wt=wt, wblkT=jnp.swapaxes(wblk, 1, 2),
        gkv=kv_norm_g.reshape(L, 1, KV_RANK),
        gkvT=jnp.broadcast_to(kv_norm_g[:, :, None], (L, KV_RANK, TM_PROJ)),
        wuvT=jnp.swapaxes(w_uv, 2, 3).astype(BF16),
        wo=w_out.astype(BF16),
        wrT=jnp.swapaxes(w_router, 1, 2).astype(BF16),
        rbias=jnp.broadcast_to(router_bias[:, :, None], (L, N_EXPERTS, TM_PROJ)),
        wgu=jnp.concatenate([w_gate, w_up], axis=-1).astype(BF16),
        wd=w_down.astype(BF16),
        sgu=jnp.concatenate([ws_gate, ws_up], axis=-1).astype(BF16),
        sd=ws_down.astype(BF16),
    )
    lbs = jnp.cumsum(jax.nn.softmax(hgrn_lb.astype(F32), axis=0), axis=0)
    lbs = jnp.clip(lbs - lbs[0:1], 0.0, 1.0 - 1e-6)
    p["llb"] = jnp.log(lbs).reshape(L, 1, B_FDIM)
    p["l1m"] = jnp.log1p(-lbs).reshape(L, 1, B_FDIM)
    return p


def kernel(x, c, w_ada, b_ada, w_in, kv_norm_g, w_uk, w_uv, rel_bias, hgrn_lb, gnorm_g, w_out, ln1_g, ln1_b,
           w_router, router_bias, w_gate, w_up, w_down, ws_gate, ws_up, ws_down, ln2_g, ln2_b):
    B, S, D = x.shape
    L = w_in.shape[0]
    alpha = (2 * L) ** 0.25
    p = _prepare_params(w_in, kv_norm_g, w_uk, w_uv, hgrn_lb, w_out, w_router, router_bias,
                        w_gate, w_up, w_down, ws_gate, ws_up, ws_down)
    mods = _adaln(c, w_ada, b_ada).reshape(L, B, 6, D)
    bn = _bias_tile(rel_bias)
    gn = gnorm_g.reshape(L, 1, B_VAL_DIM)
    ln1g, ln1b = ln1_g.reshape(L, 1, D), ln1_b.reshape(L, 1, D)
    ln2g, ln2b = ln2_g.reshape(L, 1, D), ln2_b.reshape(L, 1, D)
    for l in range(L):
        mod = mods[l]
        (qlat, ckv, ckvT, iq, ikA, ikB, iwT, hq, hk, hlf, hv, hgate) = _inproj(
            l, x, mod, p["wp"], p["wt"], p["wblkT"], p["gkv"], p["gkvT"], p["llb"], p["l1m"])
        ya = _dsa(l, iq, iwT, qlat, ikA, ikB, ckv, ckvT, bn, p["wuvT"])
        yb = _hgrn(l, hq, hk, hlf, hv, hgate, gn)
        x1, u2, gates, rank, gates_t = _outproj(l, ya, yb, x, mod, p["wo"], ln1g, ln1b, p["wrT"], p["rbias"], alpha)
        u2f, x1f = u2.reshape(B * S, D), x1.reshape(B * S, D)

        def moe_sparse(l=l, mod=mod, u2f=u2f, x1f=x1f, rank=rank, gates_t=gates_t):
            ys = _experts(l, _dispatch(u2f, rank), p["wgu"], p["wd"])
            return _combine(l, ys, rank, gates_t, u2f, x1f, mod, p["sgu"], p["sd"], ln2g, ln2b, alpha, S)

        def moe_dense(l=l, mod=mod, u2f=u2f, x1f=x1f, gates=gates):
            return _moe(l, u2f, gates.reshape(B * S, 2 * N_EXPERTS), x1f, mod,
                        p["wgu"], p["wd"], p["sgu"], p["sd"], ln2g, ln2b, alpha, S)

        x = lax.cond(jnp.any(rank >= MOE_CAP), moe_dense, moe_sparse).reshape(B, S, D)
    return x
```

```python
import functools
import math

import numpy as np
import jax
import jax.numpy as jnp
from jax import lax
from jax.experimental import pallas as pl
from jax.experimental.pallas import tpu as pltpu

F32 = jnp.float32
BF16 = jnp.bfloat16
I32 = jnp.int32

D_MODEL = 1024
CHUNK = 64
A_HEADS = 8
A_HEAD_DIM = 64
A_WIDTH = A_HEADS * A_HEAD_DIM
KV_RANK = 128
IDX_HEADS = 8
IDX_DIM = 64
IDX_TOPK_MAX = 256
IDX_W_SCALE = (IDX_HEADS ** -0.5) * (IDX_DIM ** -0.5)
ATTN_SCALE = A_HEAD_DIM ** -0.5
LOG2E = math.log2(math.e)
KV_EXT = KV_RANK + 16
NUM_BUCKETS = 32
MAX_DISTANCE = 128
B_HEADS = 4
B_KEY_DIM = 128
B_VAL_DIM = 128
B_WIDTH = B_HEADS * B_VAL_DIM
B_FDIM = B_HEADS * B_KEY_DIM
N_EXPERTS = 64
TOP_K = 8
N_GROUPS = 8
TOPK_GROUPS = 4
EXPERT_DIM = 256
SHARED_DIM = 256
ROUTED_SCALE = 2.5
LN_EPS = 1e-5
RMS_EPS = 1e-6

LANES = 128
SUBLANES = 8
VMEM_LIMIT_BYTES = 56 * 1024 * 1024

INT_MIN = -(2 ** 31)
NEG_INF = float("-inf")

TM_PROJ = 1024
TQ = 128
UNIT = 512
NEAR = 2 * TQ
COUNT_ACCS = 8
PLANE_ROWS = 32 * SUBLANES
TM_MOE = 1024
MOE_EXPERTS_PER_STEP = 4
MOE_GROUP = 256
MOE_CAP = 80
MOE_EXPERT_ROWS = 2560

_C_CKV, _C_IKA, _C_IKB, _C_HQ, _C_HF, _C_HG, _C_HI, _C_END = (0, 128, 256, 384, 896, 1408, 1920, 2432)
_R_AQ, _R_IQ, _R_CKV, _R_IW, _R_END = (0, 512, 1024, 1152, 1160)


def _silu(v):
    return v * (1.0 / (1.0 + jnp.exp(-v)))


def _nt_dot(a, b):
    return lax.dot_general(a, b, (((1,), (1,)), ((), ())), preferred_element_type=F32)


def _cparams(sem):
    return pltpu.CompilerParams(dimension_semantics=sem, vmem_limit_bytes=VMEM_LIMIT_BYTES)


def _adaln_kernel(c_ref, w_ref, b_ref, o_ref):
    cond = _silu(c_ref[...])
    o_ref[0] = jnp.dot(cond.astype(BF16), w_ref[0].astype(BF16), preferred_element_type=F32) + b_ref[0]


def _adaln(c, w_ada, b_ada):
    L, D, D6 = w_ada.shape
    B = c.shape[0]
    nb = D6 // D
    return pl.pallas_call(
        _adaln_kernel,
        grid=(L, nb),
        in_specs=[
            pl.BlockSpec((B, D), lambda l, j: (0, 0)),
            pl.BlockSpec((1, D, D), lambda l, j: (l, 0, j)),
            pl.BlockSpec((1, 1, D), lambda l, j: (l, 0, j)),
        ],
        out_specs=pl.BlockSpec((1, B, D), lambda l, j: (l, 0, j)),
        out_shape=jax.ShapeDtypeStruct((L, B, D6), F32),
        compiler_params=_cparams(("arbitrary", "arbitrary")),
        name="adaln_mod",
    )(c, w_ada, b_ada.reshape(L, 1, D6))


_T5_NB = NUM_BUCKETS // 2
_T5_EXACT = _T5_NB // 2
_T5_THRESHOLDS = tuple(
    int(math.ceil(_T5_EXACT * (MAX_DISTANCE / _T5_EXACT) ** (j / (_T5_NB - _T5_EXACT)) - 1e-9))
    for j in range(1, _T5_NB - _T5_EXACT))
FAR_BUCKET = _T5_NB - 1
assert _T5_THRESHOLDS[-1] <= TQ, "keys further than one query block behind must share the far bucket"


def _bias_kernel(rb_ref, o_ref):
    kr = lax.broadcasted_iota(I32, (NEAR + TQ, TQ), 0)
    ql = lax.broadcasted_iota(I32, (NEAR + TQ, TQ), 1)
    rel = kr - TQ - ql
    n = jnp.abs(rel)
    large = jnp.full(rel.shape, _T5_EXACT, I32)
    for t in _T5_THRESHOLDS:
        large = large + (n >= t).astype(I32)
    bucket = jnp.where(rel > 0, _T5_NB, 0) + jnp.where(n < _T5_EXACT, n, large)
    for h in range(A_HEADS):
        acc = jnp.zeros(rel.shape, F32)
        for bk in range(NUM_BUCKETS):
            acc = jnp.where(bucket == bk, rb_ref[bk, h], acc)
        o_ref[h] = (acc - rb_ref[FAR_BUCKET, h]) * LOG2E


def _bias_tile(rel_bias):
    return pl.pallas_call(
        _bias_kernel,
        in_specs=[pl.BlockSpec(memory_space=pltpu.SMEM)],
        out_specs=pl.BlockSpec(memory_space=pltpu.VMEM),
        out_shape=jax.ShapeDtypeStruct((A_HEADS, NEAR + TQ, TQ), F32),
        name="rel_bias_tile",
    )(rel_bias)


def _inproj_kernel(x_ref, mod_ref, wp_ref, wt_ref, wblkT_ref, gkv_ref, gkvT_ref, llb_ref, l1m_ref,
                   qlatT_ref, ckv_ref, ckvT_ref, iqT_ref, ikA_ref, ikB_ref, iwT_ref,
                   hq_ref, hk_ref, hlf_ref, hv_ref, hgate_ref):
    x = x_ref[0]
    sh1 = mod_ref[0, 0:1, :]
    sc1 = mod_ref[0, 1:2, :]
    u = (x * (1.0 + sc1) + sh1).astype(BF16)
    z = jnp.dot(u, wp_ref[0], preferred_element_type=F32)
    zt = _nt_dot(wt_ref[0], u)

    def proj(lo, hi):
        return z[:, lo:hi]

    qlatT_ref[0] = jnp.dot(wblkT_ref[0], zt[_R_AQ:_R_IQ].astype(BF16), preferred_element_type=F32).astype(BF16)

    zc = proj(_C_CKV, _C_IKA)
    inv = lax.rsqrt(jnp.mean(zc * zc, axis=-1, keepdims=True) + RMS_EPS)
    ckv_ref[0] = (zc * inv * gkv_ref[0]).astype(BF16)
    zct = zt[_R_CKV:_R_IW]
    inv_t = lax.rsqrt(jnp.mean(zct * zct, axis=0, keepdims=True) + RMS_EPS)
    ckvT_ref[0, 0:KV_RANK, :] = (zct * inv_t * gkvT_ref[0]).astype(BF16)
    ckvT_ref[0, KV_RANK:KV_EXT, :] = jnp.ones((KV_EXT - KV_RANK, zct.shape[1]), BF16)

    iqT_ref[0] = zt[_R_IQ:_R_CKV].astype(BF16)
    ikA_ref[0] = proj(_C_IKA, _C_IKB).astype(BF16)
    ikB_ref[0] = proj(_C_IKB, _C_HQ).astype(BF16)
    iwT_ref[0] = zt[_R_IW:_R_END] * IDX_W_SCALE

    hq_ref[0] = _silu(proj(_C_HQ, _C_HF))
    zf = proj(_C_HF, _C_HG)
    log_sig = jnp.minimum(zf, 0.0) - jnp.log1p(jnp.exp(-jnp.abs(zf)))
    a = llb_ref[0]
    c = l1m_ref[0] + log_sig
    logf = jnp.maximum(a, c) + jnp.log1p(jnp.exp(-jnp.abs(a - c)))
    hlf_ref[0] = logf
    hk_ref[0] = 1.0 - jnp.exp(logf)
    hgate_ref[0] = _silu(proj(_C_HG, _C_HI))
    hv_ref[0] = proj(_C_HI, _C_END).astype(BF16)


def _inproj(l, x, mod, wp, wt, wblkT, gkv, gkvT, llb, l1m):
    B, S, D = x.shape
    tm = TM_PROJ
    grid = (B, S // tm)
    lw3 = lambda b, i: (l, 0, 0)
    tok = lambda b, i: (b, i, 0)
    tokT = lambda b, i: (b, 0, i)
    hd4 = lambda b, i: (b, 0, i, 0)
    outs = [
        (jax.ShapeDtypeStruct((B, A_HEADS * KV_RANK, S), BF16), pl.BlockSpec((1, A_HEADS * KV_RANK, tm), tokT)),
        (jax.ShapeDtypeStruct((B, S, KV_RANK), BF16), pl.BlockSpec((1, tm, KV_RANK), tok)),
        (jax.ShapeDtypeStruct((B, KV_EXT, S), BF16), pl.BlockSpec((1, KV_EXT, tm), tokT)),
        (jax.ShapeDtypeStruct((B, IDX_HEADS * IDX_DIM, S), BF16), pl.BlockSpec((1, IDX_HEADS * IDX_DIM, tm), tokT)),
        (jax.ShapeDtypeStruct((B, S, LANES), BF16), pl.BlockSpec((1, tm, LANES), tok)),
        (jax.ShapeDtypeStruct((B, S, LANES), BF16), pl.BlockSpec((1, tm, LANES), tok)),
        (jax.ShapeDtypeStruct((B, IDX_HEADS, S), F32), pl.BlockSpec((1, IDX_HEADS, tm), tokT)),
        (jax.ShapeDtypeStruct((B, S, B_FDIM), F32), pl.BlockSpec((1, tm, B_FDIM), tok)),
        (jax.ShapeDtypeStruct((B, S, B_FDIM), F32), pl.BlockSpec((1, tm, B_FDIM), tok)),
        (jax.ShapeDtypeStruct((B, S, B_FDIM), F32), pl.BlockSpec((1, tm, B_FDIM), tok)),
        (jax.ShapeDtypeStruct((B, S, B_WIDTH), BF16), pl.BlockSpec((1, tm, B_WIDTH), tok)),
        (jax.ShapeDtypeStruct((B, S, B_WIDTH), F32), pl.BlockSpec((1, tm, B_WIDTH), tok)),
    ]
    return pl.pallas_call(
        _inproj_kernel,
        grid=grid,
        in_specs=[
            pl.BlockSpec((1, tm, D), tok),
            pl.BlockSpec((1, 6, D), lambda b, i: (b, 0, 0)),
            pl.BlockSpec((1, D, _C_END), lw3),
            pl.BlockSpec((1, _R_END, D), lw3),
            pl.BlockSpec((1, A_HEADS * KV_RANK, A_WIDTH), lw3),
            pl.BlockSpec((1, 1, KV_RANK), lw3),
            pl.BlockSpec((1, KV_RANK, tm), lw3),
            pl.BlockSpec((1, 1, B_FDIM), lw3),
            pl.BlockSpec((1, 1, B_FDIM), lw3),
        ],
        out_specs=[o[1] for o in outs],
        out_shape=[o[0] for o in outs],
        compiler_params=_cparams(("arbitrary", "arbitrary")),
        name="inproj",
    )(x, mod, wp, wt, wblkT, gkv, gkvT, llb, l1m)


def _dsa_kernel(iq_ref, iwT_ref, qlat_ref, ikA_ref, ikB_ref, ckv_ref, ckvT_ref, bn_ref, wuvT_ref, out_ref,
                sc_ref, plane_ref, madd_ref, maddn_ref, la_ref, lb_ref, pma_ref, pmb_ref, ot_ref, yaT_ref,
                *, k_sel, n_idx_bits):
    j = pl.program_id(1)
    q0 = j * TQ
    nk = q0 + TQ
    nunit = (nk + UNIT - 1) // UNIT
    near0 = pl.multiple_of(jnp.maximum(nk - NEAR, 0), TQ)
    bn_row0 = pl.multiple_of(jnp.where(j == 0, TQ, 0), TQ)
    lane = lax.broadcasted_iota(I32, (1, TQ), 1)
    limit = (((q0 + lane) >> 6) + 1) << 6
    row_iota = lax.broadcasted_iota(I32, (UNIT, TQ), 0)

    def unit_rows(u):
        return pl.ds(pl.multiple_of(u * UNIT, UNIT), UNIT)

    iqs = jnp.concatenate([iq_ref[0, p * LANES:(p + 1) * LANES, :] for p in range(IDX_HEADS // 2)], axis=1)
    iw = iwT_ref[0]

    last_unit = sc_ref.shape[0] // UNIT - 1
    half = IDX_HEADS // 2 * TQ

    def issue_scores(u, buf_ref):
        rows = unit_rows(jnp.minimum(u, last_unit))
        buf_ref[:, 0:half] = jnp.dot(ikA_ref[0, rows, :], iqs, preferred_element_type=F32)
        buf_ref[:, half:2 * half] = jnp.dot(ikB_ref[0, rows, :], iqs, preferred_element_type=F32)

    def reduce_scores(u, buf_ref):
        acc = jnp.zeros((UNIT, TQ), F32)
        for p in range(IDX_HEADS // 2):
            acc = acc + iw[2 * p:2 * p + 1, :] * jnp.maximum(buf_ref[:, p * TQ:(p + 1) * TQ], 0.0)
            acc = acc + iw[2 * p + 1:2 * p + 2, :] * jnp.maximum(buf_ref[:, half + p * TQ:half + (p + 1) * TQ], 0.0)
        bits = lax.bitcast_convert_type(acc, I32)
        key = bits ^ ((bits >> 31) & 0x7FFFFFFF)
        sc_ref[unit_rows(u), :] = jnp.where(row_iota + u * UNIT < limit, key, INT_MIN)

    issue_scores(0, la_ref)

    def score_pair(i, carry):
        issue_scores(2 * i + 1, lb_ref)
        reduce_scores(2 * i, la_ref)
        issue_scores(2 * i + 2, la_ref)
        reduce_scores(2 * i + 1, lb_ref)
        return carry

    lax.fori_loop(0, nunit // 2, score_pair, 0)

    @pl.when(nunit % 2 == 1)
    def _():
        reduce_scores(nunit - 1, la_ref)

    ngroups = (nk + PLANE_ROWS - 1) // PLANE_ROWS

    def plane_group(g, carry):
        rows = pl.ds(pl.multiple_of(g * PLANE_ROWS, PLANE_ROWS), PLANE_ROWS)
        words = (sc_ref[rows, :] ^ INT_MIN).reshape(32, SUBLANES, TQ)
        w = [words[i] for i in range(32)]
        j, m = 16, 0x0000FFFF
        while j:
            mask = np.int32(np.uint32(m).view(np.int32))
            k = 0
            while k < 32:
                t = (w[k] ^ lax.shift_right_logical(w[k + j], jnp.full(w[k].shape, j, I32))) & mask
                w[k] = w[k] ^ t
                w[k + j] = w[k + j] ^ (t << j)
                k = (k + j + 1) & ~j
            j >>= 1
            m = (m ^ (m << j)) & 0xFFFFFFFF
        for i in range(32):
            plane_ref[i, pl.ds(g * SUBLANES, SUBLANES), :] = w[i]
        return carry

    lax.fori_loop(0, ngroups, plane_group, 0)

    n_words = sc_ref.shape[0] // PLANE_ROWS * SUBLANES
    group_of_word = lax.broadcasted_iota(I32, (n_words, TQ), 0) // SUBLANES

    def bit_step(i, carry):
        alive, above, t_off, c_ge = carry
        hit = alive & plane_ref[i]
        cnt = above + jnp.sum(lax.population_count(hit), axis=0, keepdims=True)
        ok = cnt >= k_sel
        alive = jnp.where(ok, hit, alive ^ hit)
        above = jnp.where(ok, above, cnt)
        t_off = jnp.where(ok, t_off | (jnp.int32(1) << (31 - i)), t_off)
        return alive, above, t_off, jnp.where(ok, cnt, c_ge)

    zero_row = jnp.zeros((1, TQ), I32)
    _, _, t_off, c_ge = lax.fori_loop(
        0, 32, bit_step,
        (jnp.where(group_of_word < ngroups, jnp.int32(-1), jnp.int32(0)), zero_row, zero_row, zero_row))
    thr = jnp.maximum(t_off ^ INT_MIN, INT_MIN + 1)
    straddle = (c_ge > k_sel).astype(I32)

    def count_where(pred):
        def body(u, acc):
            hit = pred(sc_ref[unit_rows(u), :], u * UNIT).reshape(-1, COUNT_ACCS * SUBLANES, TQ)
            for s in range(hit.shape[0]):
                acc = jnp.where(hit[s], acc + 1, acc)
            return acc
        acc = lax.fori_loop(0, nunit, body, jnp.zeros((COUNT_ACCS * SUBLANES, TQ), I32))
        return jnp.sum(acc, axis=0, keepdims=True)

    def tie_bound():
        c_gt = count_where(lambda blk, r0: blk > thr)
        need = k_sel - c_gt

        def tie_body(i, j0):
            cand = j0 | (jnp.int32(1) << (n_idx_bits - 1 - i))
            cnt = count_where(lambda blk, r0: jnp.where(blk == thr, row_iota + r0, cand) < cand)
            return jnp.where(cnt < need, cand, j0)

        j0 = lax.fori_loop(0, n_idx_bits, tie_body, jnp.zeros((1, TQ), I32))
        return jnp.where(straddle > 0, j0 + 1, jnp.int32(2 ** n_idx_bits))

    jstar = lax.cond(jnp.max(straddle) > 0, tie_bound, lambda: jnp.full((1, TQ), 2 ** n_idx_bits, I32))

    def madd_unit(u, carry):
        rows = unit_rows(u)
        key = sc_ref[rows, :]
        tie_keep = jnp.where(row_iota + u * UNIT < jstar, 0.0, NEG_INF)
        madd_ref[rows, :] = jnp.where(key > thr, 0.0, jnp.where(key == thr, tie_keep, NEG_INF))
        return carry

    lax.fori_loop(0, nunit, madd_unit, 0)
    maddn_ref[...] = madd_ref[pl.ds(near0, NEAR), :]
    madd_ref[pl.ds(near0, NEAR), :] = jnp.full((NEAR, TQ), NEG_INF, F32)

    qall = jnp.concatenate([qlat_ref[0, h * KV_RANK:(h + 1) * KV_RANK, :] for h in range(A_HEADS)], axis=1)

    def col_max(v):
        return jnp.max(v.reshape(v.shape[0] // SUBLANES, SUBLANES, A_HEADS * TQ), axis=0)

    def fold(xl, part_max, ckv_t, m_old):
        m_new = jnp.maximum(m_old, jnp.max(part_max, axis=0, keepdims=True))
        m_use = jnp.where(m_new == NEG_INF, 0.0, m_new)
        p = jnp.exp2(xl - m_use).astype(BF16)
        ot_ref[...] = ot_ref[...] * jnp.exp2(m_old - m_use) + jnp.dot(ckv_t, p, preferred_element_type=F32)
        return m_new

    ot_ref[...] = jnp.zeros(ot_ref.shape, F32)
    near_rows = pl.ds(near0, NEAR)
    xn = jnp.dot(ckv_ref[0, near_rows, :], qall, preferred_element_type=F32)
    xn = xn + jnp.concatenate([maddn_ref[...]] * A_HEADS, axis=1)
    xn = xn + jnp.concatenate([bn_ref[h, pl.ds(bn_row0, NEAR), :] for h in range(A_HEADS)], axis=1)
    m_run = fold(xn, col_max(xn), ckvT_ref[0, :, near_rows], jnp.full((1, A_HEADS * TQ), NEG_INF, F32))


    def issue_logits(u, buf_ref, pm_ref):
        rows = unit_rows(jnp.minimum(u, last_unit))
        xl = jnp.dot(ckv_ref[0, rows, :], qall, preferred_element_type=F32)
        xl = xl + jnp.concatenate([madd_ref[rows, :]] * A_HEADS, axis=1)
        buf_ref[...] = xl
        pm_ref[...] = col_max(xl)

    def consume_logits(u, buf_ref, pm_ref, m_old):
        return fold(buf_ref[...], pm_ref[...], ckvT_ref[0, :, unit_rows(u)], m_old)

    issue_logits(0, la_ref, pma_ref)

    def pair_step(i, m_old):
        issue_logits(2 * i + 1, lb_ref, pmb_ref)
        m_mid = consume_logits(2 * i, la_ref, pma_ref, m_old)
        issue_logits(2 * i + 2, la_ref, pma_ref)
        return consume_logits(2 * i + 1, lb_ref, pmb_ref, m_mid)

    nfar = (near0 + UNIT - 1) // UNIT
    m_run = lax.fori_loop(0, nfar // 2, pair_step, m_run)

    @pl.when(nfar % 2 == 1)
    def _():
        consume_logits(nfar - 1, la_ref, pma_ref, m_run)
    o_t = (ot_ref[0:KV_RANK, :] * (1.0 / ot_ref[KV_RANK:KV_RANK + 1, :])).astype(BF16)
    for h in range(A_HEADS):
        yaT_ref[h * A_HEAD_DIM:(h + 1) * A_HEAD_DIM, :] = jnp.dot(
            wuvT_ref[0, h], o_t[:, h * TQ:(h + 1) * TQ], preferred_element_type=F32)

    out_ref[0] = yaT_ref[...].T.astype(BF16)


def _dsa(l, iq, iwT, qlat, ikA, ikB, ckv, ckvT, bn, wuvT):
    B, S = ckv.shape[0], ckv.shape[1]
    assert S % (2 * UNIT) == 0 and UNIT % TQ == 0 and TQ % CHUNK == 0 and CHUNK == 64 and NEAR <= UNIT
    k_sel = min(IDX_TOPK_MAX, S // 4)
    n_idx_bits = int(math.log2(S))
    assert 2 ** n_idx_bits == S
    grid = (B, S // TQ)
    blk = lambda b, i: (b, 0, i, 0)
    full = lambda b, i: (b, 0, 0)
    kern = functools.partial(_dsa_kernel, k_sel=k_sel, n_idx_bits=n_idx_bits)
    return pl.pallas_call(
        kern,
        grid=grid,
        in_specs=[
            pl.BlockSpec((1, IDX_HEADS * IDX_DIM, TQ), lambda b, i: (b, 0, i)),
            pl.BlockSpec((1, IDX_HEADS, TQ), lambda b, i: (b, 0, i)),
            pl.BlockSpec((1, A_HEADS * KV_RANK, TQ), lambda b, i: (b, 0, i)),
            pl.BlockSpec((1, S, LANES), full),
            pl.BlockSpec((1, S, LANES), full),
            pl.BlockSpec((1, S, KV_RANK), full),
            pl.BlockSpec((1, KV_EXT, S), full),
            pl.BlockSpec((A_HEADS, NEAR + TQ, TQ), lambda b, i: (0, 0, 0)),
            pl.BlockSpec((1, A_HEADS, A_HEAD_DIM, KV_RANK), lambda b, i: (l, 0, 0, 0)),
        ],
        out_specs=pl.BlockSpec((1, TQ, A_WIDTH), lambda b, i: (b, i, 0)),
        out_shape=jax.ShapeDtypeStruct((B, S, A_WIDTH), BF16),
        scratch_shapes=[
            pltpu.VMEM((S, TQ), I32),
            pltpu.VMEM((32, S // PLANE_ROWS * SUBLANES, TQ), I32),
            pltpu.VMEM((S, TQ), F32),
            pltpu.VMEM((NEAR, TQ), F32),
            pltpu.VMEM((UNIT, A_HEADS * TQ), F32),
            pltpu.VMEM((UNIT, A_HEADS * TQ), F32),
            pltpu.VMEM((SUBLANES, A_HEADS * TQ), F32),
            pltpu.VMEM((SUBLANES, A_HEADS * TQ), F32),
            pltpu.VMEM((KV_EXT, A_HEADS * TQ), F32),
            pltpu.VMEM((A_WIDTH, TQ), F32),
        ],
        compiler_params=_cparams(("arbitrary", "arbitrary")),
        name="dsa_attention",
    )(iq, iwT, qlat, ikA, ikB, ckv, ckvT, bn, wuvT)


def _hgrn_constants():
    c = CHUNK
    r = np.arange(c)[:, None]
    jj = np.arange(c)[None, :]
    mats = [(jj <= r), (jj > r)]
    masks = [np.eye(c, dtype=bool)]
    m = c // 2
    while m >= 1:
        start = (r // (2 * m)) * (2 * m)
        bd = start + m - 1
        upper = r > bd
        mats.append(np.where(upper, (jj > bd) & (jj <= r), (jj > r) & (jj <= bd)))
        same_parent = (r // (2 * m)) == (jj // (2 * m))
        masks.append(same_parent & upper & (jj <= (jj // (2 * m)) * (2 * m) + m - 1))
        m //= 2
    m_all = np.concatenate(mats, axis=0).astype(np.float32)
    total = np.zeros((c, c), np.int32)
    for mk in masks:
        total += mk
    assert (total == np.tril(np.ones((c, c), np.int32))).all()
    return np.concatenate([m_all] * 3, axis=1), np.stack(masks).astype(np.float32)


_HGRN_M3, _HGRN_MASKS = _hgrn_constants()
_HGRN_LEVELS = _HGRN_MASKS.shape[0] - 1
HGRN_STEP_CHUNKS = 2
HGRN_STEP_BATCH = 4


def _hgrn_kernel(q_ref, k_ref, lf_ref, v_ref, gate_ref, m3_ref, mask_ref, gn_ref, out_ref, st_ref):
    @pl.when(pl.program_id(1) == 0)
    def _():
        st_ref[...] = jnp.zeros(st_ref.shape, F32)

    c = CHUNK
    intra = {}
    for ci in range(HGRN_STEP_CHUNKS):
        rows = slice(ci * c, (ci + 1) * c)
        for bi in range(HGRN_STEP_BATCH):
            g = lf_ref[bi, rows, :]
            g_hi = g.astype(BF16)
            r1 = g - g_hi.astype(F32)
            g_mid = r1.astype(BF16)
            g_lo = (r1 - g_mid.astype(F32)).astype(BF16)
            sums = jnp.dot(m3_ref[...], jnp.concatenate([g_hi, g_mid, g_lo], axis=0), preferred_element_type=F32)
            e_all = jnp.exp(sums)
            for h in range(B_HEADS):
                cols = slice(h * B_KEY_DIM, (h + 1) * B_KEY_DIM)
                qh = q_ref[bi, rows, cols]
                kh = k_ref[bi, rows, cols]
                att = mask_ref[0] * _nt_dot(qh.astype(BF16), kh.astype(BF16))
                for lv in range(_HGRN_LEVELS):
                    e_l = e_all[(2 + lv) * c:(3 + lv) * c, cols]
                    att = att + mask_ref[lv + 1] * _nt_dot((qh * e_l).astype(BF16), (kh * e_l).astype(BF16))
                e_b = e_all[0:c, cols]
                intra[bi, ci, h] = (jnp.dot(att.astype(BF16), v_ref[bi, rows, cols], preferred_element_type=F32),
                                    (qh * e_b).astype(BF16), (kh * e_all[c:2 * c, cols]).astype(BF16),
                                    e_b[c - 1:c, :])
    for ci in range(HGRN_STEP_CHUNKS):
        rows = slice(ci * c, (ci + 1) * c)
        for bi in range(HGRN_STEP_BATCH):
            for h in range(B_HEADS):
                cols = slice(h * B_KEY_DIM, (h + 1) * B_KEY_DIM)
                o_intra, q_dec, k_rem, decay_all = intra[bi, ci, h]
                st = st_ref[bi, h]
                o = o_intra + _nt_dot(q_dec, st.astype(BF16))
                upd = lax.dot_general(v_ref[bi, rows, cols], k_rem, (((0,), (0,)), ((), ())),
                                      preferred_element_type=F32)
                st_ref[bi, h] = st * decay_all + upd
                o = o * lax.rsqrt(jnp.mean(o * o, axis=-1, keepdims=True) + RMS_EPS) * gn_ref[0]
                out_ref[bi, rows, cols] = (o * gate_ref[bi, rows, cols]).astype(BF16)


def _hgrn(l, hq, hk, hlf, hv, hgate, gnorm):
    B, S, W = hq.shape
    ts = CHUNK * HGRN_STEP_CHUNKS
    nb = HGRN_STEP_BATCH
    assert B % nb == 0
    tok = lambda b, i: (b, i, 0)
    return pl.pallas_call(
        _hgrn_kernel,
        grid=(B // nb, S // ts),
        in_specs=[
            pl.BlockSpec((nb, ts, W), tok),
            pl.BlockSpec((nb, ts, W), tok),
            pl.BlockSpec((nb, ts, W), tok),
            pl.BlockSpec((nb, ts, W), tok),
            pl.BlockSpec((nb, ts, W), tok),
            pl.BlockSpec(_HGRN_M3.shape, lambda b, i: (0, 0)),
            pl.BlockSpec(_HGRN_MASKS.shape, lambda b, i: (0, 0, 0)),
            pl.BlockSpec((1, 1, B_VAL_DIM), lambda b, i: (l, 0, 0)),
        ],
        out_specs=pl.BlockSpec((nb, ts, W), tok),
        out_shape=jax.ShapeDtypeStruct((B, S, W), BF16),
        scratch_shapes=[pltpu.VMEM((nb, B_HEADS, B_VAL_DIM, B_KEY_DIM), F32)],
        compiler_params=_cparams(("arbitrary", "arbitrary")),
        name="hgrn2",
    )(hq, hk, hlf, hv, hgate, jnp.asarray(_HGRN_M3, BF16), jnp.asarray(_HGRN_MASKS), gnorm)


def _layernorm(v, g, b):
    mu = jnp.mean(v, axis=-1, keepdims=True)
    d = v - mu
    var = jnp.mean(d * d, axis=-1, keepdims=True)
    return d * lax.rsqrt(var + LN_EPS) * g + b


def _first_argmax(v, idx, axes, big):
    mx = v
    for ax in axes:
        mx = jnp.max(mx, axis=ax, keepdims=True)
    pos = jnp.where(v == mx, idx, big)
    for ax in axes:
        pos = jnp.min(pos, axis=ax, keepdims=True)
    return mx, pos


def _outproj_kernel(ya_ref, yb_ref, x_ref, mod_ref, wo_ref, lng_ref, lnb_ref, wrT_ref, rbias_ref, tri_ref,
                    x1_ref, u2_ref, gates_ref, rank_ref, gatesT_ref, *, alpha):
    y = jnp.dot(ya_ref[0], wo_ref[0, 0:A_WIDTH, :], preferred_element_type=F32)
    y = y + jnp.dot(yb_ref[0], wo_ref[0, A_WIDTH:, :], preferred_element_type=F32)
    g1 = mod_ref[0, 2:3, :]
    x1 = _layernorm(alpha * x_ref[0] + (1.0 + g1) * y, lng_ref[0], lnb_ref[0])
    x1_ref[0] = x1
    u2 = (x1 * (1.0 + mod_ref[0, 4:5, :]) + mod_ref[0, 3:4, :]).astype(BF16)
    u2_ref[0] = u2

    tm = u2.shape[0]
    gsz = N_EXPERTS // N_GROUPS
    scores = 1.0 / (1.0 + jnp.exp(-_nt_dot(wrT_ref[0], u2)))
    sel = (scores + rbias_ref[0]).reshape(N_GROUPS, gsz, tm)
    scores = scores.reshape(N_GROUPS, gsz, tm)
    i_m = lax.broadcasted_iota(I32, (N_GROUPS, gsz, tm), 1)
    i_g = lax.broadcasted_iota(I32, (N_GROUPS, 1, tm), 0)
    i_e = lax.broadcasted_iota(I32, (N_GROUPS, gsz, tm), 0) * gsz + i_m
    m1, p1 = _first_argmax(sel, i_m, (1,), gsz)
    m2 = jnp.max(jnp.where(i_m == p1, NEG_INF, sel), axis=1, keepdims=True)
    gs = m1 + m2
    gmask = jnp.zeros(gs.shape, F32)
    for _ in range(TOPK_GROUPS):
        _, pg = _first_argmax(gs, i_g, (0,), N_GROUPS)
        hit = i_g == pg
        gmask = jnp.where(hit, 1.0, gmask)
        gs = jnp.where(hit, NEG_INF, gs)
    cand = jnp.where(jnp.broadcast_to(gmask, sel.shape) > 0.0, sel, NEG_INF)
    w = jnp.zeros(sel.shape, F32)
    chosen = jnp.zeros(sel.shape, F32)
    for _ in range(TOP_K):
        _, pe = _first_argmax(cand, i_e, (1, 0), N_EXPERTS)
        hit = i_e == pe
        w = jnp.where(hit, scores, w)
        chosen = jnp.where(hit, 1.0, chosen)
        cand = jnp.where(hit, NEG_INF, cand)
    wsum = jnp.sum(jnp.sum(w, axis=1, keepdims=True), axis=0, keepdims=True)
    gates = (w / wsum * ROUTED_SCALE).reshape(N_EXPERTS, tm)
    g_hi = gates.astype(BF16).astype(F32)
    g_lo = (gates - g_hi).astype(BF16).astype(F32)
    gates_ref[0] = jnp.concatenate([g_hi, g_lo], axis=0).T.astype(BF16)

    chosen2 = chosen.reshape(N_EXPERTS, tm)
    rank = jnp.concatenate(
        [jnp.dot(chosen2[:, g * MOE_GROUP:(g + 1) * MOE_GROUP].astype(BF16), tri_ref[...],
                 preferred_element_type=F32) for g in range(tm // MOE_GROUP)], axis=1)
    rank_ref[0] = jnp.where(chosen2 > 0.0, rank, -1.0).astype(I32)
    gatesT_ref[0] = gates


def _outproj(l, ya, yb, x, mod, wo, ln_g, ln_b, wrT, rbias, alpha):
    B, S, D = x.shape
    tm = TM_PROJ
    tok = lambda b, i: (b, i, 0)
    lw3 = lambda b, i: (l, 0, 0)
    return pl.pallas_call(
        functools.partial(_outproj_kernel, alpha=alpha),
        grid=(B, S // tm),
        in_specs=[
            pl.BlockSpec((1, tm, A_WIDTH), tok),
            pl.BlockSpec((1, tm, B_WIDTH), tok),
            pl.BlockSpec((1, tm, D), tok),
            pl.BlockSpec((1, 6, D), lambda b, i: (b, 0, 0)),
            pl.BlockSpec((1, D, D), lw3),
            pl.BlockSpec((1, 1, D), lw3),
            pl.BlockSpec((1, 1, D), lw3),
            pl.BlockSpec((1, N_EXPERTS, D), lw3),
            pl.BlockSpec((1, N_EXPERTS, tm), lw3),
            pl.BlockSpec((MOE_GROUP, MOE_GROUP), lambda b, i: (0, 0)),
        ],
        out_specs=[pl.BlockSpec((1, tm, D), tok), pl.BlockSpec((1, tm, D), tok),
                   pl.BlockSpec((1, tm, 2 * N_EXPERTS), tok),
                   pl.BlockSpec((1, N_EXPERTS, tm), lambda b, i: (b, 0, i)),
                   pl.BlockSpec((1, N_EXPERTS, tm), lambda b, i: (b, 0, i))],
        out_shape=[jax.ShapeDtypeStruct((B, S, D), F32), jax.ShapeDtypeStruct((B, S, D), BF16),
                   jax.ShapeDtypeStruct((B, S, 2 * N_EXPERTS), BF16),
                   jax.ShapeDtypeStruct((B, N_EXPERTS, S), I32),
                   jax.ShapeDtypeStruct((B, N_EXPERTS, S), F32)],
        compiler_params=_cparams(("arbitrary", "arbitrary")),
        name="outproj_router",
    )(ya, yb, x, mod, wo, ln_g, ln_b, wrT, rbias,
      jnp.asarray(np.triu(np.ones((MOE_GROUP, MOE_GROUP), np.float32), 1), BF16))


MOE_CHUNK_EXPERTS = 8


def _slot_onehot(rank_rows, values):
    row = lax.broadcasted_iota(I32, (MOE_CAP, rank_rows.shape[1]), 0)
    return jnp.concatenate(
        [jnp.where(row == rank_rows[e:e + 1, :], values[e:e + 1, :], 0.0) for e in range(rank_rows.shape[0])], axis=0)


def _dispatch_kernel(u_ref, rank_ref, x_ref):
    u = u_ref[...]
    ones = jnp.ones((MOE_CHUNK_EXPERTS, MOE_GROUP), F32)
    for c in range(N_EXPERTS // MOE_CHUNK_EXPERTS):
        es = slice(c * MOE_CHUNK_EXPERTS, (c + 1) * MOE_CHUNK_EXPERTS)
        onehot = _slot_onehot(rank_ref[0, es, :], ones)
        xs = jnp.dot(onehot.astype(BF16), u, preferred_element_type=F32).astype(BF16)
        x_ref[es] = xs.reshape(MOE_CHUNK_EXPERTS, MOE_CAP, -1)


def _dispatch(u2, rank):
    T, D = u2.shape
    ng = T // MOE_GROUP
    gps = rank.shape[-1] // MOE_GROUP
    return pl.pallas_call(
        _dispatch_kernel,
        grid=(ng,),
        in_specs=[pl.BlockSpec((MOE_GROUP, D), lambda g: (g, 0)),
                  pl.BlockSpec((1, N_EXPERTS, MOE_GROUP), lambda g: (g // gps, 0, g % gps))],
        out_specs=pl.BlockSpec((N_EXPERTS, MOE_CAP, D), lambda g: (0, g, 0)),
        out_shape=jax.ShapeDtypeStruct((N_EXPERTS, ng * MOE_CAP, D), BF16),
        compiler_params=_cparams(("arbitrary",)),
        name="moe_dispatch",
    )(u2, rank)


def _expert_kernel(x_ref, wgu_ref, wd_ref, y_ref):
    hgu = jnp.dot(x_ref[0], wgu_ref[0, 0], preferred_element_type=F32)
    h = _silu(hgu[:, :EXPERT_DIM]) * hgu[:, EXPERT_DIM:]
    y_ref[0] = jnp.dot(h.astype(BF16), wd_ref[0, 0], preferred_element_type=F32).astype(BF16)


def _experts(l, xs, wgu, wd):
    E, R, D = xs.shape
    tr = min(R, MOE_EXPERT_ROWS)
    assert R % tr == 0
    return pl.pallas_call(
        _expert_kernel,
        grid=(E, R // tr),
        in_specs=[pl.BlockSpec((1, tr, D), lambda e, i: (e, i, 0)),
                  pl.BlockSpec((1, 1, D, 2 * EXPERT_DIM), lambda e, i: (l, e, 0, 0)),
                  pl.BlockSpec((1, 1, EXPERT_DIM, D), lambda e, i: (l, e, 0, 0))],
        out_specs=pl.BlockSpec((1, tr, D), lambda e, i: (e, i, 0)),
        out_shape=jax.ShapeDtypeStruct((E, R, D), BF16),
        compiler_params=_cparams(("arbitrary", "arbitrary")),
        name="moe_experts",
    )(xs, wgu, wd)


def _combine_kernel(y_ref, rank_ref, gates_ref, u_ref, x1_ref, mod_ref, sgu_ref, sd_ref, lng_ref, lnb_ref,
                    out_ref, *, alpha):
    hgu = jnp.dot(u_ref[...], sgu_ref[0], preferred_element_type=F32)
    hs = _silu(hgu[:, :SHARED_DIM]) * hgu[:, SHARED_DIM:]
    y = jnp.dot(hs.astype(BF16), sd_ref[0], preferred_element_type=F32)
    for c in range(N_EXPERTS // MOE_CHUNK_EXPERTS):
        es = slice(c * MOE_CHUNK_EXPERTS, (c + 1) * MOE_CHUNK_EXPERTS)
        pick = _slot_onehot(rank_ref[0, es, :], gates_ref[0, es, :])
        ys = y_ref[es].reshape(MOE_CHUNK_EXPERTS * MOE_CAP, -1)
        y = y + lax.dot_general(pick.astype(BF16), ys, (((0,), (0,)), ((), ())), preferred_element_type=F32)
    g2 = mod_ref[0, 5:6, :]
    out_ref[...] = _layernorm(alpha * x1_ref[...] + (1.0 + g2) * y, lng_ref[0], lnb_ref[0])


def _combine(l, ys, rank, gates_t, u2, x1, mod, sgu, sd, ln_g, ln_b, alpha, seq):
    T, D = u2.shape
    tok = lambda g: (g, 0)
    lw3 = lambda g: (l, 0, 0)
    gps = seq // MOE_GROUP
    per_group = lambda g: (g // gps, 0, g % gps)
    return pl.pallas_call(
        functools.partial(_combine_kernel, alpha=alpha),
        grid=(T // MOE_GROUP,),
        in_specs=[
            pl.BlockSpec((N_EXPERTS, MOE_CAP, D), lambda g: (0, g, 0)),
            pl.BlockSpec((1, N_EXPERTS, MOE_GROUP), per_group),
            pl.BlockSpec((1, N_EXPERTS, MOE_GROUP), per_group),
            pl.BlockSpec((MOE_GROUP, D), tok),
            pl.BlockSpec((MOE_GROUP, D), tok),
            pl.BlockSpec((1, 6, D), lambda g: ((g * MOE_GROUP) // seq, 0, 0)),
            pl.BlockSpec((1, D, 2 * SHARED_DIM), lw3),
            pl.BlockSpec((1, SHARED_DIM, D), lw3),
            pl.BlockSpec((1, 1, D), lw3),
            pl.BlockSpec((1, 1, D), lw3),
        ],
        out_specs=pl.BlockSpec((MOE_GROUP, D), tok),
        out_shape=jax.ShapeDtypeStruct((T, D), F32),
        compiler_params=_cparams(("arbitrary",)),
        name="moe_combine",
    )(ys, rank, gates_t, u2, x1, mod, sgu, sd, ln_g, ln_b)


def _moe_kernel(u_ref, gates_ref, x1_ref, mod_ref, wgu_ref, wd_ref, sgu_ref, sd_ref, lng_ref, lnb_ref,
                out_ref, acc_ref, *, alpha):
    s = pl.program_id(1)
    u = u_ref[...]

    def hidden(wgu):
        hgu = jnp.dot(u, wgu, preferred_element_type=F32)
        return _silu(hgu[:, :EXPERT_DIM]) * hgu[:, EXPERT_DIM:]

    @pl.when(s == 0)
    def _():
        acc_ref[...] = jnp.dot(hidden(sgu_ref[0]).astype(BF16), sd_ref[0], preferred_element_type=F32)

    rows = lax.broadcasted_iota(I32, (2 * N_EXPERTS, MOE_EXPERTS_PER_STEP * EXPERT_DIM), 0) & (N_EXPERTS - 1)
    cols = lax.broadcasted_iota(I32, (2 * N_EXPERTS, MOE_EXPERTS_PER_STEP * EXPERT_DIM), 1)
    onehot = jnp.where(rows == s * MOE_EXPERTS_PER_STEP + cols // EXPERT_DIM, 1.0, 0.0).astype(BF16)
    gate = jnp.dot(gates_ref[...], onehot, preferred_element_type=F32)
    h = jnp.concatenate(
        [(hidden(wgu_ref[0, k]) * gate[:, k * EXPERT_DIM:(k + 1) * EXPERT_DIM]).astype(BF16)
         for k in range(MOE_EXPERTS_PER_STEP)], axis=1)
    wd = wd_ref[0].reshape(MOE_EXPERTS_PER_STEP * EXPERT_DIM, wd_ref.shape[-1])
    acc_ref[...] += jnp.dot(h, wd, preferred_element_type=F32)

    @pl.when(s == pl.num_programs(1) - 1)
    def _():
        g2 = mod_ref[0, 5:6, :]
        out_ref[...] = _layernorm(alpha * x1_ref[...] + (1.0 + g2) * acc_ref[...], lng_ref[0], lnb_ref[0])


def _moe(l, u2, gates, x1, mod, wgu, wd, sgu, sd, ln_g, ln_b, alpha, seq):
    T, D = u2.shape
    tm = TM_MOE
    assert seq % tm == 0
    tok = lambda i, e: (i, 0)
    lw3 = lambda i, e: (l, 0, 0)
    return pl.pallas_call(
        functools.partial(_moe_kernel, alpha=alpha),
        grid=(T // tm, N_EXPERTS // MOE_EXPERTS_PER_STEP),
        in_specs=[
            pl.BlockSpec((tm, D), tok),
            pl.BlockSpec((tm, 2 * N_EXPERTS), tok),
            pl.BlockSpec((tm, D), tok),
            pl.BlockSpec((1, 6, D), lambda i, e: ((i * tm) // seq, 0, 0)),
            pl.BlockSpec((1, MOE_EXPERTS_PER_STEP, D, 2 * EXPERT_DIM), lambda i, e: (l, e, 0, 0)),
            pl.BlockSpec((1, MOE_EXPERTS_PER_STEP, EXPERT_DIM, D), lambda i, e: (l, e, 0, 0)),
            pl.BlockSpec((1, D, 2 * SHARED_DIM), lw3),
            pl.BlockSpec((1, SHARED_DIM, D), lw3),
            pl.BlockSpec((1, 1, D), lw3),
            pl.BlockSpec((1, 1, D), lw3),
        ],
        out_specs=pl.BlockSpec((tm, D), tok),
        out_shape=jax.ShapeDtypeStruct((T, D), F32),
        scratch_shapes=[pltpu.VMEM((tm, D), F32)],
        compiler_params=_cparams(("arbitrary", "arbitrary")),
        name="moe_dense",
    )(u2, gates, x1, mod, wgu, wd, sgu, sd, ln_g, ln_b)


def _prepare_params(w_in, kv_norm_g, w_uk, w_uv, hgrn_lb, w_out, w_router, router_bias,
                    w_gate, w_up, w_down, ws_gate, ws_up, ws_down):
    L = w_in.shape[0]
    sizes = (A_WIDTH, KV_RANK, IDX_HEADS * IDX_DIM, IDX_DIM, IDX_HEADS, B_FDIM, B_FDIM, B_WIDTH, B_WIDTH)
    offs = np.concatenate([[0], np.cumsum(sizes)])
    seg = lambda i: w_in[:, :, offs[i]:offs[i + 1]]
    w_aq, w_ckv, w_iq, w_ik, w_iw, w_hq, w_hf, w_hi, w_hg = (seg(i) for i in range(9))
    zik = jnp.zeros_like(w_ik)
    wp = jnp.concatenate([w_ckv, w_ik, zik, zik, w_ik, w_hq, w_hf, w_hg, w_hi], axis=-1).astype(BF16)
    wt = jnp.swapaxes(jnp.concatenate([w_aq, w_iq, w_ckv, w_iw], axis=-1), 1, 2).astype(BF16)
    assert wp.shape[-1] == _C_END and wt.shape[1] == _R_END
    eye = jnp.eye(A_HEADS, dtype=F32)
    wblk = (jnp.einsum('lhdr,hg->lhdgr', w_uk * (ATTN_SCALE * LOG2E), eye)
            .reshape(L, A_WIDTH, A_HEADS * KV_RANK).astype(BF16))
    p = dict(
        wp=wp, wt=wt, wblkT=jnp.swapaxes(wblk, 1, 2),
        gkv=kv_norm_g.reshape(L, 1, KV_RANK),
        gkvT=jnp.broadcast_to(kv_norm_g[:, :, None], (L, KV_RANK, TM_PROJ)),
        wuvT=jnp.swapaxes(w_uv, 2, 3).astype(BF16),
        wo=w_out.astype(BF16),
        wrT=jnp.swapaxes(w_router, 1, 2).astype(BF16),
        rbias=jnp.broadcast_to(router_bias[:, :, None], (L, N_EXPERTS, TM_PROJ)),
        wgu=jnp.concatenate([w_gate, w_up], axis=-1).astype(BF16),
        wd=w_down.astype(BF16),
        sgu=jnp.concatenate([ws_gate, ws_up], axis=-1).astype(BF16),
        sd=ws_down.astype(BF16),
    )
    lbs = jnp.cumsum(jax.nn.softmax(hgrn_lb.astype(F32), axis=0), axis=0)
    lbs = jnp.clip(lbs - lbs[0:1], 0.0, 1.0 - 1e-6)
    p["llb"] = jnp.log(lbs).reshape(L, 1, B_FDIM)
    p["l1m"] = jnp.log1p(-lbs).reshape(L, 1, B_FDIM)
    return p


def kernel(x, c, w_ada, b_ada, w_in, kv_norm_g, w_uk, w_uv, rel_bias, hgrn_lb, gnorm_g, w_out, ln1_g, ln1_b,
           w_router, router_bias, w_gate, w_up, w_down, ws_gate, ws_up, ws_down, ln2_g, ln2_b):
    B, S, D = x.shape
    L = w_in.shape[0]
    alpha = (2 * L) ** 0.25
    p = _prepare_params(w_in, kv_norm_g, w_uk, w_uv, hgrn_lb, w_out, w_router, router_bias,
                        w_gate, w_up, w_down, ws_gate, ws_up, ws_down)
    mods = _adaln(c, w_ada, b_ada).reshape(L, B, 6, D)
    bn = _bias_tile(rel_bias)
    gn = gnorm_g.reshape(L, 1, B_VAL_DIM)
    ln1g, ln1b = ln1_g.reshape(L, 1, D), ln1_b.reshape(L, 1, D)
    ln2g, ln2b = ln2_g.reshape(L, 1, D), ln2_b.reshape(L, 1, D)
    for l in range(L):
        mod = mods[l]
        (qlat, ckv, ckvT, iq, ikA, ikB, iwT, hq, hk, hlf, hv, hgate) = _inproj(
            l, x, mod, p["wp"], p["wt"], p["wblkT"], p["gkv"], p["gkvT"], p["llb"], p["l1m"])
        ya = _dsa(l, iq, iwT, qlat, ikA, ikB, ckv, ckvT, bn, p["wuvT"])
        yb = _hgrn(l, hq, hk, hlf, hv, hgate, gn)
        x1, u2, gates, rank, gates_t = _outproj(l, ya, yb, x, mod, p["wo"], ln1g, ln1b, p["wrT"], p["rbias"], alpha)
        u2f, x1f = u2.reshape(B * S, D), x1.reshape(B * S, D)

        def moe_sparse(l=l, mod=mod, u2f=u2f, x1f=x1f, rank=rank, gates_t=gates_t):
            ys = _experts(l, _dispatch(u2f, rank), p["wgu"], p["wd"])
            return _combine(l, ys, rank, gates_t, u2f, x1f, mod, p["sgu"], p["sd"], ln2g, ln2b, alpha, S)

        def moe_dense(l=l, mod=mod, u2f=u2f, x1f=x1f, gates=gates):
            return _moe(l, u2f, gates.reshape(B * S, 2 * N_EXPERTS), x1f, mod,
                        p["wgu"], p["wd"], p["sgu"], p["sd"], ln2g, ln2b, alpha, S)

        x = lax.cond(jnp.any(rank >= MOE_CAP), moe_dense, moe_sparse).reshape(B, S, D)
    return x
```

```python
import functools
import math

import numpy as np
import jax
import jax.numpy as jnp
from jax import lax
from jax.experimental import pallas as pl
from jax.experimental.pallas import tpu as pltpu

F32 = jnp.float32
BF16 = jnp.bfloat16
I32 = jnp.int32

D_MODEL = 1024
CHUNK = 64
A_HEADS = 8
A_HEAD_DIM = 64
A_WIDTH = A_HEADS * A_HEAD_DIM
KV_RANK = 128
IDX_HEADS = 8
IDX_DIM = 64
IDX_TOPK_MAX = 256
IDX_W_SCALE = (IDX_HEADS ** -0.5) * (IDX_DIM ** -0.5)
ATTN_SCALE = A_HEAD_DIM ** -0.5
LOG2E = math.log2(math.e)
KV_EXT = KV_RANK + 16
NUM_BUCKETS = 32
MAX_DISTANCE = 128
B_HEADS = 4
B_KEY_DIM = 128
B_VAL_DIM = 128
B_WIDTH = B_HEADS * B_VAL_DIM
B_FDIM = B_HEADS * B_KEY_DIM
N_EXPERTS = 64
TOP_K = 8
N_GROUPS = 8
TOPK_GROUPS = 4
EXPERT_DIM = 256
SHARED_DIM = 256
ROUTED_SCALE = 2.5
LN_EPS = 1e-5
RMS_EPS = 1e-6

LANES = 128
SUBLANES = 8
VMEM_LIMIT_BYTES = 56 * 1024 * 1024

INT_MIN = -(2 ** 31)
NEG_INF = float("-inf")

TM_PROJ = 1024
TQ = 128
UNIT = 512
NEAR = 2 * TQ
COUNT_ACCS = 8
PLANE_ROWS = 32 * SUBLANES
TM_MOE = 1024
MOE_EXPERTS_PER_STEP = 4
MOE_GROUP = 256
MOE_CAPS = (64, 80)
MOE_EXPERT_STEPS = 4

_C_CKV, _C_IKA, _C_IKB, _C_HQ, _C_HF, _C_HG, _C_HI, _C_END = (0, 128, 256, 384, 896, 1408, 1920, 2432)
_R_AQ, _R_IQ, _R_CKV, _R_IW, _R_END = (0, 512, 1024, 1152, 1160)


def _silu(v):
    return v * (1.0 / (1.0 + jnp.exp(-v)))


def _nt_dot(a, b):
    return lax.dot_general(a, b, (((1,), (1,)), ((), ())), preferred_element_type=F32)


def _cparams(sem):
    return pltpu.CompilerParams(dimension_semantics=sem, vmem_limit_bytes=VMEM_LIMIT_BYTES)


def _adaln_kernel(c_ref, w_ref, b_ref, o_ref):
    cond = _silu(c_ref[...])
    o_ref[0] = jnp.dot(cond.astype(BF16), w_ref[0].astype(BF16), preferred_element_type=F32) + b_ref[0]


def _adaln(c, w_ada, b_ada):
    L, D, D6 = w_ada.shape
    B = c.shape[0]
    nb = D6 // D
    return pl.pallas_call(
        _adaln_kernel,
        grid=(L, nb),
        in_specs=[
            pl.BlockSpec((B, D), lambda l, j: (0, 0)),
            pl.BlockSpec((1, D, D), lambda l, j: (l, 0, j)),
            pl.BlockSpec((1, 1, D), lambda l, j: (l, 0, j)),
        ],
        out_specs=pl.BlockSpec((1, B, D), lambda l, j: (l, 0, j)),
        out_shape=jax.ShapeDtypeStruct((L, B, D6), F32),
        compiler_params=_cparams(("arbitrary", "arbitrary")),
        name="adaln_mod",
    )(c, w_ada, b_ada.reshape(L, 1, D6))


_T5_NB = NUM_BUCKETS // 2
_T5_EXACT = _T5_NB // 2
_T5_THRESHOLDS = tuple(
    int(math.ceil(_T5_EXACT * (MAX_DISTANCE / _T5_EXACT) ** (j / (_T5_NB - _T5_EXACT)) - 1e-9))
    for j in range(1, _T5_NB - _T5_EXACT))
FAR_BUCKET = _T5_NB - 1
assert _T5_THRESHOLDS[-1] <= TQ, "keys further than one query block behind must share the far bucket"


def _bias_kernel(rb_ref, o_ref):
    kr = lax.broadcasted_iota(I32, (NEAR + TQ, TQ), 0)
    ql = lax.broadcasted_iota(I32, (NEAR + TQ, TQ), 1)
    rel = kr - TQ - ql
    n = jnp.abs(rel)
    large = jnp.full(rel.shape, _T5_EXACT, I32)
    for t in _T5_THRESHOLDS:
        large = large + (n >= t).astype(I32)
    bucket = jnp.where(rel > 0, _T5_NB, 0) + jnp.where(n < _T5_EXACT, n, large)
    for h in range(A_HEADS):
        acc = jnp.zeros(rel.shape, F32)
        for bk in range(NUM_BUCKETS):
            acc = jnp.where(bucket == bk, rb_ref[bk, h], acc)
        o_ref[h] = (acc - rb_ref[FAR_BUCKET, h]) * LOG2E


def _bias_tile(rel_bias):
    return pl.pallas_call(
        _bias_kernel,
        in_specs=[pl.BlockSpec(memory_space=pltpu.SMEM)],
        out_specs=pl.BlockSpec(memory_space=pltpu.VMEM),
        out_shape=jax.ShapeDtypeStruct((A_HEADS, NEAR + TQ, TQ), F32),
        name="rel_bias_tile",
    )(rel_bias)


def _inproj_kernel(x_ref, mod_ref, wp_ref, wt_ref, wblkT_ref, gkv_ref, gkvT_ref, llb_ref, l1m_ref,
                   qlatT_ref, ckv_ref, ckvT_ref, iqT_ref, ikA_ref, ikB_ref, iwT_ref,
                   hq_ref, hk_ref, hlf_ref, hv_ref, hgate_ref):
    x = x_ref[0]
    sh1 = mod_ref[0, 0:1, :]
    sc1 = mod_ref[0, 1:2, :]
    u = (x * (1.0 + sc1) + sh1).astype(BF16)
    z = jnp.dot(u, wp_ref[0], preferred_element_type=F32)
    zt = _nt_dot(wt_ref[0], u)

    def proj(lo, hi):
        return z[:, lo:hi]

    qlatT_ref[0] = jnp.dot(wblkT_ref[0], zt[_R_AQ:_R_IQ].astype(BF16), preferred_element_type=F32).astype(BF16)

    zc = proj(_C_CKV, _C_IKA)
    inv = lax.rsqrt(jnp.mean(zc * zc, axis=-1, keepdims=True) + RMS_EPS)
    ckv_ref[0] = (zc * inv * gkv_ref[0]).astype(BF16)
    zct = zt[_R_CKV:_R_IW]
    inv_t = lax.rsqrt(jnp.mean(zct * zct, axis=0, keepdims=True) + RMS_EPS)
    ckvT_ref[0, 0:KV_RANK, :] = (zct * inv_t * gkvT_ref[0]).astype(BF16)
    ckvT_ref[0, KV_RANK:KV_EXT, :] = jnp.ones((KV_EXT - KV_RANK, zct.shape[1]), BF16)

    iqT_ref[0] = zt[_R_IQ:_R_CKV].astype(BF16)
    ikA_ref[0] = proj(_C_IKA, _C_IKB).astype(BF16)
    ikB_ref[0] = proj(_C_IKB, _C_HQ).astype(BF16)
    iwT_ref[0] = zt[_R_IW:_R_END] * IDX_W_SCALE

    hq_ref[0] = _silu(proj(_C_HQ, _C_HF))
    zf = proj(_C_HF, _C_HG)
    log_sig = jnp.minimum(zf, 0.0) - jnp.log1p(jnp.exp(-jnp.abs(zf)))
    a = llb_ref[0]
    c = l1m_ref[0] + log_sig
    logf = jnp.maximum(a, c) + jnp.log1p(jnp.exp(-jnp.abs(a - c)))
    hlf_ref[0] = logf
    hk_ref[0] = 1.0 - jnp.exp(logf)
    hgate_ref[0] = _silu(proj(_C_HG, _C_HI))
    hv_ref[0] = proj(_C_HI, _C_END).astype(BF16)


def _inproj(l, x, mod, wp, wt, wblkT, gkv, gkvT, llb, l1m):
    B, S, D = x.shape
    tm = TM_PROJ
    grid = (B, S // tm)
    lw3 = lambda b, i: (l, 0, 0)
    tok = lambda b, i: (b, i, 0)
    tokT = lambda b, i: (b, 0, i)
    hd4 = lambda b, i: (b, 0, i, 0)
    outs = [
        (jax.ShapeDtypeStruct((B, A_HEADS * KV_RANK, S), BF16), pl.BlockSpec((1, A_HEADS * KV_RANK, tm), tokT)),
        (jax.ShapeDtypeStruct((B, S, KV_RANK), BF16), pl.BlockSpec((1, tm, KV_RANK), tok)),
        (jax.ShapeDtypeStruct((B, KV_EXT, S), BF16), pl.BlockSpec((1, KV_EXT, tm), tokT)),
        (jax.ShapeDtypeStruct((B, IDX_HEADS * IDX_DIM, S), BF16), pl.BlockSpec((1, IDX_HEADS * IDX_DIM, tm), tokT)),
        (jax.ShapeDtypeStruct((B, S, LANES), BF16), pl.BlockSpec((1, tm, LANES), tok)),
        (jax.ShapeDtypeStruct((B, S, LANES), BF16), pl.BlockSpec((1, tm, LANES), tok)),
        (jax.ShapeDtypeStruct((B, IDX_HEADS, S), F32), pl.BlockSpec((1, IDX_HEADS, tm), tokT)),
        (jax.ShapeDtypeStruct((B, S, B_FDIM), F32), pl.BlockSpec((1, tm, B_FDIM), tok)),
        (jax.ShapeDtypeStruct((B, S, B_FDIM), F32), pl.BlockSpec((1, tm, B_FDIM), tok)),
        (jax.ShapeDtypeStruct((B, S, B_FDIM), F32), pl.BlockSpec((1, tm, B_FDIM), tok)),
        (jax.ShapeDtypeStruct((B, S, B_WIDTH), BF16), pl.BlockSpec((1, tm, B_WIDTH), tok)),
        (jax.ShapeDtypeStruct((B, S, B_WIDTH), F32), pl.BlockSpec((1, tm, B_WIDTH), tok)),
    ]
    return pl.pallas_call(
        _inproj_kernel,
        grid=grid,
        in_specs=[
            pl.BlockSpec((1, tm, D), tok),
            pl.BlockSpec((1, 6, D), lambda b, i: (b, 0, 0)),
            pl.BlockSpec((1, D, _C_END), lw3),
            pl.BlockSpec((1, _R_END, D), lw3),
            pl.BlockSpec((1, A_HEADS * KV_RANK, A_WIDTH), lw3),
            pl.BlockSpec((1, 1, KV_RANK), lw3),
            pl.BlockSpec((1, KV_RANK, tm), lw3),
            pl.BlockSpec((1, 1, B_FDIM), lw3),
            pl.BlockSpec((1, 1, B_FDIM), lw3),
        ],
        out_specs=[o[1] for o in outs],
        out_shape=[o[0] for o in outs],
        compiler_params=_cparams(("arbitrary", "arbitrary")),
        name="inproj",
    )(x, mod, wp, wt, wblkT, gkv, gkvT, llb, l1m)


def _dsa_kernel(iq_ref, iwT_ref, qlat_ref, ikA_ref, ikB_ref, ckv_ref, ckvT_ref, bn_ref, wuvT_ref, out_ref,
                sc_ref, plane_ref, madd_ref, maddn_ref, la_ref, lb_ref, pma_ref, pmb_ref, ot_ref, yaT_ref,
                *, k_sel, n_idx_bits):
    j = pl.program_id(1)
    q0 = j * TQ
    nk = q0 + TQ
    nunit = (nk + UNIT - 1) // UNIT
    near0 = pl.multiple_of(jnp.maximum(nk - NEAR, 0), TQ)
    bn_row0 = pl.multiple_of(jnp.where(j == 0, TQ, 0), TQ)
    lane = lax.broadcasted_iota(I32, (1, TQ), 1)
    limit = (((q0 + lane) >> 6) + 1) << 6
    row_iota = lax.broadcasted_iota(I32, (UNIT, TQ), 0)

    def unit_rows(u):
        return pl.ds(pl.multiple_of(u * UNIT, UNIT), UNIT)

    iqs = jnp.concatenate([iq_ref[0, p * LANES:(p + 1) * LANES, :] for p in range(IDX_HEADS // 2)], axis=1)
    iw = iwT_ref[0]

    last_unit = sc_ref.shape[0] // UNIT - 1
    half = IDX_HEADS // 2 * TQ

    def issue_scores(u, buf_ref):
        rows = unit_rows(jnp.minimum(u, last_unit))
        buf_ref[:, 0:half] = jnp.dot(ikA_ref[0, rows, :], iqs, preferred_element_type=F32)
        buf_ref[:, half:2 * half] = jnp.dot(ikB_ref[0, rows, :], iqs, preferred_element_type=F32)

    def reduce_scores(u, buf_ref):
        acc = jnp.zeros((UNIT, TQ), F32)
        for p in range(IDX_HEADS // 2):
            acc = acc + iw[2 * p:2 * p + 1, :] * jnp.maximum(buf_ref[:, p * TQ:(p + 1) * TQ], 0.0)
            acc = acc + iw[2 * p + 1:2 * p + 2, :] * jnp.maximum(buf_ref[:, half + p * TQ:half + (p + 1) * TQ], 0.0)
        bits = lax.bitcast_convert_type(acc, I32)
        key = bits ^ ((bits >> 31) & 0x7FFFFFFF)
        sc_ref[unit_rows(u), :] = jnp.where(row_iota + u * UNIT < limit, key, INT_MIN)

    issue_scores(0, la_ref)

    def score_pair(i, carry):
        issue_scores(2 * i + 1, lb_ref)
        reduce_scores(2 * i, la_ref)
        issue_scores(2 * i + 2, la_ref)
        reduce_scores(2 * i + 1, lb_ref)
        return carry

    lax.fori_loop(0, nunit // 2, score_pair, 0)

    @pl.when(nunit % 2 == 1)
    def _():
        reduce_scores(nunit - 1, la_ref)

    ngroups = (nk + PLANE_ROWS - 1) // PLANE_ROWS

    def plane_group(g, carry):
        rows = pl.ds(pl.multiple_of(g * PLANE_ROWS, PLANE_ROWS), PLANE_ROWS)
        words = (sc_ref[rows, :] ^ INT_MIN).reshape(32, SUBLANES, TQ)
        w = [words[i] for i in range(32)]
        j, m = 16, 0x0000FFFF
        while j:
            mask = np.int32(np.uint32(m).view(np.int32))
            k = 0
            while k < 32:
                t = (w[k] ^ lax.shift_right_logical(w[k + j], jnp.full(w[k].shape, j, I32))) & mask
                w[k] = w[k] ^ t
                w[k + j] = w[k + j] ^ (t << j)
                k = (k + j + 1) & ~j
            j >>= 1
            m = (m ^ (m << j)) & 0xFFFFFFFF
        for i in range(32):
            plane_ref[i, pl.ds(g * SUBLANES, SUBLANES), :] = w[i]
        return carry

    lax.fori_loop(0, ngroups, plane_group, 0)

    n_words = sc_ref.shape[0] // PLANE_ROWS * SUBLANES
    group_of_word = lax.broadcasted_iota(I32, (n_words, TQ), 0) // SUBLANES

    def bit_step(i, carry):
        alive, above, t_off, c_ge = carry
        hit = alive & plane_ref[i]
        cnt = above + jnp.sum(lax.population_count(hit), axis=0, keepdims=True)
        ok = cnt >= k_sel
        alive = jnp.where(ok, hit, alive ^ hit)
        above = jnp.where(ok, above, cnt)
        t_off = jnp.where(ok, t_off | (jnp.int32(1) << (31 - i)), t_off)
        return alive, above, t_off, jnp.where(ok, cnt, c_ge)

    zero_row = jnp.zeros((1, TQ), I32)
    _, _, t_off, c_ge = lax.fori_loop(
        0, 32, bit_step,
        (jnp.where(group_of_word < ngroups, jnp.int32(-1), jnp.int32(0)), zero_row, zero_row, zero_row))
    thr = jnp.maximum(t_off ^ INT_MIN, INT_MIN + 1)
    straddle = (c_ge > k_sel).astype(I32)

    def count_where(pred):
        def body(u, acc):
            hit = pred(sc_ref[unit_rows(u), :], u * UNIT).reshape(-1, COUNT_ACCS * SUBLANES, TQ)
            for s in range(hit.shape[0]):
                acc = jnp.where(hit[s], acc + 1, acc)
            return acc
        acc = lax.fori_loop(0, nunit, body, jnp.zeros((COUNT_ACCS * SUBLANES, TQ), I32))
        return jnp.sum(acc, axis=0, keepdims=True)

    def tie_bound():
        c_gt = count_where(lambda blk, r0: blk > thr)
        need = k_sel - c_gt

        def tie_body(i, j0):
            cand = j0 | (jnp.int32(1) << (n_idx_bits - 1 - i))
            cnt = count_where(lambda blk, r0: jnp.where(blk == thr, row_iota + r0, cand) < cand)
            return jnp.where(cnt < need, cand, j0)

        j0 = lax.fori_loop(0, n_idx_bits, tie_body, jnp.zeros((1, TQ), I32))
        return jnp.where(straddle > 0, j0 + 1, jnp.int32(2 ** n_idx_bits))

    jstar = lax.cond(jnp.max(straddle) > 0, tie_bound, lambda: jnp.full((1, TQ), 2 ** n_idx_bits, I32))

    def madd_unit(u, carry):
        rows = unit_rows(u)
        key = sc_ref[rows, :]
        tie_keep = jnp.where(row_iota + u * UNIT < jstar, 0.0, NEG_INF)
        madd_ref[rows, :] = jnp.where(key > thr, 0.0, jnp.where(key == thr, tie_keep, NEG_INF))
        return carry

    lax.fori_loop(0, nunit, madd_unit, 0)
    maddn_ref[...] = madd_ref[pl.ds(near0, NEAR), :]
    madd_ref[pl.ds(near0, NEAR), :] = jnp.full((NEAR, TQ), NEG_INF, F32)

    qall = jnp.concatenate([qlat_ref[0, h * KV_RANK:(h + 1) * KV_RANK, :] for h in range(A_HEADS)], axis=1)

    def col_max(v):
        return jnp.max(v.reshape(v.shape[0] // SUBLANES, SUBLANES, A_HEADS * TQ), axis=0)

    def fold(xl, part_max, ckv_t, m_old):
        m_new = jnp.maximum(m_old, jnp.max(part_max, axis=0, keepdims=True))
        m_use = jnp.where(m_new == NEG_INF, 0.0, m_new)
        p = jnp.exp2(xl - m_use).astype(BF16)
        ot_ref[...] = ot_ref[...] * jnp.exp2(m_old - m_use) + jnp.dot(ckv_t, p, preferred_element_type=F32)
        return m_new

    ot_ref[...] = jnp.zeros(ot_ref.shape, F32)
    near_rows = pl.ds(near0, NEAR)
    xn = jnp.dot(ckv_ref[0, near_rows, :], qall, preferred_element_type=F32)
    xn = xn + jnp.concatenate([maddn_ref[...]] * A_HEADS, axis=1)
    xn = xn + jnp.concatenate([bn_ref[h, pl.ds(bn_row0, NEAR), :] for h in range(A_HEADS)], axis=1)
    m_run = fold(xn, col_max(xn), ckvT_ref[0, :, near_rows], jnp.full((1, A_HEADS * TQ), NEG_INF, F32))


    def issue_logits(u, buf_ref, pm_ref):
        rows = unit_rows(jnp.minimum(u, last_unit))
        xl = jnp.dot(ckv_ref[0, rows, :], qall, preferred_element_type=F32)
        xl = xl + jnp.concatenate([madd_ref[rows, :]] * A_HEADS, axis=1)
        buf_ref[...] = xl
        pm_ref[...] = col_max(xl)

    def consume_logits(u, buf_ref, pm_ref, m_old):
        return fold(buf_ref[...], pm_ref[...], ckvT_ref[0, :, unit_rows(u)], m_old)

    issue_logits(0, la_ref, pma_ref)

    def pair_step(i, m_old):
        issue_logits(2 * i + 1, lb_ref, pmb_ref)
        m_mid = consume_logits(2 * i, la_ref, pma_ref, m_old)
        issue_logits(2 * i + 2, la_ref, pma_ref)
        return consume_logits(2 * i + 1, lb_ref, pmb_ref, m_mid)

    nfar = (near0 + UNIT - 1) // UNIT
    m_run = lax.fori_loop(0, nfar // 2, pair_step, m_run)

    @pl.when(nfar % 2 == 1)
    def _():
        consume_logits(nfar - 1, la_ref, pma_ref, m_run)
    o_t = (ot_ref[0:KV_RANK, :] * (1.0 / ot_ref[KV_RANK:KV_RANK + 1, :])).astype(BF16)
    for h in range(A_HEADS):
        yaT_ref[h * A_HEAD_DIM:(h + 1) * A_HEAD_DIM, :] = jnp.dot(
            wuvT_ref[0, h], o_t[:, h * TQ:(h + 1) * TQ], preferred_element_type=F32)

    out_ref[0] = yaT_ref[...].T.astype(BF16)


def _dsa(l, iq, iwT, qlat, ikA, ikB, ckv, ckvT, bn, wuvT):
    B, S = ckv.shape[0], ckv.shape[1]
    assert S % (2 * UNIT) == 0 and UNIT % TQ == 0 and TQ % CHUNK == 0 and CHUNK == 64 and NEAR <= UNIT
    k_sel = min(IDX_TOPK_MAX, S // 4)
    n_idx_bits = int(math.log2(S))
    assert 2 ** n_idx_bits == S
    grid = (B, S // TQ)
    blk = lambda b, i: (b, 0, i, 0)
    full = lambda b, i: (b, 0, 0)
    kern = functools.partial(_dsa_kernel, k_sel=k_sel, n_idx_bits=n_idx_bits)
    return pl.pallas_call(
        kern,
        grid=grid,
        in_specs=[
            pl.BlockSpec((1, IDX_HEADS * IDX_DIM, TQ), lambda b, i: (b, 0, i)),
            pl.BlockSpec((1, IDX_HEADS, TQ), lambda b, i: (b, 0, i)),
            pl.BlockSpec((1, A_HEADS * KV_RANK, TQ), lambda b, i: (b, 0, i)),
            pl.BlockSpec((1, S, LANES), full),
            pl.BlockSpec((1, S, LANES), full),
            pl.BlockSpec((1, S, KV_RANK), full),
            pl.BlockSpec((1, KV_EXT, S), full),
            pl.BlockSpec((A_HEADS, NEAR + TQ, TQ), lambda b, i: (0, 0, 0)),
            pl.BlockSpec((1, A_HEADS, A_HEAD_DIM, KV_RANK), lambda b, i: (l, 0, 0, 0)),
        ],
        out_specs=pl.BlockSpec((1, TQ, A_WIDTH), lambda b, i: (b, i, 0)),
        out_shape=jax.ShapeDtypeStruct((B, S, A_WIDTH), BF16),
        scratch_shapes=[
            pltpu.VMEM((S, TQ), I32),
            pltpu.VMEM((32, S // PLANE_ROWS * SUBLANES, TQ), I32),
            pltpu.VMEM((S, TQ), F32),
            pltpu.VMEM((NEAR, TQ), F32),
            pltpu.VMEM((UNIT, A_HEADS * TQ), F32),
            pltpu.VMEM((UNIT, A_HEADS * TQ), F32),
            pltpu.VMEM((SUBLANES, A_HEADS * TQ), F32),
            pltpu.VMEM((SUBLANES, A_HEADS * TQ), F32),
            pltpu.VMEM((KV_EXT, A_HEADS * TQ), F32),
            pltpu.VMEM((A_WIDTH, TQ), F32),
        ],
        compiler_params=_cparams(("arbitrary", "arbitrary")),
        name="dsa_attention",
    )(iq, iwT, qlat, ikA, ikB, ckv, ckvT, bn, wuvT)


def _hgrn_constants():
    c = CHUNK
    r = np.arange(c)[:, None]
    jj = np.arange(c)[None, :]
    mats = [(jj <= r), (jj > r)]
    masks = [np.eye(c, dtype=bool)]
    m = c // 2
    while m >= 1:
        start = (r // (2 * m)) * (2 * m)
        bd = start + m - 1
        upper = r > bd
        mats.append(np.where(upper, (jj > bd) & (jj <= r), (jj > r) & (jj <= bd)))
        same_parent = (r // (2 * m)) == (jj // (2 * m))
        masks.append(same_parent & upper & (jj <= (jj // (2 * m)) * (2 * m) + m - 1))
        m //= 2
    m_all = np.concatenate(mats, axis=0).astype(np.float32)
    total = np.zeros((c, c), np.int32)
    for mk in masks:
        total += mk
    assert (total == np.tril(np.ones((c, c), np.int32))).all()
    return np.concatenate([m_all] * 3, axis=1), np.stack(masks).astype(np.float32)


_HGRN_M3, _HGRN_MASKS = _hgrn_constants()
_HGRN_LEVELS = _HGRN_MASKS.shape[0] - 1
HGRN_STEP_CHUNKS = 2
HGRN_STEP_BATCH = 4


def _hgrn_kernel(q_ref, k_ref, lf_ref, v_ref, gate_ref, m3_ref, mask_ref, gn_ref, out_ref, st_ref):
    @pl.when(pl.program_id(1) == 0)
    def _():
        st_ref[...] = jnp.zeros(st_ref.shape, F32)

    c = CHUNK
    intra = {}
    for ci in range(HGRN_STEP_CHUNKS):
        rows = slice(ci * c, (ci + 1) * c)
        for bi in range(HGRN_STEP_BATCH):
            g = lf_ref[bi, rows, :]
            g_hi = g.astype(BF16)
            r1 = g - g_hi.astype(F32)
            g_mid = r1.astype(BF16)
            g_lo = (r1 - g_mid.astype(F32)).astype(BF16)
            sums = jnp.dot(m3_ref[...], jnp.concatenate([g_hi, g_mid, g_lo], axis=0), preferred_element_type=F32)
            e_all = jnp.exp(sums)
            for h in range(B_HEADS):
                cols = slice(h * B_KEY_DIM, (h + 1) * B_KEY_DIM)
                qh = q_ref[bi, rows, cols]
                kh = k_ref[bi, rows, cols]
                att = mask_ref[0] * _nt_dot(qh.astype(BF16), kh.astype(BF16))
                for lv in range(_HGRN_LEVELS):
                    e_l = e_all[(2 + lv) * c:(3 + lv) * c, cols]
                    att = att + mask_ref[lv + 1] * _nt_dot((qh * e_l).astype(BF16), (kh * e_l).astype(BF16))
                e_b = e_all[0:c, cols]
                intra[bi, ci, h] = (jnp.dot(att.astype(BF16), v_ref[bi, rows, cols], preferred_element_type=F32),
                                    (qh * e_b).astype(BF16), (kh * e_all[c:2 * c, cols]).astype(BF16),
                                    e_b[c - 1:c, :])
    for ci in range(HGRN_STEP_CHUNKS):
        rows = slice(ci * c, (ci + 1) * c)
        for bi in range(HGRN_STEP_BATCH):
            for h in range(B_HEADS):
                cols = slice(h * B_KEY_DIM, (h + 1) * B_KEY_DIM)
                o_intra, q_dec, k_rem, decay_all = intra[bi, ci, h]
                st = st_ref[bi, h]
                o = o_intra + _nt_dot(q_dec, st.astype(BF16))
                upd = lax.dot_general(v_ref[bi, rows, cols], k_rem, (((0,), (0,)), ((), ())),
                                      preferred_element_type=F32)
                st_ref[bi, h] = st * decay_all + upd
                o = o * lax.rsqrt(jnp.mean(o * o, axis=-1, keepdims=True) + RMS_EPS) * gn_ref[0]
                out_ref[bi, rows, cols] = (o * gate_ref[bi, rows, cols]).astype(BF16)


def _hgrn(l, hq, hk, hlf, hv, hgate, gnorm):
    B, S, W = hq.shape
    ts = CHUNK * HGRN_STEP_CHUNKS
    nb = HGRN_STEP_BATCH
    assert B % nb == 0
    tok = lambda b, i: (b, i, 0)
    return pl.pallas_call(
        _hgrn_kernel,
        grid=(B // nb, S // ts),
        in_specs=[
            pl.BlockSpec((nb, ts, W), tok),
            pl.BlockSpec((nb, ts, W), tok),
            pl.BlockSpec((nb, ts, W), tok),
            pl.BlockSpec((nb, ts, W), tok),
            pl.BlockSpec((nb, ts, W), tok),
            pl.BlockSpec(_HGRN_M3.shape, lambda b, i: (0, 0)),
            pl.BlockSpec(_HGRN_MASKS.shape, lambda b, i: (0, 0, 0)),
            pl.BlockSpec((1, 1, B_VAL_DIM), lambda b, i: (l, 0, 0)),
        ],
        out_specs=pl.BlockSpec((nb, ts, W), tok),
        out_shape=jax.ShapeDtypeStruct((B, S, W), BF16),
        scratch_shapes=[pltpu.VMEM((nb, B_HEADS, B_VAL_DIM, B_KEY_DIM), F32)],
        compiler_params=_cparams(("arbitrary", "arbitrary")),
        name="hgrn2",
    )(hq, hk, hlf, hv, hgate, jnp.asarray(_HGRN_M3, BF16), jnp.asarray(_HGRN_MASKS), gnorm)


def _layernorm(v, g, b):
    mu = jnp.mean(v, axis=-1, keepdims=True)
    d = v - mu
    var = jnp.mean(d * d, axis=-1, keepdims=True)
    return d * lax.rsqrt(var + LN_EPS) * g + b


def _first_argmax(v, idx, axes, big):
    mx = v
    for ax in axes:
        mx = jnp.max(mx, axis=ax, keepdims=True)
    pos = jnp.where(v == mx, idx, big)
    for ax in axes:
        pos = jnp.min(pos, axis=ax, keepdims=True)
    return mx, pos


def _outproj_kernel(ya_ref, yb_ref, x_ref, mod_ref, wo_ref, lng_ref, lnb_ref, wrT_ref, rbias_ref, tri_ref,
                    x1_ref, u2_ref, gates_ref, rank_ref, gatesT_ref, *, alpha):
    y = jnp.dot(ya_ref[0], wo_ref[0, 0:A_WIDTH, :], preferred_element_type=F32)
    y = y + jnp.dot(yb_ref[0], wo_ref[0, A_WIDTH:, :], preferred_element_type=F32)
    g1 = mod_ref[0, 2:3, :]
    x1 = _layernorm(alpha * x_ref[0] + (1.0 + g1) * y, lng_ref[0], lnb_ref[0])
    x1_ref[0] = x1
    u2 = (x1 * (1.0 + mod_ref[0, 4:5, :]) + mod_ref[0, 3:4, :]).astype(BF16)
    u2_ref[0] = u2

    tm = u2.shape[0]
    gsz = N_EXPERTS // N_GROUPS
    scores = 1.0 / (1.0 + jnp.exp(-_nt_dot(wrT_ref[0], u2)))
    sel = (scores + rbias_ref[0]).reshape(N_GROUPS, gsz, tm)
    scores = scores.reshape(N_GROUPS, gsz, tm)
    i_m = lax.broadcasted_iota(I32, (N_GROUPS, gsz, tm), 1)
    i_g = lax.broadcasted_iota(I32, (N_GROUPS, 1, tm), 0)
    i_e = lax.broadcasted_iota(I32, (N_GROUPS, gsz, tm), 0) * gsz + i_m
    m1, p1 = _first_argmax(sel, i_m, (1,), gsz)
    m2 = jnp.max(jnp.where(i_m == p1, NEG_INF, sel), axis=1, keepdims=True)
    gs = m1 + m2
    gmask = jnp.zeros(gs.shape, F32)
    for _ in range(TOPK_GROUPS):
        _, pg = _first_argmax(gs, i_g, (0,), N_GROUPS)
        hit = i_g == pg
        gmask = jnp.where(hit, 1.0, gmask)
        gs = jnp.where(hit, NEG_INF, gs)
    cand = jnp.where(jnp.broadcast_to(gmask, sel.shape) > 0.0, sel, NEG_INF)
    w = jnp.zeros(sel.shape, F32)
    chosen = jnp.zeros(sel.shape, F32)
    for _ in range(TOP_K):
        _, pe = _first_argmax(cand, i_e, (1, 0), N_EXPERTS)
        hit = i_e == pe
        w = jnp.where(hit, scores, w)
        chosen = jnp.where(hit, 1.0, chosen)
        cand = jnp.where(hit, NEG_INF, cand)
    wsum = jnp.sum(jnp.sum(w, axis=1, keepdims=True), axis=0, keepdims=True)
    gates = (w / wsum * ROUTED_SCALE).reshape(N_EXPERTS, tm)
    g_hi = gates.astype(BF16).astype(F32)
    g_lo = (gates - g_hi).astype(BF16).astype(F32)
    gates_ref[0] = jnp.concatenate([g_hi, g_lo], axis=0).T.astype(BF16)

    chosen2 = chosen.reshape(N_EXPERTS, tm)
    rank = jnp.concatenate(
        [jnp.dot(chosen2[:, g * MOE_GROUP:(g + 1) * MOE_GROUP].astype(BF16), tri_ref[...],
                 preferred_element_type=F32) for g in range(tm // MOE_GROUP)], axis=1)
    rank_ref[0] = jnp.where(chosen2 > 0.0, rank, -1.0).astype(I32)
    gatesT_ref[0] = gates


def _outproj(l, ya, yb, x, mod, wo, ln_g, ln_b, wrT, rbias, alpha):
    B, S, D = x.shape
    tm = TM_PROJ
    tok = lambda b, i: (b, i, 0)
    lw3 = lambda b, i: (l, 0, 0)
    return pl.pallas_call(
        functools.partial(_outproj_kernel, alpha=alpha),
        grid=(B, S // tm),
        in_specs=[
            pl.BlockSpec((1, tm, A_WIDTH), tok),
            pl.BlockSpec((1, tm, B_WIDTH), tok),
            pl.BlockSpec((1, tm, D), tok),
            pl.BlockSpec((1, 6, D), lambda b, i: (b, 0, 0)),
            pl.BlockSpec((1, D, D), lw3),
            pl.BlockSpec((1, 1, D), lw3),
            pl.BlockSpec((1, 1, D), lw3),
            pl.BlockSpec((1, N_EXPERTS, D), lw3),
            pl.BlockSpec((1, N_EXPERTS, tm), lw3),
            pl.BlockSpec((MOE_GROUP, MOE_GROUP), lambda b, i: (0, 0)),
        ],
        out_specs=[pl.BlockSpec((1, tm, D), tok), pl.BlockSpec((1, tm, D), tok),
                   pl.BlockSpec((1, tm, 2 * N_EXPERTS), tok),
                   pl.BlockSpec((1, N_EXPERTS, tm), lambda b, i: (b, 0, i)),
                   pl.BlockSpec((1, N_EXPERTS, tm), lambda b, i: (b, 0, i))],
        out_shape=[jax.ShapeDtypeStruct((B, S, D), F32), jax.ShapeDtypeStruct((B, S, D), BF16),
                   jax.ShapeDtypeStruct((B, S, 2 * N_EXPERTS), BF16),
                   jax.ShapeDtypeStruct((B, N_EXPERTS, S), I32),
                   jax.ShapeDtypeStruct((B, N_EXPERTS, S), F32)],
        compiler_params=_cparams(("arbitrary", "arbitrary")),
        name="outproj_router",
    )(ya, yb, x, mod, wo, ln_g, ln_b, wrT, rbias,
      jnp.asarray(np.triu(np.ones((MOE_GROUP, MOE_GROUP), np.float32), 1), BF16))


MOE_CHUNK_EXPERTS = 8


def _slot_onehot(rank_rows, values, cap):
    row = lax.broadcasted_iota(I32, (cap, rank_rows.shape[1]), 0)
    return jnp.concatenate(
        [jnp.where(row == rank_rows[e:e + 1, :], values[e:e + 1, :], 0.0) for e in range(rank_rows.shape[0])], axis=0)


def _dispatch_kernel(u_ref, rank_ref, x_ref):
    u = u_ref[...]
    cap = x_ref.shape[1]
    ones = jnp.ones((MOE_CHUNK_EXPERTS, MOE_GROUP), F32)
    for c in range(N_EXPERTS // MOE_CHUNK_EXPERTS):
        es = slice(c * MOE_CHUNK_EXPERTS, (c + 1) * MOE_CHUNK_EXPERTS)
        onehot = _slot_onehot(rank_ref[0, es, :], ones, cap)
        xs = jnp.dot(onehot.astype(BF16), u, preferred_element_type=F32).astype(BF16)
        x_ref[es] = xs.reshape(MOE_CHUNK_EXPERTS, cap, -1)


def _dispatch(u2, rank, cap):
    T, D = u2.shape
    ng = T // MOE_GROUP
    gps = rank.shape[-1] // MOE_GROUP
    return pl.pallas_call(
        _dispatch_kernel,
        grid=(ng,),
        in_specs=[pl.BlockSpec((MOE_GROUP, D), lambda g: (g, 0)),
                  pl.BlockSpec((1, N_EXPERTS, MOE_GROUP), lambda g: (g // gps, 0, g % gps))],
        out_specs=pl.BlockSpec((N_EXPERTS, cap, D), lambda g: (0, g, 0)),
        out_shape=jax.ShapeDtypeStruct((N_EXPERTS, ng * cap, D), BF16),
        compiler_params=_cparams(("arbitrary",)),
        name="moe_dispatch",
    )(u2, rank)


def _expert_kernel(x_ref, wgu_ref, wd_ref, y_ref):
    hgu = jnp.dot(x_ref[0], wgu_ref[0, 0], preferred_element_type=F32)
    h = _silu(hgu[:, :EXPERT_DIM]) * hgu[:, EXPERT_DIM:]
    y_ref[0] = jnp.dot(h.astype(BF16), wd_ref[0, 0], preferred_element_type=F32).astype(BF16)


def _experts(l, xs, wgu, wd):
    E, R, D = xs.shape
    tr = R // MOE_EXPERT_STEPS
    assert R % MOE_EXPERT_STEPS == 0 and tr % 16 == 0
    return pl.pallas_call(
        _expert_kernel,
        grid=(E, R // tr),
        in_specs=[pl.BlockSpec((1, tr, D), lambda e, i: (e, i, 0)),
                  pl.BlockSpec((1, 1, D, 2 * EXPERT_DIM), lambda e, i: (l, e, 0, 0)),
                  pl.BlockSpec((1, 1, EXPERT_DIM, D), lambda e, i: (l, e, 0, 0))],
        out_specs=pl.BlockSpec((1, tr, D), lambda e, i: (e, i, 0)),
        out_shape=jax.ShapeDtypeStruct((E, R, D), BF16),
        compiler_params=_cparams(("arbitrary", "arbitrary")),
        name="moe_experts",
    )(xs, wgu, wd)


def _combine_kernel(y_ref, rank_ref, gates_ref, u_ref, x1_ref, mod_ref, sgu_ref, sd_ref, lng_ref, lnb_ref,
                    out_ref, *, alpha):
    hgu = jnp.dot(u_ref[...], sgu_ref[0], preferred_element_type=F32)
    hs = _silu(hgu[:, :SHARED_DIM]) * hgu[:, SHARED_DIM:]
    y = jnp.dot(hs.astype(BF16), sd_ref[0], preferred_element_type=F32)
    cap = y_ref.shape[1]
    for c in range(N_EXPERTS // MOE_CHUNK_EXPERTS):
        es = slice(c * MOE_CHUNK_EXPERTS, (c + 1) * MOE_CHUNK_EXPERTS)
        pick = _slot_onehot(rank_ref[0, es, :], gates_ref[0, es, :], cap)
        ys = y_ref[es].reshape(MOE_CHUNK_EXPERTS * cap, -1)
        y = y + lax.dot_general(pick.astype(BF16), ys, (((0,), (0,)), ((), ())), preferred_element_type=F32)
    g2 = mod_ref[0, 5:6, :]
    out_ref[...] = _layernorm(alpha * x1_ref[...] + (1.0 + g2) * y, lng_ref[0], lnb_ref[0])


def _combine(l, ys, rank, gates_t, u2, x1, mod, sgu, sd, ln_g, ln_b, alpha, seq):
    T, D = u2.shape
    tok = lambda g: (g, 0)
    lw3 = lambda g: (l, 0, 0)
    gps = seq // MOE_GROUP
    cap = ys.shape[1] // (T // MOE_GROUP)
    per_group = lambda g: (g // gps, 0, g % gps)
    return pl.pallas_call(
        functools.partial(_combine_kernel, alpha=alpha),
        grid=(T // MOE_GROUP,),
        in_specs=[
            pl.BlockSpec((N_EXPERTS, cap, D), lambda g: (0, g, 0)),
            pl.BlockSpec((1, N_EXPERTS, MOE_GROUP), per_group),
            pl.BlockSpec((1, N_EXPERTS, MOE_GROUP), per_group),
            pl.BlockSpec((MOE_GROUP, D), tok),
            pl.BlockSpec((MOE_GROUP, D), tok),
            pl.BlockSpec((1, 6, D), lambda g: ((g * MOE_GROUP) // seq, 0, 0)),
            pl.BlockSpec((1, D, 2 * SHARED_DIM), lw3),
            pl.BlockSpec((1, SHARED_DIM, D), lw3),
            pl.BlockSpec((1, 1, D), lw3),
            pl.BlockSpec((1, 1, D), lw3),
        ],
        out_specs=pl.BlockSpec((MOE_GROUP, D), tok),
        out_shape=jax.ShapeDtypeStruct((T, D), F32),
        compiler_params=_cparams(("arbitrary",)),
        name="moe_combine",
    )(ys, rank, gates_t, u2, x1, mod, sgu, sd, ln_g, ln_b)


def _moe_kernel(u_ref, gates_ref, x1_ref, mod_ref, wgu_ref, wd_ref, sgu_ref, sd_ref, lng_ref, lnb_ref,
                out_ref, acc_ref, *, alpha):
    s = pl.program_id(1)
    u = u_ref[...]

    def hidden(wgu):
        hgu = jnp.dot(u, wgu, preferred_element_type=F32)
        return _silu(hgu[:, :EXPERT_DIM]) * hgu[:, EXPERT_DIM:]

    @pl.when(s == 0)
    def _():
        acc_ref[...] = jnp.dot(hidden(sgu_ref[0]).astype(BF16), sd_ref[0], preferred_element_type=F32)

    rows = lax.broadcasted_iota(I32, (2 * N_EXPERTS, MOE_EXPERTS_PER_STEP * EXPERT_DIM), 0) & (N_EXPERTS - 1)
    cols = lax.broadcasted_iota(I32, (2 * N_EXPERTS, MOE_EXPERTS_PER_STEP * EXPERT_DIM), 1)
    onehot = jnp.where(rows == s * MOE_EXPERTS_PER_STEP + cols // EXPERT_DIM, 1.0, 0.0).astype(BF16)
    gate = jnp.dot(gates_ref[...], onehot, preferred_element_type=F32)
    h = jnp.concatenate(
        [(hidden(wgu_ref[0, k]) * gate[:, k * EXPERT_DIM:(k + 1) * EXPERT_DIM]).astype(BF16)
         for k in range(MOE_EXPERTS_PER_STEP)], axis=1)
    wd = wd_ref[0].reshape(MOE_EXPERTS_PER_STEP * EXPERT_DIM, wd_ref.shape[-1])
    acc_ref[...] += jnp.dot(h, wd, preferred_element_type=F32)

    @pl.when(s == pl.num_programs(1) - 1)
    def _():
        g2 = mod_ref[0, 5:6, :]
        out_ref[...] = _layernorm(alpha * x1_ref[...] + (1.0 + g2) * acc_ref[...], lng_ref[0], lnb_ref[0])


def _moe(l, u2, gates, x1, mod, wgu, wd, sgu, sd, ln_g, ln_b, alpha, seq):
    T, D = u2.shape
    tm = TM_MOE
    assert seq % tm == 0
    tok = lambda i, e: (i, 0)
    lw3 = lambda i, e: (l, 0, 0)
    return pl.pallas_call(
        functools.partial(_moe_kernel, alpha=alpha),
        grid=(T // tm, N_EXPERTS // MOE_EXPERTS_PER_STEP),
        in_specs=[
            pl.BlockSpec((tm, D), tok),
            pl.BlockSpec((tm, 2 * N_EXPERTS), tok),
            pl.BlockSpec((tm, D), tok),
            pl.BlockSpec((1, 6, D), lambda i, e: ((i * tm) // seq, 0, 0)),
            pl.BlockSpec((1, MOE_EXPERTS_PER_STEP, D, 2 * EXPERT_DIM), lambda i, e: (l, e, 0, 0)),
            pl.BlockSpec((1, MOE_EXPERTS_PER_STEP, EXPERT_DIM, D), lambda i, e: (l, e, 0, 0)),
            pl.BlockSpec((1, D, 2 * SHARED_DIM), lw3),
            pl.BlockSpec((1, SHARED_DIM, D), lw3),
            pl.BlockSpec((1, 1, D), lw3),
            pl.BlockSpec((1, 1, D), lw3),
        ],
        out_specs=pl.BlockSpec((tm, D), tok),
        out_shape=jax.ShapeDtypeStruct((T, D), F32),
        scratch_shapes=[pltpu.VMEM((tm, D), F32)],
        compiler_params=_cparams(("arbitrary", "arbitrary")),
        name="moe_dense",
    )(u2, gates, x1, mod, wgu, wd, sgu, sd, ln_g, ln_b)


def _prepare_params(w_in, kv_norm_g, w_uk, w_uv, hgrn_lb, w_out, w_router, router_bias,
                    w_gate, w_up, w_down, ws_gate, ws_up, ws_down):
    L = w_in.shape[0]
    sizes = (A_WIDTH, KV_RANK, IDX_HEADS * IDX_DIM, IDX_DIM, IDX_HEADS, B_FDIM, B_FDIM, B_WIDTH, B_WIDTH)
    offs = np.concatenate([[0], np.cumsum(sizes)])
    seg = lambda i: w_in[:, :, offs[i]:offs[i + 1]]
    w_aq, w_ckv, w_iq, w_ik, w_iw, w_hq, w_hf, w_hi, w_hg = (seg(i) for i in range(9))
    zik = jnp.zeros_like(w_ik)
    wp = jnp.concatenate([w_ckv, w_ik, zik, zik, w_ik, w_hq, w_hf, w_hg, w_hi], axis=-1).astype(BF16)
    wt = jnp.swapaxes(jnp.concatenate([w_aq, w_iq, w_ckv, w_iw], axis=-1), 1, 2).astype(BF16)
    assert wp.shape[-1] == _C_END and wt.shape[1] == _R_END
    eye = jnp.eye(A_HEADS, dtype=F32)
    wblk = (jnp.einsum('lhdr,hg->lhdgr', w_uk * (ATTN_SCALE * LOG2E), eye)
            .reshape(L, A_WIDTH, A_HEADS * KV_RANK).astype(BF16))
    p = dict(
        wp=wp, wt=wt, wblkT=jnp.swapaxes(wblk, 1, 2),
        gkv=kv_norm_g.reshape(L, 1, KV_RANK),
        gkvT=jnp.broadcast_to(kv_norm_g[:, :, None], (L, KV_RANK, TM_PROJ)),
        wuvT=jnp.swapaxes(w_uv, 2, 3).astype(BF16),
        wo=w_out.astype(BF16),
        wrT=jnp.swapaxes(w_router, 1, 2).astype(BF16),
        rbias=jnp.broadcast_to(router_bias[:, :, None], (L, N_EXPERTS, TM_PROJ)),
        wgu=jnp.concatenate([w_gate, w_up], axis=-1).astype(BF16),
        wd=w_down.astype(BF16),
        sgu=jnp.concatenate([ws_gate, ws_up], axis=-1).astype(BF16),
        sd=ws_down.astype(BF16),
    )
    lbs = jnp.cumsum(jax.nn.softmax(hgrn_lb.astype(F32), axis=0), axis=0)
    lbs = jnp.clip(lbs - lbs[0:1], 0.0, 1.0 - 1e-6)
    p["llb"] = jnp.log(lbs).reshape(L, 1, B_FDIM)
    p["l1m"] = jnp.log1p(-lbs).reshape(L, 1, B_FDIM)
    return p


def kernel(x, c, w_ada, b_ada, w_in, kv_norm_g, w_uk, w_uv, rel_bias, hgrn_lb, gnorm_g, w_out, ln1_g, ln1_b,
           w_router, router_bias, w_gate, w_up, w_down, ws_gate, ws_up, ws_down, ln2_g, ln2_b):
    B, S, D = x.shape
    L = w_in.shape[0]
    alpha = (2 * L) ** 0.25
    p = _prepare_params(w_in, kv_norm_g, w_uk, w_uv, hgrn_lb, w_out, w_router, router_bias,
                        w_gate, w_up, w_down, ws_gate, ws_up, ws_down)
    mods = _adaln(c, w_ada, b_ada).reshape(L, B, 6, D)
    bn = _bias_tile(rel_bias)
    gn = gnorm_g.reshape(L, 1, B_VAL_DIM)
    ln1g, ln1b = ln1_g.reshape(L, 1, D), ln1_b.reshape(L, 1, D)
    ln2g, ln2b = ln2_g.reshape(L, 1, D), ln2_b.reshape(L, 1, D)
    for l in range(L):
        mod = mods[l]
        (qlat, ckv, ckvT, iq, ikA, ikB, iwT, hq, hk, hlf, hv, hgate) = _inproj(
            l, x, mod, p["wp"], p["wt"], p["wblkT"], p["gkv"], p["gkvT"], p["llb"], p["l1m"])
        ya = _dsa(l, iq, iwT, qlat, ikA, ikB, ckv, ckvT, bn, p["wuvT"])
        yb = _hgrn(l, hq, hk, hlf, hv, hgate, gn)
        x1, u2, gates, rank, gates_t = _outproj(l, ya, yb, x, mod, p["wo"], ln1g, ln1b, p["wrT"], p["rbias"], alpha)
        u2f, x1f = u2.reshape(B * S, D), x1.reshape(B * S, D)

        def moe_sparse(cap, l=l, mod=mod, u2f=u2f, x1f=x1f, rank=rank, gates_t=gates_t):
            ys = _experts(l, _dispatch(u2f, rank, cap), p["wgu"], p["wd"])
            return _combine(l, ys, rank, gates_t, u2f, x1f, mod, p["sgu"], p["sd"], ln2g, ln2b, alpha, S)

        def moe_dense(l=l, mod=mod, u2f=u2f, x1f=x1f, gates=gates):
            return _moe(l, u2f, gates.reshape(B * S, 2 * N_EXPERTS), x1f, mod,
                        p["wgu"], p["wd"], p["sgu"], p["sd"], ln2g, ln2b, alpha, S)

        branches = [functools.partial(moe_sparse, cap) for cap in MOE_CAPS] + [moe_dense]
        tier = sum((jnp.max(rank) >= cap).astype(I32) for cap in MOE_CAPS)
        x = lax.switch(tier, branches).reshape(B, S, D)
    return x
```

```python
import functools
import math

import numpy as np
import jax
import jax.numpy as jnp
from jax import lax
from jax.experimental import pallas as pl
from jax.experimental.pallas import tpu as pltpu

F32 = jnp.float32
BF16 = jnp.bfloat16
I32 = jnp.int32

D_MODEL = 1024
CHUNK = 64
A_HEADS = 8
A_HEAD_DIM = 64
A_WIDTH = A_HEADS * A_HEAD_DIM
KV_RANK = 128
IDX_HEADS = 8
IDX_DIM = 64
IDX_TOPK_MAX = 256
IDX_W_SCALE = (IDX_HEADS ** -0.5) * (IDX_DIM ** -0.5)
ATTN_SCALE = A_HEAD_DIM ** -0.5
LOG2E = math.log2(math.e)
KV_EXT = KV_RANK + 16
NUM_BUCKETS = 32
MAX_DISTANCE = 128
B_HEADS = 4
B_KEY_DIM = 128
B_VAL_DIM = 128
B_WIDTH = B_HEADS * B_VAL_DIM
B_FDIM = B_HEADS * B_KEY_DIM
N_EXPERTS = 64
TOP_K = 8
N_GROUPS = 8
TOPK_GROUPS = 4
EXPERT_DIM = 256
SHARED_DIM = 256
ROUTED_SCALE = 2.5
LN_EPS = 1e-5
RMS_EPS = 1e-6

LANES = 128
SUBLANES = 8
VMEM_LIMIT_BYTES = 56 * 1024 * 1024

INT_MIN = -(2 ** 31)
NEG_INF = float("-inf")

TM_PROJ = 1024
TQ = 128
UNIT = 512
NEAR = 2 * TQ
COUNT_ACCS = 8
PLANE_ROWS = 32 * SUBLANES
TM_MOE = 1024
MOE_EXPERTS_PER_STEP = 4
MOE_GROUP = 256
MOE_CAPS = (64, 80)
MOE_EXPERT_STEPS = 4

_C_CKV, _C_IKA, _C_IKB, _C_HQ, _C_HF, _C_HG, _C_HI, _C_END = (0, 128, 256, 384, 896, 1408, 1920, 2432)
_R_AQ, _R_IQ, _R_CKV, _R_IW, _R_END = (0, 512, 1024, 1152, 1160)


def _silu(v):
    return v * (1.0 / (1.0 + jnp.exp(-v)))


def _nt_dot(a, b):
    return lax.dot_general(a, b, (((1,), (1,)), ((), ())), preferred_element_type=F32)


def _cparams(sem):
    return pltpu.CompilerParams(dimension_semantics=sem, vmem_limit_bytes=VMEM_LIMIT_BYTES)


def _adaln_kernel(c_ref, w_ref, b_ref, o_ref):
    cond = _silu(c_ref[...])
    o_ref[0] = jnp.dot(cond.astype(BF16), w_ref[0].astype(BF16), preferred_element_type=F32) + b_ref[0]


def _adaln(c, w_ada, b_ada):
    L, D, D6 = w_ada.shape
    B = c.shape[0]
    nb = D6 // D
    return pl.pallas_call(
        _adaln_kernel,
        grid=(L, nb),
        in_specs=[
            pl.BlockSpec((B, D), lambda l, j: (0, 0)),
            pl.BlockSpec((1, D, D), lambda l, j: (l, 0, j)),
            pl.BlockSpec((1, 1, D), lambda l, j: (l, 0, j)),
        ],
        out_specs=pl.BlockSpec((1, B, D), lambda l, j: (l, 0, j)),
        out_shape=jax.ShapeDtypeStruct((L, B, D6), F32),
        compiler_params=_cparams(("arbitrary", "arbitrary")),
        name="adaln_mod",
    )(c, w_ada, b_ada.reshape(L, 1, D6))


_T5_NB = NUM_BUCKETS // 2
_T5_EXACT = _T5_NB // 2
_T5_THRESHOLDS = tuple(
    int(math.ceil(_T5_EXACT * (MAX_DISTANCE / _T5_EXACT) ** (j / (_T5_NB - _T5_EXACT)) - 1e-9))
    for j in range(1, _T5_NB - _T5_EXACT))
FAR_BUCKET = _T5_NB - 1
assert _T5_THRESHOLDS[-1] <= TQ, "keys further than one query block behind must share the far bucket"


def _bias_kernel(rb_ref, o_ref):
    kr = lax.broadcasted_iota(I32, (NEAR + TQ, TQ), 0)
    ql = lax.broadcasted_iota(I32, (NEAR + TQ, TQ), 1)
    rel = kr - TQ - ql
    n = jnp.abs(rel)
    large = jnp.full(rel.shape, _T5_EXACT, I32)
    for t in _T5_THRESHOLDS:
        large = large + (n >= t).astype(I32)
    bucket = jnp.where(rel > 0, _T5_NB, 0) + jnp.where(n < _T5_EXACT, n, large)
    for h in range(A_HEADS):
        acc = jnp.zeros(rel.shape, F32)
        for bk in range(NUM_BUCKETS):
            acc = jnp.where(bucket == bk, rb_ref[bk, h], acc)
        o_ref[h] = (acc - rb_ref[FAR_BUCKET, h]) * LOG2E


def _bias_tile(rel_bias):
    return pl.pallas_call(
        _bias_kernel,
        in_specs=[pl.BlockSpec(memory_space=pltpu.SMEM)],
        out_specs=pl.BlockSpec(memory_space=pltpu.VMEM),
        out_shape=jax.ShapeDtypeStruct((A_HEADS, NEAR + TQ, TQ), F32),
        name="rel_bias_tile",
    )(rel_bias)


def _inproj_kernel(x_ref, mod_ref, wp_ref, wt_ref, wblkT_ref, gkv_ref, gkvT_ref, llb_ref, l1m_ref,
                   qlatT_ref, ckv_ref, ckvT_ref, iqT_ref, ikA_ref, ikB_ref, iwT_ref,
                   hq_ref, hk_ref, hlf_ref, hv_ref, hgate_ref):
    x = x_ref[0]
    sh1 = mod_ref[0, 0:1, :]
    sc1 = mod_ref[0, 1:2, :]
    u = (x * (1.0 + sc1) + sh1).astype(BF16)
    z = jnp.dot(u, wp_ref[0], preferred_element_type=F32)
    zt = _nt_dot(wt_ref[0], u)

    def proj(lo, hi):
        return z[:, lo:hi]

    qlatT_ref[0] = jnp.dot(wblkT_ref[0], zt[_R_AQ:_R_IQ].astype(BF16), preferred_element_type=F32).astype(BF16)

    zc = proj(_C_CKV, _C_IKA)
    inv = lax.rsqrt(jnp.mean(zc * zc, axis=-1, keepdims=True) + RMS_EPS)
    ckv_ref[0] = (zc * inv * gkv_ref[0]).astype(BF16)
    zct = zt[_R_CKV:_R_IW]
    inv_t = lax.rsqrt(jnp.mean(zct * zct, axis=0, keepdims=True) + RMS_EPS)
    ckvT_ref[0, 0:KV_RANK, :] = (zct * inv_t * gkvT_ref[0]).astype(BF16)
    ckvT_ref[0, KV_RANK:KV_EXT, :] = jnp.ones((KV_EXT - KV_RANK, zct.shape[1]), BF16)

    iqT_ref[0] = zt[_R_IQ:_R_CKV].astype(BF16)
    ikA_ref[0] = proj(_C_IKA, _C_IKB).astype(BF16)
    ikB_ref[0] = proj(_C_IKB, _C_HQ).astype(BF16)
    iwT_ref[0] = zt[_R_IW:_R_END] * IDX_W_SCALE

    hq_ref[0] = _silu(proj(_C_HQ, _C_HF))
    zf = proj(_C_HF, _C_HG)
    log_sig = jnp.minimum(zf, 0.0) - jnp.log1p(jnp.exp(-jnp.abs(zf)))
    a = llb_ref[0]
    c = l1m_ref[0] + log_sig
    logf = jnp.maximum(a, c) + jnp.log1p(jnp.exp(-jnp.abs(a - c)))
    hlf_ref[0] = logf
    hk_ref[0] = 1.0 - jnp.exp(logf)
    hgate_ref[0] = _silu(proj(_C_HG, _C_HI))
    hv_ref[0] = proj(_C_HI, _C_END).astype(BF16)


def _inproj(l, x, mod, wp, wt, wblkT, gkv, gkvT, llb, l1m):
    B, S, D = x.shape
    tm = TM_PROJ
    grid = (B, S // tm)
    lw3 = lambda b, i: (l, 0, 0)
    tok = lambda b, i: (b, i, 0)
    tokT = lambda b, i: (b, 0, i)
    hd4 = lambda b, i: (b, 0, i, 0)
    outs = [
        (jax.ShapeDtypeStruct((B, A_HEADS * KV_RANK, S), BF16), pl.BlockSpec((1, A_HEADS * KV_RANK, tm), tokT)),
        (jax.ShapeDtypeStruct((B, S, KV_RANK), BF16), pl.BlockSpec((1, tm, KV_RANK), tok)),
        (jax.ShapeDtypeStruct((B, KV_EXT, S), BF16), pl.BlockSpec((1, KV_EXT, tm), tokT)),
        (jax.ShapeDtypeStruct((B, IDX_HEADS * IDX_DIM, S), BF16), pl.BlockSpec((1, IDX_HEADS * IDX_DIM, tm), tokT)),
        (jax.ShapeDtypeStruct((B, S, LANES), BF16), pl.BlockSpec((1, tm, LANES), tok)),
        (jax.ShapeDtypeStruct((B, S, LANES), BF16), pl.BlockSpec((1, tm, LANES), tok)),
        (jax.ShapeDtypeStruct((B, IDX_HEADS, S), F32), pl.BlockSpec((1, IDX_HEADS, tm), tokT)),
        (jax.ShapeDtypeStruct((B, S, B_FDIM), F32), pl.BlockSpec((1, tm, B_FDIM), tok)),
        (jax.ShapeDtypeStruct((B, S, B_FDIM), F32), pl.BlockSpec((1, tm, B_FDIM), tok)),
        (jax.ShapeDtypeStruct((B, S, B_FDIM), F32), pl.BlockSpec((1, tm, B_FDIM), tok)),
        (jax.ShapeDtypeStruct((B, S, B_WIDTH), BF16), pl.BlockSpec((1, tm, B_WIDTH), tok)),
        (jax.ShapeDtypeStruct((B, S, B_WIDTH), F32), pl.BlockSpec((1, tm, B_WIDTH), tok)),
    ]
    return pl.pallas_call(
        _inproj_kernel,
        grid=grid,
        in_specs=[
            pl.BlockSpec((1, tm, D), tok),
            pl.BlockSpec((1, 6, D), lambda b, i: (b, 0, 0)),
            pl.BlockSpec((1, D, _C_END), lw3),
            pl.BlockSpec((1, _R_END, D), lw3),
            pl.BlockSpec((1, A_HEADS * KV_RANK, A_WIDTH), lw3),
            pl.BlockSpec((1, 1, KV_RANK), lw3),
            pl.BlockSpec((1, KV_RANK, tm), lw3),
            pl.BlockSpec((1, 1, B_FDIM), lw3),
            pl.BlockSpec((1, 1, B_FDIM), lw3),
        ],
        out_specs=[o[1] for o in outs],
        out_shape=[o[0] for o in outs],
        compiler_params=_cparams(("arbitrary", "arbitrary")),
        name="inproj",
    )(x, mod, wp, wt, wblkT, gkv, gkvT, llb, l1m)


def _dsa_kernel(iq_ref, iwT_ref, qlat_ref, ikA_ref, ikB_ref, ckv_ref, ckvT_ref, bn_ref, wuvT_ref, out_ref,
                sc_ref, plane_ref, madd_ref, maddn_ref, la_ref, lb_ref, pma_ref, pmb_ref, ot_ref, yaT_ref,
                *, k_sel, n_idx_bits):
    j = pl.program_id(1)
    q0 = j * TQ
    nk = q0 + TQ
    nunit = (nk + UNIT - 1) // UNIT
    near0 = pl.multiple_of(jnp.maximum(nk - NEAR, 0), TQ)
    bn_row0 = pl.multiple_of(jnp.where(j == 0, TQ, 0), TQ)
    lane = lax.broadcasted_iota(I32, (1, TQ), 1)
    limit = (((q0 + lane) >> 6) + 1) << 6
    row_iota = lax.broadcasted_iota(I32, (UNIT, TQ), 0)

    def unit_rows(u):
        return pl.ds(pl.multiple_of(u * UNIT, UNIT), UNIT)

    @pl.when((pl.program_id(0) == 0) & (j == 0))
    def _():
        plane_ref[...] = jnp.zeros(plane_ref.shape, I32)
        madd_ref[...] = jnp.full(madd_ref.shape, NEG_INF, F32)

    iqs = jnp.concatenate([iq_ref[0, p * LANES:(p + 1) * LANES, :] for p in range(IDX_HEADS // 2)], axis=1)
    iw = iwT_ref[0]

    last_unit = sc_ref.shape[0] // UNIT - 1
    half = IDX_HEADS // 2 * TQ

    def issue_scores(u, buf_ref):
        rows = unit_rows(jnp.minimum(u, last_unit))
        buf_ref[:, 0:half] = jnp.dot(ikA_ref[0, rows, :], iqs, preferred_element_type=F32)
        buf_ref[:, half:2 * half] = jnp.dot(ikB_ref[0, rows, :], iqs, preferred_element_type=F32)

    def reduce_scores(u, buf_ref):
        acc = jnp.zeros((UNIT, TQ), F32)
        for p in range(IDX_HEADS // 2):
            acc = acc + iw[2 * p:2 * p + 1, :] * jnp.maximum(buf_ref[:, p * TQ:(p + 1) * TQ], 0.0)
            acc = acc + iw[2 * p + 1:2 * p + 2, :] * jnp.maximum(buf_ref[:, half + p * TQ:half + (p + 1) * TQ], 0.0)
        bits = lax.bitcast_convert_type(acc, I32)
        key = bits ^ ((bits >> 31) & 0x7FFFFFFF)
        sc_ref[unit_rows(u), :] = jnp.where(row_iota + u * UNIT < limit, key, INT_MIN)

    issue_scores(0, la_ref)

    def score_pair(i, carry):
        issue_scores(2 * i + 1, lb_ref)
        reduce_scores(2 * i, la_ref)
        issue_scores(2 * i + 2, la_ref)
        reduce_scores(2 * i + 1, lb_ref)
        return carry

    lax.fori_loop(0, nunit // 2, score_pair, 0)

    @pl.when(nunit % 2 == 1)
    def _():
        reduce_scores(nunit - 1, la_ref)

    ngroups = (nk + PLANE_ROWS - 1) // PLANE_ROWS

    def plane_group(g, carry):
        rows = pl.ds(pl.multiple_of(g * PLANE_ROWS, PLANE_ROWS), PLANE_ROWS)
        words = (sc_ref[rows, :] ^ INT_MIN).reshape(32, SUBLANES, TQ)
        w = [words[i] for i in range(32)]
        j, m = 16, 0x0000FFFF
        while j:
            mask = np.int32(np.uint32(m).view(np.int32))
            k = 0
            while k < 32:
                t = (w[k] ^ lax.shift_right_logical(w[k + j], jnp.full(w[k].shape, j, I32))) & mask
                w[k] = w[k] ^ t
                w[k + j] = w[k + j] ^ (t << j)
                k = (k + j + 1) & ~j
            j >>= 1
            m = (m ^ (m << j)) & 0xFFFFFFFF
        for i in range(32):
            plane_ref[i, pl.ds(g * SUBLANES, SUBLANES), :] = w[i]
        return carry

    lax.fori_loop(0, ngroups, plane_group, 0)

    n_words = sc_ref.shape[0] // PLANE_ROWS * SUBLANES
    group_of_word = lax.broadcasted_iota(I32, (n_words, TQ), 0) // SUBLANES

    def bit_step(i, carry):
        alive, above, t_off, c_ge = carry
        hit = alive & plane_ref[i]
        cnt = above + jnp.sum(lax.population_count(hit), axis=0, keepdims=True)
        ok = cnt >= k_sel
        alive = jnp.where(ok, hit, alive ^ hit)
        above = jnp.where(ok, above, cnt)
        t_off = jnp.where(ok, t_off | (jnp.int32(1) << (31 - i)), t_off)
        return alive, above, t_off, jnp.where(ok, cnt, c_ge)

    zero_row = jnp.zeros((1, TQ), I32)
    _, _, t_off, c_ge = lax.fori_loop(
        0, 32, bit_step,
        (jnp.where(group_of_word < ngroups, jnp.int32(-1), jnp.int32(0)), zero_row, zero_row, zero_row))
    thr = jnp.maximum(t_off ^ INT_MIN, INT_MIN + 1)
    straddle = (c_ge > k_sel).astype(I32)

    def count_where(pred):
        def body(u, acc):
            hit = pred(sc_ref[unit_rows(u), :], u * UNIT).reshape(-1, COUNT_ACCS * SUBLANES, TQ)
            for s in range(hit.shape[0]):
                acc = jnp.where(hit[s], acc + 1, acc)
            return acc
        acc = lax.fori_loop(0, nunit, body, jnp.zeros((COUNT_ACCS * SUBLANES, TQ), I32))
        return jnp.sum(acc, axis=0, keepdims=True)

    def tie_bound():
        c_gt = count_where(lambda blk, r0: blk > thr)
        need = k_sel - c_gt

        def tie_body(i, j0):
            cand = j0 | (jnp.int32(1) << (n_idx_bits - 1 - i))
            cnt = count_where(lambda blk, r0: jnp.where(blk == thr, row_iota + r0, cand) < cand)
            return jnp.where(cnt < need, cand, j0)

        j0 = lax.fori_loop(0, n_idx_bits, tie_body, jnp.zeros((1, TQ), I32))
        return jnp.where(straddle > 0, j0 + 1, jnp.int32(2 ** n_idx_bits))

    jstar = lax.cond(jnp.max(straddle) > 0, tie_bound, lambda: jnp.full((1, TQ), 2 ** n_idx_bits, I32))

    def madd_unit(u, carry):
        rows = unit_rows(u)
        key = sc_ref[rows, :]
        tie_keep = jnp.where(row_iota + u * UNIT < jstar, 0.0, NEG_INF)
        madd_ref[rows, :] = jnp.where(key > thr, 0.0, jnp.where(key == thr, tie_keep, NEG_INF))
        return carry

    lax.fori_loop(0, nunit, madd_unit, 0)
    maddn_ref[...] = madd_ref[pl.ds(near0, NEAR), :]
    madd_ref[pl.ds(near0, NEAR), :] = jnp.full((NEAR, TQ), NEG_INF, F32)

    qall = jnp.concatenate([qlat_ref[0, h * KV_RANK:(h + 1) * KV_RANK, :] for h in range(A_HEADS)], axis=1)

    def col_max(v):
        return jnp.max(v.reshape(v.shape[0] // SUBLANES, SUBLANES, A_HEADS * TQ), axis=0)

    def fold(xl, part_max, ckv_t, m_old):
        m_new = jnp.maximum(m_old, jnp.max(part_max, axis=0, keepdims=True))
        m_use = jnp.where(m_new == NEG_INF, 0.0, m_new)
        p = jnp.exp2(xl - m_use).astype(BF16)
        ot_ref[...] = ot_ref[...] * jnp.exp2(m_old - m_use) + jnp.dot(ckv_t, p, preferred_element_type=F32)
        return m_new

    ot_ref[...] = jnp.zeros(ot_ref.shape, F32)
    near_rows = pl.ds(near0, NEAR)
    xn = jnp.dot(ckv_ref[0, near_rows, :], qall, preferred_element_type=F32)
    xn = xn + jnp.concatenate([maddn_ref[...]] * A_HEADS, axis=1)
    xn = xn + jnp.concatenate([bn_ref[h, pl.ds(bn_row0, NEAR), :] for h in range(A_HEADS)], axis=1)
    m_run = fold(xn, col_max(xn), ckvT_ref[0, :, near_rows], jnp.full((1, A_HEADS * TQ), NEG_INF, F32))


    def issue_logits(u, buf_ref, pm_ref):
        rows = unit_rows(jnp.minimum(u, last_unit))
        xl = jnp.dot(ckv_ref[0, rows, :], qall, preferred_element_type=F32)
        xl = xl + jnp.concatenate([madd_ref[rows, :]] * A_HEADS, axis=1)
        buf_ref[...] = xl
        pm_ref[...] = col_max(xl)

    def consume_logits(u, buf_ref, pm_ref, m_old):
        return fold(buf_ref[...], pm_ref[...], ckvT_ref[0, :, unit_rows(u)], m_old)

    issue_logits(0, la_ref, pma_ref)

    def pair_step(i, m_old):
        issue_logits(2 * i + 1, lb_ref, pmb_ref)
        m_mid = consume_logits(2 * i, la_ref, pma_ref, m_old)
        issue_logits(2 * i + 2, la_ref, pma_ref)
        return consume_logits(2 * i + 1, lb_ref, pmb_ref, m_mid)

    nfar = (near0 + UNIT - 1) // UNIT
    m_run = lax.fori_loop(0, nfar // 2, pair_step, m_run)

    @pl.when(nfar % 2 == 1)
    def _():
        consume_logits(nfar - 1, la_ref, pma_ref, m_run)
    o_t = (ot_ref[0:KV_RANK, :] * (1.0 / ot_ref[KV_RANK:KV_RANK + 1, :])).astype(BF16)
    for h in range(A_HEADS):
        yaT_ref[h * A_HEAD_DIM:(h + 1) * A_HEAD_DIM, :] = jnp.dot(
            wuvT_ref[0, h], o_t[:, h * TQ:(h + 1) * TQ], preferred_element_type=F32)

    out_ref[0] = yaT_ref[...].T.astype(BF16)


def _dsa(l, iq, iwT, qlat, ikA, ikB, ckv, ckvT, bn, wuvT):
    B, S = ckv.shape[0], ckv.shape[1]
    assert S % (2 * UNIT) == 0 and UNIT % TQ == 0 and TQ % CHUNK == 0 and CHUNK == 64 and NEAR <= UNIT
    k_sel = min(IDX_TOPK_MAX, S // 4)
    n_idx_bits = int(math.log2(S))
    assert 2 ** n_idx_bits == S
    grid = (B, S // TQ)
    blk = lambda b, i: (b, 0, i, 0)
    full = lambda b, i: (b, 0, 0)
    kern = functools.partial(_dsa_kernel, k_sel=k_sel, n_idx_bits=n_idx_bits)
    return pl.pallas_call(
        kern,
        grid=grid,
        in_specs=[
            pl.BlockSpec((1, IDX_HEADS * IDX_DIM, TQ), lambda b, i: (b, 0, i)),
            pl.BlockSpec((1, IDX_HEADS, TQ), lambda b, i: (b, 0, i)),
            pl.BlockSpec((1, A_HEADS * KV_RANK, TQ), lambda b, i: (b, 0, i)),
            pl.BlockSpec((1, S, LANES), full),
            pl.BlockSpec((1, S, LANES), full),
            pl.BlockSpec((1, S, KV_RANK), full),
            pl.BlockSpec((1, KV_EXT, S), full),
            pl.BlockSpec((A_HEADS, NEAR + TQ, TQ), lambda b, i: (0, 0, 0)),
            pl.BlockSpec((1, A_HEADS, A_HEAD_DIM, KV_RANK), lambda b, i: (l, 0, 0, 0)),
        ],
        out_specs=pl.BlockSpec((1, TQ, A_WIDTH), lambda b, i: (b, i, 0)),
        out_shape=jax.ShapeDtypeStruct((B, S, A_WIDTH), BF16),
        scratch_shapes=[
            pltpu.VMEM((S, TQ), I32),
            pltpu.VMEM((32, S // PLANE_ROWS * SUBLANES, TQ), I32),
            pltpu.VMEM((S, TQ), F32),
            pltpu.VMEM((NEAR, TQ), F32),
            pltpu.VMEM((UNIT, A_HEADS * TQ), F32),
            pltpu.VMEM((UNIT, A_HEADS * TQ), F32),
            pltpu.VMEM((SUBLANES, A_HEADS * TQ), F32),
            pltpu.VMEM((SUBLANES, A_HEADS * TQ), F32),
            pltpu.VMEM((KV_EXT, A_HEADS * TQ), F32),
            pltpu.VMEM((A_WIDTH, TQ), F32),
        ],
        compiler_params=_cparams(("arbitrary", "arbitrary")),
        name="dsa_attention",
    )(iq, iwT, qlat, ikA, ikB, ckv, ckvT, bn, wuvT)


def _hgrn_constants():
    c = CHUNK
    r = np.arange(c)[:, None]
    jj = np.arange(c)[None, :]
    mats = [(jj <= r), (jj > r)]
    masks = [np.eye(c, dtype=bool)]
    m = c // 2
    while m >= 1:
        start = (r // (2 * m)) * (2 * m)
        bd = start + m - 1
        upper = r > bd
        mats.append(np.where(upper, (jj > bd) & (jj <= r), (jj > r) & (jj <= bd)))
        same_parent = (r // (2 * m)) == (jj // (2 * m))
        masks.append(same_parent & upper & (jj <= (jj // (2 * m)) * (2 * m) + m - 1))
        m //= 2
    m_all = np.concatenate(mats, axis=0).astype(np.float32)
    total = np.zeros((c, c), np.int32)
    for mk in masks:
        total += mk
    assert (total == np.tril(np.ones((c, c), np.int32))).all()
    return np.concatenate([m_all] * 3, axis=1), np.stack(masks).astype(np.float32)


_HGRN_M3, _HGRN_MASKS = _hgrn_constants()
_HGRN_LEVELS = _HGRN_MASKS.shape[0] - 1
HGRN_STEP_CHUNKS = 2
HGRN_STEP_BATCH = 4


def _hgrn_kernel(q_ref, k_ref, lf_ref, v_ref, gate_ref, m3_ref, mask_ref, gn_ref, out_ref, st_ref):
    @pl.when(pl.program_id(1) == 0)
    def _():
        st_ref[...] = jnp.zeros(st_ref.shape, F32)

    c = CHUNK
    intra = {}
    for ci in range(HGRN_STEP_CHUNKS):
        rows = slice(ci * c, (ci + 1) * c)
        for bi in range(HGRN_STEP_BATCH):
            g = lf_ref[bi, rows, :]
            g_hi = g.astype(BF16)
            r1 = g - g_hi.astype(F32)
            g_mid = r1.astype(BF16)
            g_lo = (r1 - g_mid.astype(F32)).astype(BF16)
            sums = jnp.dot(m3_ref[...], jnp.concatenate([g_hi, g_mid, g_lo], axis=0), preferred_element_type=F32)
            e_all = jnp.exp(sums)
            for h in range(B_HEADS):
                cols = slice(h * B_KEY_DIM, (h + 1) * B_KEY_DIM)
                qh = q_ref[bi, rows, cols]
                kh = k_ref[bi, rows, cols]
                att = mask_ref[0] * _nt_dot(qh.astype(BF16), kh.astype(BF16))
                for lv in range(_HGRN_LEVELS):
                    e_l = e_all[(2 + lv) * c:(3 + lv) * c, cols]
                    att = att + mask_ref[lv + 1] * _nt_dot((qh * e_l).astype(BF16), (kh * e_l).astype(BF16))
                e_b = e_all[0:c, cols]
                intra[bi, ci, h] = (jnp.dot(att.astype(BF16), v_ref[bi, rows, cols], preferred_element_type=F32),
                                    (qh * e_b).astype(BF16), (kh * e_all[c:2 * c, cols]).astype(BF16),
                                    e_b[c - 1:c, :])
    for ci in range(HGRN_STEP_CHUNKS):
        rows = slice(ci * c, (ci + 1) * c)
        for bi in range(HGRN_STEP_BATCH):
            for h in range(B_HEADS):
                cols = slice(h * B_KEY_DIM, (h + 1) * B_KEY_DIM)
                o_intra, q_dec, k_rem, decay_all = intra[bi, ci, h]
                st = st_ref[bi, h]
                o = o_intra + _nt_dot(q_dec, st.astype(BF16))
                upd = lax.dot_general(v_ref[bi, rows, cols], k_rem, (((0,), (0,)), ((), ())),
                                      preferred_element_type=F32)
                st_ref[bi, h] = st * decay_all + upd
                o = o * lax.rsqrt(jnp.mean(o * o, axis=-1, keepdims=True) + RMS_EPS) * gn_ref[0]
                out_ref[bi, rows, cols] = (o * gate_ref[bi, rows, cols]).astype(BF16)


def _hgrn(l, hq, hk, hlf, hv, hgate, gnorm):
    B, S, W = hq.shape
    ts = CHUNK * HGRN_STEP_CHUNKS
    nb = HGRN_STEP_BATCH
    assert B % nb == 0
    tok = lambda b, i: (b, i, 0)
    return pl.pallas_call(
        _hgrn_kernel,
        grid=(B // nb, S // ts),
        in_specs=[
            pl.BlockSpec((nb, ts, W), tok),
            pl.BlockSpec((nb, ts, W), tok),
            pl.BlockSpec((nb, ts, W), tok),
            pl.BlockSpec((nb, ts, W), tok),
            pl.BlockSpec((nb, ts, W), tok),
            pl.BlockSpec(_HGRN_M3.shape, lambda b, i: (0, 0)),
            pl.BlockSpec(_HGRN_MASKS.shape, lambda b, i: (0, 0, 0)),
            pl.BlockSpec((1, 1, B_VAL_DIM), lambda b, i: (l, 0, 0)),
        ],
        out_specs=pl.BlockSpec((nb, ts, W), tok),
        out_shape=jax.ShapeDtypeStruct((B, S, W), BF16),
        scratch_shapes=[pltpu.VMEM((nb, B_HEADS, B_VAL_DIM, B_KEY_DIM), F32)],
        compiler_params=_cparams(("arbitrary", "arbitrary")),
        name="hgrn2",
    )(hq, hk, hlf, hv, hgate, jnp.asarray(_HGRN_M3, BF16), jnp.asarray(_HGRN_MASKS), gnorm)


def _layernorm(v, g, b):
    mu = jnp.mean(v, axis=-1, keepdims=True)
    d = v - mu
    var = jnp.mean(d * d, axis=-1, keepdims=True)
    return d * lax.rsqrt(var + LN_EPS) * g + b


def _first_argmax(v, idx, axes, big):
    mx = v
    for ax in axes:
        mx = jnp.max(mx, axis=ax, keepdims=True)
    pos = jnp.where(v == mx, idx, big)
    for ax in axes:
        pos = jnp.min(pos, axis=ax, keepdims=True)
    return mx, pos


def _outproj_kernel(ya_ref, yb_ref, x_ref, mod_ref, wo_ref, lng_ref, lnb_ref, wrT_ref, rbias_ref, tri_ref,
                    x1_ref, u2_ref, gates_ref, rank_ref, gatesT_ref, *, alpha):
    y = jnp.dot(ya_ref[0], wo_ref[0, 0:A_WIDTH, :], preferred_element_type=F32)
    y = y + jnp.dot(yb_ref[0], wo_ref[0, A_WIDTH:, :], preferred_element_type=F32)
    g1 = mod_ref[0, 2:3, :]
    x1 = _layernorm(alpha * x_ref[0] + (1.0 + g1) * y, lng_ref[0], lnb_ref[0])
    x1_ref[0] = x1
    u2 = (x1 * (1.0 + mod_ref[0, 4:5, :]) + mod_ref[0, 3:4, :]).astype(BF16)
    u2_ref[0] = u2

    tm = u2.shape[0]
    gsz = N_EXPERTS // N_GROUPS
    scores = 1.0 / (1.0 + jnp.exp(-_nt_dot(wrT_ref[0], u2)))
    sel = (scores + rbias_ref[0]).reshape(N_GROUPS, gsz, tm)
    scores = scores.reshape(N_GROUPS, gsz, tm)
    i_m = lax.broadcasted_iota(I32, (N_GROUPS, gsz, tm), 1)
    i_g = lax.broadcasted_iota(I32, (N_GROUPS, 1, tm), 0)
    i_e = lax.broadcasted_iota(I32, (N_GROUPS, gsz, tm), 0) * gsz + i_m
    m1, p1 = _first_argmax(sel, i_m, (1,), gsz)
    m2 = jnp.max(jnp.where(i_m == p1, NEG_INF, sel), axis=1, keepdims=True)
    gs = m1 + m2
    gmask = jnp.zeros(gs.shape, F32)
    for _ in range(TOPK_GROUPS):
        _, pg = _first_argmax(gs, i_g, (0,), N_GROUPS)
        hit = i_g == pg
        gmask = jnp.where(hit, 1.0, gmask)
        gs = jnp.where(hit, NEG_INF, gs)
    cand = jnp.where(jnp.broadcast_to(gmask, sel.shape) > 0.0, sel, NEG_INF)
    w = jnp.zeros(sel.shape, F32)
    chosen = jnp.zeros(sel.shape, F32)
    for _ in range(TOP_K):
        _, pe = _first_argmax(cand, i_e, (1, 0), N_EXPERTS)
        hit = i_e == pe
        w = jnp.where(hit, scores, w)
        chosen = jnp.where(hit, 1.0, chosen)
        cand = jnp.where(hit, NEG_INF, cand)
    wsum = jnp.sum(jnp.sum(w, axis=1, keepdims=True), axis=0, keepdims=True)
    gates = (w / wsum * ROUTED_SCALE).reshape(N_EXPERTS, tm)
    g_hi = gates.astype(BF16).astype(F32)
    g_lo = (gates - g_hi).astype(BF16).astype(F32)
    gates_ref[0] = jnp.concatenate([g_hi, g_lo], axis=0).T.astype(BF16)

    chosen2 = chosen.reshape(N_EXPERTS, tm)
    rank = jnp.concatenate(
        [jnp.dot(chosen2[:, g * MOE_GROUP:(g + 1) * MOE_GROUP].astype(BF16), tri_ref[...],
                 preferred_element_type=F32) for g in range(tm // MOE_GROUP)], axis=1)
    rank_ref[0] = jnp.where(chosen2 > 0.0, rank, -1.0).astype(I32)
    gatesT_ref[0] = gates


def _outproj(l, ya, yb, x, mod, wo, ln_g, ln_b, wrT, rbias, alpha):
    B, S, D = x.shape
    tm = TM_PROJ
    tok = lambda b, i: (b, i, 0)
    lw3 = lambda b, i: (l, 0, 0)
    return pl.pallas_call(
        functools.partial(_outproj_kernel, alpha=alpha),
        grid=(B, S // tm),
        in_specs=[
            pl.BlockSpec((1, tm, A_WIDTH), tok),
            pl.BlockSpec((1, tm, B_WIDTH), tok),
            pl.BlockSpec((1, tm, D), tok),
            pl.BlockSpec((1, 6, D), lambda b, i: (b, 0, 0)),
            pl.BlockSpec((1, D, D), lw3),
            pl.BlockSpec((1, 1, D), lw3),
            pl.BlockSpec((1, 1, D), lw3),
            pl.BlockSpec((1, N_EXPERTS, D), lw3),
            pl.BlockSpec((1, N_EXPERTS, tm), lw3),
            pl.BlockSpec((MOE_GROUP, MOE_GROUP), lambda b, i: (0, 0)),
        ],
        out_specs=[pl.BlockSpec((1, tm, D), tok), pl.BlockSpec((1, tm, D), tok),
                   pl.BlockSpec((1, tm, 2 * N_EXPERTS), tok),
                   pl.BlockSpec((1, N_EXPERTS, tm), lambda b, i: (b, 0, i)),
                   pl.BlockSpec((1, N_EXPERTS, tm), lambda b, i: (b, 0, i))],
        out_shape=[jax.ShapeDtypeStruct((B, S, D), F32), jax.ShapeDtypeStruct((B, S, D), BF16),
                   jax.ShapeDtypeStruct((B, S, 2 * N_EXPERTS), BF16),
                   jax.ShapeDtypeStruct((B, N_EXPERTS, S), I32),
                   jax.ShapeDtypeStruct((B, N_EXPERTS, S), F32)],
        compiler_params=_cparams(("arbitrary", "arbitrary")),
        name="outproj_router",
    )(ya, yb, x, mod, wo, ln_g, ln_b, wrT, rbias,
      jnp.asarray(np.triu(np.ones((MOE_GROUP, MOE_GROUP), np.float32), 1), BF16))


MOE_CHUNK_EXPERTS = 8


def _slot_onehot(rank_rows, values, cap):
    row = lax.broadcasted_iota(I32, (cap, rank_rows.shape[1]), 0)
    return jnp.concatenate(
        [jnp.where(row == rank_rows[e:e + 1, :], values[e:e + 1, :], 0.0) for e in range(rank_rows.shape[0])], axis=0)


def _dispatch_kernel(u_ref, rank_ref, x_ref):
    u = u_ref[...]
    cap = x_ref.shape[1]
    ones = jnp.ones((MOE_CHUNK_EXPERTS, MOE_GROUP), F32)
    for c in range(N_EXPERTS // MOE_CHUNK_EXPERTS):
        es = slice(c * MOE_CHUNK_EXPERTS, (c + 1) * MOE_CHUNK_EXPERTS)
        onehot = _slot_onehot(rank_ref[0, es, :], ones, cap)
        xs = jnp.dot(onehot.astype(BF16), u, preferred_element_type=F32).astype(BF16)
        x_ref[es] = xs.reshape(MOE_CHUNK_EXPERTS, cap, -1)


def _dispatch(u2, rank, cap):
    T, D = u2.shape
    ng = T // MOE_GROUP
    gps = rank.shape[-1] // MOE_GROUP
    return pl.pallas_call(
        _dispatch_kernel,
        grid=(ng,),
        in_specs=[pl.BlockSpec((MOE_GROUP, D), lambda g: (g, 0)),
                  pl.BlockSpec((1, N_EXPERTS, MOE_GROUP), lambda g: (g // gps, 0, g % gps))],
        out_specs=pl.BlockSpec((N_EXPERTS, cap, D), lambda g: (0, g, 0)),
        out_shape=jax.ShapeDtypeStruct((N_EXPERTS, ng * cap, D), BF16),
        compiler_params=_cparams(("arbitrary",)),
        name="moe_dispatch",
    )(u2, rank)


def _expert_kernel(x_ref, wgu_ref, wd_ref, y_ref):
    hgu = jnp.dot(x_ref[0], wgu_ref[0, 0], preferred_element_type=F32)
    h = _silu(hgu[:, :EXPERT_DIM]) * hgu[:, EXPERT_DIM:]
    y_ref[0] = jnp.dot(h.astype(BF16), wd_ref[0, 0], preferred_element_type=F32).astype(BF16)


def _experts(l, xs, wgu, wd):
    E, R, D = xs.shape
    tr = R // MOE_EXPERT_STEPS
    assert R % MOE_EXPERT_STEPS == 0 and tr % 16 == 0
    return pl.pallas_call(
        _expert_kernel,
        grid=(E, R // tr),
        in_specs=[pl.BlockSpec((1, tr, D), lambda e, i: (e, i, 0)),
                  pl.BlockSpec((1, 1, D, 2 * EXPERT_DIM), lambda e, i: (l, e, 0, 0)),
                  pl.BlockSpec((1, 1, EXPERT_DIM, D), lambda e, i: (l, e, 0, 0))],
        out_specs=pl.BlockSpec((1, tr, D), lambda e, i: (e, i, 0)),
        out_shape=jax.ShapeDtypeStruct((E, R, D), BF16),
        compiler_params=_cparams(("arbitrary", "arbitrary")),
        name="moe_experts",
    )(xs, wgu, wd)


def _combine_kernel(y_ref, rank_ref, gates_ref, u_ref, x1_ref, mod_ref, sgu_ref, sd_ref, lng_ref, lnb_ref,
                    out_ref, *, alpha):
    hgu = jnp.dot(u_ref[...], sgu_ref[0], preferred_element_type=F32)
    hs = _silu(hgu[:, :SHARED_DIM]) * hgu[:, SHARED_DIM:]
    y = jnp.dot(hs.astype(BF16), sd_ref[0], preferred_element_type=F32)
    cap = y_ref.shape[1]
    for c in range(N_EXPERTS // MOE_CHUNK_EXPERTS):
        es = slice(c * MOE_CHUNK_EXPERTS, (c + 1) * MOE_CHUNK_EXPERTS)
        pick = _slot_onehot(rank_ref[0, es, :], gates_ref[0, es, :], cap)
        ys = y_ref[es].reshape(MOE_CHUNK_EXPERTS * cap, -1)
        y = y + lax.dot_general(pick.astype(BF16), ys, (((0,), (0,)), ((), ())), preferred_element_type=F32)
    g2 = mod_ref[0, 5:6, :]
    out_ref[...] = _layernorm(alpha * x1_ref[...] + (1.0 + g2) * y, lng_ref[0], lnb_ref[0])


def _combine(l, ys, rank, gates_t, u2, x1, mod, sgu, sd, ln_g, ln_b, alpha, seq):
    T, D = u2.shape
    tok = lambda g: (g, 0)
    lw3 = lambda g: (l, 0, 0)
    gps = seq // MOE_GROUP
    cap = ys.shape[1] // (T // MOE_GROUP)
    per_group = lambda g: (g // gps, 0, g % gps)
    return pl.pallas_call(
        functools.partial(_combine_kernel, alpha=alpha),
        grid=(T // MOE_GROUP,),
        in_specs=[
            pl.BlockSpec((N_EXPERTS, cap, D), lambda g: (0, g, 0)),
            pl.BlockSpec((1, N_EXPERTS, MOE_GROUP), per_group),
            pl.BlockSpec((1, N_EXPERTS, MOE_GROUP), per_group),
            pl.BlockSpec((MOE_GROUP, D), tok),
            pl.BlockSpec((MOE_GROUP, D), tok),
            pl.BlockSpec((1, 6, D), lambda g: ((g * MOE_GROUP) // seq, 0, 0)),
            pl.BlockSpec((1, D, 2 * SHARED_DIM), lw3),
            pl.BlockSpec((1, SHARED_DIM, D), lw3),
            pl.BlockSpec((1, 1, D), lw3),
            pl.BlockSpec((1, 1, D), lw3),
        ],
        out_specs=pl.BlockSpec((MOE_GROUP, D), tok),
        out_shape=jax.ShapeDtypeStruct((T, D), F32),
        compiler_params=_cparams(("arbitrary",)),
        name="moe_combine",
    )(ys, rank, gates_t, u2, x1, mod, sgu, sd, ln_g, ln_b)


def _moe_kernel(u_ref, gates_ref, x1_ref, mod_ref, wgu_ref, wd_ref, sgu_ref, sd_ref, lng_ref, lnb_ref,
                out_ref, acc_ref, *, alpha):
    s = pl.program_id(1)
    u = u_ref[...]

    def hidden(wgu):
        hgu = jnp.dot(u, wgu, preferred_element_type=F32)
        return _silu(hgu[:, :EXPERT_DIM]) * hgu[:, EXPERT_DIM:]

    @pl.when(s == 0)
    def _():
        acc_ref[...] = jnp.dot(hidden(sgu_ref[0]).astype(BF16), sd_ref[0], preferred_element_type=F32)

    rows = lax.broadcasted_iota(I32, (2 * N_EXPERTS, MOE_EXPERTS_PER_STEP * EXPERT_DIM), 0) & (N_EXPERTS - 1)
    cols = lax.broadcasted_iota(I32, (2 * N_EXPERTS, MOE_EXPERTS_PER_STEP * EXPERT_DIM), 1)
    onehot = jnp.where(rows == s * MOE_EXPERTS_PER_STEP + cols // EXPERT_DIM, 1.0, 0.0).astype(BF16)
    gate = jnp.dot(gates_ref[...], onehot, preferred_element_type=F32)
    h = jnp.concatenate(
        [(hidden(wgu_ref[0, k]) * gate[:, k * EXPERT_DIM:(k + 1) * EXPERT_DIM]).astype(BF16)
         for k in range(MOE_EXPERTS_PER_STEP)], axis=1)
    wd = wd_ref[0].reshape(MOE_EXPERTS_PER_STEP * EXPERT_DIM, wd_ref.shape[-1])
    acc_ref[...] += jnp.dot(h, wd, preferred_element_type=F32)

    @pl.when(s == pl.num_programs(1) - 1)
    def _():
        g2 = mod_ref[0, 5:6, :]
        out_ref[...] = _layernorm(alpha * x1_ref[...] + (1.0 + g2) * acc_ref[...], lng_ref[0], lnb_ref[0])


def _moe(l, u2, gates, x1, mod, wgu, wd, sgu, sd, ln_g, ln_b, alpha, seq):
    T, D = u2.shape
    tm = TM_MOE
    assert seq % tm == 0
    tok = lambda i, e: (i, 0)
    lw3 = lambda i, e: (l, 0, 0)
    return pl.pallas_call(
        functools.partial(_moe_kernel, alpha=alpha),
        grid=(T // tm, N_EXPERTS // MOE_EXPERTS_PER_STEP),
        in_specs=[
            pl.BlockSpec((tm, D), tok),
            pl.BlockSpec((tm, 2 * N_EXPERTS), tok),
            pl.BlockSpec((tm, D), tok),
            pl.BlockSpec((1, 6, D), lambda i, e: ((i * tm) // seq, 0, 0)),
            pl.BlockSpec((1, MOE_EXPERTS_PER_STEP, D, 2 * EXPERT_DIM), lambda i, e: (l, e, 0, 0)),
            pl.BlockSpec((1, MOE_EXPERTS_PER_STEP, EXPERT_DIM, D), lambda i, e: (l, e, 0, 0)),
            pl.BlockSpec((1, D, 2 * SHARED_DIM), lw3),
            pl.BlockSpec((1, SHARED_DIM, D), lw3),
            pl.BlockSpec((1, 1, D), lw3),
            pl.BlockSpec((1, 1, D), lw3),
        ],
        out_specs=pl.BlockSpec((tm, D), tok),
        out_shape=jax.ShapeDtypeStruct((T, D), F32),
        scratch_shapes=[pltpu.VMEM((tm, D), F32)],
        compiler_params=_cparams(("arbitrary", "arbitrary")),
        name="moe_dense",
    )(u2, gates, x1, mod, wgu, wd, sgu, sd, ln_g, ln_b)


def _prepare_params(w_in, kv_norm_g, w_uk, w_uv, hgrn_lb, w_out, w_router, router_bias,
                    w_gate, w_up, w_down, ws_gate, ws_up, ws_down):
    L = w_in.shape[0]
    sizes = (A_WIDTH, KV_RANK, IDX_HEADS * IDX_DIM, IDX_DIM, IDX_HEADS, B_FDIM, B_FDIM, B_WIDTH, B_WIDTH)
    offs = np.concatenate([[0], np.cumsum(sizes)])
    seg = lambda i: w_in[:, :, offs[i]:offs[i + 1]]
    w_aq, w_ckv, w_iq, w_ik, w_iw, w_hq, w_hf, w_hi, w_hg = (seg(i) for i in range(9))
    zik = jnp.zeros_like(w_ik)
    wp = jnp.concatenate([w_ckv, w_ik, zik, zik, w_ik, w_hq, w_hf, w_hg, w_hi], axis=-1).astype(BF16)
    wt = jnp.swapaxes(jnp.concatenate([w_aq, w_iq, w_ckv, w_iw], axis=-1), 1, 2).astype(BF16)
    assert wp.shape[-1] == _C_END and wt.shape[1] == _R_END
    eye = jnp.eye(A_HEADS, dtype=F32)
    wblk = (jnp.einsum('lhdr,hg->lhdgr', w_uk * (ATTN_SCALE * LOG2E), eye)
            .reshape(L, A_WIDTH, A_HEADS * KV_RANK).astype(BF16))
    p = dict(
        wp=wp, wt=wt, wblkT=jnp.swapaxes(wblk, 1, 2),
        gkv=kv_norm_g.reshape(L, 1, KV_RANK),
        gkvT=jnp.broadcast_to(kv_norm_g[:, :, None], (L, KV_RANK, TM_PROJ)),
        wuvT=jnp.swapaxes(w_uv, 2, 3).astype(BF16),
        wo=w_out.astype(BF16),
        wrT=jnp.swapaxes(w_router, 1, 2).astype(BF16),
        rbias=jnp.broadcast_to(router_bias[:, :, None], (L, N_EXPERTS, TM_PROJ)),
        wgu=jnp.concatenate([w_gate, w_up], axis=-1).astype(BF16),
        wd=w_down.astype(BF16),
        sgu=jnp.concatenate([ws_gate, ws_up], axis=-1).astype(BF16),
        sd=ws_down.astype(BF16),
    )
    lbs = jnp.cumsum(jax.nn.softmax(hgrn_lb.astype(F32), axis=0), axis=0)
    lbs = jnp.clip(lbs - lbs[0:1], 0.0, 1.0 - 1e-6)
    p["llb"] = jnp.log(lbs).reshape(L, 1, B_FDIM)
    p["l1m"] = jnp.log1p(-lbs).reshape(L, 1, B_FDIM)
    return p


def kernel(x, c, w_ada, b_ada, w_in, kv_norm_g, w_uk, w_uv, rel_bias, hgrn_lb, gnorm_g, w_out, ln1_g, ln1_b,
           w_router, router_bias, w_gate, w_up, w_down, ws_gate, ws_up, ws_down, ln2_g, ln2_b):
    B, S, D = x.shape
    L = w_in.shape[0]
    alpha = (2 * L) ** 0.25
    p = _prepare_params(w_in, kv_norm_g, w_uk, w_uv, hgrn_lb, w_out, w_router, router_bias,
                        w_gate, w_up, w_down, ws_gate, ws_up, ws_down)
    mods = _adaln(c, w_ada, b_ada).reshape(L, B, 6, D)
    bn = _bias_tile(rel_bias)
    gn = gnorm_g.reshape(L, 1, B_VAL_DIM)
    ln1g, ln1b = ln1_g.reshape(L, 1, D), ln1_b.reshape(L, 1, D)
    ln2g, ln2b = ln2_g.reshape(L, 1, D), ln2_b.reshape(L, 1, D)
    for l in range(L):
        mod = mods[l]
        (qlat, ckv, ckvT, iq, ikA, ikB, iwT, hq, hk, hlf, hv, hgate) = _inproj(
            l, x, mod, p["wp"], p["wt"], p["wblkT"], p["gkv"], p["gkvT"], p["llb"], p["l1m"])
        ya = _dsa(l, iq, iwT, qlat, ikA, ikB, ckv, ckvT, bn, p["wuvT"])
        yb = _hgrn(l, hq, hk, hlf, hv, hgate, gn)
        x1, u2, gates, rank, gates_t = _outproj(l, ya, yb, x, mod, p["wo"], ln1g, ln1b, p["wrT"], p["rbias"], alpha)
        u2f, x1f = u2.reshape(B * S, D), x1.reshape(B * S, D)

        def moe_sparse(cap, l=l, mod=mod, u2f=u2f, x1f=x1f, rank=rank, gates_t=gates_t):
            ys = _experts(l, _dispatch(u2f, rank, cap), p["wgu"], p["wd"])
            return _combine(l, ys, rank, gates_t, u2f, x1f, mod, p["sgu"], p["sd"], ln2g, ln2b, alpha, S)

        def moe_dense(l=l, mod=mod, u2f=u2f, x1f=x1f, gates=gates):
            return _moe(l, u2f, gates.reshape(B * S, 2 * N_EXPERTS), x1f, mod,
                        p["wgu"], p["wd"], p["sgu"], p["sd"], ln2g, ln2b, alpha, S)

        branches = [functools.partial(moe_sparse, cap) for cap in MOE_CAPS] + [moe_dense]
        tier = sum((jnp.max(rank) >= cap).astype(I32) for cap in MOE_CAPS)
        x = lax.switch(tier, branches).reshape(B, S, D)
    return x
```

```python
import functools
import math

import numpy as np
import jax
import jax.numpy as jnp
from jax import lax
from jax.experimental import pallas as pl
from jax.experimental.pallas import tpu as pltpu

F32 = jnp.float32
BF16 = jnp.bfloat16
I32 = jnp.int32

D_MODEL = 1024
CHUNK = 64
CHUNK_SHIFT = CHUNK.bit_length() - 1
assert 1 << CHUNK_SHIFT == CHUNK
A_HEADS = 8
A_HEAD_DIM = 64
A_WIDTH = A_HEADS * A_HEAD_DIM
KV_RANK = 128
IDX_HEADS = 8
IDX_DIM = 64
IDX_TOPK_MAX = 256
IDX_W_SCALE = (IDX_HEADS ** -0.5) * (IDX_DIM ** -0.5)
ATTN_SCALE = A_HEAD_DIM ** -0.5
LOG2E = math.log2(math.e)
KV_EXT = KV_RANK + 16
NUM_BUCKETS = 32
MAX_DISTANCE = 128
B_HEADS = 4
B_KEY_DIM = 128
B_VAL_DIM = 128
B_WIDTH = B_HEADS * B_VAL_DIM
B_FDIM = B_HEADS * B_KEY_DIM
N_EXPERTS = 64
TOP_K = 8
N_GROUPS = 8
TOPK_GROUPS = 4
EXPERT_DIM = 256
SHARED_DIM = 256
ROUTED_SCALE = 2.5
LN_EPS = 1e-5
RMS_EPS = 1e-6

LANES = 128
SUBLANES = 8
VMEM_LIMIT_BYTES = 56 * 1024 * 1024

INT_MIN = -(2 ** 31)
NEG_INF = float("-inf")

TM_PROJ = 1024
TQ = 128
UNIT = 512
NEAR = 2 * TQ
COUNT_ACCS = 8
PLANE_ROWS = 32 * SUBLANES
TM_MOE = 1024
MOE_EXPERTS_PER_STEP = 4
MOE_GROUP = 256
MOE_CAPS = (64, 80)
MOE_EXPERT_STEPS = 4

_C_CKV, _C_IKA, _C_IKB, _C_HQ, _C_HF, _C_HG, _C_HI, _C_END = (0, 128, 256, 384, 896, 1408, 1920, 2432)
_R_AQ, _R_IQ, _R_CKV, _R_IW, _R_END = (0, 512, 1024, 1152, 1160)


def _silu(v):
    return v * (1.0 / (1.0 + jnp.exp(-v)))


def _nt_dot(a, b):
    return lax.dot_general(a, b, (((1,), (1,)), ((), ())), preferred_element_type=F32)


def _cparams(sem):
    return pltpu.CompilerParams(dimension_semantics=sem, vmem_limit_bytes=VMEM_LIMIT_BYTES)


def _adaln_kernel(c_ref, w_ref, b_ref, o_ref):
    cond = _silu(c_ref[...])
    o_ref[0] = jnp.dot(cond.astype(BF16), w_ref[0].astype(BF16), preferred_element_type=F32) + b_ref[0]


def _adaln(c, w_ada, b_ada):
    L, D, D6 = w_ada.shape
    B = c.shape[0]
    nb = D6 // D
    return pl.pallas_call(
        _adaln_kernel,
        grid=(L, nb),
        in_specs=[
            pl.BlockSpec((B, D), lambda l, j: (0, 0)),
            pl.BlockSpec((1, D, D), lambda l, j: (l, 0, j)),
            pl.BlockSpec((1, 1, D), lambda l, j: (l, 0, j)),
        ],
        out_specs=pl.BlockSpec((1, B, D), lambda l, j: (l, 0, j)),
        out_shape=jax.ShapeDtypeStruct((L, B, D6), F32),
        compiler_params=_cparams(("arbitrary", "arbitrary")),
        name="adaln_mod",
    )(c, w_ada, b_ada.reshape(L, 1, D6))


_T5_NB = NUM_BUCKETS // 2
_T5_EXACT = _T5_NB // 2
_T5_THRESHOLDS = tuple(
    int(math.ceil(_T5_EXACT * (MAX_DISTANCE / _T5_EXACT) ** (j / (_T5_NB - _T5_EXACT)) - 1e-9))
    for j in range(1, _T5_NB - _T5_EXACT))
FAR_BUCKET = _T5_NB - 1
assert _T5_THRESHOLDS[-1] <= TQ, "keys further than one query block behind must share the far bucket"


def _bias_kernel(rb_ref, o_ref):
    kr = lax.broadcasted_iota(I32, (NEAR + TQ, TQ), 0)
    ql = lax.broadcasted_iota(I32, (NEAR + TQ, TQ), 1)
    rel = kr - TQ - ql
    n = jnp.abs(rel)
    large = jnp.full(rel.shape, _T5_EXACT, I32)
    for t in _T5_THRESHOLDS:
        large = large + (n >= t).astype(I32)
    bucket = jnp.where(rel > 0, _T5_NB, 0) + jnp.where(n < _T5_EXACT, n, large)
    for h in range(A_HEADS):
        acc = jnp.zeros(rel.shape, F32)
        for bk in range(NUM_BUCKETS):
            acc = jnp.where(bucket == bk, rb_ref[bk, h], acc)
        o_ref[h] = (acc - rb_ref[FAR_BUCKET, h]) * LOG2E


def _bias_tile(rel_bias):
    return pl.pallas_call(
        _bias_kernel,
        in_specs=[pl.BlockSpec(memory_space=pltpu.SMEM)],
        out_specs=pl.BlockSpec(memory_space=pltpu.VMEM),
        out_shape=jax.ShapeDtypeStruct((A_HEADS, NEAR + TQ, TQ), F32),
        name="rel_bias_tile",
    )(rel_bias)


def _inproj_kernel(x_ref, mod_ref, wp_ref, wt_ref, wblkT_ref, gkv_ref, gkvT_ref, llb_ref, l1m_ref,
                   qlatT_ref, ckv_ref, ckvT_ref, iqT_ref, ikA_ref, ikB_ref, iwT_ref,
                   hq_ref, hk_ref, hlf_ref, hv_ref, hgate_ref):
    x = x_ref[0]
    sh1 = mod_ref[0, 0:1, :]
    sc1 = mod_ref[0, 1:2, :]
    u = (x * (1.0 + sc1) + sh1).astype(BF16)
    z = jnp.dot(u, wp_ref[0], preferred_element_type=F32)
    zt = _nt_dot(wt_ref[0], u)

    def proj(lo, hi):
        return z[:, lo:hi]

    qlatT_ref[0] = jnp.dot(wblkT_ref[0], zt[_R_AQ:_R_IQ].astype(BF16), preferred_element_type=F32).astype(BF16)

    zc = proj(_C_CKV, _C_IKA)
    inv = lax.rsqrt(jnp.mean(zc * zc, axis=-1, keepdims=True) + RMS_EPS)
    ckv_ref[0] = (zc * inv * gkv_ref[0]).astype(BF16)
    zct = zt[_R_CKV:_R_IW]
    inv_t = lax.rsqrt(jnp.mean(zct * zct, axis=0, keepdims=True) + RMS_EPS)
    ckvT_ref[0, 0:KV_RANK, :] = (zct * inv_t * gkvT_ref[0]).astype(BF16)
    ckvT_ref[0, KV_RANK:KV_EXT, :] = jnp.ones((KV_EXT - KV_RANK, zct.shape[1]), BF16)

    iqT_ref[0] = zt[_R_IQ:_R_CKV].astype(BF16)
    ikA_ref[0] = proj(_C_IKA, _C_IKB).astype(BF16)
    ikB_ref[0] = proj(_C_IKB, _C_HQ).astype(BF16)
    iwT_ref[0] = zt[_R_IW:_R_END] * IDX_W_SCALE

    hq_ref[0] = _silu(proj(_C_HQ, _C_HF))
    zf = proj(_C_HF, _C_HG)
    log_sig = jnp.minimum(zf, 0.0) - jnp.log1p(jnp.exp(-jnp.abs(zf)))
    a = llb_ref[0]
    c = l1m_ref[0] + log_sig
    logf = jnp.maximum(a, c) + jnp.log1p(jnp.exp(-jnp.abs(a - c)))
    hlf_ref[0] = logf
    hk_ref[0] = 1.0 - jnp.exp(logf)
    hgate_ref[0] = _silu(proj(_C_HG, _C_HI))
    hv_ref[0] = proj(_C_HI, _C_END).astype(BF16)


def _inproj(l, x, mod, wp, wt, wblkT, gkv, gkvT, llb, l1m):
    B, S, D = x.shape
    tm = TM_PROJ
    grid = (B, S // tm)
    lw3 = lambda b, i: (l, 0, 0)
    tok = lambda b, i: (b, i, 0)
    tokT = lambda b, i: (b, 0, i)
    hd4 = lambda b, i: (b, 0, i, 0)
    outs = [
        (jax.ShapeDtypeStruct((B, A_HEADS * KV_RANK, S), BF16), pl.BlockSpec((1, A_HEADS * KV_RANK, tm), tokT)),
        (jax.ShapeDtypeStruct((B, S, KV_RANK), BF16), pl.BlockSpec((1, tm, KV_RANK), tok)),
        (jax.ShapeDtypeStruct((B, KV_EXT, S), BF16), pl.BlockSpec((1, KV_EXT, tm), tokT)),
        (jax.ShapeDtypeStruct((B, IDX_HEADS * IDX_DIM, S), BF16), pl.BlockSpec((1, IDX_HEADS * IDX_DIM, tm), tokT)),
        (jax.ShapeDtypeStruct((B, S, LANES), BF16), pl.BlockSpec((1, tm, LANES), tok)),
        (jax.ShapeDtypeStruct((B, S, LANES), BF16), pl.BlockSpec((1, tm, LANES), tok)),
        (jax.ShapeDtypeStruct((B, IDX_HEADS, S), F32), pl.BlockSpec((1, IDX_HEADS, tm), tokT)),
        (jax.ShapeDtypeStruct((B, S, B_FDIM), F32), pl.BlockSpec((1, tm, B_FDIM), tok)),
        (jax.ShapeDtypeStruct((B, S, B_FDIM), F32), pl.BlockSpec((1, tm, B_FDIM), tok)),
        (jax.ShapeDtypeStruct((B, S, B_FDIM), F32), pl.BlockSpec((1, tm, B_FDIM), tok)),
        (jax.ShapeDtypeStruct((B, S, B_WIDTH), BF16), pl.BlockSpec((1, tm, B_WIDTH), tok)),
        (jax.ShapeDtypeStruct((B, S, B_WIDTH), F32), pl.BlockSpec((1, tm, B_WIDTH), tok)),
    ]
    return pl.pallas_call(
        _inproj_kernel,
        grid=grid,
        in_specs=[
            pl.BlockSpec((1, tm, D), tok),
            pl.BlockSpec((1, 6, D), lambda b, i: (b, 0, 0)),
            pl.BlockSpec((1, D, _C_END), lw3),
            pl.BlockSpec((1, _R_END, D), lw3),
            pl.BlockSpec((1, A_HEADS * KV_RANK, A_WIDTH), lw3),
            pl.BlockSpec((1, 1, KV_RANK), lw3),
            pl.BlockSpec((1, KV_RANK, tm), lw3),
            pl.BlockSpec((1, 1, B_FDIM), lw3),
            pl.BlockSpec((1, 1, B_FDIM), lw3),
        ],
        out_specs=[o[1] for o in outs],
        out_shape=[o[0] for o in outs],
        compiler_params=_cparams(("arbitrary", "arbitrary")),
        name="inproj",
    )(x, mod, wp, wt, wblkT, gkv, gkvT, llb, l1m)


def _dsa_kernel(iq_ref, iwT_ref, qlat_ref, ikA_ref, ikB_ref, ckv_ref, ckvT_ref, bn_ref, wuvT_ref, out_ref,
                sc_ref, plane_ref, madd_ref, maddn_ref, la_ref, lb_ref, pma_ref, pmb_ref, ot_ref, yaT_ref,
                *, k_sel, n_idx_bits):
    j = pl.program_id(1)
    q0 = j * TQ
    nk = q0 + TQ
    nunit = (nk + UNIT - 1) // UNIT
    near0 = pl.multiple_of(jnp.maximum(nk - NEAR, 0), TQ)
    bn_row0 = pl.multiple_of(jnp.where(j == 0, TQ, 0), TQ)
    lane = lax.broadcasted_iota(I32, (1, TQ), 1)
    limit = (((q0 + lane) >> CHUNK_SHIFT) + 1) << CHUNK_SHIFT
    row_iota = lax.broadcasted_iota(I32, (UNIT, TQ), 0)

    def unit_rows(u):
        return pl.ds(pl.multiple_of(u * UNIT, UNIT), UNIT)

    @pl.when((pl.program_id(0) == 0) & (j == 0))
    def _():
        plane_ref[...] = jnp.zeros(plane_ref.shape, I32)
        madd_ref[...] = jnp.full(madd_ref.shape, NEG_INF, F32)

    iqs = jnp.concatenate([iq_ref[0, p * LANES:(p + 1) * LANES, :] for p in range(IDX_HEADS // 2)], axis=1)
    iw = iwT_ref[0]

    last_unit = sc_ref.shape[0] // UNIT - 1
    half = IDX_HEADS // 2 * TQ

    def issue_scores(u, buf_ref):
        rows = unit_rows(jnp.minimum(u, last_unit))
        buf_ref[:, 0:half] = jnp.dot(ikA_ref[0, rows, :], iqs, preferred_element_type=F32)
        buf_ref[:, half:2 * half] = jnp.dot(ikB_ref[0, rows, :], iqs, preferred_element_type=F32)

    def reduce_scores(u, buf_ref):
        acc = jnp.zeros((UNIT, TQ), F32)
        for p in range(IDX_HEADS // 2):
            acc = acc + iw[2 * p:2 * p + 1, :] * jnp.maximum(buf_ref[:, p * TQ:(p + 1) * TQ], 0.0)
            acc = acc + iw[2 * p + 1:2 * p + 2, :] * jnp.maximum(buf_ref[:, half + p * TQ:half + (p + 1) * TQ], 0.0)
        bits = lax.bitcast_convert_type(acc, I32)
        key = bits ^ ((bits >> 31) & 0x7FFFFFFF)
        sc_ref[unit_rows(u), :] = jnp.where(row_iota + u * UNIT < limit, key, INT_MIN)

    issue_scores(0, la_ref)

    def score_pair(i, carry):
        issue_scores(2 * i + 1, lb_ref)
        reduce_scores(2 * i, la_ref)
        issue_scores(2 * i + 2, la_ref)
        reduce_scores(2 * i + 1, lb_ref)
        return carry

    lax.fori_loop(0, nunit // 2, score_pair, 0)

    @pl.when(nunit % 2 == 1)
    def _():
        reduce_scores(nunit - 1, la_ref)

    ngroups = (nk + PLANE_ROWS - 1) // PLANE_ROWS

    def plane_group(g, carry):
        rows = pl.ds(pl.multiple_of(g * PLANE_ROWS, PLANE_ROWS), PLANE_ROWS)
        words = (sc_ref[rows, :] ^ INT_MIN).reshape(32, SUBLANES, TQ)
        w = [words[i] for i in range(32)]
        j, m = 16, 0x0000FFFF
        while j:
            mask = np.int32(np.uint32(m).view(np.int32))
            k = 0
            while k < 32:
                t = (w[k] ^ lax.shift_right_logical(w[k + j], jnp.full(w[k].shape, j, I32))) & mask
                w[k] = w[k] ^ t
                w[k + j] = w[k + j] ^ (t << j)
                k = (k + j + 1) & ~j
            j >>= 1
            m = (m ^ (m << j)) & 0xFFFFFFFF
        for i in range(32):
            plane_ref[i, pl.ds(g * SUBLANES, SUBLANES), :] = w[i]
        return carry

    lax.fori_loop(0, ngroups, plane_group, 0)

    n_words = sc_ref.shape[0] // PLANE_ROWS * SUBLANES
    group_of_word = lax.broadcasted_iota(I32, (n_words, TQ), 0) // SUBLANES

    def bit_step(i, carry):
        alive, above, t_off, c_ge = carry
        hit = alive & plane_ref[i]
        cnt = above + jnp.sum(lax.population_count(hit), axis=0, keepdims=True)
        ok = cnt >= k_sel
        alive = jnp.where(ok, hit, alive ^ hit)
        above = jnp.where(ok, above, cnt)
        t_off = jnp.where(ok, t_off | (jnp.int32(1) << (31 - i)), t_off)
        return alive, above, t_off, jnp.where(ok, cnt, c_ge)

    zero_row = jnp.zeros((1, TQ), I32)
    _, _, t_off, c_ge = lax.fori_loop(
        0, 32, bit_step,
        (jnp.where(group_of_word < ngroups, jnp.int32(-1), jnp.int32(0)), zero_row, zero_row, zero_row))
    thr = jnp.maximum(t_off ^ INT_MIN, INT_MIN + 1)
    straddle = (c_ge > k_sel).astype(I32)

    def count_where(pred):
        def body(u, acc):
            hit = pred(sc_ref[unit_rows(u), :], u * UNIT).reshape(-1, COUNT_ACCS * SUBLANES, TQ)
            for s in range(hit.shape[0]):
                acc = jnp.where(hit[s], acc + 1, acc)
            return acc
        acc = lax.fori_loop(0, nunit, body, jnp.zeros((COUNT_ACCS * SUBLANES, TQ), I32))
        return jnp.sum(acc, axis=0, keepdims=True)

    def tie_bound():
        c_gt = count_where(lambda blk, r0: blk > thr)
        need = k_sel - c_gt

        def tie_body(i, j0):
            cand = j0 | (jnp.int32(1) << (n_idx_bits - 1 - i))
            cnt = count_where(lambda blk, r0: jnp.where(blk == thr, row_iota + r0, cand) < cand)
            return jnp.where(cnt < need, cand, j0)

        j0 = lax.fori_loop(0, n_idx_bits, tie_body, jnp.zeros((1, TQ), I32))
        return jnp.where(straddle > 0, j0 + 1, jnp.int32(2 ** n_idx_bits))

    jstar = lax.cond(jnp.max(straddle) > 0, tie_bound, lambda: jnp.full((1, TQ), 2 ** n_idx_bits, I32))

    def madd_unit(u, carry):
        rows = unit_rows(u)
        key = sc_ref[rows, :]
        tie_keep = jnp.where(row_iota + u * UNIT < jstar, 0.0, NEG_INF)
        madd_ref[rows, :] = jnp.where(key > thr, 0.0, jnp.where(key == thr, tie_keep, NEG_INF))
        return carry

    lax.fori_loop(0, nunit, madd_unit, 0)
    maddn_ref[...] = madd_ref[pl.ds(near0, NEAR), :]
    madd_ref[pl.ds(near0, NEAR), :] = jnp.full((NEAR, TQ), NEG_INF, F32)

    qall = jnp.concatenate([qlat_ref[0, h * KV_RANK:(h + 1) * KV_RANK, :] for h in range(A_HEADS)], axis=1)

    def col_max(v):
        return jnp.max(v.reshape(v.shape[0] // SUBLANES, SUBLANES, A_HEADS * TQ), axis=0)

    def fold(xl, part_max, ckv_t, m_old):
        m_new = jnp.maximum(m_old, jnp.max(part_max, axis=0, keepdims=True))
        m_use = jnp.where(m_new == NEG_INF, 0.0, m_new)
        p = jnp.exp2(xl - m_use).astype(BF16)
        ot_ref[...] = ot_ref[...] * jnp.exp2(m_old - m_use) + jnp.dot(ckv_t, p, preferred_element_type=F32)
        return m_new

    ot_ref[...] = jnp.zeros(ot_ref.shape, F32)
    near_rows = pl.ds(near0, NEAR)
    xn = jnp.dot(ckv_ref[0, near_rows, :], qall, preferred_element_type=F32)
    xn = xn + jnp.concatenate([maddn_ref[...]] * A_HEADS, axis=1)
    xn = xn + jnp.concatenate([bn_ref[h, pl.ds(bn_row0, NEAR), :] for h in range(A_HEADS)], axis=1)
    m_run = fold(xn, col_max(xn), ckvT_ref[0, :, near_rows], jnp.full((1, A_HEADS * TQ), NEG_INF, F32))


    def issue_logits(u, buf_ref, pm_ref):
        rows = unit_rows(jnp.minimum(u, last_unit))
        xl = jnp.dot(ckv_ref[0, rows, :], qall, preferred_element_type=F32)
        xl = xl + jnp.concatenate([madd_ref[rows, :]] * A_HEADS, axis=1)
        buf_ref[...] = xl
        pm_ref[...] = col_max(xl)

    def consume_logits(u, buf_ref, pm_ref, m_old):
        return fold(buf_ref[...], pm_ref[...], ckvT_ref[0, :, unit_rows(u)], m_old)

    issue_logits(0, la_ref, pma_ref)

    def pair_step(i, m_old):
        issue_logits(2 * i + 1, lb_ref, pmb_ref)
        m_mid = consume_logits(2 * i, la_ref, pma_ref, m_old)
        issue_logits(2 * i + 2, la_ref, pma_ref)
        return consume_logits(2 * i + 1, lb_ref, pmb_ref, m_mid)

    nfar = (near0 + UNIT - 1) // UNIT
    m_run = lax.fori_loop(0, nfar // 2, pair_step, m_run)

    @pl.when(nfar % 2 == 1)
    def _():
        consume_logits(nfar - 1, la_ref, pma_ref, m_run)
    o_t = (ot_ref[0:KV_RANK, :] * (1.0 / ot_ref[KV_RANK:KV_RANK + 1, :])).astype(BF16)
    for h in range(A_HEADS):
        yaT_ref[h * A_HEAD_DIM:(h + 1) * A_HEAD_DIM, :] = jnp.dot(
            wuvT_ref[0, h], o_t[:, h * TQ:(h + 1) * TQ], preferred_element_type=F32)

    out_ref[0] = yaT_ref[...].T.astype(BF16)


def _dsa(l, iq, iwT, qlat, ikA, ikB, ckv, ckvT, bn, wuvT):
    B, S = ckv.shape[0], ckv.shape[1]
    assert S % (2 * UNIT) == 0 and UNIT % TQ == 0 and TQ % CHUNK == 0 and NEAR <= UNIT
    k_sel = min(IDX_TOPK_MAX, S // 4)
    n_idx_bits = int(math.log2(S))
    assert 2 ** n_idx_bits == S
    grid = (B, S // TQ)
    blk = lambda b, i: (b, 0, i, 0)
    full = lambda b, i: (b, 0, 0)
    kern = functools.partial(_dsa_kernel, k_sel=k_sel, n_idx_bits=n_idx_bits)
    return pl.pallas_call(
        kern,
        grid=grid,
        in_specs=[
            pl.BlockSpec((1, IDX_HEADS * IDX_DIM, TQ), lambda b, i: (b, 0, i)),
            pl.BlockSpec((1, IDX_HEADS, TQ), lambda b, i: (b, 0, i)),
            pl.BlockSpec((1, A_HEADS * KV_RANK, TQ), lambda b, i: (b, 0, i)),
            pl.BlockSpec((1, S, LANES), full),
            pl.BlockSpec((1, S, LANES), full),
            pl.BlockSpec((1, S, KV_RANK), full),
            pl.BlockSpec((1, KV_EXT, S), full),
            pl.BlockSpec((A_HEADS, NEAR + TQ, TQ), lambda b, i: (0, 0, 0)),
            pl.BlockSpec((1, A_HEADS, A_HEAD_DIM, KV_RANK), lambda b, i: (l, 0, 0, 0)),
        ],
        out_specs=pl.BlockSpec((1, TQ, A_WIDTH), lambda b, i: (b, i, 0)),
        out_shape=jax.ShapeDtypeStruct((B, S, A_WIDTH), BF16),
        scratch_shapes=[
            pltpu.VMEM((S, TQ), I32),
            pltpu.VMEM((32, S // PLANE_ROWS * SUBLANES, TQ), I32),
            pltpu.VMEM((S, TQ), F32),
            pltpu.VMEM((NEAR, TQ), F32),
            pltpu.VMEM((UNIT, A_HEADS * TQ), F32),
            pltpu.VMEM((UNIT, A_HEADS * TQ), F32),
            pltpu.VMEM((SUBLANES, A_HEADS * TQ), F32),
            pltpu.VMEM((SUBLANES, A_HEADS * TQ), F32),
            pltpu.VMEM((KV_EXT, A_HEADS * TQ), F32),
            pltpu.VMEM((A_WIDTH, TQ), F32),
        ],
        compiler_params=_cparams(("arbitrary", "arbitrary")),
        name="dsa_attention",
    )(iq, iwT, qlat, ikA, ikB, ckv, ckvT, bn, wuvT)


def _hgrn_constants():
    c = CHUNK
    r = np.arange(c)[:, None]
    jj = np.arange(c)[None, :]
    mats = [(jj <= r), (jj > r)]
    masks = [np.eye(c, dtype=bool)]
    m = c // 2
    while m >= 1:
        start = (r // (2 * m)) * (2 * m)
        bd = start + m - 1
        upper = r > bd
        mats.append(np.where(upper, (jj > bd) & (jj <= r), (jj > r) & (jj <= bd)))
        same_parent = (r // (2 * m)) == (jj // (2 * m))
        masks.append(same_parent & upper & (jj <= (jj // (2 * m)) * (2 * m) + m - 1))
        m //= 2
    m_all = np.concatenate(mats, axis=0).astype(np.float32)
    total = np.zeros((c, c), np.int32)
    for mk in masks:
        total += mk
    assert (total == np.tril(np.ones((c, c), np.int32))).all()
    return np.concatenate([m_all] * 3, axis=1), np.stack(masks).astype(np.float32)


_HGRN_M3, _HGRN_MASKS = _hgrn_constants()
_HGRN_LEVELS = _HGRN_MASKS.shape[0] - 1
HGRN_STEP_CHUNKS = 2
HGRN_STEP_BATCH = 4


def _hgrn_kernel(q_ref, k_ref, lf_ref, v_ref, gate_ref, m3_ref, mask_ref, gn_ref, out_ref, st_ref):
    @pl.when(pl.program_id(1) == 0)
    def _():
        st_ref[...] = jnp.zeros(st_ref.shape, F32)

    c = CHUNK
    intra = {}
    for ci in range(HGRN_STEP_CHUNKS):
        rows = slice(ci * c, (ci + 1) * c)
        for bi in range(HGRN_STEP_BATCH):
            g = lf_ref[bi, rows, :]
            g_hi = g.astype(BF16)
            r1 = g - g_hi.astype(F32)
            g_mid = r1.astype(BF16)
            g_lo = (r1 - g_mid.astype(F32)).astype(BF16)
            sums = jnp.dot(m3_ref[...], jnp.concatenate([g_hi, g_mid, g_lo], axis=0), preferred_element_type=F32)
            e_all = jnp.exp(sums)
            for h in range(B_HEADS):
                cols = slice(h * B_KEY_DIM, (h + 1) * B_KEY_DIM)
                qh = q_ref[bi, rows, cols]
                kh = k_ref[bi, rows, cols]
                att = mask_ref[0] * _nt_dot(qh.astype(BF16), kh.astype(BF16))
                for lv in range(_HGRN_LEVELS):
                    e_l = e_all[(2 + lv) * c:(3 + lv) * c, cols]
                    att = att + mask_ref[lv + 1] * _nt_dot((qh * e_l).astype(BF16), (kh * e_l).astype(BF16))
                e_b = e_all[0:c, cols]
                intra[bi, ci, h] = (jnp.dot(att.astype(BF16), v_ref[bi, rows, cols], preferred_element_type=F32),
                                    (qh * e_b).astype(BF16), (kh * e_all[c:2 * c, cols]).astype(BF16),
                                    e_b[c - 1:c, :])
    for ci in range(HGRN_STEP_CHUNKS):
        rows = slice(ci * c, (ci + 1) * c)
        for bi in range(HGRN_STEP_BATCH):
            for h in range(B_HEADS):
                cols = slice(h * B_KEY_DIM, (h + 1) * B_KEY_DIM)
                o_intra, q_dec, k_rem, decay_all = intra[bi, ci, h]
                st = st_ref[bi, h]
                o = o_intra + _nt_dot(q_dec, st.astype(BF16))
                upd = lax.dot_general(v_ref[bi, rows, cols], k_rem, (((0,), (0,)), ((), ())),
                                      preferred_element_type=F32)
                st_ref[bi, h] = st * decay_all + upd
                o = o * lax.rsqrt(jnp.mean(o * o, axis=-1, keepdims=True) + RMS_EPS) * gn_ref[0]
                out_ref[bi, rows, cols] = (o * gate_ref[bi, rows, cols]).astype(BF16)


def _hgrn(l, hq, hk, hlf, hv, hgate, gnorm):
    B, S, W = hq.shape
    ts = CHUNK * HGRN_STEP_CHUNKS
    nb = HGRN_STEP_BATCH
    assert B % nb == 0
    tok = lambda b, i: (b, i, 0)
    return pl.pallas_call(
        _hgrn_kernel,
        grid=(B // nb, S // ts),
        in_specs=[
            pl.BlockSpec((nb, ts, W), tok),
            pl.BlockSpec((nb, ts, W), tok),
            pl.BlockSpec((nb, ts, W), tok),
            pl.BlockSpec((nb, ts, W), tok),
            pl.BlockSpec((nb, ts, W), tok),
            pl.BlockSpec(_HGRN_M3.shape, lambda b, i: (0, 0)),
            pl.BlockSpec(_HGRN_MASKS.shape, lambda b, i: (0, 0, 0)),
            pl.BlockSpec((1, 1, B_VAL_DIM), lambda b, i: (l, 0, 0)),
        ],
        out_specs=pl.BlockSpec((nb, ts, W), tok),
        out_shape=jax.ShapeDtypeStruct((B, S, W), BF16),
        scratch_shapes=[pltpu.VMEM((nb, B_HEADS, B_VAL_DIM, B_KEY_DIM), F32)],
        compiler_params=_cparams(("arbitrary", "arbitrary")),
        name="hgrn2",
    )(hq, hk, hlf, hv, hgate, jnp.asarray(_HGRN_M3, BF16), jnp.asarray(_HGRN_MASKS), gnorm)


def _layernorm(v, g, b):
    mu = jnp.mean(v, axis=-1, keepdims=True)
    d = v - mu
    var = jnp.mean(d * d, axis=-1, keepdims=True)
    return d * lax.rsqrt(var + LN_EPS) * g + b


def _first_argmax(v, idx, axes, big):
    mx = v
    for ax in axes:
        mx = jnp.max(mx, axis=ax, keepdims=True)
    pos = jnp.where(v == mx, idx, big)
    for ax in axes:
        pos = jnp.min(pos, axis=ax, keepdims=True)
    return mx, pos


def _outproj_kernel(ya_ref, yb_ref, x_ref, mod_ref, wo_ref, lng_ref, lnb_ref, wrT_ref, rbias_ref, tri_ref,
                    x1_ref, u2_ref, gates_ref, rank_ref, gatesT_ref, *, alpha):
    y = jnp.dot(ya_ref[0], wo_ref[0, 0:A_WIDTH, :], preferred_element_type=F32)
    y = y + jnp.dot(yb_ref[0], wo_ref[0, A_WIDTH:, :], preferred_element_type=F32)
    g1 = mod_ref[0, 2:3, :]
    x1 = _layernorm(alpha * x_ref[0] + (1.0 + g1) * y, lng_ref[0], lnb_ref[0])
    x1_ref[0] = x1
    u2 = (x1 * (1.0 + mod_ref[0, 4:5, :]) + mod_ref[0, 3:4, :]).astype(BF16)
    u2_ref[0] = u2

    tm = u2.shape[0]
    gsz = N_EXPERTS // N_GROUPS
    scores = 1.0 / (1.0 + jnp.exp(-_nt_dot(wrT_ref[0], u2)))
    sel = (scores + rbias_ref[0]).reshape(N_GROUPS, gsz, tm)
    scores = scores.reshape(N_GROUPS, gsz, tm)
    i_m = lax.broadcasted_iota(I32, (N_GROUPS, gsz, tm), 1)
    i_g = lax.broadcasted_iota(I32, (N_GROUPS, 1, tm), 0)
    i_e = lax.broadcasted_iota(I32, (N_GROUPS, gsz, tm), 0) * gsz + i_m
    m1, p1 = _first_argmax(sel, i_m, (1,), gsz)
    m2 = jnp.max(jnp.where(i_m == p1, NEG_INF, sel), axis=1, keepdims=True)
    gs = m1 + m2
    gmask = jnp.zeros(gs.shape, F32)
    for _ in range(TOPK_GROUPS):
        _, pg = _first_argmax(gs, i_g, (0,), N_GROUPS)
        hit = i_g == pg
        gmask = jnp.where(hit, 1.0, gmask)
        gs = jnp.where(hit, NEG_INF, gs)
    cand = jnp.where(jnp.broadcast_to(gmask, sel.shape) > 0.0, sel, NEG_INF)
    w = jnp.zeros(sel.shape, F32)
    chosen = jnp.zeros(sel.shape, F32)
    for _ in range(TOP_K):
        _, pe = _first_argmax(cand, i_e, (1, 0), N_EXPERTS)
        hit = i_e == pe
        w = jnp.where(hit, scores, w)
        chosen = jnp.where(hit, 1.0, chosen)
        cand = jnp.where(hit, NEG_INF, cand)
    wsum = jnp.sum(jnp.sum(w, axis=1, keepdims=True), axis=0, keepdims=True)
    gates = (w / wsum * ROUTED_SCALE).reshape(N_EXPERTS, tm)
    g_hi = gates.astype(BF16).astype(F32)
    g_lo = (gates - g_hi).astype(BF16).astype(F32)
    gates_ref[0] = jnp.concatenate([g_hi, g_lo], axis=0).T.astype(BF16)

    chosen2 = chosen.reshape(N_EXPERTS, tm)
    rank = jnp.concatenate(
        [jnp.dot(chosen2[:, g * MOE_GROUP:(g + 1) * MOE_GROUP].astype(BF16), tri_ref[...],
                 preferred_element_type=F32) for g in range(tm // MOE_GROUP)], axis=1)
    rank_ref[0] = jnp.where(chosen2 > 0.0, rank, -1.0).astype(I32)
    gatesT_ref[0] = gates


def _outproj(l, ya, yb, x, mod, wo, ln_g, ln_b, wrT, rbias, alpha):
    B, S, D = x.shape
    tm = TM_PROJ
    tok = lambda b, i: (b, i, 0)
    lw3 = lambda b, i: (l, 0, 0)
    return pl.pallas_call(
        functools.partial(_outproj_kernel, alpha=alpha),
        grid=(B, S // tm),
        in_specs=[
            pl.BlockSpec((1, tm, A_WIDTH), tok),
            pl.BlockSpec((1, tm, B_WIDTH), tok),
            pl.BlockSpec((1, tm, D), tok),
            pl.BlockSpec((1, 6, D), lambda b, i: (b, 0, 0)),
            pl.BlockSpec((1, D, D), lw3),
            pl.BlockSpec((1, 1, D), lw3),
            pl.BlockSpec((1, 1, D), lw3),
            pl.BlockSpec((1, N_EXPERTS, D), lw3),
            pl.BlockSpec((1, N_EXPERTS, tm), lw3),
            pl.BlockSpec((MOE_GROUP, MOE_GROUP), lambda b, i: (0, 0)),
        ],
        out_specs=[pl.BlockSpec((1, tm, D), tok), pl.BlockSpec((1, tm, D), tok),
                   pl.BlockSpec((1, tm, 2 * N_EXPERTS), tok),
                   pl.BlockSpec((1, N_EXPERTS, tm), lambda b, i: (b, 0, i)),
                   pl.BlockSpec((1, N_EXPERTS, tm), lambda b, i: (b, 0, i))],
        out_shape=[jax.ShapeDtypeStruct((B, S, D), F32), jax.ShapeDtypeStruct((B, S, D), BF16),
                   jax.ShapeDtypeStruct((B, S, 2 * N_EXPERTS), BF16),
                   jax.ShapeDtypeStruct((B, N_EXPERTS, S), I32),
                   jax.ShapeDtypeStruct((B, N_EXPERTS, S), F32)],
        compiler_params=_cparams(("arbitrary", "arbitrary")),
        name="outproj_router",
    )(ya, yb, x, mod, wo, ln_g, ln_b, wrT, rbias,
      jnp.asarray(np.triu(np.ones((MOE_GROUP, MOE_GROUP), np.float32), 1), BF16))


MOE_CHUNK_EXPERTS = 8


def _slot_onehot(rank_rows, values, cap):
    row = lax.broadcasted_iota(I32, (cap, rank_rows.shape[1]), 0)
    return jnp.concatenate(
        [jnp.where(row == rank_rows[e:e + 1, :], values[e:e + 1, :], 0.0) for e in range(rank_rows.shape[0])], axis=0)


def _dispatch_kernel(u_ref, rank_ref, x_ref):
    u = u_ref[...]
    cap = x_ref.shape[1]
    ones = jnp.ones((MOE_CHUNK_EXPERTS, MOE_GROUP), F32)
    for c in range(N_EXPERTS // MOE_CHUNK_EXPERTS):
        es = slice(c * MOE_CHUNK_EXPERTS, (c + 1) * MOE_CHUNK_EXPERTS)
        onehot = _slot_onehot(rank_ref[0, es, :], ones, cap)
        xs = jnp.dot(onehot.astype(BF16), u, preferred_element_type=F32).astype(BF16)
        x_ref[es] = xs.reshape(MOE_CHUNK_EXPERTS, cap, -1)


def _dispatch(u2, rank, cap):
    T, D = u2.shape
    ng = T // MOE_GROUP
    gps = rank.shape[-1] // MOE_GROUP
    return pl.pallas_call(
        _dispatch_kernel,
        grid=(ng,),
        in_specs=[pl.BlockSpec((MOE_GROUP, D), lambda g: (g, 0)),
                  pl.BlockSpec((1, N_EXPERTS, MOE_GROUP), lambda g: (g // gps, 0, g % gps))],
        out_specs=pl.BlockSpec((N_EXPERTS, cap, D), lambda g: (0, g, 0)),
        out_shape=jax.ShapeDtypeStruct((N_EXPERTS, ng * cap, D), BF16),
        compiler_params=_cparams(("arbitrary",)),
        name="moe_dispatch",
    )(u2, rank)


def _expert_kernel(x_ref, wgu_ref, wd_ref, y_ref):
    hgu = jnp.dot(x_ref[0], wgu_ref[0, 0], preferred_element_type=F32)
    h = _silu(hgu[:, :EXPERT_DIM]) * hgu[:, EXPERT_DIM:]
    y_ref[0] = jnp.dot(h.astype(BF16), wd_ref[0, 0], preferred_element_type=F32).astype(BF16)


def _experts(l, xs, wgu, wd):
    E, R, D = xs.shape
    tr = R // MOE_EXPERT_STEPS
    assert R % MOE_EXPERT_STEPS == 0 and tr % 16 == 0
    return pl.pallas_call(
        _expert_kernel,
        grid=(E, R // tr),
        in_specs=[pl.BlockSpec((1, tr, D), lambda e, i: (e, i, 0)),
                  pl.BlockSpec((1, 1, D, 2 * EXPERT_DIM), lambda e, i: (l, e, 0, 0)),
                  pl.BlockSpec((1, 1, EXPERT_DIM, D), lambda e, i: (l, e, 0, 0))],
        out_specs=pl.BlockSpec((1, tr, D), lambda e, i: (e, i, 0)),
        out_shape=jax.ShapeDtypeStruct((E, R, D), BF16),
        compiler_params=_cparams(("arbitrary", "arbitrary")),
        name="moe_experts",
    )(xs, wgu, wd)


def _combine_kernel(y_ref, rank_ref, gates_ref, u_ref, x1_ref, mod_ref, sgu_ref, sd_ref, lng_ref, lnb_ref,
                    out_ref, *, alpha):
    hgu = jnp.dot(u_ref[...], sgu_ref[0], preferred_element_type=F32)
    hs = _silu(hgu[:, :SHARED_DIM]) * hgu[:, SHARED_DIM:]
    y = jnp.dot(hs.astype(BF16), sd_ref[0], preferred_element_type=F32)
    cap = y_ref.shape[1]
    for c in range(N_EXPERTS // MOE_CHUNK_EXPERTS):
        es = slice(c * MOE_CHUNK_EXPERTS, (c + 1) * MOE_CHUNK_EXPERTS)
        pick = _slot_onehot(rank_ref[0, es, :], gates_ref[0, es, :], cap)
        ys = y_ref[es].reshape(MOE_CHUNK_EXPERTS * cap, -1)
        y = y + lax.dot_general(pick.astype(BF16), ys, (((0,), (0,)), ((), ())), preferred_element_type=F32)
    g2 = mod_ref[0, 5:6, :]
    out_ref[...] = _layernorm(alpha * x1_ref[...] + (1.0 + g2) * y, lng_ref[0], lnb_ref[0])


def _combine(l, ys, rank, gates_t, u2, x1, mod, sgu, sd, ln_g, ln_b, alpha, seq):
    T, D = u2.shape
    tok = lambda g: (g, 0)
    lw3 = lambda g: (l, 0, 0)
    gps = seq // MOE_GROUP
    cap = ys.shape[1] // (T // MOE_GROUP)
    per_group = lambda g: (g // gps, 0, g % gps)
    return pl.pallas_call(
        functools.partial(_combine_kernel, alpha=alpha),
        grid=(T // MOE_GROUP,),
        in_specs=[
            pl.BlockSpec((N_EXPERTS, cap, D), lambda g: (0, g, 0)),
            pl.BlockSpec((1, N_EXPERTS, MOE_GROUP), per_group),
            pl.BlockSpec((1, N_EXPERTS, MOE_GROUP), per_group),
            pl.BlockSpec((MOE_GROUP, D), tok),
            pl.BlockSpec((MOE_GROUP, D), tok),
            pl.BlockSpec((1, 6, D), lambda g: ((g * MOE_GROUP) // seq, 0, 0)),
            pl.BlockSpec((1, D, 2 * SHARED_DIM), lw3),
            pl.BlockSpec((1, SHARED_DIM, D), lw3),
            pl.BlockSpec((1, 1, D), lw3),
            pl.BlockSpec((1, 1, D), lw3),
        ],
        out_specs=pl.BlockSpec((MOE_GROUP, D), tok),
        out_shape=jax.ShapeDtypeStruct((T, D), F32),
        compiler_params=_cparams(("arbitrary",)),
        name="moe_combine",
    )(ys, rank, gates_t, u2, x1, mod, sgu, sd, ln_g, ln_b)


def _moe_kernel(u_ref, gates_ref, x1_ref, mod_ref, wgu_ref, wd_ref, sgu_ref, sd_ref, lng_ref, lnb_ref,
                out_ref, acc_ref, *, alpha):
    s = pl.program_id(1)
    u = u_ref[...]

    def hidden(wgu):
        hgu = jnp.dot(u, wgu, preferred_element_type=F32)
        return _silu(hgu[:, :EXPERT_DIM]) * hgu[:, EXPERT_DIM:]

    @pl.when(s == 0)
    def _():
        acc_ref[...] = jnp.dot(hidden(sgu_ref[0]).astype(BF16), sd_ref[0], preferred_element_type=F32)

    rows = lax.broadcasted_iota(I32, (2 * N_EXPERTS, MOE_EXPERTS_PER_STEP * EXPERT_DIM), 0) & (N_EXPERTS - 1)
    cols = lax.broadcasted_iota(I32, (2 * N_EXPERTS, MOE_EXPERTS_PER_STEP * EXPERT_DIM), 1)
    onehot = jnp.where(rows == s * MOE_EXPERTS_PER_STEP + cols // EXPERT_DIM, 1.0, 0.0).astype(BF16)
    gate = jnp.dot(gates_ref[...], onehot, preferred_element_type=F32)
    h = jnp.concatenate(
        [(hidden(wgu_ref[0, k]) * gate[:, k * EXPERT_DIM:(k + 1) * EXPERT_DIM]).astype(BF16)
         for k in range(MOE_EXPERTS_PER_STEP)], axis=1)
    wd = wd_ref[0].reshape(MOE_EXPERTS_PER_STEP * EXPERT_DIM, wd_ref.shape[-1])
    acc_ref[...] += jnp.dot(h, wd, preferred_element_type=F32)

    @pl.when(s == pl.num_programs(1) - 1)
    def _():
        g2 = mod_ref[0, 5:6, :]
        out_ref[...] = _layernorm(alpha * x1_ref[...] + (1.0 + g2) * acc_ref[...], lng_ref[0], lnb_ref[0])


def _moe(l, u2, gates, x1, mod, wgu, wd, sgu, sd, ln_g, ln_b, alpha, seq):
    T, D = u2.shape
    tm = TM_MOE
    assert seq % tm == 0
    tok = lambda i, e: (i, 0)
    lw3 = lambda i, e: (l, 0, 0)
    return pl.pallas_call(
        functools.partial(_moe_kernel, alpha=alpha),
        grid=(T // tm, N_EXPERTS // MOE_EXPERTS_PER_STEP),
        in_specs=[
            pl.BlockSpec((tm, D), tok),
            pl.BlockSpec((tm, 2 * N_EXPERTS), tok),
            pl.BlockSpec((tm, D), tok),
            pl.BlockSpec((1, 6, D), lambda i, e: ((i * tm) // seq, 0, 0)),
            pl.BlockSpec((1, MOE_EXPERTS_PER_STEP, D, 2 * EXPERT_DIM), lambda i, e: (l, e, 0, 0)),
            pl.BlockSpec((1, MOE_EXPERTS_PER_STEP, EXPERT_DIM, D), lambda i, e: (l, e, 0, 0)),
            pl.BlockSpec((1, D, 2 * SHARED_DIM), lw3),
            pl.BlockSpec((1, SHARED_DIM, D), lw3),
            pl.BlockSpec((1, 1, D), lw3),
            pl.BlockSpec((1, 1, D), lw3),
        ],
        out_specs=pl.BlockSpec((tm, D), tok),
        out_shape=jax.ShapeDtypeStruct((T, D), F32),
        scratch_shapes=[pltpu.VMEM((tm, D), F32)],
        compiler_params=_cparams(("arbitrary", "arbitrary")),
        name="moe_dense",
    )(u2, gates, x1, mod, wgu, wd, sgu, sd, ln_g, ln_b)


def _prepare_params(w_in, kv_norm_g, w_uk, w_uv, hgrn_lb, w_out, w_router, router_bias,
                    w_gate, w_up, w_down, ws_gate, ws_up, ws_down):
    L = w_in.shape[0]
    sizes = (A_WIDTH, KV_RANK, IDX_HEADS * IDX_DIM, IDX_DIM, IDX_HEADS, B_FDIM, B_FDIM, B_WIDTH, B_WIDTH)
    offs = np.concatenate([[0], np.cumsum(sizes)])
    seg = lambda i: w_in[:, :, offs[i]:offs[i + 1]]
    w_aq, w_ckv, w_iq, w_ik, w_iw, w_hq, w_hf, w_hi, w_hg = (seg(i) for i in range(9))
    zik = jnp.zeros_like(w_ik)
    wp = jnp.concatenate([w_ckv, w_ik, zik, zik, w_ik, w_hq, w_hf, w_hg, w_hi], axis=-1).astype(BF16)
    wt = jnp.swapaxes(jnp.concatenate([w_aq, w_iq, w_ckv, w_iw], axis=-1), 1, 2).astype(BF16)
    assert wp.shape[-1] == _C_END and wt.shape[1] == _R_END
    eye = jnp.eye(A_HEADS, dtype=F32)
    wblk = (jnp.einsum('lhdr,hg->lhdgr', w_uk * (ATTN_SCALE * LOG2E), eye)
            .reshape(L, A_WIDTH, A_HEADS * KV_RANK).astype(BF16))
    p = dict(
        wp=wp, wt=wt, wblkT=jnp.swapaxes(wblk, 1, 2),
        gkv=kv_norm_g.reshape(L, 1, KV_RANK),
        gkvT=jnp.broadcast_to(kv_norm_g[:, :, None], (L, KV_RANK, TM_PROJ)),
        wuvT=jnp.swapaxes(w_uv, 2, 3).astype(BF16),
        wo=w_out.astype(BF16),
        wrT=jnp.swapaxes(w_router, 1, 2).astype(BF16),
        rbias=jnp.broadcast_to(router_bias[:, :, None], (L, N_EXPERTS, TM_PROJ)),
        wgu=jnp.concatenate([w_gate, w_up], axis=-1).astype(BF16),
        wd=w_down.astype(BF16),
        sgu=jnp.concatenate([ws_gate, ws_up], axis=-1).astype(BF16),
        sd=ws_down.astype(BF16),
    )
    lbs = jnp.cumsum(jax.nn.softmax(hgrn_lb.astype(F32), axis=0), axis=0)
    lbs = jnp.clip(lbs - lbs[0:1], 0.0, 1.0 - 1e-6)
    p["llb"] = jnp.log(lbs).reshape(L, 1, B_FDIM)
    p["l1m"] = jnp.log1p(-lbs).reshape(L, 1, B_FDIM)
    return p


def kernel(x, c, w_ada, b_ada, w_in, kv_norm_g, w_uk, w_uv, rel_bias, hgrn_lb, gnorm_g, w_out, ln1_g, ln1_b,
           w_router, router_bias, w_gate, w_up, w_down, ws_gate, ws_up, ws_down, ln2_g, ln2_b):
    B, S, D = x.shape
    L = w_in.shape[0]
    alpha = (2 * L) ** 0.25
    p = _prepare_params(w_in, kv_norm_g, w_uk, w_uv, hgrn_lb, w_out, w_router, router_bias,
                        w_gate, w_up, w_down, ws_gate, ws_up, ws_down)
    mods = _adaln(c, w_ada, b_ada).reshape(L, B, 6, D)
    bn = _bias_tile(rel_bias)
    gn = gnorm_g.reshape(L, 1, B_VAL_DIM)
    ln1g, ln1b = ln1_g.reshape(L, 1, D), ln1_b.reshape(L, 1, D)
    ln2g, ln2b = ln2_g.reshape(L, 1, D), ln2_b.reshape(L, 1, D)
    for l in range(L):
        mod = mods[l]
        (qlat, ckv, ckvT, iq, ikA, ikB, iwT, hq, hk, hlf, hv, hgate) = _inproj(
            l, x, mod, p["wp"], p["wt"], p["wblkT"], p["gkv"], p["gkvT"], p["llb"], p["l1m"])
        ya = _dsa(l, iq, iwT, qlat, ikA, ikB, ckv, ckvT, bn, p["wuvT"])
        yb = _hgrn(l, hq, hk, hlf, hv, hgate, gn)
        x1, u2, gates, rank, gates_t = _outproj(l, ya, yb, x, mod, p["wo"], ln1g, ln1b, p["wrT"], p["rbias"], alpha)
        u2f, x1f = u2.reshape(B * S, D), x1.reshape(B * S, D)

        def moe_sparse(cap, l=l, mod=mod, u2f=u2f, x1f=x1f, rank=rank, gates_t=gates_t):
            ys = _experts(l, _dispatch(u2f, rank, cap), p["wgu"], p["wd"])
            return _combine(l, ys, rank, gates_t, u2f, x1f, mod, p["sgu"], p["sd"], ln2g, ln2b, alpha, S)

        def moe_dense(l=l, mod=mod, u2f=u2f, x1f=x1f, gates=gates):
            return _moe(l, u2f, gates.reshape(B * S, 2 * N_EXPERTS), x1f, mod,
                        p["wgu"], p["wd"], p["sgu"], p["sd"], ln2g, ln2b, alpha, S)

        branches = [functools.partial(moe_sparse, cap) for cap in MOE_CAPS] + [moe_dense]
        tier = sum((jnp.max(rank) >= cap).astype(I32) for cap in MOE_CAPS)
        x = lax.switch(tier, branches).reshape(B, S, D)
    return x
```

```python
import functools
import math

import numpy as np
import jax
import jax.numpy as jnp
from jax import lax
from jax.experimental import pallas as pl
from jax.experimental.pallas import tpu as pltpu

F32 = jnp.float32
BF16 = jnp.bfloat16
I32 = jnp.int32

D_MODEL = 1024
CHUNK = 64
CHUNK_SHIFT = CHUNK.bit_length() - 1
assert 1 << CHUNK_SHIFT == CHUNK
A_HEADS = 8
A_HEAD_DIM = 64
A_WIDTH = A_HEADS * A_HEAD_DIM
KV_RANK = 128
IDX_HEADS = 8
IDX_DIM = 64
IDX_TOPK_MAX = 256
IDX_W_SCALE = (IDX_HEADS ** -0.5) * (IDX_DIM ** -0.5)
ATTN_SCALE = A_HEAD_DIM ** -0.5
LOG2E = math.log2(math.e)
KV_EXT = KV_RANK + 16
NUM_BUCKETS = 32
MAX_DISTANCE = 128
B_HEADS = 4
B_KEY_DIM = 128
B_VAL_DIM = 128
B_WIDTH = B_HEADS * B_VAL_DIM
B_FDIM = B_HEADS * B_KEY_DIM
N_EXPERTS = 64
TOP_K = 8
N_GROUPS = 8
TOPK_GROUPS = 4
EXPERT_DIM = 256
SHARED_DIM = 256
ROUTED_SCALE = 2.5
LN_EPS = 1e-5
RMS_EPS = 1e-6

LANES = 128
SUBLANES = 8
VMEM_LIMIT_BYTES = 56 * 1024 * 1024

INT_MIN = -(2 ** 31)
NEG_INF = float("-inf")

TM_PROJ = 1024
TQ = 128
UNIT = 512
NEAR = 2 * TQ
COUNT_ACCS = 8
PLANE_ROWS = 32 * SUBLANES
TM_MOE = 1024
MOE_EXPERTS_PER_STEP = 4
MOE_GROUP = 256
MOE_CAPS = (64, 80)
MOE_EXPERT_STEPS = 4

_C_CKV, _C_IKA, _C_IKB, _C_HQ, _C_HF, _C_HG, _C_HI, _C_END = (0, 128, 256, 384, 896, 1408, 1920, 2432)
_R_AQ, _R_IQ, _R_CKV, _R_IW, _R_END = (0, 512, 1024, 1152, 1160)


def _silu(v):
    return v * (1.0 / (1.0 + jnp.exp(-v)))


def _nt_dot(a, b):
    return lax.dot_general(a, b, (((1,), (1,)), ((), ())), preferred_element_type=F32)


def _cparams(sem):
    return pltpu.CompilerParams(dimension_semantics=sem, vmem_limit_bytes=VMEM_LIMIT_BYTES)


def _adaln_kernel(c_ref, w_ref, b_ref, o_ref):
    cond = _silu(c_ref[...])
    o_ref[0] = jnp.dot(cond.astype(BF16), w_ref[0].astype(BF16), preferred_element_type=F32) + b_ref[0]


def _adaln(c, w_ada, b_ada):
    L, D, D6 = w_ada.shape
    B = c.shape[0]
    nb = D6 // D
    return pl.pallas_call(
        _adaln_kernel,
        grid=(L, nb),
        in_specs=[
            pl.BlockSpec((B, D), lambda l, j: (0, 0)),
            pl.BlockSpec((1, D, D), lambda l, j: (l, 0, j)),
            pl.BlockSpec((1, 1, D), lambda l, j: (l, 0, j)),
        ],
        out_specs=pl.BlockSpec((1, B, D), lambda l, j: (l, 0, j)),
        out_shape=jax.ShapeDtypeStruct((L, B, D6), F32),
        compiler_params=_cparams(("arbitrary", "arbitrary")),
        name="adaln_mod",
    )(c, w_ada, b_ada.reshape(L, 1, D6))


_T5_NB = NUM_BUCKETS // 2
_T5_EXACT = _T5_NB // 2
_T5_THRESHOLDS = tuple(
    int(math.ceil(_T5_EXACT * (MAX_DISTANCE / _T5_EXACT) ** (j / (_T5_NB - _T5_EXACT)) - 1e-9))
    for j in range(1, _T5_NB - _T5_EXACT))
FAR_BUCKET = _T5_NB - 1
assert _T5_THRESHOLDS[-1] <= TQ, "keys further than one query block behind must share the far bucket"


def _bias_kernel(rb_ref, o_ref):
    kr = lax.broadcasted_iota(I32, (NEAR + TQ, TQ), 0)
    ql = lax.broadcasted_iota(I32, (NEAR + TQ, TQ), 1)
    rel = kr - TQ - ql
    n = jnp.abs(rel)
    large = jnp.full(rel.shape, _T5_EXACT, I32)
    for t in _T5_THRESHOLDS:
        large = large + (n >= t).astype(I32)
    bucket = jnp.where(rel > 0, _T5_NB, 0) + jnp.where(n < _T5_EXACT, n, large)
    for h in range(A_HEADS):
        acc = jnp.zeros(rel.shape, F32)
        for bk in range(NUM_BUCKETS):
            acc = jnp.where(bucket == bk, rb_ref[bk, h], acc)
        o_ref[h] = (acc - rb_ref[FAR_BUCKET, h]) * LOG2E


def _bias_tile(rel_bias):
    return pl.pallas_call(
        _bias_kernel,
        in_specs=[pl.BlockSpec(memory_space=pltpu.SMEM)],
        out_specs=pl.BlockSpec(memory_space=pltpu.VMEM),
        out_shape=jax.ShapeDtypeStruct((A_HEADS, NEAR + TQ, TQ), F32),
        name="rel_bias_tile",
    )(rel_bias)


def _inproj_kernel(x_ref, mod_ref, wp_ref, wt_ref, wblkT_ref, gkv_ref, gkvT_ref, llb_ref, l1m_ref,
                   qlatT_ref, ckv_ref, ckvT_ref, iqT_ref, ikA_ref, ikB_ref, iwT_ref,
                   hq_ref, hk_ref, hlf_ref, hv_ref, hgate_ref):
    x = x_ref[0]
    sh1 = mod_ref[0, 0:1, :]
    sc1 = mod_ref[0, 1:2, :]
    u = (x * (1.0 + sc1) + sh1).astype(BF16)
    z = jnp.dot(u, wp_ref[0], preferred_element_type=F32)
    zt = _nt_dot(wt_ref[0], u)

    def proj(lo, hi):
        return z[:, lo:hi]

    qlatT_ref[0] = jnp.dot(wblkT_ref[0], zt[_R_AQ:_R_IQ].astype(BF16), preferred_element_type=F32).astype(BF16)

    zc = proj(_C_CKV, _C_IKA)
    inv = lax.rsqrt(jnp.mean(zc * zc, axis=-1, keepdims=True) + RMS_EPS)
    ckv_ref[0] = (zc * inv * gkv_ref[0]).astype(BF16)
    zct = zt[_R_CKV:_R_IW]
    inv_t = lax.rsqrt(jnp.mean(zct * zct, axis=0, keepdims=True) + RMS_EPS)
    ckvT_ref[0, 0:KV_RANK, :] = (zct * inv_t * gkvT_ref[0]).astype(BF16)
    ckvT_ref[0, KV_RANK:KV_EXT, :] = jnp.ones((KV_EXT - KV_RANK, zct.shape[1]), BF16)

    iqT_ref[0] = zt[_R_IQ:_R_CKV].astype(BF16)
    ikA_ref[0] = proj(_C_IKA, _C_IKB).astype(BF16)
    ikB_ref[0] = proj(_C_IKB, _C_HQ).astype(BF16)
    iwT_ref[0] = zt[_R_IW:_R_END] * IDX_W_SCALE

    hq_ref[0] = _silu(proj(_C_HQ, _C_HF))
    zf = proj(_C_HF, _C_HG)
    log_sig = jnp.minimum(zf, 0.0) - jnp.log1p(jnp.exp(-jnp.abs(zf)))
    a = llb_ref[0]
    c = l1m_ref[0] + log_sig
    logf = jnp.maximum(a, c) + jnp.log1p(jnp.exp(-jnp.abs(a - c)))
    hlf_ref[0] = logf
    hk_ref[0] = 1.0 - jnp.exp(logf)
    hgate_ref[0] = _silu(proj(_C_HG, _C_HI))
    hv_ref[0] = proj(_C_HI, _C_END).astype(BF16)


def _inproj(l, x, mod, wp, wt, wblkT, gkv, gkvT, llb, l1m):
    B, S, D = x.shape
    tm = TM_PROJ
    grid = (B, S // tm)
    lw3 = lambda b, i: (l, 0, 0)
    tok = lambda b, i: (b, i, 0)
    tokT = lambda b, i: (b, 0, i)
    hd4 = lambda b, i: (b, 0, i, 0)
    outs = [
        (jax.ShapeDtypeStruct((B, A_HEADS * KV_RANK, S), BF16), pl.BlockSpec((1, A_HEADS * KV_RANK, tm), tokT)),
        (jax.ShapeDtypeStruct((B, S, KV_RANK), BF16), pl.BlockSpec((1, tm, KV_RANK), tok)),
        (jax.ShapeDtypeStruct((B, KV_EXT, S), BF16), pl.BlockSpec((1, KV_EXT, tm), tokT)),
        (jax.ShapeDtypeStruct((B, IDX_HEADS * IDX_DIM, S), BF16), pl.BlockSpec((1, IDX_HEADS * IDX_DIM, tm), tokT)),
        (jax.ShapeDtypeStruct((B, S, LANES), BF16), pl.BlockSpec((1, tm, LANES), tok)),
        (jax.ShapeDtypeStruct((B, S, LANES), BF16), pl.BlockSpec((1, tm, LANES), tok)),
        (jax.ShapeDtypeStruct((B, IDX_HEADS, S), F32), pl.BlockSpec((1, IDX_HEADS, tm), tokT)),
        (jax.ShapeDtypeStruct((B, S, B_FDIM), F32), pl.BlockSpec((1, tm, B_FDIM), tok)),
        (jax.ShapeDtypeStruct((B, S, B_FDIM), F32), pl.BlockSpec((1, tm, B_FDIM), tok)),
        (jax.ShapeDtypeStruct((B, S, B_FDIM), F32), pl.BlockSpec((1, tm, B_FDIM), tok)),
        (jax.ShapeDtypeStruct((B, S, B_WIDTH), BF16), pl.BlockSpec((1, tm, B_WIDTH), tok)),
        (jax.ShapeDtypeStruct((B, S, B_WIDTH), F32), pl.BlockSpec((1, tm, B_WIDTH), tok)),
    ]
    return pl.pallas_call(
        _inproj_kernel,
        grid=grid,
        in_specs=[
            pl.BlockSpec((1, tm, D), tok),
            pl.BlockSpec((1, 6, D), lambda b, i: (b, 0, 0)),
            pl.BlockSpec((1, D, _C_END), lw3),
            pl.BlockSpec((1, _R_END, D), lw3),
            pl.BlockSpec((1, A_HEADS * KV_RANK, A_WIDTH), lw3),
            pl.BlockSpec((1, 1, KV_RANK), lw3),
            pl.BlockSpec((1, KV_RANK, tm), lw3),
            pl.BlockSpec((1, 1, B_FDIM), lw3),
            pl.BlockSpec((1, 1, B_FDIM), lw3),
        ],
        out_specs=[o[1] for o in outs],
        out_shape=[o[0] for o in outs],
        compiler_params=_cparams(("arbitrary", "arbitrary")),
        name="inproj",
    )(x, mod, wp, wt, wblkT, gkv, gkvT, llb, l1m)


def _dsa_kernel(iq_ref, iwT_ref, qlat_ref, ikA_ref, ikB_ref, ckv_ref, ckvT_ref, bn_ref, wuvT_ref, out_ref,
                sc_ref, plane_ref, madd_ref, maddn_ref, la_ref, lb_ref, pma_ref, pmb_ref, ot_ref, yaT_ref,
                *, k_sel, n_idx_bits):
    j = pl.program_id(1)
    q0 = j * TQ
    nk = q0 + TQ
    nunit = (nk + UNIT - 1) // UNIT
    near0 = pl.multiple_of(jnp.maximum(nk - NEAR, 0), TQ)
    bn_row0 = pl.multiple_of(jnp.where(j == 0, TQ, 0), TQ)
    lane = lax.broadcasted_iota(I32, (1, TQ), 1)
    limit = (((q0 + lane) >> CHUNK_SHIFT) + 1) << CHUNK_SHIFT
    row_iota = lax.broadcasted_iota(I32, (UNIT, TQ), 0)

    def unit_rows(u):
        return pl.ds(pl.multiple_of(u * UNIT, UNIT), UNIT)

    @pl.when((pl.program_id(0) == 0) & (j == 0))
    def _():
        plane_ref[...] = jnp.zeros(plane_ref.shape, I32)
        madd_ref[...] = jnp.full(madd_ref.shape, NEG_INF, F32)

    iqs = jnp.concatenate([iq_ref[0, p * LANES:(p + 1) * LANES, :] for p in range(IDX_HEADS // 2)], axis=1)
    iw = iwT_ref[0]

    last_unit = sc_ref.shape[0] // UNIT - 1
    half = IDX_HEADS // 2 * TQ

    def issue_scores(u, buf_ref):
        rows = unit_rows(jnp.minimum(u, last_unit))
        buf_ref[:, 0:half] = jnp.dot(ikA_ref[0, rows, :], iqs, preferred_element_type=F32)
        buf_ref[:, half:2 * half] = jnp.dot(ikB_ref[0, rows, :], iqs, preferred_element_type=F32)

    def reduce_scores(u, buf_ref):
        acc = jnp.zeros((UNIT, TQ), F32)
        for p in range(IDX_HEADS // 2):
            acc = acc + iw[2 * p:2 * p + 1, :] * jnp.maximum(buf_ref[:, p * TQ:(p + 1) * TQ], 0.0)
            acc = acc + iw[2 * p + 1:2 * p + 2, :] * jnp.maximum(buf_ref[:, half + p * TQ:half + (p + 1) * TQ], 0.0)
        bits = lax.bitcast_convert_type(acc, I32)
        key = bits ^ ((bits >> 31) & 0x7FFFFFFF)
        sc_ref[unit_rows(u), :] = jnp.where(row_iota + u * UNIT < limit, key, INT_MIN)

    issue_scores(0, la_ref)

    def score_pair(i, carry):
        issue_scores(2 * i + 1, lb_ref)
        reduce_scores(2 * i, la_ref)
        issue_scores(2 * i + 2, la_ref)
        reduce_scores(2 * i + 1, lb_ref)
        return carry

    lax.fori_loop(0, nunit // 2, score_pair, 0)

    @pl.when(nunit % 2 == 1)
    def _():
        reduce_scores(nunit - 1, la_ref)

    ngroups = (nk + PLANE_ROWS - 1) // PLANE_ROWS

    def plane_group(g, carry):
        rows = pl.ds(pl.multiple_of(g * PLANE_ROWS, PLANE_ROWS), PLANE_ROWS)
        words = (sc_ref[rows, :] ^ INT_MIN).reshape(32, SUBLANES, TQ)
        w = [words[i] for i in range(32)]
        j, m = 16, 0x0000FFFF
        while j:
            mask = np.int32(np.uint32(m).view(np.int32))
            k = 0
            while k < 32:
                t = (w[k] ^ lax.shift_right_logical(w[k + j], jnp.full(w[k].shape, j, I32))) & mask
                w[k] = w[k] ^ t
                w[k + j] = w[k + j] ^ (t << j)
                k = (k + j + 1) & ~j
            j >>= 1
            m = (m ^ (m << j)) & 0xFFFFFFFF
        for i in range(32):
            plane_ref[i, pl.ds(g * SUBLANES, SUBLANES), :] = w[i]
        return carry

    lax.fori_loop(0, ngroups, plane_group, 0)

    n_words = sc_ref.shape[0] // PLANE_ROWS * SUBLANES
    group_of_word = lax.broadcasted_iota(I32, (n_words, TQ), 0) // SUBLANES

    def bit_step(i, carry):
        alive, above, t_off, c_ge = carry
        hit = alive & plane_ref[i]
        cnt = above + jnp.sum(lax.population_count(hit), axis=0, keepdims=True)
        ok = cnt >= k_sel
        alive = jnp.where(ok, hit, alive ^ hit)
        above = jnp.where(ok, above, cnt)
        t_off = jnp.where(ok, t_off | (jnp.int32(1) << (31 - i)), t_off)
        return alive, above, t_off, jnp.where(ok, cnt, c_ge)

    zero_row = jnp.zeros((1, TQ), I32)
    _, _, t_off, c_ge = lax.fori_loop(
        0, 32, bit_step,
        (jnp.where(group_of_word < ngroups, jnp.int32(-1), jnp.int32(0)), zero_row, zero_row, zero_row))
    thr = jnp.maximum(t_off ^ INT_MIN, INT_MIN + 1)
    straddle = (c_ge > k_sel).astype(I32)

    def count_where(pred):
        def body(u, acc):
            hit = pred(sc_ref[unit_rows(u), :], u * UNIT).reshape(-1, COUNT_ACCS * SUBLANES, TQ)
            for s in range(hit.shape[0]):
                acc = jnp.where(hit[s], acc + 1, acc)
            return acc
        acc = lax.fori_loop(0, nunit, body, jnp.zeros((COUNT_ACCS * SUBLANES, TQ), I32))
        return jnp.sum(acc, axis=0, keepdims=True)

    def tie_bound():
        c_gt = count_where(lambda blk, r0: blk > thr)
        need = k_sel - c_gt

        def tie_body(i, j0):
            cand = j0 | (jnp.int32(1) << (n_idx_bits - 1 - i))
            cnt = count_where(lambda blk, r0: jnp.where(blk == thr, row_iota + r0, cand) < cand)
            return jnp.where(cnt < need, cand, j0)

        j0 = lax.fori_loop(0, n_idx_bits, tie_body, jnp.zeros((1, TQ), I32))
        return jnp.where(straddle > 0, j0 + 1, jnp.int32(2 ** n_idx_bits))

    jstar = lax.cond(jnp.max(straddle) > 0, tie_bound, lambda: jnp.full((1, TQ), 2 ** n_idx_bits, I32))

    def madd_unit(u, carry):
        rows = unit_rows(u)
        key = sc_ref[rows, :]
        tie_keep = jnp.where(row_iota + u * UNIT < jstar, 0.0, NEG_INF)
        madd_ref[rows, :] = jnp.where(key > thr, 0.0, jnp.where(key == thr, tie_keep, NEG_INF))
        return carry

    lax.fori_loop(0, nunit, madd_unit, 0)
    maddn_ref[...] = madd_ref[pl.ds(near0, NEAR), :]
    madd_ref[pl.ds(near0, NEAR), :] = jnp.full((NEAR, TQ), NEG_INF, F32)

    qall = jnp.concatenate([qlat_ref[0, h * KV_RANK:(h + 1) * KV_RANK, :] for h in range(A_HEADS)], axis=1)

    def col_max(v):
        return jnp.max(v.reshape(v.shape[0] // SUBLANES, SUBLANES, A_HEADS * TQ), axis=0)

    def fold(xl, part_max, ckv_t, m_old):
        m_new = jnp.maximum(m_old, jnp.max(part_max, axis=0, keepdims=True))
        m_use = jnp.where(m_new == NEG_INF, 0.0, m_new)
        p = jnp.exp2(xl - m_use).astype(BF16)
        ot_ref[...] = ot_ref[...] * jnp.exp2(m_old - m_use) + jnp.dot(ckv_t, p, preferred_element_type=F32)
        return m_new

    ot_ref[...] = jnp.zeros(ot_ref.shape, F32)
    near_rows = pl.ds(near0, NEAR)
    xn = jnp.dot(ckv_ref[0, near_rows, :], qall, preferred_element_type=F32)
    xn = xn + jnp.concatenate([maddn_ref[...]] * A_HEADS, axis=1)
    xn = xn + jnp.concatenate([bn_ref[h, pl.ds(bn_row0, NEAR), :] for h in range(A_HEADS)], axis=1)
    m_run = fold(xn, col_max(xn), ckvT_ref[0, :, near_rows], jnp.full((1, A_HEADS * TQ), NEG_INF, F32))


    def issue_logits(u, buf_ref, pm_ref):
        rows = unit_rows(jnp.minimum(u, last_unit))
        xl = jnp.dot(ckv_ref[0, rows, :], qall, preferred_element_type=F32)
        xl = xl + jnp.concatenate([madd_ref[rows, :]] * A_HEADS, axis=1)
        buf_ref[...] = xl
        pm_ref[...] = col_max(xl)

    def consume_logits(u, buf_ref, pm_ref, m_old):
        return fold(buf_ref[...], pm_ref[...], ckvT_ref[0, :, unit_rows(u)], m_old)

    issue_logits(0, la_ref, pma_ref)

    def pair_step(i, m_old):
        issue_logits(2 * i + 1, lb_ref, pmb_ref)
        m_mid = consume_logits(2 * i, la_ref, pma_ref, m_old)
        issue_logits(2 * i + 2, la_ref, pma_ref)
        return consume_logits(2 * i + 1, lb_ref, pmb_ref, m_mid)

    nfar = (near0 + UNIT - 1) // UNIT
    m_run = lax.fori_loop(0, nfar // 2, pair_step, m_run)

    @pl.when(nfar % 2 == 1)
    def _():
        consume_logits(nfar - 1, la_ref, pma_ref, m_run)
    o_t = (ot_ref[0:KV_RANK, :] * (1.0 / ot_ref[KV_RANK:KV_RANK + 1, :])).astype(BF16)
    for h in range(A_HEADS):
        yaT_ref[h * A_HEAD_DIM:(h + 1) * A_HEAD_DIM, :] = jnp.dot(
            wuvT_ref[0, h], o_t[:, h * TQ:(h + 1) * TQ], preferred_element_type=F32)

    out_ref[0] = yaT_ref[...].T.astype(BF16)


def _dsa(l, iq, iwT, qlat, ikA, ikB, ckv, ckvT, bn, wuvT):
    B, S = ckv.shape[0], ckv.shape[1]
    assert S % (2 * UNIT) == 0 and UNIT % TQ == 0 and TQ % CHUNK == 0 and NEAR <= UNIT
    k_sel = min(IDX_TOPK_MAX, S // 4)
    n_idx_bits = int(math.log2(S))
    assert 2 ** n_idx_bits == S
    grid = (B, S // TQ)
    blk = lambda b, i: (b, 0, i, 0)
    full = lambda b, i: (b, 0, 0)
    kern = functools.partial(_dsa_kernel, k_sel=k_sel, n_idx_bits=n_idx_bits)
    return pl.pallas_call(
        kern,
        grid=grid,
        in_specs=[
            pl.BlockSpec((1, IDX_HEADS * IDX_DIM, TQ), lambda b, i: (b, 0, i)),
            pl.BlockSpec((1, IDX_HEADS, TQ), lambda b, i: (b, 0, i)),
            pl.BlockSpec((1, A_HEADS * KV_RANK, TQ), lambda b, i: (b, 0, i)),
            pl.BlockSpec((1, S, LANES), full),
            pl.BlockSpec((1, S, LANES), full),
            pl.BlockSpec((1, S, KV_RANK), full),
            pl.BlockSpec((1, KV_EXT, S), full),
            pl.BlockSpec((A_HEADS, NEAR + TQ, TQ), lambda b, i: (0, 0, 0)),
            pl.BlockSpec((1, A_HEADS, A_HEAD_DIM, KV_RANK), lambda b, i: (l, 0, 0, 0)),
        ],
        out_specs=pl.BlockSpec((1, TQ, A_WIDTH), lambda b, i: (b, i, 0)),
        out_shape=jax.ShapeDtypeStruct((B, S, A_WIDTH), BF16),
        scratch_shapes=[
            pltpu.VMEM((S, TQ), I32),
            pltpu.VMEM((32, S // PLANE_ROWS * SUBLANES, TQ), I32),
            pltpu.VMEM((S, TQ), F32),
            pltpu.VMEM((NEAR, TQ), F32),
            pltpu.VMEM((UNIT, A_HEADS * TQ), F32),
            pltpu.VMEM((UNIT, A_HEADS * TQ), F32),
            pltpu.VMEM((SUBLANES, A_HEADS * TQ), F32),
            pltpu.VMEM((SUBLANES, A_HEADS * TQ), F32),
            pltpu.VMEM((KV_EXT, A_HEADS * TQ), F32),
            pltpu.VMEM((A_WIDTH, TQ), F32),
        ],
        compiler_params=_cparams(("arbitrary", "arbitrary")),
        name="dsa_attention",
    )(iq, iwT, qlat, ikA, ikB, ckv, ckvT, bn, wuvT)


def _hgrn_constants():
    c = CHUNK
    r = np.arange(c)[:, None]
    jj = np.arange(c)[None, :]
    mats = [(jj <= r), (jj > r)]
    masks = [np.eye(c, dtype=bool)]
    m = c // 2
    while m >= 1:
        start = (r // (2 * m)) * (2 * m)
        bd = start + m - 1
        upper = r > bd
        mats.append(np.where(upper, (jj > bd) & (jj <= r), (jj > r) & (jj <= bd)))
        same_parent = (r // (2 * m)) == (jj // (2 * m))
        masks.append(same_parent & upper & (jj <= (jj // (2 * m)) * (2 * m) + m - 1))
        m //= 2
    m_all = np.concatenate(mats, axis=0).astype(np.float32)
    total = np.zeros((c, c), np.int32)
    for mk in masks:
        total += mk
    assert (total == np.tril(np.ones((c, c), np.int32))).all()
    return np.concatenate([m_all] * 3, axis=1), np.stack(masks).astype(np.float32)


_HGRN_M3, _HGRN_MASKS = _hgrn_constants()
_HGRN_LEVELS = _HGRN_MASKS.shape[0] - 1
HGRN_STEP_CHUNKS = 2
HGRN_STEP_BATCH = 8


def _hgrn_kernel(q_ref, k_ref, lf_ref, v_ref, gate_ref, m3_ref, mask_ref, gn_ref, out_ref, st_ref):
    @pl.when(pl.program_id(1) == 0)
    def _():
        st_ref[...] = jnp.zeros(st_ref.shape, F32)

    c = CHUNK
    intra = {}
    for ci in range(HGRN_STEP_CHUNKS):
        rows = slice(ci * c, (ci + 1) * c)
        for bi in range(HGRN_STEP_BATCH):
            g = lf_ref[bi, rows, :]
            g_hi = g.astype(BF16)
            r1 = g - g_hi.astype(F32)
            g_mid = r1.astype(BF16)
            g_lo = (r1 - g_mid.astype(F32)).astype(BF16)
            sums = jnp.dot(m3_ref[...], jnp.concatenate([g_hi, g_mid, g_lo], axis=0), preferred_element_type=F32)
            e_all = jnp.exp(sums)
            for h in range(B_HEADS):
                cols = slice(h * B_KEY_DIM, (h + 1) * B_KEY_DIM)
                qh = q_ref[bi, rows, cols]
                kh = k_ref[bi, rows, cols]
                att = mask_ref[0] * _nt_dot(qh.astype(BF16), kh.astype(BF16))
                for lv in range(_HGRN_LEVELS):
                    e_l = e_all[(2 + lv) * c:(3 + lv) * c, cols]
                    att = att + mask_ref[lv + 1] * _nt_dot((qh * e_l).astype(BF16), (kh * e_l).astype(BF16))
                e_b = e_all[0:c, cols]
                intra[bi, ci, h] = (jnp.dot(att.astype(BF16), v_ref[bi, rows, cols], preferred_element_type=F32),
                                    (qh * e_b).astype(BF16), (kh * e_all[c:2 * c, cols]).astype(BF16),
                                    e_b[c - 1:c, :])
    for ci in range(HGRN_STEP_CHUNKS):
        rows = slice(ci * c, (ci + 1) * c)
        for bi in range(HGRN_STEP_BATCH):
            for h in range(B_HEADS):
                cols = slice(h * B_KEY_DIM, (h + 1) * B_KEY_DIM)
                o_intra, q_dec, k_rem, decay_all = intra[bi, ci, h]
                st = st_ref[bi, h]
                o = o_intra + _nt_dot(q_dec, st.astype(BF16))
                upd = lax.dot_general(v_ref[bi, rows, cols], k_rem, (((0,), (0,)), ((), ())),
                                      preferred_element_type=F32)
                st_ref[bi, h] = st * decay_all + upd
                o = o * lax.rsqrt(jnp.mean(o * o, axis=-1, keepdims=True) + RMS_EPS) * gn_ref[0]
                out_ref[bi, rows, cols] = (o * gate_ref[bi, rows, cols]).astype(BF16)


def _hgrn(l, hq, hk, hlf, hv, hgate, gnorm):
    B, S, W = hq.shape
    ts = CHUNK * HGRN_STEP_CHUNKS
    nb = HGRN_STEP_BATCH
    assert B % nb == 0
    tok = lambda b, i: (b, i, 0)
    return pl.pallas_call(
        _hgrn_kernel,
        grid=(B // nb, S // ts),
        in_specs=[
            pl.BlockSpec((nb, ts, W), tok),
            pl.BlockSpec((nb, ts, W), tok),
            pl.BlockSpec((nb, ts, W), tok),
            pl.BlockSpec((nb, ts, W), tok),
            pl.BlockSpec((nb, ts, W), tok),
            pl.BlockSpec(_HGRN_M3.shape, lambda b, i: (0, 0)),
            pl.BlockSpec(_HGRN_MASKS.shape, lambda b, i: (0, 0, 0)),
            pl.BlockSpec((1, 1, B_VAL_DIM), lambda b, i: (l, 0, 0)),
        ],
        out_specs=pl.BlockSpec((nb, ts, W), tok),
        out_shape=jax.ShapeDtypeStruct((B, S, W), BF16),
        scratch_shapes=[pltpu.VMEM((nb, B_HEADS, B_VAL_DIM, B_KEY_DIM), F32)],
        compiler_params=_cparams(("arbitrary", "arbitrary")),
        name="hgrn2",
    )(hq, hk, hlf, hv, hgate, jnp.asarray(_HGRN_M3, BF16), jnp.asarray(_HGRN_MASKS), gnorm)


def _layernorm(v, g, b):
    mu = jnp.mean(v, axis=-1, keepdims=True)
    d = v - mu
    var = jnp.mean(d * d, axis=-1, keepdims=True)
    return d * lax.rsqrt(var + LN_EPS) * g + b


def _first_argmax(v, idx, axes, big):
    mx = v
    for ax in axes:
        mx = jnp.max(mx, axis=ax, keepdims=True)
    pos = jnp.where(v == mx, idx, big)
    for ax in axes:
        pos = jnp.min(pos, axis=ax, keepdims=True)
    return mx, pos


def _outproj_kernel(ya_ref, yb_ref, x_ref, mod_ref, wo_ref, lng_ref, lnb_ref, wrT_ref, rbias_ref, tri_ref,
                    x1_ref, u2_ref, gates_ref, rank_ref, gatesT_ref, *, alpha):
    y = jnp.dot(ya_ref[0], wo_ref[0, 0:A_WIDTH, :], preferred_element_type=F32)
    y = y + jnp.dot(yb_ref[0], wo_ref[0, A_WIDTH:, :], preferred_element_type=F32)
    g1 = mod_ref[0, 2:3, :]
    x1 = _layernorm(alpha * x_ref[0] + (1.0 + g1) * y, lng_ref[0], lnb_ref[0])
    x1_ref[0] = x1
    u2 = (x1 * (1.0 + mod_ref[0, 4:5, :]) + mod_ref[0, 3:4, :]).astype(BF16)
    u2_ref[0] = u2

    tm = u2.shape[0]
    gsz = N_EXPERTS // N_GROUPS
    scores = 1.0 / (1.0 + jnp.exp(-_nt_dot(wrT_ref[0], u2)))
    sel = (scores + rbias_ref[0]).reshape(N_GROUPS, gsz, tm)
    scores = scores.reshape(N_GROUPS, gsz, tm)
    i_m = lax.broadcasted_iota(I32, (N_GROUPS, gsz, tm), 1)
    i_g = lax.broadcasted_iota(I32, (N_GROUPS, 1, tm), 0)
    i_e = lax.broadcasted_iota(I32, (N_GROUPS, gsz, tm), 0) * gsz + i_m
    m1, p1 = _first_argmax(sel, i_m, (1,), gsz)
    m2 = jnp.max(jnp.where(i_m == p1, NEG_INF, sel), axis=1, keepdims=True)
    gs = m1 + m2
    gmask = jnp.zeros(gs.shape, F32)
    for _ in range(TOPK_GROUPS):
        _, pg = _first_argmax(gs, i_g, (0,), N_GROUPS)
        hit = i_g == pg
        gmask = jnp.where(hit, 1.0, gmask)
        gs = jnp.where(hit, NEG_INF, gs)
    cand = jnp.where(jnp.broadcast_to(gmask, sel.shape) > 0.0, sel, NEG_INF)
    w = jnp.zeros(sel.shape, F32)
    chosen = jnp.zeros(sel.shape, F32)
    for _ in range(TOP_K):
        _, pe = _first_argmax(cand, i_e, (1, 0), N_EXPERTS)
        hit = i_e == pe
        w = jnp.where(hit, scores, w)
        chosen = jnp.where(hit, 1.0, chosen)
        cand = jnp.where(hit, NEG_INF, cand)
    wsum = jnp.sum(jnp.sum(w, axis=1, keepdims=True), axis=0, keepdims=True)
    gates = (w / wsum * ROUTED_SCALE).reshape(N_EXPERTS, tm)
    g_hi = gates.astype(BF16).astype(F32)
    g_lo = (gates - g_hi).astype(BF16).astype(F32)
    gates_ref[0] = jnp.concatenate([g_hi, g_lo], axis=0).T.astype(BF16)

    chosen2 = chosen.reshape(N_EXPERTS, tm)
    rank = jnp.concatenate(
        [jnp.dot(chosen2[:, g * MOE_GROUP:(g + 1) * MOE_GROUP].astype(BF16), tri_ref[...],
                 preferred_element_type=F32) for g in range(tm // MOE_GROUP)], axis=1)
    rank_ref[0] = jnp.where(chosen2 > 0.0, rank, -1.0).astype(I32)
    gatesT_ref[0] = gates


def _outproj(l, ya, yb, x, mod, wo, ln_g, ln_b, wrT, rbias, alpha):
    B, S, D = x.shape
    tm = TM_PROJ
    tok = lambda b, i: (b, i, 0)
    lw3 = lambda b, i: (l, 0, 0)
    return pl.pallas_call(
        functools.partial(_outproj_kernel, alpha=alpha),
        grid=(B, S // tm),
        in_specs=[
            pl.BlockSpec((1, tm, A_WIDTH), tok),
            pl.BlockSpec((1, tm, B_WIDTH), tok),
            pl.BlockSpec((1, tm, D), tok),
            pl.BlockSpec((1, 6, D), lambda b, i: (b, 0, 0)),
            pl.BlockSpec((1, D, D), lw3),
            pl.BlockSpec((1, 1, D), lw3),
            pl.BlockSpec((1, 1, D), lw3),
            pl.BlockSpec((1, N_EXPERTS, D), lw3),
            pl.BlockSpec((1, N_EXPERTS, tm), lw3),
            pl.BlockSpec((MOE_GROUP, MOE_GROUP), lambda b, i: (0, 0)),
        ],
        out_specs=[pl.BlockSpec((1, tm, D), tok), pl.BlockSpec((1, tm, D), tok),
                   pl.BlockSpec((1, tm, 2 * N_EXPERTS), tok),
                   pl.BlockSpec((1, N_EXPERTS, tm), lambda b, i: (b, 0, i)),
                   pl.BlockSpec((1, N_EXPERTS, tm), lambda b, i: (b, 0, i))],
        out_shape=[jax.ShapeDtypeStruct((B, S, D), F32), jax.ShapeDtypeStruct((B, S, D), BF16),
                   jax.ShapeDtypeStruct((B, S, 2 * N_EXPERTS), BF16),
                   jax.ShapeDtypeStruct((B, N_EXPERTS, S), I32),
                   jax.ShapeDtypeStruct((B, N_EXPERTS, S), F32)],
        compiler_params=_cparams(("arbitrary", "arbitrary")),
        name="outproj_router",
    )(ya, yb, x, mod, wo, ln_g, ln_b, wrT, rbias,
      jnp.asarray(np.triu(np.ones((MOE_GROUP, MOE_GROUP), np.float32), 1), BF16))


MOE_CHUNK_EXPERTS = 16


def _slot_onehot(rank_rows, values, cap):
    row = lax.broadcasted_iota(I32, (cap, rank_rows.shape[1]), 0)
    return jnp.concatenate(
        [jnp.where(row == rank_rows[e:e + 1, :], values[e:e + 1, :], 0.0) for e in range(rank_rows.shape[0])], axis=0)


def _dispatch_kernel(u_ref, rank_ref, x_ref):
    u = u_ref[...]
    cap = x_ref.shape[1]
    ones = jnp.ones((MOE_CHUNK_EXPERTS, MOE_GROUP), F32)
    for c in range(N_EXPERTS // MOE_CHUNK_EXPERTS):
        es = slice(c * MOE_CHUNK_EXPERTS, (c + 1) * MOE_CHUNK_EXPERTS)
        onehot = _slot_onehot(rank_ref[0, es, :], ones, cap)
        xs = jnp.dot(onehot.astype(BF16), u, preferred_element_type=F32).astype(BF16)
        x_ref[es] = xs.reshape(MOE_CHUNK_EXPERTS, cap, -1)


def _dispatch(u2, rank, cap):
    T, D = u2.shape
    ng = T // MOE_GROUP
    gps = rank.shape[-1] // MOE_GROUP
    return pl.pallas_call(
        _dispatch_kernel,
        grid=(ng,),
        in_specs=[pl.BlockSpec((MOE_GROUP, D), lambda g: (g, 0)),
                  pl.BlockSpec((1, N_EXPERTS, MOE_GROUP), lambda g: (g // gps, 0, g % gps))],
        out_specs=pl.BlockSpec((N_EXPERTS, cap, D), lambda g: (0, g, 0)),
        out_shape=jax.ShapeDtypeStruct((N_EXPERTS, ng * cap, D), BF16),
        compiler_params=_cparams(("arbitrary",)),
        name="moe_dispatch",
    )(u2, rank)


def _expert_kernel(x_ref, wgu_ref, wd_ref, y_ref):
    hgu = jnp.dot(x_ref[0], wgu_ref[0, 0], preferred_element_type=F32)
    h = _silu(hgu[:, :EXPERT_DIM]) * hgu[:, EXPERT_DIM:]
    y_ref[0] = jnp.dot(h.astype(BF16), wd_ref[0, 0], preferred_element_type=F32).astype(BF16)


def _experts(l, xs, wgu, wd):
    E, R, D = xs.shape
    tr = R // MOE_EXPERT_STEPS
    assert R % MOE_EXPERT_STEPS == 0 and tr % 16 == 0
    return pl.pallas_call(
        _expert_kernel,
        grid=(E, R // tr),
        in_specs=[pl.BlockSpec((1, tr, D), lambda e, i: (e, i, 0)),
                  pl.BlockSpec((1, 1, D, 2 * EXPERT_DIM), lambda e, i: (l, e, 0, 0)),
                  pl.BlockSpec((1, 1, EXPERT_DIM, D), lambda e, i: (l, e, 0, 0))],
        out_specs=pl.BlockSpec((1, tr, D), lambda e, i: (e, i, 0)),
        out_shape=jax.ShapeDtypeStruct((E, R, D), BF16),
        compiler_params=_cparams(("arbitrary", "arbitrary")),
        name="moe_experts",
    )(xs, wgu, wd)


def _combine_kernel(y_ref, rank_ref, gates_ref, u_ref, x1_ref, mod_ref, sgu_ref, sd_ref, lng_ref, lnb_ref,
                    out_ref, *, alpha):
    hgu = jnp.dot(u_ref[...], sgu_ref[0], preferred_element_type=F32)
    hs = _silu(hgu[:, :SHARED_DIM]) * hgu[:, SHARED_DIM:]
    y = jnp.dot(hs.astype(BF16), sd_ref[0], preferred_element_type=F32)
    cap = y_ref.shape[1]
    for c in range(N_EXPERTS // MOE_CHUNK_EXPERTS):
        es = slice(c * MOE_CHUNK_EXPERTS, (c + 1) * MOE_CHUNK_EXPERTS)
        pick = _slot_onehot(rank_ref[0, es, :], gates_ref[0, es, :], cap)
        ys = y_ref[es].reshape(MOE_CHUNK_EXPERTS * cap, -1)
        y = y + lax.dot_general(pick.astype(BF16), ys, (((0,), (0,)), ((), ())), preferred_element_type=F32)
    g2 = mod_ref[0, 5:6, :]
    out_ref[...] = _layernorm(alpha * x1_ref[...] + (1.0 + g2) * y, lng_ref[0], lnb_ref[0])


def _combine(l, ys, rank, gates_t, u2, x1, mod, sgu, sd, ln_g, ln_b, alpha, seq):
    T, D = u2.shape
    tok = lambda g: (g, 0)
    lw3 = lambda g: (l, 0, 0)
    gps = seq // MOE_GROUP
    cap = ys.shape[1] // (T // MOE_GROUP)
    per_group = lambda g: (g // gps, 0, g % gps)
    return pl.pallas_call(
        functools.partial(_combine_kernel, alpha=alpha),
        grid=(T // MOE_GROUP,),
        in_specs=[
            pl.BlockSpec((N_EXPERTS, cap, D), lambda g: (0, g, 0)),
            pl.BlockSpec((1, N_EXPERTS, MOE_GROUP), per_group),
            pl.BlockSpec((1, N_EXPERTS, MOE_GROUP), per_group),
            pl.BlockSpec((MOE_GROUP, D), tok),
            pl.BlockSpec((MOE_GROUP, D), tok),
            pl.BlockSpec((1, 6, D), lambda g: ((g * MOE_GROUP) // seq, 0, 0)),
            pl.BlockSpec((1, D, 2 * SHARED_DIM), lw3),
            pl.BlockSpec((1, SHARED_DIM, D), lw3),
            pl.BlockSpec((1, 1, D), lw3),
            pl.BlockSpec((1, 1, D), lw3),
        ],
        out_specs=pl.BlockSpec((MOE_GROUP, D), tok),
        out_shape=jax.ShapeDtypeStruct((T, D), F32),
        compiler_params=_cparams(("arbitrary",)),
        name="moe_combine",
    )(ys, rank, gates_t, u2, x1, mod, sgu, sd, ln_g, ln_b)


def _moe_kernel(u_ref, gates_ref, x1_ref, mod_ref, wgu_ref, wd_ref, sgu_ref, sd_ref, lng_ref, lnb_ref,
                out_ref, acc_ref, *, alpha):
    s = pl.program_id(1)
    u = u_ref[...]

    def hidden(wgu):
        hgu = jnp.dot(u, wgu, preferred_element_type=F32)
        return _silu(hgu[:, :EXPERT_DIM]) * hgu[:, EXPERT_DIM:]

    @pl.when(s == 0)
    def _():
        acc_ref[...] = jnp.dot(hidden(sgu_ref[0]).astype(BF16), sd_ref[0], preferred_element_type=F32)

    rows = lax.broadcasted_iota(I32, (2 * N_EXPERTS, MOE_EXPERTS_PER_STEP * EXPERT_DIM), 0) & (N_EXPERTS - 1)
    cols = lax.broadcasted_iota(I32, (2 * N_EXPERTS, MOE_EXPERTS_PER_STEP * EXPERT_DIM), 1)
    onehot = jnp.where(rows == s * MOE_EXPERTS_PER_STEP + cols // EXPERT_DIM, 1.0, 0.0).astype(BF16)
    gate = jnp.dot(gates_ref[...], onehot, preferred_element_type=F32)
    h = jnp.concatenate(
        [(hidden(wgu_ref[0, k]) * gate[:, k * EXPERT_DIM:(k + 1) * EXPERT_DIM]).astype(BF16)
         for k in range(MOE_EXPERTS_PER_STEP)], axis=1)
    wd = wd_ref[0].reshape(MOE_EXPERTS_PER_STEP * EXPERT_DIM, wd_ref.shape[-1])
    acc_ref[...] += jnp.dot(h, wd, preferred_element_type=F32)

    @pl.when(s == pl.num_programs(1) - 1)
    def _():
        g2 = mod_ref[0, 5:6, :]
        out_ref[...] = _layernorm(alpha * x1_ref[...] + (1.0 + g2) * acc_ref[...], lng_ref[0], lnb_ref[0])


def _moe(l, u2, gates, x1, mod, wgu, wd, sgu, sd, ln_g, ln_b, alpha, seq):
    T, D = u2.shape
    tm = TM_MOE
    assert seq % tm == 0
    tok = lambda i, e: (i, 0)
    lw3 = lambda i, e: (l, 0, 0)
    return pl.pallas_call(
        functools.partial(_moe_kernel, alpha=alpha),
        grid=(T // tm, N_EXPERTS // MOE_EXPERTS_PER_STEP),
        in_specs=[
            pl.BlockSpec((tm, D), tok),
            pl.BlockSpec((tm, 2 * N_EXPERTS), tok),
            pl.BlockSpec((tm, D), tok),
            pl.BlockSpec((1, 6, D), lambda i, e: ((i * tm) // seq, 0, 0)),
            pl.BlockSpec((1, MOE_EXPERTS_PER_STEP, D, 2 * EXPERT_DIM), lambda i, e: (l, e, 0, 0)),
            pl.BlockSpec((1, MOE_EXPERTS_PER_STEP, EXPERT_DIM, D), lambda i, e: (l, e, 0, 0)),
            pl.BlockSpec((1, D, 2 * SHARED_DIM), lw3),
            pl.BlockSpec((1, SHARED_DIM, D), lw3),
            pl.BlockSpec((1, 1, D), lw3),
            pl.BlockSpec((1, 1, D), lw3),
        ],
        out_specs=pl.BlockSpec((tm, D), tok),
        out_shape=jax.ShapeDtypeStruct((T, D), F32),
        scratch_shapes=[pltpu.VMEM((tm, D), F32)],
        compiler_params=_cparams(("arbitrary", "arbitrary")),
        name="moe_dense",
    )(u2, gates, x1, mod, wgu, wd, sgu, sd, ln_g, ln_b)


def _prepare_params(w_in, kv_norm_g, w_uk, w_uv, hgrn_lb, w_out, w_router, router_bias,
                    w_gate, w_up, w_down, ws_gate, ws_up, ws_down):
    L = w_in.shape[0]
    sizes = (A_WIDTH, KV_RANK, IDX_HEADS * IDX_DIM, IDX_DIM, IDX_HEADS, B_FDIM, B_FDIM, B_WIDTH, B_WIDTH)
    offs = np.concatenate([[0], np.cumsum(sizes)])
    seg = lambda i: w_in[:, :, offs[i]:offs[i + 1]]
    w_aq, w_ckv, w_iq, w_ik, w_iw, w_hq, w_hf, w_hi, w_hg = (seg(i) for i in range(9))
    zik = jnp.zeros_like(w_ik)
    wp = jnp.concatenate([w_ckv, w_ik, zik, zik, w_ik, w_hq, w_hf, w_hg, w_hi], axis=-1).astype(BF16)
    wt = jnp.swapaxes(jnp.concatenate([w_aq, w_iq, w_ckv, w_iw], axis=-1), 1, 2).astype(BF16)
    assert wp.shape[-1] == _C_END and wt.shape[1] == _R_END
    eye = jnp.eye(A_HEADS, dtype=F32)
    wblk = (jnp.einsum('lhdr,hg->lhdgr', w_uk * (ATTN_SCALE * LOG2E), eye)
            .reshape(L, A_WIDTH, A_HEADS * KV_RANK).astype(BF16))
    p = dict(
        wp=wp, wt=wt, wblkT=jnp.swapaxes(wblk, 1, 2),
        gkv=kv_norm_g.reshape(L, 1, KV_RANK),
        gkvT=jnp.broadcast_to(kv_norm_g[:, :, None], (L, KV_RANK, TM_PROJ)),
        wuvT=jnp.swapaxes(w_uv, 2, 3).astype(BF16),
        wo=w_out.astype(BF16),
        wrT=jnp.swapaxes(w_router, 1, 2).astype(BF16),
        rbias=jnp.broadcast_to(router_bias[:, :, None], (L, N_EXPERTS, TM_PROJ)),
        wgu=jnp.concatenate([w_gate, w_up], axis=-1).astype(BF16),
        wd=w_down.astype(BF16),
        sgu=jnp.concatenate([ws_gate, ws_up], axis=-1).astype(BF16),
        sd=ws_down.astype(BF16),
    )
    lbs = jnp.cumsum(jax.nn.softmax(hgrn_lb.astype(F32), axis=0), axis=0)
    lbs = jnp.clip(lbs - lbs[0:1], 0.0, 1.0 - 1e-6)
    p["llb"] = jnp.log(lbs).reshape(L, 1, B_FDIM)
    p["l1m"] = jnp.log1p(-lbs).reshape(L, 1, B_FDIM)
    return p


def kernel(x, c, w_ada, b_ada, w_in, kv_norm_g, w_uk, w_uv, rel_bias, hgrn_lb, gnorm_g, w_out, ln1_g, ln1_b,
           w_router, router_bias, w_gate, w_up, w_down, ws_gate, ws_up, ws_down, ln2_g, ln2_b):
    B, S, D = x.shape
    L = w_in.shape[0]
    alpha = (2 * L) ** 0.25
    p = _prepare_params(w_in, kv_norm_g, w_uk, w_uv, hgrn_lb, w_out, w_router, router_bias,
                        w_gate, w_up, w_down, ws_gate, ws_up, ws_down)
    mods = _adaln(c, w_ada, b_ada).reshape(L, B, 6, D)
    bn = _bias_tile(rel_bias)
    gn = gnorm_g.reshape(L, 1, B_VAL_DIM)
    ln1g, ln1b = ln1_g.reshape(L, 1, D), ln1_b.reshape(L, 1, D)
    ln2g, ln2b = ln2_g.reshape(L, 1, D), ln2_b.reshape(L, 1, D)
    for l in range(L):
        mod = mods[l]
        (qlat, ckv, ckvT, iq, ikA, ikB, iwT, hq, hk, hlf, hv, hgate) = _inproj(
            l, x, mod, p["wp"], p["wt"], p["wblkT"], p["gkv"], p["gkvT"], p["llb"], p["l1m"])
        ya = _dsa(l, iq, iwT, qlat, ikA, ikB, ckv, ckvT, bn, p["wuvT"])
        yb = _hgrn(l, hq, hk, hlf, hv, hgate, gn)
        x1, u2, gates, rank, gates_t = _outproj(l, ya, yb, x, mod, p["wo"], ln1g, ln1b, p["wrT"], p["rbias"], alpha)
        u2f, x1f = u2.reshape(B * S, D), x1.reshape(B * S, D)

        def moe_sparse(cap, l=l, mod=mod, u2f=u2f, x1f=x1f, rank=rank, gates_t=gates_t):
            ys = _experts(l, _dispatch(u2f, rank, cap), p["wgu"], p["wd"])
            return _combine(l, ys, rank, gates_t, u2f, x1f, mod, p["sgu"], p["sd"], ln2g, ln2b, alpha, S)

        def moe_dense(l=l, mod=mod, u2f=u2f, x1f=x1f, gates=gates):
            return _moe(l, u2f, gates.reshape(B * S, 2 * N_EXPERTS), x1f, mod,
                        p["wgu"], p["wd"], p["sgu"], p["sd"], ln2g, ln2b, alpha, S)

        branches = [functools.partial(moe_sparse, cap) for cap in MOE_CAPS] + [moe_dense]
        tier = sum((jnp.max(rank) >= cap).astype(I32) for cap in MOE_CAPS)
        x = lax.switch(tier, branches).reshape(B, S, D)
    return x
```
